```python
import math
import jax, jax.numpy as jnp
from jax import lax
import numpy as np

D_MODEL = 1024
BATCH = 8
SEQ = 4096
DEPTH = 2

N_MIXERS = 2
POOL_WINDOWS = (2, 4, 8, 16)
POOL_GROUPS = len(POOL_WINDOWS)
POOL_GROUP_DIM = D_MODEL // POOL_GROUPS
N_HEADS = 16
N_KV_GROUPS = 4
HEADS_PER_GROUP = N_HEADS // N_KV_GROUPS
HEAD_DIM = D_MODEL // N_HEADS
Q_DIM = N_HEADS * HEAD_DIM
KV_DIM = N_KV_GROUPS * HEAD_DIM
N_BRANCHES = 3
IN_PROJ_DIM = Q_DIM + 2 * N_BRANCHES * KV_DIM + N_BRANCHES * N_HEADS
CMP_STRIDE = 16
CMP_BLOCK = 2 * CMP_STRIDE
CMP_HIDDEN = 2 * HEAD_DIM
SEL_BLOCK = 64
SEL_TOP_N = 16
WINDOW = 512
Q_BLOCK = 64
FORCE_BONUS = 1.0e3
NEG_INF = -1.0e30
ROPE_THETA = 10000.0
ATTN_SCALE = HEAD_DIM ** -0.5
D_FF = 2816
N_EXPERTS = 8
TOP_K = 2
D_FF_EXPERT = 3584
LN_EPS = 1e-5
ALPHA = (2 * DEPTH) ** 0.25
BETA = (8 * DEPTH) ** -0.25

kernel_name = "hybrid_pool_nsa_moe_deepnorm"


def layer_norm(x, g, b):
    xf = x.astype(jnp.float32)
    mu = jnp.mean(xf, axis=-1, keepdims=True)
    var = jnp.mean(jnp.square(xf - mu), axis=-1, keepdims=True)
    return ((xf - mu) * lax.rsqrt(var + LN_EPS) * g.astype(jnp.float32) + b.astype(jnp.float32)).astype(x.dtype)


def rope(x, pos):
    half = x.shape[-1] // 2
    freqs = jnp.power(ROPE_THETA, -jnp.arange(half, dtype=jnp.float32) / half)
    ang = pos.astype(jnp.float32)[:, None] * freqs[None, :]
    cos = jnp.cos(ang)[None, :, None, :]
    sin = jnp.sin(ang)[None, :, None, :]
    xf = x.astype(jnp.float32)
    x1, x2 = xf[..., :half], xf[..., half:]
    return jnp.concatenate([x1 * cos - x2 * sin, x1 * sin + x2 * cos], axis=-1).astype(x.dtype)


def pool_mixer(x, w_grp, scale):
    B, S, _ = x.shape
    xg = x.reshape(B, S, POOL_GROUPS, POOL_GROUP_DIM)
    c = jnp.concatenate([jnp.zeros((B, 1, POOL_GROUPS, POOL_GROUP_DIM), jnp.float32),
                         jnp.cumsum(xg.astype(jnp.float32), axis=1)], axis=1)
    t1 = jnp.arange(1, S + 1)
    pooled = []
    for gi, w in enumerate(POOL_WINDOWS):
        cg = c[:, :, gi]
        lo = cg[:, jnp.maximum(t1 - w, 0)]
        cnt = jnp.minimum(t1, w).astype(jnp.float32)
        pooled.append((cg[:, 1:] - lo) / cnt[None, :, None])
    pooled = jnp.stack(pooled, axis=2).astype(x.dtype)
    y = jnp.einsum('bsgc,gcd->bsgd', pooled - xg, w_grp).reshape(B, S, D_MODEL)
    return y * scale


def compress(k, pe, w1, w2):
    B, S, G, D = k.shape
    ch = k.reshape(B, S // CMP_STRIDE, CMP_STRIDE, G, D)
    blk = jnp.concatenate([ch[:, :-1], ch[:, 1:]], axis=2)
    blk = blk + pe[None, None, :, None, :]
    nc = blk.shape[1]
    blk = blk.transpose(0, 1, 3, 2, 4).reshape(B, nc, G, CMP_BLOCK * D)
    return jax.nn.silu(blk @ w1) @ w2


def nsa_mixer(x, w_in, pe_k, w1_k, w2_k, pe_v, w1_v, w2_v, w_out):
    B, S, _ = x.shape
    G, HG, D = N_KV_GROUPS, HEADS_PER_GROUP, HEAD_DIM
    proj = x @ w_in
    splits = [Q_DIM + i * KV_DIM for i in range(2 * N_BRANCHES + 1)]
    q, kc_raw, vc_raw, ks, vs, kw, vw, g_logits = jnp.split(proj, splits, axis=-1)
    pos = jnp.arange(S)
    q = rope(q.reshape(B, S, N_HEADS, D), pos)
    q = q.reshape(B, S, G, HG, D).transpose(0, 2, 3, 1, 4)
    kv = lambda a: a.reshape(B, S, G, D)
    ks = rope(kv(ks), pos)
    kw = rope(kv(kw), pos)
    vs, vw = kv(vs), kv(vw)
    kc = compress(kv(kc_raw), pe_k, w1_k, w2_k)
    vc = compress(kv(vc_raw), pe_v, w1_v, w2_v)
    nc = kc.shape[1]
    cend = CMP_STRIDE * jnp.arange(nc) + CMP_BLOCK - 1
    kc = rope(kc, cend)
    gates = jax.nn.sigmoid(g_logits.astype(jnp.float32)).astype(x.dtype).reshape(B, S, N_HEADS, N_BRANCHES)

    nsel = S // SEL_BLOCK
    n_top = min(SEL_TOP_N, nsel)
    ks_blk = ks.reshape(B, nsel, SEL_BLOCK, G, D).transpose(0, 3, 1, 2, 4)
    vs_blk = vs.reshape(B, nsel, SEL_BLOCK, G, D).transpose(0, 3, 1, 2, 4)
    kw_pad = jnp.pad(kw, ((0, 0), (WINDOW, 0), (0, 0), (0, 0)))
    vw_pad = jnp.pad(vw, ((0, 0), (WINDOW, 0), (0, 0), (0, 0)))
    cstart = CMP_STRIDE * jnp.arange(nc)
    sstart = SEL_BLOCK * jnp.arange(nsel)
    overlap = ((cstart[:, None] <= sstart[None, :] + SEL_BLOCK - 1) &
               (cstart[:, None] + CMP_BLOCK - 1 >= sstart[None, :])).astype(jnp.float32)
    bi = jnp.arange(B)[:, None, None, None]
    gi = jnp.arange(G)[None, :, None, None]
    jsel = jnp.arange(nsel)

    def query_block(i):
        t0 = i * Q_BLOCK
        t = t0 + jnp.arange(Q_BLOCK)
        qb = lax.dynamic_slice_in_dim(q, t0, Q_BLOCK, axis=3)
        sc = jnp.einsum('bghtd,bngd->bghtn', qb, kc).astype(jnp.float32) * ATTN_SCALE
        cvalid = cend[None, :] <= t[:, None]
        pc = jax.nn.softmax(jnp.where(cvalid, sc, NEG_INF), axis=-1)
        pc = pc * jnp.any(cvalid, axis=-1).astype(jnp.float32)[:, None]
        o_c = jnp.einsum('bghtn,bngd->btghd', pc.astype(vc.dtype), vc)
        imp = jnp.einsum('bghtn,nj->bgtj', pc, overlap)
        blk_t = t // SEL_BLOCK
        bvalid = jsel[None, :] <= blk_t[:, None]
        forced = (jsel[None, :] == 0) | (jsel[None, :] == blk_t[:, None]) | (jsel[None, :] == blk_t[:, None] - 1)
        score = jnp.where(bvalid, imp + FORCE_BONUS * forced.astype(jnp.float32), -1.0)
        _, idx = lax.top_k(score, n_top)
        sel_valid = idx <= blk_t[None, None, :, None]
        kg = ks_blk[bi, gi, idx]
        vg = vs_blk[bi, gi, idx]
        ss = jnp.einsum('bghtd,bgtnld->bghtnl', qb, kg).astype(jnp.float32) * ATTN_SCALE
        kpos = idx[..., None] * SEL_BLOCK + jnp.arange(SEL_BLOCK)
        ms = (kpos <= t[None, None, :, None, None]) & sel_valid[..., None]
        ss = jnp.where(ms[:, :, None], ss, NEG_INF)
        ps = jax.nn.softmax(ss.reshape(ss.shape[:4] + (n_top * SEL_BLOCK,)), axis=-1).reshape(ss.shape)
        o_s = jnp.einsum('bghtnl,bgtnld->btghd', ps.astype(vg.dtype), vg)
        kwb = lax.dynamic_slice_in_dim(kw_pad, t0, Q_BLOCK + WINDOW, axis=1)
        vwb = lax.dynamic_slice_in_dim(vw_pad, t0, Q_BLOCK + WINDOW, axis=1)
        kpos_w = t0 - WINDOW + jnp.arange(Q_BLOCK + WINDOW)
        mw = (kpos_w[None, :] <= t[:, None]) & (kpos_w[None, :] > t[:, None] - WINDOW) & (kpos_w[None, :] >= 0)
        sw = jnp.einsum('bghtd,bkgd->bghtk', qb, kwb).astype(jnp.float32) * ATTN_SCALE
        pw = jax.nn.softmax(jnp.where(mw, sw, NEG_INF), axis=-1)
        o_w = jnp.einsum('bghtk,bkgd->btghd', pw.astype(vwb.dtype), vwb)
        gb = lax.dynamic_slice_in_dim(gates, t0, Q_BLOCK, axis=1).reshape(B, Q_BLOCK, G, HG, N_BRANCHES)
        return gb[..., 0:1] * o_c + gb[..., 1:2] * o_s + gb[..., 2:3] * o_w

    out = lax.map(query_block, jnp.arange(S // Q_BLOCK))
    out = out.transpose(1, 0, 2, 3, 4, 5).reshape(B, S, Q_DIM)
    return out @ w_out


def swiglu(x, w_gu, w_down):
    g, u = jnp.split(x @ w_gu, 2, axis=-1)
    return (jax.nn.silu(g) * u) @ w_down


def moe_ffn(x, router, w_gu, w_down):
    B, S, D = x.shape
    xt = x.reshape(B * S, D)
    logits = (xt @ router).astype(jnp.float32)
    top_v, top_i = lax.top_k(logits, TOP_K)
    w = jax.nn.softmax(top_v, axis=-1)
    combine = jnp.einsum('tk,tke->te', w, jax.nn.one_hot(top_i, N_EXPERTS, dtype=jnp.float32)).astype(x.dtype)
    y = jnp.zeros_like(xt)
    for e in range(N_EXPERTS):
        y = y + combine[:, e:e + 1] * swiglu(xt, w_gu[e], w_down[e])
    return y.reshape(B, S, D)


def setup_inputs(seed: int = 0) -> dict:
    key = jax.random.key(seed)
    keys = iter(jax.random.split(key, 32))
    nrm = lambda shape, s: jax.random.normal(next(keys), shape, jnp.float32) * s
    na = (DEPTH + 1) // 2
    nb = DEPTH // 2
    return {
        "x": nrm((BATCH, SEQ, D_MODEL), 1.0),
        "ln_g": 1.0 + nrm((DEPTH, 2, D_MODEL), 0.05),
        "ln_b": nrm((DEPTH, 2, D_MODEL), 0.02),
        "pool_w": nrm((na, POOL_GROUPS, POOL_GROUP_DIM, POOL_GROUP_DIM), BETA * POOL_GROUP_DIM ** -0.5),
        "pool_scale": 1.0 + nrm((na, D_MODEL), 0.1),
        "nsa_w_in": nrm((nb, D_MODEL, IN_PROJ_DIM), D_MODEL ** -0.5),
        "nsa_pe_k": nrm((nb, CMP_BLOCK, HEAD_DIM), 0.1),
        "nsa_w1_k": nrm((nb, CMP_BLOCK * HEAD_DIM, CMP_HIDDEN), (CMP_BLOCK * HEAD_DIM) ** -0.5),
        "nsa_w2_k": nrm((nb, CMP_HIDDEN, HEAD_DIM), CMP_HIDDEN ** -0.5),
        "nsa_pe_v": nrm((nb, CMP_BLOCK, HEAD_DIM), 0.1),
        "nsa_w1_v": nrm((nb, CMP_BLOCK * HEAD_DIM, CMP_HIDDEN), (CMP_BLOCK * HEAD_DIM) ** -0.5),
        "nsa_w2_v": nrm((nb, CMP_HIDDEN, HEAD_DIM), CMP_HIDDEN ** -0.5),
        "nsa_w_out": nrm((nb, Q_DIM, D_MODEL), BETA * Q_DIM ** -0.5),
        "ffn_w_gu": nrm((na, D_MODEL, 2 * D_FF), D_MODEL ** -0.5),
        "ffn_w_down": nrm((na, D_FF, D_MODEL), BETA * D_FF ** -0.5),
        "moe_router": nrm((nb, D_MODEL, N_EXPERTS), D_MODEL ** -0.5),
        "moe_w_gu": nrm((nb, N_EXPERTS, D_MODEL, 2 * D_FF_EXPERT), D_MODEL ** -0.5),
        "moe_w_down": nrm((nb, N_EXPERTS, D_FF_EXPERT, D_MODEL), BETA * D_FF_EXPERT ** -0.5),
    }


def reference(x, ln_g, ln_b, pool_w, pool_scale, nsa_w_in, nsa_pe_k, nsa_w1_k, nsa_w2_k,
              nsa_pe_v, nsa_w1_v, nsa_w2_v, nsa_w_out, ffn_w_gu, ffn_w_down,
              moe_router, moe_w_gu, moe_w_down):
    for i in range(DEPTH):
        j = i // N_MIXERS
        if i % N_MIXERS == 0:
            h = pool_mixer(x, pool_w[j], pool_scale[j])
        else:
            h = nsa_mixer(x, nsa_w_in[j], nsa_pe_k[j], nsa_w1_k[j], nsa_w2_k[j],
                          nsa_pe_v[j], nsa_w1_v[j], nsa_w2_v[j], nsa_w_out[j])
        x = layer_norm(ALPHA * x + h, ln_g[i, 0], ln_b[i, 0])
        if i % 2 == 0:
            f = swiglu(x, ffn_w_gu[j], ffn_w_down[j])
        else:
            f = moe_ffn(x, moe_router[j], moe_w_gu[j], moe_w_down[j])
        x = layer_norm(ALPHA * x + f, ln_g[i, 1], ln_b[i, 1])
    return x
```

```python
import functools

import jax
import jax.numpy as jnp
from jax import lax
from jax.experimental import pallas as pl
from jax.experimental.pallas import tpu as pltpu

D_MODEL = 1024
DEPTH = 2
POOL_WINDOWS = (2, 4, 8, 16)
POOL_GROUP_DIM = D_MODEL // len(POOL_WINDOWS)
POOL_HALO = 16
N_HEADS = 16
N_KV_GROUPS = 4
HEADS_PER_GROUP = N_HEADS // N_KV_GROUPS
HEAD_DIM = D_MODEL // N_HEADS
HALF_DIM = HEAD_DIM // 2
GROUP_Q_DIM = HEADS_PER_GROUP * HEAD_DIM
KV_DIM = N_KV_GROUPS * HEAD_DIM
N_BRANCHES = 3
N_GATES = N_BRANCHES * N_HEADS
CMP_STRIDE = 16
CMP_BLOCK = 2 * CMP_STRIDE
CMP_HIDDEN = 2 * HEAD_DIM
SEL_BLOCK = 64
SEL_SHIFT = SEL_BLOCK.bit_length() - 1
SEL_TOP_N = 16
WINDOW = 512
FORCE_BONUS = 1.0e3
NEG_INF = -1.0e30
ROPE_THETA = 10000.0
ATTN_SCALE = HEAD_DIM ** -0.5
N_EXPERTS = 8
LN_EPS = 1e-5
ALPHA = (2 * DEPTH) ** 0.25

LANES = 128
VMEM_LIMIT_BYTES = 56 * 1024 * 1024

POOL_ROWS = 512
FFN_ROWS = 512
PROJ_ROWS = 512
PROJ_COLS = 256
CMP_Q_ROWS = 512
ATT_Q_ROWS = 256
ATT_K_ROWS = 256

F32 = jnp.float32
BF16 = jnp.bfloat16


def _dot(a, b):
    return jnp.dot(a, b, preferred_element_type=F32)


def _dot_nt(a, b):
    return lax.dot_general(a, b, (((1,), (1,)), ((), ())), preferred_element_type=F32)


def _layer_norm(z, g, b):
    mu = jnp.mean(z, axis=-1, keepdims=True)
    zc = z - mu
    var = jnp.mean(zc * zc, axis=-1, keepdims=True)
    return zc * lax.rsqrt(var + LN_EPS) * g + b


def _silu(x):
    return x / (1.0 + jnp.exp(-x))


def _params(*semantics):
    return pltpu.CompilerParams(dimension_semantics=semantics, vmem_limit_bytes=VMEM_LIMIT_BYTES)


def _pool_ln_kernel(x_ref, halo_ref, w_ref, scale_ref, g_ref, b_ref, o_ref, ext_ref):
    i = pl.program_id(1)
    rows = x_ref.shape[1]
    x = x_ref[0]
    ext_ref[0:POOL_HALO, :] = jnp.where(i > 0, halo_ref[0], 0.0)
    ext_ref[POOL_HALO:, :] = x
    pos = i * rows + lax.broadcasted_iota(jnp.int32, (rows, 1), 0)
    ys = []
    for gi, w in enumerate(POOL_WINDOWS):
        c0 = gi * POOL_GROUP_DIM
        xg = x[:, c0:c0 + POOL_GROUP_DIM]
        acc = xg
        for k in range(1, w):
            acc = acc + ext_ref[POOL_HALO - k:POOL_HALO - k + rows, c0:c0 + POOL_GROUP_DIM]
        cnt = jnp.minimum(pos + 1, w).astype(F32)
        diff = acc / cnt - xg
        ys.append(_dot(diff.astype(BF16), w_ref[gi]))
    h = jnp.concatenate(ys, axis=1) * scale_ref[...]
    o_ref[0] = _layer_norm(ALPHA * x + h, g_ref[...], b_ref[...])


def _pool_ln(x, w, scale, g, b):
    B, S, D = x.shape
    rows = POOL_ROWS
    halo_blocks = rows // POOL_HALO
    row2 = lambda v: v.reshape(1, D)
    return pl.pallas_call(
        _pool_ln_kernel,
        grid=(B, S // rows),
        in_specs=[
            pl.BlockSpec((1, rows, D), lambda bi, i: (bi, i, 0)),
            pl.BlockSpec((1, POOL_HALO, D), lambda bi, i: (bi, jnp.maximum(i * halo_blocks - 1, 0), 0)),
            pl.BlockSpec(w.shape, lambda bi, i: (0, 0, 0)),
            pl.BlockSpec((1, D), lambda bi, i: (0, 0)),
            pl.BlockSpec((1, D), lambda bi, i: (0, 0)),
            pl.BlockSpec((1, D), lambda bi, i: (0, 0)),
        ],
        out_specs=pl.BlockSpec((1, rows, D), lambda bi, i: (bi, i, 0)),
        out_shape=jax.ShapeDtypeStruct((B, S, D), F32),
        scratch_shapes=[pltpu.VMEM((rows + POOL_HALO, D), F32)],
        compiler_params=_params("parallel", "arbitrary"),
    )(x, x, w.astype(BF16), row2(scale), row2(g), row2(b))


def _ffn_ln_kernel(x_ref, wg_ref, wu_ref, wd_ref, g_ref, b_ref, o_ref, xb_ref, acc_ref):
    f = pl.program_id(1)

    @pl.when(f == 0)
    def _():
        xb_ref[...] = x_ref[...].astype(BF16)
        acc_ref[...] = jnp.zeros_like(acc_ref)

    xb = xb_ref[...]
    a = _silu(_dot(xb, wg_ref[...])) * _dot(xb, wu_ref[...])
    acc_ref[...] += _dot(a.astype(BF16), wd_ref[...])

    @pl.when(f == pl.num_programs(1) - 1)
    def _():
        o_ref[...] = _layer_norm(ALPHA * x_ref[...] + acc_ref[...], g_ref[...], b_ref[...])


def _ff_chunk(d_ff, target):
    best = LANES
    for c in range(LANES, target + 1, LANES):
        if d_ff % c == 0:
            best = c
    return best


def _ffn_ln(x, w_gu, w_down, g, b):
    T, D = x.shape
    d_ff = w_down.shape[0]
    fc = _ff_chunk(d_ff, 1536)
    nf = d_ff // fc
    rows = FFN_ROWS
    return pl.pallas_call(
        _ffn_ln_kernel,
        grid=(T // rows, nf),
        in_specs=[
            pl.BlockSpec((rows, D), lambda i, f: (i, 0)),
            pl.BlockSpec((D, fc), lambda i, f: (0, f)),
            pl.BlockSpec((D, fc), lambda i, f: (0, nf + f)),
            pl.BlockSpec((fc, D), lambda i, f: (f, 0)),
            pl.BlockSpec((1, D), lambda i, f: (0, 0)),
            pl.BlockSpec((1, D), lambda i, f: (0, 0)),
        ],
        out_specs=pl.BlockSpec((rows, D), lambda i, f: (i, 0)),
        out_shape=jax.ShapeDtypeStruct((T, D), F32),
        scratch_shapes=[pltpu.VMEM((rows, D), BF16), pltpu.VMEM((rows, D), F32)],
        compiler_params=_params("parallel", "arbitrary"),
    )(x, w_gu, w_gu, w_down, g.reshape(1, D), b.reshape(1, D))


def _rope_tables(pos, reps):
    freqs = jnp.power(ROPE_THETA, -jnp.arange(HALF_DIM, dtype=F32) / HALF_DIM)
    ang = pos.astype(F32)[:, None] * freqs[None, :]
    cos, sin = jnp.cos(ang), jnp.sin(ang)
    return (jnp.tile(jnp.concatenate([cos, cos], axis=1), (1, reps)),
            jnp.tile(jnp.concatenate([-sin, sin], axis=1), (1, reps)))


def _in_proj_kernel(x_ref, w_ref, cos_ref, sin_ref,
                    q_ref, kcv_ref, ks_ref, vs_ref, kw_ref, vw_ref, gate_ref):
    xb = x_ref[...].astype(BF16)
    cos = cos_ref[...]
    sin = sin_ref[...]
    lane = lax.broadcasted_iota(jnp.int32, cos.shape, 1)
    first_half = (lane & (HEAD_DIM - 1)) < HALF_DIM

    def col_tile(j):
        return _dot(xb, w_ref[:, j * PROJ_COLS:(j + 1) * PROJ_COLS])

    def rope(y):
        rot = jnp.where(first_half,
                        pltpu.roll(y, PROJ_COLS - HALF_DIM, 1),
                        pltpu.roll(y, HALF_DIM, 1))
        return y * cos + rot * sin

    def store_groups(ref, y):
        for gi in range(N_KV_GROUPS):
            ref[0, gi] = y[:, gi * HEAD_DIM:(gi + 1) * HEAD_DIM].astype(ref.dtype)

    n_q = D_MODEL // PROJ_COLS
    for j in range(n_q):
        q_ref[:, j * PROJ_COLS:(j + 1) * PROJ_COLS] = (rope(col_tile(j)) * ATTN_SCALE).astype(q_ref.dtype)
    kcv_ref[:, 0:KV_DIM] = col_tile(n_q)
    kcv_ref[:, KV_DIM:2 * KV_DIM] = col_tile(n_q + 1)
    store_groups(ks_ref, rope(col_tile(n_q + 2)))
    store_groups(vs_ref, col_tile(n_q + 3))
    store_groups(kw_ref, rope(col_tile(n_q + 4)))
    store_groups(vw_ref, col_tile(n_q + 5))
    logits = col_tile(n_q + 6)[:, 0:LANES]
    gate_ref[...] = 1.0 / (1.0 + jnp.exp(-logits))


def _in_proj(x, w_in, B, S):
    T, D = x.shape
    rows = PROJ_ROWS
    steps_per_seq = S // rows
    n_tiles = D_MODEL // PROJ_COLS + 2 * N_BRANCHES + 1
    w = jnp.pad(w_in, ((0, 0), (0, n_tiles * PROJ_COLS - w_in.shape[1]))).astype(BF16)
    cos, sin = _rope_tables(jnp.arange(S), PROJ_COLS // HEAD_DIM)
    kv_shape = jax.ShapeDtypeStruct((B, N_KV_GROUPS, S, HEAD_DIM), BF16)
    kv_spec = pl.BlockSpec((1, N_KV_GROUPS, rows, HEAD_DIM),
                           lambda i: (i // steps_per_seq, 0, i % steps_per_seq, 0))
    return pl.pallas_call(
        _in_proj_kernel,
        grid=(T // rows,),
        in_specs=[
            pl.BlockSpec((rows, D), lambda i: (i, 0)),
            pl.BlockSpec(w.shape, lambda i: (0, 0)),
            pl.BlockSpec((rows, PROJ_COLS), lambda i: (i % steps_per_seq, 0)),
            pl.BlockSpec((rows, PROJ_COLS), lambda i: (i % steps_per_seq, 0)),
        ],
        out_specs=[
            pl.BlockSpec((rows, D_MODEL), lambda i: (i, 0)),
            pl.BlockSpec((rows, 2 * KV_DIM), lambda i: (i, 0)),
            kv_spec, kv_spec, kv_spec, kv_spec,
            pl.BlockSpec((rows, LANES), lambda i: (i, 0)),
        ],
        out_shape=[
            jax.ShapeDtypeStruct((T, D_MODEL), BF16),
            jax.ShapeDtypeStruct((T, 2 * KV_DIM), F32),
            kv_shape, kv_shape, kv_shape, kv_shape,
            jax.ShapeDtypeStruct((T, LANES), F32),
        ],
        compiler_params=_params("parallel"),
    )(x, w, cos, sin)


def _compress_kernel(ch_ref, w1_ref, pe_ref, w2_ref, cos_ref, sin_ref, kc_ref, vc_ref):
    half = CMP_STRIDE * HEAD_DIM
    for which, o_ref in enumerate((kc_ref, vc_ref)):
        w1 = w1_ref[which]
        r = _dot(ch_ref[which, 0, 0], w1)
        pe = pe_ref[which]
        bias = (_dot(pe[:, 0:half], w1[:, 0:CMP_HIDDEN])
                + _dot(pe[:, half:2 * half], w1[:, CMP_HIDDEN:2 * CMP_HIDDEN]))[0:1, :]
        nxt = pltpu.roll(r[:, CMP_HIDDEN:2 * CMP_HIDDEN], r.shape[0] - 1, 0)
        pre = r[:, 0:CMP_HIDDEN] + nxt + bias
        o = _dot(_silu(pre).astype(BF16), w2_ref[which])
        if which == 0:
            rot = jnp.concatenate([o[:, HALF_DIM:], o[:, :HALF_DIM]], axis=1)
            o = o * cos_ref[...] + rot * sin_ref[...]
        o_ref[0, 0] = o.astype(o_ref.dtype)


def _compress(kcv, w1_k, w2_k, pe_k, w1_v, w2_v, pe_v, B, S):
    n_chunks = S // CMP_STRIDE
    half = CMP_STRIDE * HEAD_DIM
    ch = kcv.reshape(B, n_chunks, CMP_STRIDE, 2, N_KV_GROUPS, HEAD_DIM)
    ch = ch.transpose(3, 0, 4, 1, 2, 5).reshape(2, B, N_KV_GROUPS, n_chunks, half).astype(BF16)
    cat = lambda w1: jnp.concatenate([w1[:half], w1[half:]], axis=1)
    w1 = jnp.stack([cat(w1_k), cat(w1_v)]).astype(BF16)
    w2 = jnp.stack([w2_k, w2_v]).astype(BF16)
    pe = jnp.stack([pe_k.reshape(1, -1), pe_v.reshape(1, -1)])
    pe = jnp.broadcast_to(pe, (2, 8, 2 * half)).astype(BF16)
    cos, sin = _rope_tables(CMP_STRIDE * jnp.arange(n_chunks) + CMP_BLOCK - 1, 1)
    out_shape = jax.ShapeDtypeStruct((B, N_KV_GROUPS, n_chunks, HEAD_DIM), BF16)
    out_spec = pl.BlockSpec((1, 1, n_chunks, HEAD_DIM), lambda bi, gi: (bi, gi, 0, 0))
    const = lambda a: pl.BlockSpec(a.shape, lambda bi, gi: (0,) * a.ndim)
    return pl.pallas_call(
        _compress_kernel,
        grid=(B, N_KV_GROUPS),
        in_specs=[
            pl.BlockSpec((2, 1, 1, n_chunks, half), lambda bi, gi: (0, bi, gi, 0, 0)),
            const(w1), const(pe), const(w2), const(cos), const(sin),
        ],
        out_specs=[out_spec, out_spec],
        out_shape=[out_shape, out_shape],
        compiler_params=_params("parallel", "parallel"),
    )(ch, w1, pe, w2, cos, sin)


def _cmp_attn_kernel(q_ref, kc_ref, vc_ref, ovl_ref, oc_ref, sel_ref, score_ref):
    i = pl.program_id(2)
    rows = q_ref.shape[1]
    n_cmp = kc_ref.shape[2]
    n_sel = ovl_ref.shape[0]
    kc = kc_ref[0, 0]
    vc = vc_ref[0, 0]
    t = i * rows + lax.broadcasted_iota(jnp.int32, (rows, 1), 0)
    cend = CMP_STRIDE * lax.broadcasted_iota(jnp.int32, (1, n_cmp), 1) + (CMP_BLOCK - 1)
    cvalid = cend <= t
    any_valid = jnp.where(t >= CMP_BLOCK - 1, 1.0, 0.0)
    pc_sum = jnp.zeros((rows, n_cmp), F32)
    for h in range(HEADS_PER_GROUP):
        qh = q_ref[0, :, h * HEAD_DIM:(h + 1) * HEAD_DIM]
        s = jnp.where(cvalid, _dot_nt(qh, kc), NEG_INF)
        e = jnp.exp(s - jnp.max(s, axis=-1, keepdims=True))
        pc = e / jnp.sum(e, axis=-1, keepdims=True) * any_valid
        oc_ref[0, :, h * HEAD_DIM:(h + 1) * HEAD_DIM] = _dot(pc.astype(BF16), vc)
        pc_sum = pc_sum + pc
    imp = _dot_nt(ovl_ref[...], pc_sum.astype(BF16))
    j = lax.broadcasted_iota(jnp.int32, (n_sel, 1), 0)
    blk_t = (i * rows + lax.broadcasted_iota(jnp.int32, (1, rows), 1)) >> SEL_SHIFT
    bvalid = j <= blk_t
    forced = (j == 0) | (j == blk_t) | (j == blk_t - 1)
    score = jnp.where(bvalid, imp + jnp.where(forced, FORCE_BONUS, 0.0), -1.0)
    score_ref[...] = score
    rank = jnp.zeros((n_sel, rows), F32)
    for jp in range(n_sel):
        other = score_ref[jp:jp + 1, :]
        ge = jnp.where(other >= score, 1.0, 0.0)
        gt = jnp.where(other > score, 1.0, 0.0)
        rank = rank + jnp.where(j > jp, ge, gt)
    selected = (rank < float(min(SEL_TOP_N, n_sel))) & bvalid
    sel_ref[0, 0] = jnp.where(selected, 0.0, NEG_INF).T.astype(sel_ref.dtype)


def _cmp_attn(q, kc, vc, B, S):
    rows = CMP_Q_ROWS
    n_cmp = S // CMP_STRIDE
    n_sel = S // SEL_BLOCK
    cstart = CMP_STRIDE * jnp.arange(n_cmp)
    sstart = SEL_BLOCK * jnp.arange(n_sel)
    overlap = ((cstart[None, :] <= sstart[:, None] + SEL_BLOCK - 1)
               & (cstart[None, :] + CMP_BLOCK - 1 >= sstart[:, None])).astype(BF16)
    q3 = q.reshape(B, S, D_MODEL)
    kv_spec = pl.BlockSpec((1, 1, n_cmp, HEAD_DIM), lambda bi, gi, i: (bi, gi, 0, 0))
    return pl.pallas_call(
        _cmp_attn_kernel,
        grid=(B, N_KV_GROUPS, S // rows),
        in_specs=[
            pl.BlockSpec((1, rows, GROUP_Q_DIM), lambda bi, gi, i: (bi, i, gi)),
            kv_spec, kv_spec,
            pl.BlockSpec(overlap.shape, lambda bi, gi, i: (0, 0)),
        ],
        out_specs=[
            pl.BlockSpec((1, rows, GROUP_Q_DIM), lambda bi, gi, i: (bi, i, gi)),
            pl.BlockSpec((1, 1, rows, n_sel), lambda bi, gi, i: (bi, gi, i, 0)),
        ],
        out_shape=[
            jax.ShapeDtypeStruct((B, S, D_MODEL), F32),
            jax.ShapeDtypeStruct((B, N_KV_GROUPS, S, n_sel), BF16),
        ],
        scratch_shapes=[pltpu.VMEM((n_sel, rows), F32)],
        compiler_params=_params("parallel", "parallel", "arbitrary"),
    )(q3, kc, vc, overlap)


def _flash_step(q, k, v, bias, m_ref, l_ref, acc_ref):
    s = _dot_nt(q, k)
    if bias is not None:
        s = s + bias
    m_prev = m_ref[...]
    m_next = jnp.maximum(m_prev, jnp.max(s, axis=-1, keepdims=True))
    alpha = jnp.exp(m_prev - m_next)
    p = jnp.exp(s - m_next)
    l_ref[...] = alpha * l_ref[...] + jnp.sum(p, axis=-1, keepdims=True)
    acc_ref[...] = alpha * acc_ref[...] + _dot(p.astype(BF16), v)
    m_ref[...] = m_next


def _sel_win_attn_kernel(q_ref, sel_ref, ks_ref, vs_ref, kw_ref, vw_ref, blk_ref, oc_ref, gate_ref,
                         o_ref, kaug_ref, qaug_ref, m_ref, l_ref, acc_ref, m2_ref, l2_ref, acc2_ref):
    i = pl.program_id(2)
    tq = q_ref.shape[1]
    n_sel = sel_ref.shape[3]
    hg = HEADS_PER_GROUP

    @pl.when(i == 0)
    def _():
        kaug_ref[:, 0:HEAD_DIM] = ks_ref[0, 0]
        kaug_ref[:, HEAD_DIM:] = blk_ref[...]

    for h in range(hg):
        qaug_ref[h * tq:(h + 1) * tq, 0:HEAD_DIM] = q_ref[0, :, h * HEAD_DIM:(h + 1) * HEAD_DIM]
        qaug_ref[h * tq:(h + 1) * tq, HEAD_DIM:] = sel_ref[0, 0]

    r = lax.broadcasted_iota(jnp.int32, (tq, ATT_K_ROWS), 0)
    a = lax.broadcasted_iota(jnp.int32, (tq, ATT_K_ROWS), 1)
    causal = jnp.tile(jnp.where(a <= r, 0.0, NEG_INF), (hg, 1))
    tail = jnp.tile(jnp.where(a > r, 0.0, NEG_INF), (hg, 1))

    def reset(m, l, acc):
        m[...] = jnp.full(m.shape, NEG_INF, F32)
        l[...] = jnp.zeros(l.shape, F32)
        acc[...] = jnp.zeros(acc.shape, F32)

    def chunk(ref, c):
        return ref[pl.ds(pl.multiple_of(c * ATT_K_ROWS, ATT_K_ROWS), ATT_K_ROWS), :]

    def chunk4(ref, c):
        return ref[0, 0, pl.ds(pl.multiple_of(c * ATT_K_ROWS, ATT_K_ROWS), ATT_K_ROWS), :]

    reset(m_ref, l_ref, acc_ref)
    qa = qaug_ref[...]

    def sel_body(c, carry):
        _flash_step(qa, chunk(kaug_ref, c), chunk4(vs_ref, c), None, m_ref, l_ref, acc_ref)
        return carry

    lax.fori_loop(0, i, sel_body, 0)
    _flash_step(qa, chunk(kaug_ref, i), chunk4(vs_ref, i), causal, m_ref, l_ref, acc_ref)

    reset(m2_ref, l2_ref, acc2_ref)
    qw = qaug_ref[:, 0:HEAD_DIM]
    n_back = WINDOW // ATT_K_ROWS
    for back in range(n_back, 0, -1):
        @pl.when(i >= back)
        def _(back=back):
            _flash_step(qw, chunk4(kw_ref, i - back), chunk4(vw_ref, i - back),
                        tail if back == n_back else None, m2_ref, l2_ref, acc2_ref)
    _flash_step(qw, chunk4(kw_ref, i), chunk4(vw_ref, i), causal, m2_ref, l2_ref, acc2_ref)

    o_s = acc_ref[...] / l_ref[...]
    o_w = acc2_ref[...] / l2_ref[...]
    gate = gate_ref[0, 0]
    for h in range(hg):
        gc = gate[:, N_BRANCHES * h + 0:N_BRANCHES * h + 1]
        gs = gate[:, N_BRANCHES * h + 1:N_BRANCHES * h + 2]
        gw = gate[:, N_BRANCHES * h + 2:N_BRANCHES * h + 3]
        o = (gc * oc_ref[0, :, h * HEAD_DIM:(h + 1) * HEAD_DIM]
             + gs * o_s[h * tq:(h + 1) * tq] + gw * o_w[h * tq:(h + 1) * tq])
        o_ref[0, :, h * HEAD_DIM:(h + 1) * HEAD_DIM] = o.astype(o_ref.dtype)


def _sel_win_attn(q, sel, ks, vs, kw, vw, oc, gates, B, S):
    tq = ATT_Q_ROWS
    n_sel = S // SEL_BLOCK
    hg = HEADS_PER_GROUP
    q3 = q.reshape(B, S, D_MODEL)
    gate_g = gates.reshape(B, S, LANES)[:, :, :N_GATES].reshape(B, S, N_KV_GROUPS, N_BRANCHES * hg)
    gate_g = gate_g.transpose(0, 2, 1, 3)
    blk_onehot = (jnp.arange(S)[:, None] // SEL_BLOCK == jnp.arange(n_sel)[None, :]).astype(BF16)
    kv_spec = pl.BlockSpec((1, 1, S, HEAD_DIM), lambda bi, gi, i: (bi, gi, 0, 0))
    q_spec = pl.BlockSpec((1, tq, GROUP_Q_DIM), lambda bi, gi, i: (bi, i, gi))
    stat = lambda: pltpu.VMEM((hg * tq, 1), F32)
    return pl.pallas_call(
        _sel_win_attn_kernel,
        grid=(B, N_KV_GROUPS, S // tq),
        in_specs=[
            q_spec,
            pl.BlockSpec((1, 1, tq, n_sel), lambda bi, gi, i: (bi, gi, i, 0)),
            kv_spec, kv_spec, kv_spec, kv_spec,
            pl.BlockSpec(blk_onehot.shape, lambda bi, gi, i: (0, 0)),
            q_spec,
            pl.BlockSpec((1, 1, tq, N_BRANCHES * hg), lambda bi, gi, i: (bi, gi, i, 0)),
        ],
        out_specs=q_spec,
        out_shape=jax.ShapeDtypeStruct((B, S, D_MODEL), BF16),
        scratch_shapes=[
            pltpu.VMEM((S, HEAD_DIM + n_sel), BF16),
            pltpu.VMEM((hg * tq, HEAD_DIM + n_sel), BF16),
            stat(), stat(), pltpu.VMEM((hg * tq, HEAD_DIM), F32),
            stat(), stat(), pltpu.VMEM((hg * tq, HEAD_DIM), F32),
        ],
        compiler_params=_params("parallel", "parallel", "arbitrary"),
    )(q3, sel, ks, vs, kw, vw, blk_onehot, oc, gate_g)


def _out_proj_ln_kernel(o_ref, w_ref, x_ref, g_ref, b_ref, y_ref):
    h = _dot(o_ref[...], w_ref[...])
    y_ref[...] = _layer_norm(ALPHA * x_ref[...] + h, g_ref[...], b_ref[...])


def _out_proj_ln(o, w_out, x, g, b):
    T, D = x.shape
    rows = PROJ_ROWS
    return pl.pallas_call(
        _out_proj_ln_kernel,
        grid=(T // rows,),
        in_specs=[
            pl.BlockSpec((rows, D), lambda i: (i, 0)),
            pl.BlockSpec((D, D), lambda i: (0, 0)),
            pl.BlockSpec((rows, D), lambda i: (i, 0)),
            pl.BlockSpec((1, D), lambda i: (0, 0)),
            pl.BlockSpec((1, D), lambda i: (0, 0)),
        ],
        out_specs=pl.BlockSpec((rows, D), lambda i: (i, 0)),
        out_shape=jax.ShapeDtypeStruct((T, D), F32),
        compiler_params=_params("parallel"),
    )(o, w_out.astype(BF16), x, g.reshape(1, D), b.reshape(1, D))


def _moe_ln_kernel(x_ref, r_ref, wg_ref, wu_ref, wd_ref, g_ref, b_ref, o_ref, xb_ref, acc_ref, comb_ref):
    e = pl.program_id(1)
    f = pl.program_id(2)

    @pl.when((e == 0) & (f == 0))
    def _():
        x = x_ref[...]
        xb = x.astype(BF16)
        xb_ref[...] = xb
        acc_ref[...] = jnp.zeros_like(acc_ref)
        x_lo = (x - xb.astype(F32)).astype(BF16)
        r_hi = r_ref[0]
        r_lo = r_ref[1]
        logits = _dot(xb, r_hi) + (_dot(x_lo, r_hi) + _dot(xb, r_lo))
        lane = lax.broadcasted_iota(jnp.int32, logits.shape, 1)
        logits = jnp.where(lane < N_EXPERTS, logits, -jnp.inf)
        v1 = jnp.max(logits, axis=-1, keepdims=True)
        i1 = jnp.min(jnp.where(logits == v1, lane, LANES), axis=-1, keepdims=True)
        rest = jnp.where(lane == i1, -jnp.inf, logits)
        v2 = jnp.max(rest, axis=-1, keepdims=True)
        i2 = jnp.min(jnp.where(rest == v2, lane, LANES), axis=-1, keepdims=True)
        e2 = jnp.exp(v2 - v1)
        w1 = 1.0 / (1.0 + e2)
        w2 = e2 / (1.0 + e2)
        comb_ref[...] = jnp.where(lane == i1, w1, 0.0) + jnp.where(lane == i2, w2, 0.0)

    lane = lax.broadcasted_iota(jnp.int32, comb_ref.shape, 1)
    c_e = jnp.sum(jnp.where(lane == e, comb_ref[...], 0.0), axis=-1, keepdims=True)
    xb = xb_ref[...]
    a = _silu(_dot(xb, wg_ref[0])) * _dot(xb, wu_ref[0]) * c_e
    acc_ref[...] += _dot(a.astype(BF16), wd_ref[0])

    @pl.when((e == pl.num_programs(1) - 1) & (f == pl.num_programs(2) - 1))
    def _():
        o_ref[...] = _layer_norm(ALPHA * x_ref[...] + acc_ref[...], g_ref[...], b_ref[...])


def _moe_ln(x, router, w_gu, w_down, g, b):
    T, D = x.shape
    n_exp, d_ff = w_down.shape[0], w_down.shape[1]
    fc = _ff_chunk(d_ff, 1792)
    nf = d_ff // fc
    rows = FFN_ROWS
    r = jnp.pad(router, ((0, 0), (0, LANES - n_exp)))
    r_hi = r.astype(BF16)
    r_lo = (r - r_hi.astype(F32)).astype(BF16)
    r2 = jnp.stack([r_hi, r_lo])
    return pl.pallas_call(
        _moe_ln_kernel,
        grid=(T // rows, n_exp, nf),
        in_specs=[
            pl.BlockSpec((rows, D), lambda i, e, f: (i, 0)),
            pl.BlockSpec(r2.shape, lambda i, e, f: (0, 0, 0)),
            pl.BlockSpec((1, D, fc), lambda i, e, f: (e, 0, f)),
            pl.BlockSpec((1, D, fc), lambda i, e, f: (e, 0, nf + f)),
            pl.BlockSpec((1, fc, D), lambda i, e, f: (e, f, 0)),
            pl.BlockSpec((1, D), lambda i, e, f: (0, 0)),
            pl.BlockSpec((1, D), lambda i, e, f: (0, 0)),
        ],
        out_specs=pl.BlockSpec((rows, D), lambda i, e, f: (i, 0)),
        out_shape=jax.ShapeDtypeStruct((T, D), F32),
        scratch_shapes=[pltpu.VMEM((rows, D), BF16), pltpu.VMEM((rows, D), F32),
                        pltpu.VMEM((rows, LANES), F32)],
        compiler_params=_params("parallel", "arbitrary", "arbitrary"),
    )(x, r2, w_gu, w_gu, w_down, g.reshape(1, D), b.reshape(1, D))


def kernel(x, ln_g, ln_b, pool_w, pool_scale, nsa_w_in, nsa_pe_k, nsa_w1_k, nsa_w2_k, nsa_pe_v, nsa_w1_v,
           nsa_w2_v, nsa_w_out, ffn_w_gu, ffn_w_down, moe_router, moe_w_gu, moe_w_down):
    B, S, D = x.shape
    T = B * S
    xa = _pool_ln(x, pool_w[0], pool_scale[0], ln_g[0, 0], ln_b[0, 0]).reshape(T, D)
    x1 = _ffn_ln(xa, ffn_w_gu[0].astype(BF16), ffn_w_down[0].astype(BF16), ln_g[0, 1], ln_b[0, 1])
    q, kcv, ks, vs, kw, vw, gates = _in_proj(x1, nsa_w_in[0], B, S)
    kc, vc = _compress(kcv, nsa_w1_k[0], nsa_w2_k[0], nsa_pe_k[0], nsa_w1_v[0], nsa_w2_v[0], nsa_pe_v[0], B, S)
    oc, sel = _cmp_attn(q, kc, vc, B, S)
    o = _sel_win_attn(q, sel, ks, vs, kw, vw, oc, gates, B, S).reshape(T, D)
    x2 = _out_proj_ln(o, nsa_w_out[0], x1, ln_g[1, 0], ln_b[1, 0])
    y = _moe_ln(x2, moe_router[0], moe_w_gu[0].astype(BF16), moe_w_down[0].astype(BF16), ln_g[1, 1], ln_b[1, 1])
    return y.reshape(B, S, D)
```

```python
import functools

import jax
import jax.numpy as jnp
from jax import lax
from jax.experimental import pallas as pl
from jax.experimental.pallas import tpu as pltpu

D_MODEL = 1024
DEPTH = 2
POOL_WINDOWS = (2, 4, 8, 16)
POOL_GROUP_DIM = D_MODEL // len(POOL_WINDOWS)
POOL_HALO = 16
N_HEADS = 16
N_KV_GROUPS = 4
HEADS_PER_GROUP = N_HEADS // N_KV_GROUPS
HEAD_DIM = D_MODEL // N_HEADS
HALF_DIM = HEAD_DIM // 2
GROUP_Q_DIM = HEADS_PER_GROUP * HEAD_DIM
KV_DIM = N_KV_GROUPS * HEAD_DIM
N_BRANCHES = 3
N_GATES = N_BRANCHES * N_HEADS
CMP_STRIDE = 16
CMP_BLOCK = 2 * CMP_STRIDE
CMP_HIDDEN = 2 * HEAD_DIM
SEL_BLOCK = 64
SEL_SHIFT = SEL_BLOCK.bit_length() - 1
SEL_TOP_N = 16
WINDOW = 512
FORCE_BONUS = 1.0e3
NEG_INF = -1.0e30
ROPE_THETA = 10000.0
ATTN_SCALE = HEAD_DIM ** -0.5
LOG2E = 1.4426950408889634
Q_SCALE = ATTN_SCALE * LOG2E
AUG_DIM = 2 * HEAD_DIM
SEL_SLOTS = AUG_DIM - HEAD_DIM
N_EXPERTS = 8
LN_EPS = 1e-5
ALPHA = (2 * DEPTH) ** 0.25

LANES = 128
VMEM_LIMIT_BYTES = 56 * 1024 * 1024

POOL_ROWS = 512
FFN_ROWS = 512
PROJ_ROWS = 512
PROJ_COLS = 256
CMP_Q_ROWS = 512
ATT_Q_ROWS = 512
ATT_K_ROWS = 512
ATT_HALF = 256

F32 = jnp.float32
BF16 = jnp.bfloat16


def _dot(a, b):
    return jnp.dot(a, b, preferred_element_type=F32)


def _dot_nt(a, b):
    return lax.dot_general(a, b, (((1,), (1,)), ((), ())), preferred_element_type=F32)


def _layer_norm(z, g, b):
    mu = jnp.mean(z, axis=-1, keepdims=True)
    zc = z - mu
    var = jnp.mean(zc * zc, axis=-1, keepdims=True)
    return zc * lax.rsqrt(var + LN_EPS) * g + b


def _silu(x):
    return x / (1.0 + jnp.exp(-x))


def _params(*semantics):
    return pltpu.CompilerParams(dimension_semantics=semantics, vmem_limit_bytes=VMEM_LIMIT_BYTES)


def _pool_ln_kernel(x_ref, halo_ref, w_ref, scale_ref, g_ref, b_ref, o_ref, ext_ref):
    i = pl.program_id(1)
    rows = x_ref.shape[1]
    x = x_ref[0]
    ext_ref[0:POOL_HALO, :] = jnp.where(i > 0, halo_ref[0], 0.0)
    ext_ref[POOL_HALO:, :] = x
    pos = i * rows + lax.broadcasted_iota(jnp.int32, (rows, 1), 0)
    ys = []
    for gi, w in enumerate(POOL_WINDOWS):
        c0 = gi * POOL_GROUP_DIM
        xg = x[:, c0:c0 + POOL_GROUP_DIM]
        acc = xg
        for k in range(1, w):
            acc = acc + ext_ref[POOL_HALO - k:POOL_HALO - k + rows, c0:c0 + POOL_GROUP_DIM]
        cnt = jnp.minimum(pos + 1, w).astype(F32)
        diff = acc / cnt - xg
        ys.append(_dot(diff.astype(BF16), w_ref[gi]))
    h = jnp.concatenate(ys, axis=1) * scale_ref[...]
    o_ref[0] = _layer_norm(ALPHA * x + h, g_ref[...], b_ref[...])


def _pool_ln(x, w, scale, g, b):
    B, S, D = x.shape
    rows = POOL_ROWS
    halo_blocks = rows // POOL_HALO
    row2 = lambda v: v.reshape(1, D)
    return pl.pallas_call(
        _pool_ln_kernel,
        grid=(B, S // rows),
        in_specs=[
            pl.BlockSpec((1, rows, D), lambda bi, i: (bi, i, 0)),
            pl.BlockSpec((1, POOL_HALO, D), lambda bi, i: (bi, jnp.maximum(i * halo_blocks - 1, 0), 0)),
            pl.BlockSpec(w.shape, lambda bi, i: (0, 0, 0)),
            pl.BlockSpec((1, D), lambda bi, i: (0, 0)),
            pl.BlockSpec((1, D), lambda bi, i: (0, 0)),
            pl.BlockSpec((1, D), lambda bi, i: (0, 0)),
        ],
        out_specs=pl.BlockSpec((1, rows, D), lambda bi, i: (bi, i, 0)),
        out_shape=jax.ShapeDtypeStruct((B, S, D), F32),
        scratch_shapes=[pltpu.VMEM((rows + POOL_HALO, D), F32)],
        compiler_params=_params("parallel", "arbitrary"),
    )(x, x, w.astype(BF16), row2(scale), row2(g), row2(b))


def _ffn_ln_kernel(x_ref, wg_ref, wu_ref, wd_ref, g_ref, b_ref, o_ref, xb_ref, acc_ref):
    f = pl.program_id(1)

    @pl.when(f == 0)
    def _():
        xb_ref[...] = x_ref[...].astype(BF16)
        acc_ref[...] = jnp.zeros_like(acc_ref)

    xb = xb_ref[...]
    a = _silu(_dot(xb, wg_ref[...])) * _dot(xb, wu_ref[...])
    acc_ref[...] += _dot(a.astype(BF16), wd_ref[...])

    @pl.when(f == pl.num_programs(1) - 1)
    def _():
        o_ref[...] = _layer_norm(ALPHA * x_ref[...] + acc_ref[...], g_ref[...], b_ref[...])


def _ff_chunk(d_ff, target):
    best = LANES
    for c in range(LANES, target + 1, LANES):
        if d_ff % c == 0:
            best = c
    return best


def _ffn_ln(x, w_gu, w_down, g, b):
    T, D = x.shape
    d_ff = w_down.shape[0]
    fc = _ff_chunk(d_ff, 1536)
    nf = d_ff // fc
    rows = FFN_ROWS
    return pl.pallas_call(
        _ffn_ln_kernel,
        grid=(T // rows, nf),
        in_specs=[
            pl.BlockSpec((rows, D), lambda i, f: (i, 0)),
            pl.BlockSpec((D, fc), lambda i, f: (0, f)),
            pl.BlockSpec((D, fc), lambda i, f: (0, nf + f)),
            pl.BlockSpec((fc, D), lambda i, f: (f, 0)),
            pl.BlockSpec((1, D), lambda i, f: (0, 0)),
            pl.BlockSpec((1, D), lambda i, f: (0, 0)),
        ],
        out_specs=pl.BlockSpec((rows, D), lambda i, f: (i, 0)),
        out_shape=jax.ShapeDtypeStruct((T, D), F32),
        scratch_shapes=[pltpu.VMEM((rows, D), BF16), pltpu.VMEM((rows, D), F32)],
        compiler_params=_params("parallel", "arbitrary"),
    )(x, w_gu, w_gu, w_down, g.reshape(1, D), b.reshape(1, D))


def _rope_tables(pos, reps):
    freqs = jnp.power(ROPE_THETA, -jnp.arange(HALF_DIM, dtype=F32) / HALF_DIM)
    ang = pos.astype(F32)[:, None] * freqs[None, :]
    cos, sin = jnp.cos(ang), jnp.sin(ang)
    return (jnp.tile(jnp.concatenate([cos, cos], axis=1), (1, reps)),
            jnp.tile(jnp.concatenate([-sin, sin], axis=1), (1, reps)))


def _in_proj_kernel(x_ref, w_ref, cos_ref, sin_ref,
                    q_ref, kcv_ref, ks_ref, vs_ref, kw_ref, vw_ref, gate_ref, *, steps_per_seq):
    rows = x_ref.shape[0]
    xb = x_ref[...].astype(BF16)
    cos = cos_ref[...]
    sin = sin_ref[...]
    lane = lax.broadcasted_iota(jnp.int32, cos.shape, 1)
    first_half = (lane & (HEAD_DIM - 1)) < HALF_DIM
    seq_step = pl.program_id(0) % steps_per_seq
    pos = seq_step * rows + lax.broadcasted_iota(jnp.int32, (rows, HEAD_DIM), 0)
    col = lax.broadcasted_iota(jnp.int32, (rows, HEAD_DIM), 1)
    blk_onehot = jnp.where((pos >> SEL_SHIFT) == col, 1.0, 0.0)
    ones_col = jnp.where(col == 0, 1.0, 0.0)
    zeros = jnp.zeros((rows, HEAD_DIM), F32)

    def col_tile(j):
        return _dot(xb, w_ref[:, j * PROJ_COLS:(j + 1) * PROJ_COLS])

    def rope(y):
        rot = jnp.where(first_half,
                        pltpu.roll(y, PROJ_COLS - HALF_DIM, 1),
                        pltpu.roll(y, HALF_DIM, 1))
        return y * cos + rot * sin

    def store_groups(ref, y, extra):
        for gi in range(N_KV_GROUPS):
            ref[0, gi, :, 0:HEAD_DIM] = y[:, gi * HEAD_DIM:(gi + 1) * HEAD_DIM].astype(ref.dtype)
            ref[0, gi, :, HEAD_DIM:] = extra.astype(ref.dtype)

    n_q = D_MODEL // PROJ_COLS
    for j in range(n_q):
        q_ref[:, j * PROJ_COLS:(j + 1) * PROJ_COLS] = (rope(col_tile(j)) * Q_SCALE).astype(q_ref.dtype)
    kcv_ref[:, 0:KV_DIM] = col_tile(n_q)
    kcv_ref[:, KV_DIM:2 * KV_DIM] = col_tile(n_q + 1)
    store_groups(ks_ref, rope(col_tile(n_q + 2)), blk_onehot)
    store_groups(vs_ref, col_tile(n_q + 3), ones_col)
    store_groups(kw_ref, rope(col_tile(n_q + 4)), zeros)
    store_groups(vw_ref, col_tile(n_q + 5), ones_col)
    logits = col_tile(n_q + 6)[:, 0:LANES]
    gate_ref[...] = 1.0 / (1.0 + jnp.exp(-logits))


def _in_proj(x, w_in, B, S):
    T, D = x.shape
    rows = PROJ_ROWS
    steps_per_seq = S // rows
    n_tiles = D_MODEL // PROJ_COLS + 2 * N_BRANCHES + 1
    w = jnp.pad(w_in, ((0, 0), (0, n_tiles * PROJ_COLS - w_in.shape[1]))).astype(BF16)
    cos, sin = _rope_tables(jnp.arange(S), PROJ_COLS // HEAD_DIM)
    kv_shape = jax.ShapeDtypeStruct((B, N_KV_GROUPS, S, AUG_DIM), BF16)
    kv_spec = pl.BlockSpec((1, N_KV_GROUPS, rows, AUG_DIM),
                           lambda i: (i // steps_per_seq, 0, i % steps_per_seq, 0))
    return pl.pallas_call(
        functools.partial(_in_proj_kernel, steps_per_seq=steps_per_seq),
        grid=(T // rows,),
        in_specs=[
            pl.BlockSpec((rows, D), lambda i: (i, 0)),
            pl.BlockSpec(w.shape, lambda i: (0, 0)),
            pl.BlockSpec((rows, PROJ_COLS), lambda i: (i % steps_per_seq, 0)),
            pl.BlockSpec((rows, PROJ_COLS), lambda i: (i % steps_per_seq, 0)),
        ],
        out_specs=[
            pl.BlockSpec((rows, D_MODEL), lambda i: (i, 0)),
            pl.BlockSpec((rows, 2 * KV_DIM), lambda i: (i, 0)),
            kv_spec, kv_spec, kv_spec, kv_spec,
            pl.BlockSpec((rows, LANES), lambda i: (i, 0)),
        ],
        out_shape=[
            jax.ShapeDtypeStruct((T, D_MODEL), BF16),
            jax.ShapeDtypeStruct((T, 2 * KV_DIM), F32),
            kv_shape, kv_shape, kv_shape, kv_shape,
            jax.ShapeDtypeStruct((T, LANES), F32),
        ],
        compiler_params=_params("parallel"),
    )(x, w, cos, sin)


def _compress_kernel(ch_ref, w1_ref, pe_ref, w2_ref, cos_ref, sin_ref, kc_ref, vc_ref):
    half = CMP_STRIDE * HEAD_DIM
    for which, o_ref in enumerate((kc_ref, vc_ref)):
        w1 = w1_ref[which]
        r = _dot(ch_ref[which, 0, 0], w1)
        pe = pe_ref[which]
        bias = (_dot(pe[:, 0:half], w1[:, 0:CMP_HIDDEN])
                + _dot(pe[:, half:2 * half], w1[:, CMP_HIDDEN:2 * CMP_HIDDEN]))[0:1, :]
        nxt = pltpu.roll(r[:, CMP_HIDDEN:2 * CMP_HIDDEN], r.shape[0] - 1, 0)
        pre = r[:, 0:CMP_HIDDEN] + nxt + bias
        o = _dot(_silu(pre).astype(BF16), w2_ref[which])
        if which == 0:
            rot = jnp.concatenate([o[:, HALF_DIM:], o[:, :HALF_DIM]], axis=1)
            o = o * cos_ref[...] + rot * sin_ref[...]
        o_ref[0, 0] = o.astype(o_ref.dtype)


def _compress(kcv, w1_k, w2_k, pe_k, w1_v, w2_v, pe_v, B, S):
    n_chunks = S // CMP_STRIDE
    half = CMP_STRIDE * HEAD_DIM
    ch = kcv.reshape(B, n_chunks, CMP_STRIDE, 2, N_KV_GROUPS, HEAD_DIM)
    ch = ch.transpose(3, 0, 4, 1, 2, 5).reshape(2, B, N_KV_GROUPS, n_chunks, half).astype(BF16)
    cat = lambda w1: jnp.concatenate([w1[:half], w1[half:]], axis=1)
    w1 = jnp.stack([cat(w1_k), cat(w1_v)]).astype(BF16)
    w2 = jnp.stack([w2_k, w2_v]).astype(BF16)
    pe = jnp.stack([pe_k.reshape(1, -1), pe_v.reshape(1, -1)])
    pe = jnp.broadcast_to(pe, (2, 8, 2 * half)).astype(BF16)
    cos, sin = _rope_tables(CMP_STRIDE * jnp.arange(n_chunks) + CMP_BLOCK - 1, 1)
    out_shape = jax.ShapeDtypeStruct((B, N_KV_GROUPS, n_chunks, HEAD_DIM), BF16)
    out_spec = pl.BlockSpec((1, 1, n_chunks, HEAD_DIM), lambda bi, gi: (bi, gi, 0, 0))
    const = lambda a: pl.BlockSpec(a.shape, lambda bi, gi: (0,) * a.ndim)
    return pl.pallas_call(
        _compress_kernel,
        grid=(B, N_KV_GROUPS),
        in_specs=[
            pl.BlockSpec((2, 1, 1, n_chunks, half), lambda bi, gi: (0, bi, gi, 0, 0)),
            const(w1), const(pe), const(w2), const(cos), const(sin),
        ],
        out_specs=[out_spec, out_spec],
        out_shape=[out_shape, out_shape],
        compiler_params=_params("parallel", "parallel"),
    )(ch, w1, pe, w2, cos, sin)


def _cmp_attn_kernel(q_ref, kc_ref, vc_ref, ovl_ref, oc_ref, sel_ref, score_ref):
    i = pl.program_id(2)
    rows = q_ref.shape[1]
    n_cmp = kc_ref.shape[2]
    n_sel = ovl_ref.shape[0]
    kc = kc_ref[0, 0]
    vc = vc_ref[0, 0]
    t = i * rows + lax.broadcasted_iota(jnp.int32, (rows, 1), 0)
    cend = CMP_STRIDE * lax.broadcasted_iota(jnp.int32, (1, n_cmp), 1) + (CMP_BLOCK - 1)
    cvalid = cend <= t
    any_valid = jnp.where(t >= CMP_BLOCK - 1, 1.0, 0.0)
    pc_sum = jnp.zeros((rows, n_cmp), F32)
    for h in range(HEADS_PER_GROUP):
        qh = q_ref[0, :, h * HEAD_DIM:(h + 1) * HEAD_DIM]
        s = jnp.where(cvalid, _dot_nt(qh, kc), NEG_INF)
        e = jnp.exp2(s - jnp.max(s, axis=-1, keepdims=True))
        pc = e / jnp.sum(e, axis=-1, keepdims=True) * any_valid
        oc_ref[0, :, h * HEAD_DIM:(h + 1) * HEAD_DIM] = _dot(pc.astype(BF16), vc)
        pc_sum = pc_sum + pc
    imp = _dot_nt(ovl_ref[...], pc_sum.astype(BF16))
    j = lax.broadcasted_iota(jnp.int32, (n_sel, 1), 0)
    blk_t = (i * rows + lax.broadcasted_iota(jnp.int32, (1, rows), 1)) >> SEL_SHIFT
    bvalid = j <= blk_t
    forced = (j == 0) | (j == blk_t) | (j == blk_t - 1)
    score = jnp.where(bvalid, imp + jnp.where(forced, FORCE_BONUS, 0.0), -1.0)
    score_ref[...] = score
    rank = jnp.zeros((n_sel, rows), F32)
    for jp in range(n_sel):
        other = score_ref[jp:jp + 1, :]
        ge = jnp.where(other >= score, 1.0, 0.0)
        gt = jnp.where(other > score, 1.0, 0.0)
        rank = rank + jnp.where(j > jp, ge, gt)
    selected = (rank < float(SEL_TOP_N)) & bvalid
    sel_ref[0, 0] = jnp.where(selected, 0.0, NEG_INF).T.astype(sel_ref.dtype)


def _cmp_attn(q, kc, vc, B, S):
    rows = CMP_Q_ROWS
    n_cmp = S // CMP_STRIDE
    n_sel = SEL_SLOTS
    assert S // SEL_BLOCK <= SEL_SLOTS
    cstart = CMP_STRIDE * jnp.arange(n_cmp)
    sstart = SEL_BLOCK * jnp.arange(n_sel)
    overlap = ((cstart[None, :] <= sstart[:, None] + SEL_BLOCK - 1)
               & (cstart[None, :] + CMP_BLOCK - 1 >= sstart[:, None])).astype(BF16)
    q3 = q.reshape(B, S, D_MODEL)
    kv_spec = pl.BlockSpec((1, 1, n_cmp, HEAD_DIM), lambda bi, gi, i: (bi, gi, 0, 0))
    return pl.pallas_call(
        _cmp_attn_kernel,
        grid=(B, N_KV_GROUPS, S // rows),
        in_specs=[
            pl.BlockSpec((1, rows, GROUP_Q_DIM), lambda bi, gi, i: (bi, i, gi)),
            kv_spec, kv_spec,
            pl.BlockSpec(overlap.shape, lambda bi, gi, i: (0, 0)),
        ],
        out_specs=[
            pl.BlockSpec((1, rows, GROUP_Q_DIM), lambda bi, gi, i: (bi, i, gi)),
            pl.BlockSpec((1, 1, rows, n_sel), lambda bi, gi, i: (bi, gi, i, 0)),
        ],
        out_shape=[
            jax.ShapeDtypeStruct((B, S, D_MODEL), F32),
            jax.ShapeDtypeStruct((B, N_KV_GROUPS, S, n_sel), BF16),
        ],
        scratch_shapes=[pltpu.VMEM((n_sel, rows), F32)],
        compiler_params=_params("parallel", "parallel", "arbitrary"),
    )(q3, kc, vc, overlap)


def _sel_win_attn_kernel(q_ref, sel_ref, ks_ref, vs_ref, kw_ref, vw_ref, diag_ref, band_ref, oc_ref, gate_ref,
                         o_ref, qaug_ref, m_ref, acc_ref, ow_ref):
    i = pl.program_id(2)
    tq = q_ref.shape[1]
    hg = HEADS_PER_GROUP
    half = ATT_HALF
    n_half = tq // half
    half_rows = hg * half

    for hq in range(n_half):
        for h in range(hg):
            r0 = (hq * hg + h) * half
            qaug_ref[r0:r0 + half, 0:HEAD_DIM] = q_ref[0, hq * half:(hq + 1) * half, h * HEAD_DIM:(h + 1) * HEAD_DIM]
            qaug_ref[r0:r0 + half, HEAD_DIM:] = sel_ref[0, 0, hq * half:(hq + 1) * half, :]
    qa = qaug_ref[...]

    def keys(ref, start, size):
        return ref[0, 0, pl.ds(pl.multiple_of(start, ATT_HALF), size), :]

    m_ref[...] = jnp.full(m_ref.shape, NEG_INF, F32)
    acc_ref[...] = jnp.zeros(acc_ref.shape, F32)

    def sel_step(c, bias):
        s = _dot_nt(qa, keys(ks_ref, c * ATT_K_ROWS, ATT_K_ROWS))
        if bias is not None:
            s = s + bias
        m_prev = m_ref[...]
        m_next = jnp.maximum(m_prev, jnp.max(s, axis=-1, keepdims=True))
        p = jnp.exp2(s - pltpu.repeat(m_next, ATT_K_ROWS // LANES, 1))
        alpha = jnp.exp2(m_prev - m_next)
        acc_ref[...] = alpha * acc_ref[...] + _dot(p.astype(BF16), keys(vs_ref, c * ATT_K_ROWS, ATT_K_ROWS))
        m_ref[...] = m_next

    def sel_body(c, carry):
        sel_step(c, None)
        return carry

    lax.fori_loop(0, i, sel_body, 0)
    sel_step(i, diag_ref[...])

    def softmax_pv(s, v):
        p = jnp.exp2(s - jnp.max(s, axis=-1, keepdims=True))
        return _dot(p.astype(BF16), v)

    @pl.when(i == 0)
    def _():
        s = _dot_nt(qa, keys(kw_ref, 0, tq)) + diag_ref[...]
        ow_ref[...] = softmax_pv(s, keys(vw_ref, 0, tq))

    @pl.when(i > 0)
    def _():
        for hq in range(n_half):
            start = i * tq + hq * half - WINDOW
            qh = qaug_ref[hq * half_rows:(hq + 1) * half_rows, :]
            s = _dot_nt(qh, keys(kw_ref, start, WINDOW + half)) + band_ref[...]
            ow_ref[hq * half_rows:(hq + 1) * half_rows, :] = softmax_pv(s, keys(vw_ref, start, WINDOW + half))

    acc_s = acc_ref[...]
    acc_w = ow_ref[...]
    o_s = acc_s[:, 0:HEAD_DIM] / acc_s[:, HEAD_DIM:HEAD_DIM + 1]
    o_w = acc_w[:, 0:HEAD_DIM] / acc_w[:, HEAD_DIM:HEAD_DIM + 1]
    gate = gate_ref[0, 0]
    for hq in range(n_half):
        tok = slice(hq * half, (hq + 1) * half)
        for h in range(hg):
            r0 = (hq * hg + h) * half
            gc = gate[tok, N_BRANCHES * h + 0:N_BRANCHES * h + 1]
            gs = gate[tok, N_BRANCHES * h + 1:N_BRANCHES * h + 2]
            gw = gate[tok, N_BRANCHES * h + 2:N_BRANCHES * h + 3]
            o = (gc * oc_ref[0, tok, h * HEAD_DIM:(h + 1) * HEAD_DIM]
                 + gs * o_s[r0:r0 + half] + gw * o_w[r0:r0 + half])
            o_ref[0, tok, h * HEAD_DIM:(h + 1) * HEAD_DIM] = o.astype(o_ref.dtype)


def _sel_win_attn(q, sel, ks, vs, kw, vw, oc, gates, B, S):
    tq = ATT_Q_ROWS
    half = ATT_HALF
    n_sel = SEL_SLOTS
    hg = HEADS_PER_GROUP
    rows = hg * tq
    q3 = q.reshape(B, S, D_MODEL)
    gate_g = gates.reshape(B, S, LANES)[:, :, :N_GATES].reshape(B, S, N_KV_GROUPS, N_BRANCHES * hg)
    gate_g = gate_g.transpose(0, 2, 1, 3)
    t_rel = (jnp.arange(tq // half)[:, None, None] * half + jnp.arange(half)[None, None, :])
    t_rel = jnp.broadcast_to(t_rel, (tq // half, hg, half)).reshape(rows, 1)
    diag = jnp.where(jnp.arange(ATT_K_ROWS)[None, :] <= t_rel, 0.0, NEG_INF).astype(F32)
    tt = jnp.broadcast_to(jnp.arange(half)[None, :], (hg, half)).reshape(hg * half, 1)
    a = jnp.arange(WINDOW + half)[None, :]
    band = jnp.where((a > tt) & (a <= tt + WINDOW), 0.0, NEG_INF).astype(F32)
    kv_spec = pl.BlockSpec((1, 1, S, AUG_DIM), lambda bi, gi, i: (bi, gi, 0, 0))
    q_spec = pl.BlockSpec((1, tq, GROUP_Q_DIM), lambda bi, gi, i: (bi, i, gi))
    const = lambda arr: pl.BlockSpec(arr.shape, lambda bi, gi, i: (0, 0))
    return pl.pallas_call(
        _sel_win_attn_kernel,
        grid=(B, N_KV_GROUPS, S // tq),
        in_specs=[
            q_spec,
            pl.BlockSpec((1, 1, tq, n_sel), lambda bi, gi, i: (bi, gi, i, 0)),
            kv_spec, kv_spec, kv_spec, kv_spec,
            const(diag), const(band),
            q_spec,
            pl.BlockSpec((1, 1, tq, N_BRANCHES * hg), lambda bi, gi, i: (bi, gi, i, 0)),
        ],
        out_specs=q_spec,
        out_shape=jax.ShapeDtypeStruct((B, S, D_MODEL), BF16),
        scratch_shapes=[
            pltpu.VMEM((rows, AUG_DIM), BF16),
            pltpu.VMEM((rows, LANES), F32),
            pltpu.VMEM((rows, AUG_DIM), F32),
            pltpu.VMEM((rows, AUG_DIM), F32),
        ],
        compiler_params=_params("parallel", "parallel", "arbitrary"),
    )(q3, sel, ks, vs, kw, vw, diag, band, oc, gate_g)


def _out_proj_ln_kernel(o_ref, w_ref, x_ref, g_ref, b_ref, y_ref):
    h = _dot(o_ref[...], w_ref[...])
    y_ref[...] = _layer_norm(ALPHA * x_ref[...] + h, g_ref[...], b_ref[...])


def _out_proj_ln(o, w_out, x, g, b):
    T, D = x.shape
    rows = PROJ_ROWS
    return pl.pallas_call(
        _out_proj_ln_kernel,
        grid=(T // rows,),
        in_specs=[
            pl.BlockSpec((rows, D), lambda i: (i, 0)),
            pl.BlockSpec((D, D), lambda i: (0, 0)),
            pl.BlockSpec((rows, D), lambda i: (i, 0)),
            pl.BlockSpec((1, D), lambda i: (0, 0)),
            pl.BlockSpec((1, D), lambda i: (0, 0)),
        ],
        out_specs=pl.BlockSpec((rows, D), lambda i: (i, 0)),
        out_shape=jax.ShapeDtypeStruct((T, D), F32),
        compiler_params=_params("parallel"),
    )(o, w_out.astype(BF16), x, g.reshape(1, D), b.reshape(1, D))


def _moe_ln_kernel(x_ref, r_ref, wg_ref, wu_ref, wd_ref, g_ref, b_ref, o_ref, xb_ref, acc_ref, comb_ref):
    e = pl.program_id(1)
    f = pl.program_id(2)

    @pl.when((e == 0) & (f == 0))
    def _():
        x = x_ref[...]
        xb = x.astype(BF16)
        xb_ref[...] = xb
        acc_ref[...] = jnp.zeros_like(acc_ref)
        x_lo = (x - xb.astype(F32)).astype(BF16)
        r_hi = r_ref[0]
        r_lo = r_ref[1]
        logits = _dot(xb, r_hi) + (_dot(x_lo, r_hi) + _dot(xb, r_lo))
        lane = lax.broadcasted_iota(jnp.int32, logits.shape, 1)
        logits = jnp.where(lane < N_EXPERTS, logits, -jnp.inf)
        v1 = jnp.max(logits, axis=-1, keepdims=True)
        i1 = jnp.min(jnp.where(logits == v1, lane, LANES), axis=-1, keepdims=True)
        rest = jnp.where(lane == i1, -jnp.inf, logits)
        v2 = jnp.max(rest, axis=-1, keepdims=True)
        i2 = jnp.min(jnp.where(rest == v2, lane, LANES), axis=-1, keepdims=True)
        e2 = jnp.exp(v2 - v1)
        w1 = 1.0 / (1.0 + e2)
        w2 = e2 / (1.0 + e2)
        comb_ref[...] = jnp.where(lane == i1, w1, 0.0) + jnp.where(lane == i2, w2, 0.0)

    lane = lax.broadcasted_iota(jnp.int32, comb_ref.shape, 1)
    c_e = jnp.sum(jnp.where(lane == e, comb_ref[...], 0.0), axis=-1, keepdims=True)
    xb = xb_ref[...]
    a = _silu(_dot(xb, wg_ref[0])) * _dot(xb, wu_ref[0]) * c_e
    acc_ref[...] += _dot(a.astype(BF16), wd_ref[0])

    @pl.when((e == pl.num_programs(1) - 1) & (f == pl.num_programs(2) - 1))
    def _():
        o_ref[...] = _layer_norm(ALPHA * x_ref[...] + acc_ref[...], g_ref[...], b_ref[...])


def _moe_ln(x, router, w_gu, w_down, g, b):
    T, D = x.shape
    n_exp, d_ff = w_down.shape[0], w_down.shape[1]
    fc = _ff_chunk(d_ff, 1792)
    nf = d_ff // fc
    rows = FFN_ROWS
    r = jnp.pad(router, ((0, 0), (0, LANES - n_exp)))
    r_hi = r.astype(BF16)
    r_lo = (r - r_hi.astype(F32)).astype(BF16)
    r2 = jnp.stack([r_hi, r_lo])
    return pl.pallas_call(
        _moe_ln_kernel,
        grid=(T // rows, n_exp, nf),
        in_specs=[
            pl.BlockSpec((rows, D), lambda i, e, f: (i, 0)),
            pl.BlockSpec(r2.shape, lambda i, e, f: (0, 0, 0)),
            pl.BlockSpec((1, D, fc), lambda i, e, f: (e, 0, f)),
            pl.BlockSpec((1, D, fc), lambda i, e, f: (e, 0, nf + f)),
            pl.BlockSpec((1, fc, D), lambda i, e, f: (e, f, 0)),
            pl.BlockSpec((1, D), lambda i, e, f: (0, 0)),
            pl.BlockSpec((1, D), lambda i, e, f: (0, 0)),
        ],
        out_specs=pl.BlockSpec((rows, D), lambda i, e, f: (i, 0)),
        out_shape=jax.ShapeDtypeStruct((T, D), F32),
        scratch_shapes=[pltpu.VMEM((rows, D), BF16), pltpu.VMEM((rows, D), F32),
                        pltpu.VMEM((rows, LANES), F32)],
        compiler_params=_params("parallel", "arbitrary", "arbitrary"),
    )(x, r2, w_gu, w_gu, w_down, g.reshape(1, D), b.reshape(1, D))


def kernel(x, ln_g, ln_b, pool_w, pool_scale, nsa_w_in, nsa_pe_k, nsa_w1_k, nsa_w2_k, nsa_pe_v, nsa_w1_v,
           nsa_w2_v, nsa_w_out, ffn_w_gu, ffn_w_down, moe_router, moe_w_gu, moe_w_down):
    B, S, D = x.shape
    T = B * S
    xa = _pool_ln(x, pool_w[0], pool_scale[0], ln_g[0, 0], ln_b[0, 0]).reshape(T, D)
    x1 = _ffn_ln(xa, ffn_w_gu[0].astype(BF16), ffn_w_down[0].astype(BF16), ln_g[0, 1], ln_b[0, 1])
    q, kcv, ks, vs, kw, vw, gates = _in_proj(x1, nsa_w_in[0], B, S)
    kc, vc = _compress(kcv, nsa_w1_k[0], nsa_w2_k[0], nsa_pe_k[0], nsa_w1_v[0], nsa_w2_v[0], nsa_pe_v[0], B, S)
    oc, sel = _cmp_attn(q, kc, vc, B, S)
    o = _sel_win_attn(q, sel, ks, vs, kw, vw, oc, gates, B, S).reshape(T, D)
    x2 = _out_proj_ln(o, nsa_w_out[0], x1, ln_g[1, 0], ln_b[1, 0])
    y = _moe_ln(x2, moe_router[0], moe_w_gu[0].astype(BF16), moe_w_down[0].astype(BF16), ln_g[1, 1], ln_b[1, 1])
    return y.reshape(B, S, D)
```

```python
import functools

import jax
import jax.numpy as jnp
from jax import lax
from jax.experimental import pallas as pl
from jax.experimental.pallas import tpu as pltpu

D_MODEL = 1024
DEPTH = 2
POOL_WINDOWS = (2, 4, 8, 16)
POOL_GROUP_DIM = D_MODEL // len(POOL_WINDOWS)
POOL_HALO = 16
N_HEADS = 16
N_KV_GROUPS = 4
HEADS_PER_GROUP = N_HEADS // N_KV_GROUPS
HEAD_DIM = D_MODEL // N_HEADS
HALF_DIM = HEAD_DIM // 2
GROUP_Q_DIM = HEADS_PER_GROUP * HEAD_DIM
KV_DIM = N_KV_GROUPS * HEAD_DIM
N_BRANCHES = 3
N_GATES = N_BRANCHES * N_HEADS
CMP_STRIDE = 16
CMP_BLOCK = 2 * CMP_STRIDE
CMP_HIDDEN = 2 * HEAD_DIM
SEL_BLOCK = 64
SEL_SHIFT = SEL_BLOCK.bit_length() - 1
SEL_TOP_N = 16
WINDOW = 512
FORCE_BONUS = 1.0e3
NEG_INF = -1.0e30
ROPE_THETA = 10000.0
ATTN_SCALE = HEAD_DIM ** -0.5
LOG2E = 1.4426950408889634
Q_SCALE = ATTN_SCALE * LOG2E
AUG_DIM = 2 * HEAD_DIM
SEL_SLOTS = AUG_DIM - HEAD_DIM
N_EXPERTS = 8
LN_EPS = 1e-5
ALPHA = (2 * DEPTH) ** 0.25

LANES = 128
VMEM_LIMIT_BYTES = 56 * 1024 * 1024

POOL_ROWS = 512
FFN_ROWS = 512
PROJ_ROWS = 512
PROJ_COLS = 256
CMP_Q_ROWS = 512
ATT_Q_ROWS = 512
ATT_K_ROWS = 512
ATT_HALF = 256
MOE_ROWS = 512

F32 = jnp.float32
BF16 = jnp.bfloat16


def _dot(a, b):
    return jnp.dot(a, b, preferred_element_type=F32)


def _dot_nt(a, b):
    return lax.dot_general(a, b, (((1,), (1,)), ((), ())), preferred_element_type=F32)


def _layer_norm(z, g, b):
    mu = jnp.mean(z, axis=-1, keepdims=True)
    zc = z - mu
    var = jnp.mean(zc * zc, axis=-1, keepdims=True)
    return zc * lax.rsqrt(var + LN_EPS) * g + b


def _silu(x):
    return x / (1.0 + jnp.exp(-x))


def _params(*semantics):
    return pltpu.CompilerParams(dimension_semantics=semantics, vmem_limit_bytes=VMEM_LIMIT_BYTES)


def _pool_ln_kernel(x_ref, halo_ref, w_ref, scale_ref, g_ref, b_ref, o_ref, ext_ref):
    i = pl.program_id(1)
    rows = x_ref.shape[1]
    x = x_ref[0]
    ext_ref[0:POOL_HALO, :] = jnp.where(i > 0, halo_ref[0], 0.0)
    ext_ref[POOL_HALO:, :] = x
    pos = i * rows + lax.broadcasted_iota(jnp.int32, (rows, 1), 0)
    ys = []
    for gi, w in enumerate(POOL_WINDOWS):
        c0 = gi * POOL_GROUP_DIM
        xg = x[:, c0:c0 + POOL_GROUP_DIM]
        acc = xg
        for k in range(1, w):
            acc = acc + ext_ref[POOL_HALO - k:POOL_HALO - k + rows, c0:c0 + POOL_GROUP_DIM]
        cnt = jnp.minimum(pos + 1, w).astype(F32)
        diff = acc / cnt - xg
        ys.append(_dot(diff.astype(BF16), w_ref[gi]))
    h = jnp.concatenate(ys, axis=1) * scale_ref[...]
    o_ref[0] = _layer_norm(ALPHA * x + h, g_ref[...], b_ref[...])


def _pool_ln(x, w, scale, g, b):
    B, S, D = x.shape
    rows = POOL_ROWS
    halo_blocks = rows // POOL_HALO
    row2 = lambda v: v.reshape(1, D)
    return pl.pallas_call(
        _pool_ln_kernel,
        grid=(B, S // rows),
        in_specs=[
            pl.BlockSpec((1, rows, D), lambda bi, i: (bi, i, 0)),
            pl.BlockSpec((1, POOL_HALO, D), lambda bi, i: (bi, jnp.maximum(i * halo_blocks - 1, 0), 0)),
            pl.BlockSpec(w.shape, lambda bi, i: (0, 0, 0)),
            pl.BlockSpec((1, D), lambda bi, i: (0, 0)),
            pl.BlockSpec((1, D), lambda bi, i: (0, 0)),
            pl.BlockSpec((1, D), lambda bi, i: (0, 0)),
        ],
        out_specs=pl.BlockSpec((1, rows, D), lambda bi, i: (bi, i, 0)),
        out_shape=jax.ShapeDtypeStruct((B, S, D), F32),
        scratch_shapes=[pltpu.VMEM((rows + POOL_HALO, D), F32)],
        compiler_params=_params("parallel", "arbitrary"),
    )(x, x, w.astype(BF16), row2(scale), row2(g), row2(b))


def _ffn_ln_kernel(x_ref, wg_ref, wu_ref, wd_ref, g_ref, b_ref, o_ref, xb_ref, acc_ref):
    f = pl.program_id(1)

    @pl.when(f == 0)
    def _():
        xb_ref[...] = x_ref[...].astype(BF16)
        acc_ref[...] = jnp.zeros_like(acc_ref)

    xb = xb_ref[...]
    a = _silu(_dot(xb, wg_ref[...])) * _dot(xb, wu_ref[...])
    acc_ref[...] += _dot(a.astype(BF16), wd_ref[...])

    @pl.when(f == pl.num_programs(1) - 1)
    def _():
        o_ref[...] = _layer_norm(ALPHA * x_ref[...] + acc_ref[...], g_ref[...], b_ref[...])


def _ff_chunk(d_ff, target):
    best = LANES
    for c in range(LANES, target + 1, LANES):
        if d_ff % c == 0:
            best = c
    return best


def _ffn_ln(x, w_gu, w_down, g, b):
    T, D = x.shape
    d_ff = w_down.shape[0]
    fc = _ff_chunk(d_ff, 1536)
    nf = d_ff // fc
    rows = FFN_ROWS
    return pl.pallas_call(
        _ffn_ln_kernel,
        grid=(T // rows, nf),
        in_specs=[
            pl.BlockSpec((rows, D), lambda i, f: (i, 0)),
            pl.BlockSpec((D, fc), lambda i, f: (0, f)),
            pl.BlockSpec((D, fc), lambda i, f: (0, nf + f)),
            pl.BlockSpec((fc, D), lambda i, f: (f, 0)),
            pl.BlockSpec((1, D), lambda i, f: (0, 0)),
            pl.BlockSpec((1, D), lambda i, f: (0, 0)),
        ],
        out_specs=pl.BlockSpec((rows, D), lambda i, f: (i, 0)),
        out_shape=jax.ShapeDtypeStruct((T, D), F32),
        scratch_shapes=[pltpu.VMEM((rows, D), BF16), pltpu.VMEM((rows, D), F32)],
        compiler_params=_params("parallel", "arbitrary"),
    )(x, w_gu, w_gu, w_down, g.reshape(1, D), b.reshape(1, D))


def _rope_tables(pos, reps):
    freqs = jnp.power(ROPE_THETA, -jnp.arange(HALF_DIM, dtype=F32) / HALF_DIM)
    ang = pos.astype(F32)[:, None] * freqs[None, :]
    cos, sin = jnp.cos(ang), jnp.sin(ang)
    return (jnp.tile(jnp.concatenate([cos, cos], axis=1), (1, reps)),
            jnp.tile(jnp.concatenate([-sin, sin], axis=1), (1, reps)))


def _in_proj_kernel(x_ref, w_ref, cos_ref, sin_ref,
                    q_ref, kcv_ref, ks_ref, vs_ref, kw_ref, vw_ref, gate_ref, *, steps_per_seq):
    rows = x_ref.shape[0]
    xb = x_ref[...].astype(BF16)
    cos = cos_ref[...]
    sin = sin_ref[...]
    lane = lax.broadcasted_iota(jnp.int32, cos.shape, 1)
    first_half = (lane & (HEAD_DIM - 1)) < HALF_DIM
    seq_step = pl.program_id(0) % steps_per_seq
    pos = seq_step * rows + lax.broadcasted_iota(jnp.int32, (rows, HEAD_DIM), 0)
    col = lax.broadcasted_iota(jnp.int32, (rows, HEAD_DIM), 1)
    blk_onehot = jnp.where((pos >> SEL_SHIFT) == col, 1.0, 0.0)
    ones_col = jnp.where(col == 0, 1.0, 0.0)
    zeros = jnp.zeros((rows, HEAD_DIM), F32)

    def col_tile(j):
        return _dot(xb, w_ref[:, j * PROJ_COLS:(j + 1) * PROJ_COLS])

    def rope(y):
        rot = jnp.where(first_half,
                        pltpu.roll(y, PROJ_COLS - HALF_DIM, 1),
                        pltpu.roll(y, HALF_DIM, 1))
        return y * cos + rot * sin

    def store_groups(ref, y, extra):
        for gi in range(N_KV_GROUPS):
            ref[0, gi, :, 0:HEAD_DIM] = y[:, gi * HEAD_DIM:(gi + 1) * HEAD_DIM].astype(ref.dtype)
            ref[0, gi, :, HEAD_DIM:] = extra.astype(ref.dtype)

    n_q = D_MODEL // PROJ_COLS
    for j in range(n_q):
        q_ref[:, j * PROJ_COLS:(j + 1) * PROJ_COLS] = (rope(col_tile(j)) * Q_SCALE).astype(q_ref.dtype)
    kcv_ref[:, 0:KV_DIM] = col_tile(n_q)
    kcv_ref[:, KV_DIM:2 * KV_DIM] = col_tile(n_q + 1)
    store_groups(ks_ref, rope(col_tile(n_q + 2)), blk_onehot)
    store_groups(vs_ref, col_tile(n_q + 3), ones_col)
    store_groups(kw_ref, rope(col_tile(n_q + 4)), zeros)
    store_groups(vw_ref, col_tile(n_q + 5), ones_col)
    logits = col_tile(n_q + 6)[:, 0:LANES]
    gate_ref[...] = 1.0 / (1.0 + jnp.exp(-logits))


def _in_proj(x, w_in, B, S):
    T, D = x.shape
    rows = PROJ_ROWS
    steps_per_seq = S // rows
    n_tiles = D_MODEL // PROJ_COLS + 2 * N_BRANCHES + 1
    w = jnp.pad(w_in, ((0, 0), (0, n_tiles * PROJ_COLS - w_in.shape[1]))).astype(BF16)
    cos, sin = _rope_tables(jnp.arange(S), PROJ_COLS // HEAD_DIM)
    kv_shape = jax.ShapeDtypeStruct((B, N_KV_GROUPS, S, AUG_DIM), BF16)
    kv_spec = pl.BlockSpec((1, N_KV_GROUPS, rows, AUG_DIM),
                           lambda i: (i // steps_per_seq, 0, i % steps_per_seq, 0))
    return pl.pallas_call(
        functools.partial(_in_proj_kernel, steps_per_seq=steps_per_seq),
        grid=(T // rows,),
        in_specs=[
            pl.BlockSpec((rows, D), lambda i: (i, 0)),
            pl.BlockSpec(w.shape, lambda i: (0, 0)),
            pl.BlockSpec((rows, PROJ_COLS), lambda i: (i % steps_per_seq, 0)),
            pl.BlockSpec((rows, PROJ_COLS), lambda i: (i % steps_per_seq, 0)),
        ],
        out_specs=[
            pl.BlockSpec((rows, D_MODEL), lambda i: (i, 0)),
            pl.BlockSpec((rows, 2 * KV_DIM), lambda i: (i, 0)),
            kv_spec, kv_spec, kv_spec, kv_spec,
            pl.BlockSpec((rows, LANES), lambda i: (i, 0)),
        ],
        out_shape=[
            jax.ShapeDtypeStruct((T, D_MODEL), BF16),
            jax.ShapeDtypeStruct((T, 2 * KV_DIM), F32),
            kv_shape, kv_shape, kv_shape, kv_shape,
            jax.ShapeDtypeStruct((T, LANES), F32),
        ],
        compiler_params=_params("parallel"),
    )(x, w, cos, sin)


def _compress_kernel(ch_ref, w1_ref, pe_ref, w2_ref, cos_ref, sin_ref, kc_ref, vc_ref):
    half = CMP_STRIDE * HEAD_DIM
    for which, o_ref in enumerate((kc_ref, vc_ref)):
        w1 = w1_ref[which]
        r = _dot(ch_ref[which, 0, 0], w1)
        pe = pe_ref[which]
        bias = (_dot(pe[:, 0:half], w1[:, 0:CMP_HIDDEN])
                + _dot(pe[:, half:2 * half], w1[:, CMP_HIDDEN:2 * CMP_HIDDEN]))[0:1, :]
        nxt = pltpu.roll(r[:, CMP_HIDDEN:2 * CMP_HIDDEN], r.shape[0] - 1, 0)
        pre = r[:, 0:CMP_HIDDEN] + nxt + bias
        o = _dot(_silu(pre).astype(BF16), w2_ref[which])
        if which == 0:
            rot = jnp.concatenate([o[:, HALF_DIM:], o[:, :HALF_DIM]], axis=1)
            o = o * cos_ref[...] + rot * sin_ref[...]
        o_ref[0, 0] = o.astype(o_ref.dtype)


def _compress(kcv, w1_k, w2_k, pe_k, w1_v, w2_v, pe_v, B, S):
    n_chunks = S // CMP_STRIDE
    half = CMP_STRIDE * HEAD_DIM
    ch = kcv.reshape(B, n_chunks, CMP_STRIDE, 2, N_KV_GROUPS, HEAD_DIM)
    ch = ch.transpose(3, 0, 4, 1, 2, 5).reshape(2, B, N_KV_GROUPS, n_chunks, half).astype(BF16)
    cat = lambda w1: jnp.concatenate([w1[:half], w1[half:]], axis=1)
    w1 = jnp.stack([cat(w1_k), cat(w1_v)]).astype(BF16)
    w2 = jnp.stack([w2_k, w2_v]).astype(BF16)
    pe = jnp.stack([pe_k.reshape(1, -1), pe_v.reshape(1, -1)])
    pe = jnp.broadcast_to(pe, (2, 8, 2 * half)).astype(BF16)
    cos, sin = _rope_tables(CMP_STRIDE * jnp.arange(n_chunks) + CMP_BLOCK - 1, 1)
    out_shape = jax.ShapeDtypeStruct((B, N_KV_GROUPS, n_chunks, HEAD_DIM), BF16)
    out_spec = pl.BlockSpec((1, 1, n_chunks, HEAD_DIM), lambda bi, gi: (bi, gi, 0, 0))
    const = lambda a: pl.BlockSpec(a.shape, lambda bi, gi: (0,) * a.ndim)
    return pl.pallas_call(
        _compress_kernel,
        grid=(B, N_KV_GROUPS),
        in_specs=[
            pl.BlockSpec((2, 1, 1, n_chunks, half), lambda bi, gi: (0, bi, gi, 0, 0)),
            const(w1), const(pe), const(w2), const(cos), const(sin),
        ],
        out_specs=[out_spec, out_spec],
        out_shape=[out_shape, out_shape],
        compiler_params=_params("parallel", "parallel"),
    )(ch, w1, pe, w2, cos, sin)


def _cmp_attn_kernel(q_ref, kc_ref, vc_ref, ovl_ref, oc_ref, sel_ref, score_ref):
    i = pl.program_id(2)
    rows = q_ref.shape[1]
    n_cmp = kc_ref.shape[2]
    n_sel = ovl_ref.shape[0]
    kc = kc_ref[0, 0]
    vc = vc_ref[0, 0]
    t = i * rows + lax.broadcasted_iota(jnp.int32, (rows, 1), 0)
    cend = CMP_STRIDE * lax.broadcasted_iota(jnp.int32, (1, n_cmp), 1) + (CMP_BLOCK - 1)
    cvalid = cend <= t
    any_valid = jnp.where(t >= CMP_BLOCK - 1, 1.0, 0.0)
    pc_sum = jnp.zeros((rows, n_cmp), F32)
    for h in range(HEADS_PER_GROUP):
        qh = q_ref[0, :, h * HEAD_DIM:(h + 1) * HEAD_DIM]
        s = jnp.where(cvalid, _dot_nt(qh, kc), NEG_INF)
        e = jnp.exp2(s - jnp.max(s, axis=-1, keepdims=True))
        pc = e / jnp.sum(e, axis=-1, keepdims=True) * any_valid
        oc_ref[0, :, h * HEAD_DIM:(h + 1) * HEAD_DIM] = _dot(pc.astype(BF16), vc)
        pc_sum = pc_sum + pc
    imp = _dot_nt(ovl_ref[...], pc_sum.astype(BF16))
    j = lax.broadcasted_iota(jnp.int32, (n_sel, 1), 0)
    blk_t = (i * rows + lax.broadcasted_iota(jnp.int32, (1, rows), 1)) >> SEL_SHIFT
    bvalid = j <= blk_t
    forced = (j == 0) | (j == blk_t) | (j == blk_t - 1)
    score = jnp.where(bvalid, imp + jnp.where(forced, FORCE_BONUS, 0.0), -1.0)
    score_ref[...] = score
    rank = jnp.zeros((n_sel, rows), F32)
    for jp in range(n_sel):
        other = score_ref[jp:jp + 1, :]
        ge = jnp.where(other >= score, 1.0, 0.0)
        gt = jnp.where(other > score, 1.0, 0.0)
        rank = rank + jnp.where(j > jp, ge, gt)
    selected = (rank < float(SEL_TOP_N)) & bvalid
    sel_ref[0, 0] = jnp.where(selected, 0.0, NEG_INF).T.astype(sel_ref.dtype)


def _cmp_attn(q, kc, vc, B, S):
    rows = CMP_Q_ROWS
    n_cmp = S // CMP_STRIDE
    n_sel = SEL_SLOTS
    assert S // SEL_BLOCK <= SEL_SLOTS
    cstart = CMP_STRIDE * jnp.arange(n_cmp)
    sstart = SEL_BLOCK * jnp.arange(n_sel)
    overlap = ((cstart[None, :] <= sstart[:, None] + SEL_BLOCK - 1)
               & (cstart[None, :] + CMP_BLOCK - 1 >= sstart[:, None])).astype(BF16)
    q3 = q.reshape(B, S, D_MODEL)
    kv_spec = pl.BlockSpec((1, 1, n_cmp, HEAD_DIM), lambda bi, gi, i: (bi, gi, 0, 0))
    return pl.pallas_call(
        _cmp_attn_kernel,
        grid=(B, N_KV_GROUPS, S // rows),
        in_specs=[
            pl.BlockSpec((1, rows, GROUP_Q_DIM), lambda bi, gi, i: (bi, i, gi)),
            kv_spec, kv_spec,
            pl.BlockSpec(overlap.shape, lambda bi, gi, i: (0, 0)),
        ],
        out_specs=[
            pl.BlockSpec((1, rows, GROUP_Q_DIM), lambda bi, gi, i: (bi, i, gi)),
            pl.BlockSpec((1, 1, rows, n_sel), lambda bi, gi, i: (bi, gi, i, 0)),
        ],
        out_shape=[
            jax.ShapeDtypeStruct((B, S, D_MODEL), F32),
            jax.ShapeDtypeStruct((B, N_KV_GROUPS, S, n_sel), BF16),
        ],
        scratch_shapes=[pltpu.VMEM((n_sel, rows), F32)],
        compiler_params=_params("parallel", "parallel", "arbitrary"),
    )(q3, kc, vc, overlap)


def _sel_win_attn_kernel(q_ref, sel_ref, ks_ref, vs_ref, kw_ref, vw_ref, diag_ref, band_ref, oc_ref, gate_ref,
                         o_ref, qaug_ref, m_ref, acc_ref, ow_ref):
    i = pl.program_id(2)
    tq = q_ref.shape[1]
    hg = HEADS_PER_GROUP
    half = ATT_HALF
    n_half = tq // half
    half_rows = hg * half

    for hq in range(n_half):
        for h in range(hg):
            r0 = (hq * hg + h) * half
            qaug_ref[r0:r0 + half, 0:HEAD_DIM] = q_ref[0, hq * half:(hq + 1) * half, h * HEAD_DIM:(h + 1) * HEAD_DIM]
            qaug_ref[r0:r0 + half, HEAD_DIM:] = sel_ref[0, 0, hq * half:(hq + 1) * half, :]
    qa = qaug_ref[...]

    def keys(ref, start, size):
        return ref[0, 0, pl.ds(pl.multiple_of(start, ATT_HALF), size), :]

    m_ref[...] = jnp.full(m_ref.shape, NEG_INF, F32)
    acc_ref[...] = jnp.zeros(acc_ref.shape, F32)

    def sel_step(c, bias):
        s = _dot_nt(qa, keys(ks_ref, c * ATT_K_ROWS, ATT_K_ROWS))
        if bias is not None:
            s = s + bias
        m_prev = m_ref[...]
        m_next = jnp.maximum(m_prev, jnp.max(s, axis=-1, keepdims=True))
        p = jnp.exp2(s - jnp.tile(m_next, (1, ATT_K_ROWS // LANES)))
        alpha = jnp.exp2(m_prev - m_next)
        acc_ref[...] = alpha * acc_ref[...] + _dot(p.astype(BF16), keys(vs_ref, c * ATT_K_ROWS, ATT_K_ROWS))
        m_ref[...] = m_next

    def sel_body(c, carry):
        sel_step(c, None)
        return carry

    lax.fori_loop(0, i, sel_body, 0)
    sel_step(i, diag_ref[...])

    def softmax_pv(s, v):
        p = jnp.exp2(s - jnp.max(s, axis=-1, keepdims=True))
        return _dot(p.astype(BF16), v)

    @pl.when(i == 0)
    def _():
        s = _dot_nt(qa, keys(kw_ref, 0, tq)) + diag_ref[...]
        ow_ref[...] = softmax_pv(s, keys(vw_ref, 0, tq))

    @pl.when(i > 0)
    def _():
        for hq in range(n_half):
            start = i * tq + hq * half - WINDOW
            qh = qaug_ref[hq * half_rows:(hq + 1) * half_rows, :]
            s = _dot_nt(qh, keys(kw_ref, start, WINDOW + half)) + band_ref[...]
            ow_ref[hq * half_rows:(hq + 1) * half_rows, :] = softmax_pv(s, keys(vw_ref, start, WINDOW + half))

    acc_s = acc_ref[...]
    acc_w = ow_ref[...]
    o_s = acc_s[:, 0:HEAD_DIM] / acc_s[:, HEAD_DIM:HEAD_DIM + 1]
    o_w = acc_w[:, 0:HEAD_DIM] / acc_w[:, HEAD_DIM:HEAD_DIM + 1]
    gate = gate_ref[0, 0]
    for hq in range(n_half):
        tok = slice(hq * half, (hq + 1) * half)
        for h in range(hg):
            r0 = (hq * hg + h) * half
            gc = gate[tok, N_BRANCHES * h + 0:N_BRANCHES * h + 1]
            gs = gate[tok, N_BRANCHES * h + 1:N_BRANCHES * h + 2]
            gw = gate[tok, N_BRANCHES * h + 2:N_BRANCHES * h + 3]
            o = (gc * oc_ref[0, tok, h * HEAD_DIM:(h + 1) * HEAD_DIM]
                 + gs * o_s[r0:r0 + half] + gw * o_w[r0:r0 + half])
            o_ref[0, tok, h * HEAD_DIM:(h + 1) * HEAD_DIM] = o.astype(o_ref.dtype)


def _sel_win_attn(q, sel, ks, vs, kw, vw, oc, gates, B, S):
    tq = ATT_Q_ROWS
    half = ATT_HALF
    n_sel = SEL_SLOTS
    hg = HEADS_PER_GROUP
    rows = hg * tq
    q3 = q.reshape(B, S, D_MODEL)
    gate_g = gates.reshape(B, S, LANES)[:, :, :N_GATES].reshape(B, S, N_KV_GROUPS, N_BRANCHES * hg)
    gate_g = gate_g.transpose(0, 2, 1, 3)
    t_rel = (jnp.arange(tq // half)[:, None, None] * half + jnp.arange(half)[None, None, :])
    t_rel = jnp.broadcast_to(t_rel, (tq // half, hg, half)).reshape(rows, 1)
    diag = jnp.where(jnp.arange(ATT_K_ROWS)[None, :] <= t_rel, 0.0, NEG_INF).astype(F32)
    tt = jnp.broadcast_to(jnp.arange(half)[None, :], (hg, half)).reshape(hg * half, 1)
    a = jnp.arange(WINDOW + half)[None, :]
    band = jnp.where((a > tt) & (a <= tt + WINDOW), 0.0, NEG_INF).astype(F32)
    kv_spec = pl.BlockSpec((1, 1, S, AUG_DIM), lambda bi, gi, i: (bi, gi, 0, 0))
    q_spec = pl.BlockSpec((1, tq, GROUP_Q_DIM), lambda bi, gi, i: (bi, i, gi))
    const = lambda arr: pl.BlockSpec(arr.shape, lambda bi, gi, i: (0, 0))
    return pl.pallas_call(
        _sel_win_attn_kernel,
        grid=(B, N_KV_GROUPS, S // tq),
        in_specs=[
            q_spec,
            pl.BlockSpec((1, 1, tq, n_sel), lambda bi, gi, i: (bi, gi, i, 0)),
            kv_spec, kv_spec, kv_spec, kv_spec,
            const(diag), const(band),
            q_spec,
            pl.BlockSpec((1, 1, tq, N_BRANCHES * hg), lambda bi, gi, i: (bi, gi, i, 0)),
        ],
        out_specs=q_spec,
        out_shape=jax.ShapeDtypeStruct((B, S, D_MODEL), BF16),
        scratch_shapes=[
            pltpu.VMEM((rows, AUG_DIM), BF16),
            pltpu.VMEM((rows, LANES), F32),
            pltpu.VMEM((rows, AUG_DIM), F32),
            pltpu.VMEM((rows, AUG_DIM), F32),
        ],
        compiler_params=_params("parallel", "parallel", "arbitrary"),
    )(q3, sel, ks, vs, kw, vw, diag, band, oc, gate_g)


def _out_proj_ln_kernel(o_ref, w_ref, x_ref, g_ref, b_ref, y_ref):
    h = _dot(o_ref[...], w_ref[...])
    y_ref[...] = _layer_norm(ALPHA * x_ref[...] + h, g_ref[...], b_ref[...])


def _out_proj_ln(o, w_out, x, g, b):
    T, D = x.shape
    rows = PROJ_ROWS
    return pl.pallas_call(
        _out_proj_ln_kernel,
        grid=(T // rows,),
        in_specs=[
            pl.BlockSpec((rows, D), lambda i: (i, 0)),
            pl.BlockSpec((D, D), lambda i: (0, 0)),
            pl.BlockSpec((rows, D), lambda i: (i, 0)),
            pl.BlockSpec((1, D), lambda i: (0, 0)),
            pl.BlockSpec((1, D), lambda i: (0, 0)),
        ],
        out_specs=pl.BlockSpec((rows, D), lambda i: (i, 0)),
        out_shape=jax.ShapeDtypeStruct((T, D), F32),
        compiler_params=_params("parallel"),
    )(o, w_out.astype(BF16), x, g.reshape(1, D), b.reshape(1, D))


def _route_kernel(x_ref, r_ref, tri_ref, xb_ref, info_ref, wcol_ref, cnt_ref, carry_ref):
    @pl.when(pl.program_id(0) == 0)
    def _():
        carry_ref[...] = jnp.zeros_like(carry_ref)

    x = x_ref[...]
    xb = x.astype(BF16)
    xb_ref[...] = xb
    x_lo = (x - xb.astype(F32)).astype(BF16)
    r_hi = r_ref[0]
    r_lo = r_ref[1]
    logits = _dot(xb, r_hi) + (_dot(x_lo, r_hi) + _dot(xb, r_lo))
    lane = lax.broadcasted_iota(jnp.int32, logits.shape, 1)
    logits = jnp.where(lane < N_EXPERTS, logits, -jnp.inf)
    v1 = jnp.max(logits, axis=-1, keepdims=True)
    i1 = jnp.min(jnp.where(logits == v1, lane, LANES), axis=-1, keepdims=True)
    rest = jnp.where(lane == i1, -jnp.inf, logits)
    v2 = jnp.max(rest, axis=-1, keepdims=True)
    i2 = jnp.min(jnp.where(rest == v2, lane, LANES), axis=-1, keepdims=True)
    e2 = jnp.exp(v2 - v1)
    w1 = 1.0 / (1.0 + e2)
    w2 = e2 / (1.0 + e2)
    m1 = jnp.where(lane == i1, 1.0, 0.0)
    m2 = jnp.where(lane == i2, 1.0, 0.0)
    routed = m1 + m2
    before = _dot(tri_ref[...], routed.astype(BF16)) + carry_ref[0:1, :]
    rank1 = jnp.sum(m1 * before, axis=-1, keepdims=True)
    rank2 = jnp.sum(m2 * before, axis=-1, keepdims=True)
    cnt = jnp.sum(routed, axis=0, keepdims=True)
    carry_ref[...] = carry_ref[...] + cnt
    cnt_ref[0] = jnp.broadcast_to(cnt, cnt_ref.shape[1:])
    info = jnp.where(lane == 0, i1.astype(F32),
                     jnp.where(lane == 1, i2.astype(F32),
                               jnp.where(lane == 2, rank1, jnp.where(lane == 3, rank2, 0.0))))
    info_ref[...] = info
    w1_hi = w1.astype(BF16).astype(F32)
    w2_hi = w2.astype(BF16).astype(F32)
    wcol = jnp.where(lane == 0, w1_hi,
                     jnp.where(lane == 1, w1 - w1_hi,
                               jnp.where(lane == 2, w2_hi, jnp.where(lane == 3, w2 - w2_hi, 0.0))))
    wcol_ref[...] = wcol.astype(BF16)


def _route(x, router):
    T, D = x.shape
    rows = MOE_ROWS
    r = jnp.pad(router, ((0, 0), (0, LANES - router.shape[1])))
    r_hi = r.astype(BF16)
    r_lo = (r - r_hi.astype(F32)).astype(BF16)
    r2 = jnp.stack([r_hi, r_lo])
    tri = (jnp.arange(rows)[None, :] < jnp.arange(rows)[:, None]).astype(BF16)
    return pl.pallas_call(
        _route_kernel,
        grid=(T // rows,),
        in_specs=[
            pl.BlockSpec((rows, D), lambda i: (i, 0)),
            pl.BlockSpec(r2.shape, lambda i: (0, 0, 0)),
            pl.BlockSpec(tri.shape, lambda i: (0, 0)),
        ],
        out_specs=[
            pl.BlockSpec((rows, D), lambda i: (i, 0)),
            pl.BlockSpec((rows, LANES), lambda i: (i, 0)),
            pl.BlockSpec((rows, LANES), lambda i: (i, 0)),
            pl.BlockSpec((1, 8, LANES), lambda i: (i, 0, 0)),
        ],
        out_shape=[
            jax.ShapeDtypeStruct((T, D), BF16),
            jax.ShapeDtypeStruct((T, LANES), F32),
            jax.ShapeDtypeStruct((T, LANES), BF16),
            jax.ShapeDtypeStruct((T // rows, 8, LANES), F32),
        ],
        scratch_shapes=[pltpu.VMEM((8, LANES), F32)],
        compiler_params=_params("arbitrary"),
    )(x, r2, tri)


def _moe_tables(cnt, n_blocks_max, n_items_max):
    R = MOE_ROWS
    C, E = cnt.shape
    i32 = jnp.int32
    cum = jnp.concatenate([jnp.zeros((1, E), i32), jnp.cumsum(cnt, axis=0)], axis=0)
    tot = cum[-1]
    nb = (tot + R - 1) // R
    nb_end = jnp.cumsum(nb)
    blk_start = nb_end - nb
    n_blocks = nb_end[-1]
    b = jnp.minimum(jnp.arange(n_blocks_max, dtype=i32), n_blocks - 1)
    bexp = jnp.minimum(jnp.searchsorted(nb_end, b, side="right").astype(i32), E - 1)
    bvalid = jnp.arange(n_blocks_max, dtype=i32) < n_blocks
    lb = b - blk_start[bexp]
    rho0 = lb * R
    rho1 = jnp.minimum((lb + 1) * R, tot[bexp]) - 1
    cum_b = cum[1:, :][:, bexp].T
    lo = jnp.minimum(jnp.sum(cum_b <= rho0[:, None], axis=1).astype(i32), C - 1)
    hi = jnp.minimum(jnp.sum(cum_b <= rho1[:, None], axis=1).astype(i32), C - 1)
    nit = jnp.where(bvalid, hi - lo + 1, 0)
    it_end = jnp.cumsum(nit)
    it_start = it_end - nit
    n_items = it_end[-1]
    i = jnp.arange(n_items_max, dtype=i32)
    ic = jnp.minimum(i, n_items - 1)
    d_blk = jnp.minimum(jnp.searchsorted(it_end, ic, side="right").astype(i32), n_blocks_max - 1)
    d_chk = lo[d_blk] + ic - it_start[d_blk]
    d_valid = i < n_items
    d_first = d_valid & (ic == it_start[d_blk])
    bl_lo = (blk_start[None, :] + cum[:-1] // R).reshape(-1)
    bl_hi = (blk_start[None, :] + (cum[1:] - 1) // R).reshape(-1)
    npair = jnp.where(cnt.reshape(-1) > 0, bl_hi - bl_lo + 1, 0)
    p_end = jnp.cumsum(npair)
    p_start = p_end - npair
    n_items2 = p_end[-1]
    jc = jnp.minimum(i, n_items2 - 1)
    pair = jnp.minimum(jnp.searchsorted(p_end, jc, side="right").astype(i32), C * E - 1)
    c_blk = bl_lo[pair] + jc - p_start[pair]
    c_chk = pair // E
    c_valid = i < n_items2
    prev_chk = jnp.concatenate([jnp.full((1,), -1, i32), c_chk[:-1]])
    next_chk = jnp.concatenate([c_chk[1:], jnp.full((1,), -1, i32)])
    next_valid = jnp.concatenate([c_valid[1:], jnp.zeros((1,), bool)])
    c_first = c_valid & (c_chk != prev_chk)
    c_last = c_valid & ((c_chk != next_chk) | ~next_valid)
    as_i32 = lambda v: v.astype(i32)
    return dict(base=blk_start * R, bexp=bexp, bvalid=as_i32(bvalid),
                d_blk=d_blk, d_chk=d_chk, d_valid=as_i32(d_valid), d_first=as_i32(d_first),
                c_blk=c_blk, c_chk=c_chk, c_valid=as_i32(c_valid), c_first=as_i32(c_first), c_last=as_i32(c_last))


def _dispatch_kernel(blk_ref, chk_ref, valid_ref, first_ref, xb_ref, pos_ref, wcol_ref,
                     xs_ref, ws_ref, acc_ref, wacc_ref):
    i = pl.program_id(0)
    R = xs_ref.shape[0]

    @pl.when(first_ref[i] == 1)
    def _():
        acc_ref[...] = jnp.zeros_like(acc_ref)
        wacc_ref[...] = jnp.zeros_like(wacc_ref)

    @pl.when(valid_ref[i] == 1)
    def _():
        rows = blk_ref[i] * R + lax.broadcasted_iota(jnp.int32, (R, 1), 0)
        p1 = jnp.where(pos_ref[0, 0:1, :] == rows, 1.0, 0.0).astype(BF16)
        p2 = jnp.where(pos_ref[0, 1:2, :] == rows, 1.0, 0.0).astype(BF16)
        acc_ref[...] += _dot(p1 + p2, xb_ref[...])
        lane = lax.broadcasted_iota(jnp.int32, wacc_ref.shape, 1)
        wc = wcol_ref[...]
        wacc_ref[...] += jnp.where(lane < 2, _dot(p1, wc), 0.0) + jnp.where((lane >= 2) & (lane < 4), _dot(p2, wc), 0.0)

    xs_ref[...] = acc_ref[...].astype(xs_ref.dtype)
    ws_ref[...] = wacc_ref[...]


def _dispatch(xb, posrow, wcol, tab, n_blocks_max, n_items_max):
    T, D = xb.shape
    R = MOE_ROWS
    grid_spec = pltpu.PrefetchScalarGridSpec(
        num_scalar_prefetch=4,
        grid=(n_items_max,),
        in_specs=[
            pl.BlockSpec((R, D), lambda i, blk, chk, valid, first: (chk[i], 0)),
            pl.BlockSpec((1, 8, R), lambda i, blk, chk, valid, first: (chk[i], 0, 0)),
            pl.BlockSpec((R, LANES), lambda i, blk, chk, valid, first: (chk[i], 0)),
        ],
        out_specs=[
            pl.BlockSpec((R, D), lambda i, blk, chk, valid, first: (blk[i], 0)),
            pl.BlockSpec((R, LANES), lambda i, blk, chk, valid, first: (blk[i], 0)),
        ],
        scratch_shapes=[pltpu.VMEM((R, D), F32), pltpu.VMEM((R, LANES), F32)],
    )
    return pl.pallas_call(
        _dispatch_kernel,
        grid_spec=grid_spec,
        out_shape=[
            jax.ShapeDtypeStruct((n_blocks_max * R, D), BF16),
            jax.ShapeDtypeStruct((n_blocks_max * R, LANES), F32),
        ],
        compiler_params=_params("arbitrary"),
    )(tab["d_blk"], tab["d_chk"], tab["d_valid"], tab["d_first"], xb, posrow, wcol)


def _expert_ffn_kernel(bexp_ref, bvalid_ref, xs_ref, ws_ref, wg_ref, wu_ref, wd_ref, ys_ref, acc_ref):
    b = pl.program_id(0)
    f = pl.program_id(1)

    @pl.when(f == 0)
    def _():
        acc_ref[...] = jnp.zeros_like(acc_ref)

    @pl.when(bvalid_ref[b] == 1)
    def _():
        xb = xs_ref[...]
        a = _silu(_dot(xb, wg_ref[0])) * _dot(xb, wu_ref[0])
        acc_ref[...] += _dot(a.astype(BF16), wd_ref[0])

    @pl.when(f == pl.num_programs(1) - 1)
    def _():
        w = jnp.sum(ws_ref[...], axis=-1, keepdims=True)
        ys_ref[...] = (acc_ref[...] * w).astype(ys_ref.dtype)


def _expert_ffn(xs, ws, w_gu, w_down, tab, n_blocks_max):
    D = xs.shape[1]
    R = MOE_ROWS
    d_ff = w_down.shape[1]
    fc = _ff_chunk(d_ff, 1792)
    nf = d_ff // fc
    f_eff = lambda b, f, bvalid: f * bvalid[b] + (nf - 1) * (1 - bvalid[b])
    grid_spec = pltpu.PrefetchScalarGridSpec(
        num_scalar_prefetch=2,
        grid=(n_blocks_max, nf),
        in_specs=[
            pl.BlockSpec((R, D), lambda b, f, bexp, bvalid: (b, 0)),
            pl.BlockSpec((R, LANES), lambda b, f, bexp, bvalid: (b, 0)),
            pl.BlockSpec((1, D, fc), lambda b, f, bexp, bvalid: (bexp[b], 0, f_eff(b, f, bvalid))),
            pl.BlockSpec((1, D, fc), lambda b, f, bexp, bvalid: (bexp[b], 0, nf + f_eff(b, f, bvalid))),
            pl.BlockSpec((1, fc, D), lambda b, f, bexp, bvalid: (bexp[b], f_eff(b, f, bvalid), 0)),
        ],
        out_specs=pl.BlockSpec((R, D), lambda b, f, bexp, bvalid: (b, 0)),
        scratch_shapes=[pltpu.VMEM((R, D), F32)],
    )
    return pl.pallas_call(
        _expert_ffn_kernel,
        grid_spec=grid_spec,
        out_shape=jax.ShapeDtypeStruct(xs.shape, BF16),
        compiler_params=_params("parallel", "arbitrary"),
    )(tab["bexp"], tab["bvalid"], xs, ws, w_gu, w_gu, w_down)


def _combine_ln_kernel(chk_ref, blk_ref, valid_ref, first_ref, last_ref,
                       ys_ref, pc1_ref, pc2_ref, x_ref, g_ref, b_ref, o_ref, acc_ref):
    i = pl.program_id(0)
    R = ys_ref.shape[0]

    @pl.when(first_ref[i] == 1)
    def _():
        acc_ref[...] = jnp.zeros_like(acc_ref)

    @pl.when(valid_ref[i] == 1)
    def _():
        pc1 = pc1_ref[...]
        pc2 = pc2_ref[...]
        lane = lax.broadcasted_iota(jnp.int32, pc1.shape, 1)
        parts = []
        for k in range(R // LANES):
            col = blk_ref[i] * R + k * LANES + lane
            parts.append((jnp.where(pc1 == col, 1.0, 0.0) + jnp.where(pc2 == col, 1.0, 0.0)).astype(BF16))
        acc_ref[...] += _dot(jnp.concatenate(parts, axis=1), ys_ref[...])

    @pl.when(last_ref[i] == 1)
    def _():
        o_ref[...] = _layer_norm(ALPHA * x_ref[...] + acc_ref[...], g_ref[...], b_ref[...])


def _combine_ln(ys, poscol1, poscol2, x, g, b, tab, n_items_max):
    T, D = x.shape
    R = MOE_ROWS
    chunk_map = lambda i, chk, blk, valid, first, last: (chk[i], 0)
    const_map = lambda i, chk, blk, valid, first, last: (0, 0)
    grid_spec = pltpu.PrefetchScalarGridSpec(
        num_scalar_prefetch=5,
        grid=(n_items_max,),
        in_specs=[
            pl.BlockSpec((R, D), lambda i, chk, blk, valid, first, last: (blk[i], 0)),
            pl.BlockSpec((R, LANES), chunk_map),
            pl.BlockSpec((R, LANES), chunk_map),
            pl.BlockSpec((R, D), chunk_map),
            pl.BlockSpec((1, D), const_map),
            pl.BlockSpec((1, D), const_map),
        ],
        out_specs=pl.BlockSpec((R, D), chunk_map),
        scratch_shapes=[pltpu.VMEM((R, D), F32)],
    )
    return pl.pallas_call(
        _combine_ln_kernel,
        grid_spec=grid_spec,
        out_shape=jax.ShapeDtypeStruct((T, D), F32),
        compiler_params=_params("arbitrary"),
    )(tab["c_chk"], tab["c_blk"], tab["c_valid"], tab["c_first"], tab["c_last"],
      ys, poscol1, poscol2, x, g.reshape(1, D), b.reshape(1, D))


def _moe_ln(x, router, w_gu, w_down, g, b):
    T, D = x.shape
    R = MOE_ROWS
    n_exp = w_down.shape[0]
    n_chunks = T // R
    n_blocks_max = (2 * T) // R + n_exp
    n_items_max = n_blocks_max + n_exp * (n_chunks - 1)
    xb, info, wcol, cnt = _route(x, router)
    tab = _moe_tables(cnt[:, 0, :n_exp].astype(jnp.int32), n_blocks_max, n_items_max)
    e1 = info[:, 0].astype(jnp.int32)
    e2 = info[:, 1].astype(jnp.int32)
    pos1 = tab["base"][e1] + info[:, 2].astype(jnp.int32)
    pos2 = tab["base"][e2] + info[:, 3].astype(jnp.int32)
    posrow = jnp.stack([pos1.reshape(n_chunks, R), pos2.reshape(n_chunks, R)], axis=1)
    posrow = jnp.pad(posrow, ((0, 0), (0, 6), (0, 0)), constant_values=-1)
    poscol1 = jnp.broadcast_to(pos1[:, None], (T, LANES))
    poscol2 = jnp.broadcast_to(pos2[:, None], (T, LANES))
    xs, ws = _dispatch(xb, posrow, wcol, tab, n_blocks_max, n_items_max)
    ys = _expert_ffn(xs, ws, w_gu, w_down, tab, n_blocks_max)
    return _combine_ln(ys, poscol1, poscol2, x, g, b, tab, n_items_max)


def kernel(x, ln_g, ln_b, pool_w, pool_scale, nsa_w_in, nsa_pe_k, nsa_w1_k, nsa_w2_k, nsa_pe_v, nsa_w1_v,
           nsa_w2_v, nsa_w_out, ffn_w_gu, ffn_w_down, moe_router, moe_w_gu, moe_w_down):
    B, S, D = x.shape
    T = B * S
    xa = _pool_ln(x, pool_w[0], pool_scale[0], ln_g[0, 0], ln_b[0, 0]).reshape(T, D)
    x1 = _ffn_ln(xa, ffn_w_gu[0].astype(BF16), ffn_w_down[0].astype(BF16), ln_g[0, 1], ln_b[0, 1])
    q, kcv, ks, vs, kw, vw, gates = _in_proj(x1, nsa_w_in[0], B, S)
    kc, vc = _compress(kcv, nsa_w1_k[0], nsa_w2_k[0], nsa_pe_k[0], nsa_w1_v[0], nsa_w2_v[0], nsa_pe_v[0], B, S)
    oc, sel = _cmp_attn(q, kc, vc, B, S)
    o = _sel_win_attn(q, sel, ks, vs, kw, vw, oc, gates, B, S).reshape(T, D)
    x2 = _out_proj_ln(o, nsa_w_out[0], x1, ln_g[1, 0], ln_b[1, 0])
    y = _moe_ln(x2, moe_router[0], moe_w_gu[0].astype(BF16), moe_w_down[0].astype(BF16), ln_g[1, 1], ln_b[1, 1])
    return y.reshape(B, S, D)
```

```python
import functools

import jax
import jax.numpy as jnp
from jax import lax
from jax.experimental import pallas as pl
from jax.experimental.pallas import tpu as pltpu

D_MODEL = 1024
DEPTH = 2
POOL_WINDOWS = (2, 4, 8, 16)
POOL_GROUP_DIM = D_MODEL // len(POOL_WINDOWS)
POOL_HALO = 16
N_HEADS = 16
N_KV_GROUPS = 4
HEADS_PER_GROUP = N_HEADS // N_KV_GROUPS
HEAD_DIM = D_MODEL // N_HEADS
HALF_DIM = HEAD_DIM // 2
GROUP_Q_DIM = HEADS_PER_GROUP * HEAD_DIM
KV_DIM = N_KV_GROUPS * HEAD_DIM
N_BRANCHES = 3
N_GATES = N_BRANCHES * N_HEADS
CMP_STRIDE = 16
CMP_BLOCK = 2 * CMP_STRIDE
CMP_HIDDEN = 2 * HEAD_DIM
SEL_BLOCK = 64
SEL_SHIFT = SEL_BLOCK.bit_length() - 1
SEL_TOP_N = 16
WINDOW = 512
FORCE_BONUS = 1.0e3
NEG_INF = -1.0e30
ROPE_THETA = 10000.0
ATTN_SCALE = HEAD_DIM ** -0.5
LOG2E = 1.4426950408889634
Q_SCALE = ATTN_SCALE * LOG2E
AUG_DIM = 2 * HEAD_DIM
SEL_SLOTS = AUG_DIM - HEAD_DIM
N_EXPERTS = 8
LN_EPS = 1e-5
ALPHA = (2 * DEPTH) ** 0.25

LANES = 128
VMEM_LIMIT_BYTES = 56 * 1024 * 1024

POOL_ROWS = 512
FFN_ROWS = 512
PROJ_ROWS = 512
PROJ_COLS = 256
CMP_Q_ROWS = 512
ATT_Q_ROWS = 512
ATT_K_ROWS = 512
ATT_HALF = 256
MOE_ROWS = 512
DISPATCH_SUB = 128
COMBINE_SUB = 256

F32 = jnp.float32
BF16 = jnp.bfloat16


def _dot(a, b):
    return jnp.dot(a, b, preferred_element_type=F32)


def _dot_nt(a, b):
    return lax.dot_general(a, b, (((1,), (1,)), ((), ())), preferred_element_type=F32)


def _layer_norm(z, g, b):
    mu = jnp.mean(z, axis=-1, keepdims=True)
    zc = z - mu
    var = jnp.mean(zc * zc, axis=-1, keepdims=True)
    return zc * lax.rsqrt(var + LN_EPS) * g + b


def _silu(x):
    return x / (1.0 + jnp.exp(-x))


def _params(*semantics):
    return pltpu.CompilerParams(dimension_semantics=semantics, vmem_limit_bytes=VMEM_LIMIT_BYTES)


def _pool_ln_kernel(x_ref, halo_ref, w_ref, scale_ref, g_ref, b_ref, o_ref, ext_ref):
    i = pl.program_id(1)
    rows = x_ref.shape[1]
    x = x_ref[0]
    ext_ref[0:POOL_HALO, :] = jnp.where(i > 0, halo_ref[0], 0.0)
    ext_ref[POOL_HALO:, :] = x
    pos = i * rows + lax.broadcasted_iota(jnp.int32, (rows, 1), 0)
    ys = []
    for gi, w in enumerate(POOL_WINDOWS):
        c0 = gi * POOL_GROUP_DIM
        xg = x[:, c0:c0 + POOL_GROUP_DIM]
        acc = xg
        for k in range(1, w):
            acc = acc + ext_ref[POOL_HALO - k:POOL_HALO - k + rows, c0:c0 + POOL_GROUP_DIM]
        cnt = jnp.minimum(pos + 1, w).astype(F32)
        diff = acc / cnt - xg
        ys.append(_dot(diff.astype(BF16), w_ref[gi]))
    h = jnp.concatenate(ys, axis=1) * scale_ref[...]
    o_ref[0] = _layer_norm(ALPHA * x + h, g_ref[...], b_ref[...])


def _pool_ln(x, w, scale, g, b):
    B, S, D = x.shape
    rows = POOL_ROWS
    halo_blocks = rows // POOL_HALO
    row2 = lambda v: v.reshape(1, D)
    return pl.pallas_call(
        _pool_ln_kernel,
        grid=(B, S // rows),
        in_specs=[
            pl.BlockSpec((1, rows, D), lambda bi, i: (bi, i, 0)),
            pl.BlockSpec((1, POOL_HALO, D), lambda bi, i: (bi, jnp.maximum(i * halo_blocks - 1, 0), 0)),
            pl.BlockSpec(w.shape, lambda bi, i: (0, 0, 0)),
            pl.BlockSpec((1, D), lambda bi, i: (0, 0)),
            pl.BlockSpec((1, D), lambda bi, i: (0, 0)),
            pl.BlockSpec((1, D), lambda bi, i: (0, 0)),
        ],
        out_specs=pl.BlockSpec((1, rows, D), lambda bi, i: (bi, i, 0)),
        out_shape=jax.ShapeDtypeStruct((B, S, D), F32),
        scratch_shapes=[pltpu.VMEM((rows + POOL_HALO, D), F32)],
        compiler_params=_params("parallel", "arbitrary"),
    )(x, x, w.astype(BF16), row2(scale), row2(g), row2(b))


def _ffn_ln_kernel(x_ref, wg_ref, wu_ref, wd_ref, g_ref, b_ref, o_ref, xb_ref, acc_ref):
    f = pl.program_id(1)

    @pl.when(f == 0)
    def _():
        xb_ref[...] = x_ref[...].astype(BF16)
        acc_ref[...] = jnp.zeros_like(acc_ref)

    xb = xb_ref[...]
    a = _silu(_dot(xb, wg_ref[...])) * _dot(xb, wu_ref[...])
    acc_ref[...] += _dot(a.astype(BF16), wd_ref[...])

    @pl.when(f == pl.num_programs(1) - 1)
    def _():
        o_ref[...] = _layer_norm(ALPHA * x_ref[...] + acc_ref[...], g_ref[...], b_ref[...])


def _ff_chunk(d_ff, target):
    best = LANES
    for c in range(LANES, target + 1, LANES):
        if d_ff % c == 0:
            best = c
    return best


def _ffn_ln(x, w_gu, w_down, g, b):
    T, D = x.shape
    d_ff = w_down.shape[0]
    fc = _ff_chunk(d_ff, 1536)
    nf = d_ff // fc
    rows = FFN_ROWS
    return pl.pallas_call(
        _ffn_ln_kernel,
        grid=(T // rows, nf),
        in_specs=[
            pl.BlockSpec((rows, D), lambda i, f: (i, 0)),
            pl.BlockSpec((D, fc), lambda i, f: (0, f)),
            pl.BlockSpec((D, fc), lambda i, f: (0, nf + f)),
            pl.BlockSpec((fc, D), lambda i, f: (f, 0)),
            pl.BlockSpec((1, D), lambda i, f: (0, 0)),
            pl.BlockSpec((1, D), lambda i, f: (0, 0)),
        ],
        out_specs=pl.BlockSpec((rows, D), lambda i, f: (i, 0)),
        out_shape=jax.ShapeDtypeStruct((T, D), F32),
        scratch_shapes=[pltpu.VMEM((rows, D), BF16), pltpu.VMEM((rows, D), F32)],
        compiler_params=_params("parallel", "arbitrary"),
    )(x, w_gu, w_gu, w_down, g.reshape(1, D), b.reshape(1, D))


def _rope_tables(pos, reps):
    freqs = jnp.power(ROPE_THETA, -jnp.arange(HALF_DIM, dtype=F32) / HALF_DIM)
    ang = pos.astype(F32)[:, None] * freqs[None, :]
    cos, sin = jnp.cos(ang), jnp.sin(ang)
    return (jnp.tile(jnp.concatenate([cos, cos], axis=1), (1, reps)),
            jnp.tile(jnp.concatenate([-sin, sin], axis=1), (1, reps)))


def _in_proj_kernel(x_ref, w_ref, cos_ref, sin_ref,
                    q_ref, kcv_ref, ks_ref, vs_ref, kw_ref, vw_ref, gate_ref, *, steps_per_seq):
    rows = x_ref.shape[0]
    xb = x_ref[...].astype(BF16)
    cos = cos_ref[...]
    sin = sin_ref[...]
    lane = lax.broadcasted_iota(jnp.int32, cos.shape, 1)
    first_half = (lane & (HEAD_DIM - 1)) < HALF_DIM
    seq_step = pl.program_id(0) % steps_per_seq
    pos = seq_step * rows + lax.broadcasted_iota(jnp.int32, (rows, HEAD_DIM), 0)
    col = lax.broadcasted_iota(jnp.int32, (rows, HEAD_DIM), 1)
    blk_onehot = jnp.where((pos >> SEL_SHIFT) == col, 1.0, 0.0)
    ones_col = jnp.where(col == 0, 1.0, 0.0)
    zeros = jnp.zeros((rows, HEAD_DIM), F32)

    def col_tile(j):
        return _dot(xb, w_ref[:, j * PROJ_COLS:(j + 1) * PROJ_COLS])

    def rope(y):
        rot = jnp.where(first_half,
                        pltpu.roll(y, PROJ_COLS - HALF_DIM, 1),
                        pltpu.roll(y, HALF_DIM, 1))
        return y * cos + rot * sin

    def store_groups(ref, y, extra):
        for gi in range(N_KV_GROUPS):
            ref[0, gi, :, 0:HEAD_DIM] = y[:, gi * HEAD_DIM:(gi + 1) * HEAD_DIM].astype(ref.dtype)
            ref[0, gi, :, HEAD_DIM:] = extra.astype(ref.dtype)

    n_q = D_MODEL // PROJ_COLS
    for j in range(n_q):
        q_ref[:, j * PROJ_COLS:(j + 1) * PROJ_COLS] = (rope(col_tile(j)) * Q_SCALE).astype(q_ref.dtype)
    kcv_ref[:, 0:KV_DIM] = col_tile(n_q)
    kcv_ref[:, KV_DIM:2 * KV_DIM] = col_tile(n_q + 1)
    store_groups(ks_ref, rope(col_tile(n_q + 2)), blk_onehot)
    store_groups(vs_ref, col_tile(n_q + 3), ones_col)
    store_groups(kw_ref, rope(col_tile(n_q + 4)), zeros)
    store_groups(vw_ref, col_tile(n_q + 5), ones_col)
    logits = col_tile(n_q + 6)[:, 0:LANES]
    gate_ref[...] = 1.0 / (1.0 + jnp.exp(-logits))


def _in_proj(x, w_in, B, S):
    T, D = x.shape
    rows = PROJ_ROWS
    steps_per_seq = S // rows
    n_tiles = D_MODEL // PROJ_COLS + 2 * N_BRANCHES + 1
    w = jnp.pad(w_in, ((0, 0), (0, n_tiles * PROJ_COLS - w_in.shape[1]))).astype(BF16)
    cos, sin = _rope_tables(jnp.arange(S), PROJ_COLS // HEAD_DIM)
    kv_shape = jax.ShapeDtypeStruct((B, N_KV_GROUPS, S, AUG_DIM), BF16)
    kv_spec = pl.BlockSpec((1, N_KV_GROUPS, rows, AUG_DIM),
                           lambda i: (i // steps_per_seq, 0, i % steps_per_seq, 0))
    return pl.pallas_call(
        functools.partial(_in_proj_kernel, steps_per_seq=steps_per_seq),
        grid=(T // rows,),
        in_specs=[
            pl.BlockSpec((rows, D), lambda i: (i, 0)),
            pl.BlockSpec(w.shape, lambda i: (0, 0)),
            pl.BlockSpec((rows, PROJ_COLS), lambda i: (i % steps_per_seq, 0)),
            pl.BlockSpec((rows, PROJ_COLS), lambda i: (i % steps_per_seq, 0)),
        ],
        out_specs=[
            pl.BlockSpec((rows, D_MODEL), lambda i: (i, 0)),
            pl.BlockSpec((rows, 2 * KV_DIM), lambda i: (i, 0)),
            kv_spec, kv_spec, kv_spec, kv_spec,
            pl.BlockSpec((rows, LANES), lambda i: (i, 0)),
        ],
        out_shape=[
            jax.ShapeDtypeStruct((T, D_MODEL), BF16),
            jax.ShapeDtypeStruct((T, 2 * KV_DIM), F32),
            kv_shape, kv_shape, kv_shape, kv_shape,
            jax.ShapeDtypeStruct((T, LANES), F32),
        ],
        compiler_params=_params("parallel"),
    )(x, w, cos, sin)


def _compress_kernel(ch_ref, w1_ref, pe_ref, w2_ref, cos_ref, sin_ref, kc_ref, vc_ref):
    half = CMP_STRIDE * HEAD_DIM
    for which, o_ref in enumerate((kc_ref, vc_ref)):
        w1 = w1_ref[which]
        r = _dot(ch_ref[which, 0, 0], w1)
        pe = pe_ref[which]
        bias = (_dot(pe[:, 0:half], w1[:, 0:CMP_HIDDEN])
                + _dot(pe[:, half:2 * half], w1[:, CMP_HIDDEN:2 * CMP_HIDDEN]))[0:1, :]
        nxt = pltpu.roll(r[:, CMP_HIDDEN:2 * CMP_HIDDEN], r.shape[0] - 1, 0)
        pre = r[:, 0:CMP_HIDDEN] + nxt + bias
        o = _dot(_silu(pre).astype(BF16), w2_ref[which])
        if which == 0:
            rot = jnp.concatenate([o[:, HALF_DIM:], o[:, :HALF_DIM]], axis=1)
            o = o * cos_ref[...] + rot * sin_ref[...]
        o_ref[0, 0] = o.astype(o_ref.dtype)


def _compress(kcv, w1_k, w2_k, pe_k, w1_v, w2_v, pe_v, B, S):
    n_chunks = S // CMP_STRIDE
    half = CMP_STRIDE * HEAD_DIM
    ch = kcv.reshape(B, n_chunks, CMP_STRIDE, 2, N_KV_GROUPS, HEAD_DIM)
    ch = ch.transpose(3, 0, 4, 1, 2, 5).reshape(2, B, N_KV_GROUPS, n_chunks, half).astype(BF16)
    cat = lambda w1: jnp.concatenate([w1[:half], w1[half:]], axis=1)
    w1 = jnp.stack([cat(w1_k), cat(w1_v)]).astype(BF16)
    w2 = jnp.stack([w2_k, w2_v]).astype(BF16)
    pe = jnp.stack([pe_k.reshape(1, -1), pe_v.reshape(1, -1)])
    pe = jnp.broadcast_to(pe, (2, 8, 2 * half)).astype(BF16)
    cos, sin = _rope_tables(CMP_STRIDE * jnp.arange(n_chunks) + CMP_BLOCK - 1, 1)
    out_shape = jax.ShapeDtypeStruct((B, N_KV_GROUPS, n_chunks, HEAD_DIM), BF16)
    out_spec = pl.BlockSpec((1, 1, n_chunks, HEAD_DIM), lambda bi, gi: (bi, gi, 0, 0))
    const = lambda a: pl.BlockSpec(a.shape, lambda bi, gi: (0,) * a.ndim)
    return pl.pallas_call(
        _compress_kernel,
        grid=(B, N_KV_GROUPS),
        in_specs=[
            pl.BlockSpec((2, 1, 1, n_chunks, half), lambda bi, gi: (0, bi, gi, 0, 0)),
            const(w1), const(pe), const(w2), const(cos), const(sin),
        ],
        out_specs=[out_spec, out_spec],
        out_shape=[out_shape, out_shape],
        compiler_params=_params("parallel", "parallel"),
    )(ch, w1, pe, w2, cos, sin)


def _cmp_attn_kernel(q_ref, kc_ref, vc_ref, ovl_ref, oc_ref, sel_ref, score_ref):
    i = pl.program_id(2)
    rows = q_ref.shape[1]
    n_cmp = kc_ref.shape[2]
    n_sel = ovl_ref.shape[0]
    kc = kc_ref[0, 0]
    vc = vc_ref[0, 0]
    t = i * rows + lax.broadcasted_iota(jnp.int32, (rows, 1), 0)
    cend = CMP_STRIDE * lax.broadcasted_iota(jnp.int32, (1, n_cmp), 1) + (CMP_BLOCK - 1)
    cvalid = cend <= t
    any_valid = jnp.where(t >= CMP_BLOCK - 1, 1.0, 0.0)
    pc_sum = jnp.zeros((rows, n_cmp), F32)
    for h in range(HEADS_PER_GROUP):
        qh = q_ref[0, :, h * HEAD_DIM:(h + 1) * HEAD_DIM]
        s = jnp.where(cvalid, _dot_nt(qh, kc), NEG_INF)
        e = jnp.exp2(s - jnp.max(s, axis=-1, keepdims=True))
        pc = e / jnp.sum(e, axis=-1, keepdims=True) * any_valid
        oc_ref[0, :, h * HEAD_DIM:(h + 1) * HEAD_DIM] = _dot(pc.astype(BF16), vc)
        pc_sum = pc_sum + pc
    imp = _dot_nt(ovl_ref[...], pc_sum.astype(BF16))
    j = lax.broadcasted_iota(jnp.int32, (n_sel, 1), 0)
    blk_t = (i * rows + lax.broadcasted_iota(jnp.int32, (1, rows), 1)) >> SEL_SHIFT
    bvalid = j <= blk_t
    forced = (j == 0) | (j == blk_t) | (j == blk_t - 1)
    score = jnp.where(bvalid, imp + jnp.where(forced, FORCE_BONUS, 0.0), -1.0)
    score_ref[...] = score
    def rank_group(grp, rank):
        base = pl.multiple_of(grp * 8, 8)
        others = score_ref[pl.ds(base, 8), :]
        for r in range(8):
            other = others[r:r + 1, :]
            ge = jnp.where(other >= score, 1.0, 0.0)
            gt = jnp.where(other > score, 1.0, 0.0)
            rank = rank + jnp.where(j > base + r, ge, gt)
        return rank

    n_groups = jnp.minimum(((i + 1) * rows) // (8 * SEL_BLOCK), n_sel // 8)
    rank = lax.fori_loop(0, n_groups, rank_group, jnp.zeros((n_sel, rows), F32))
    selected = (rank < float(SEL_TOP_N)) & bvalid
    sel_ref[0, 0] = jnp.where(selected, 0.0, NEG_INF).T.astype(sel_ref.dtype)


def _cmp_attn(q, kc, vc, B, S):
    rows = CMP_Q_ROWS
    n_cmp = S // CMP_STRIDE
    n_sel = SEL_SLOTS
    assert S // SEL_BLOCK <= SEL_SLOTS
    cstart = CMP_STRIDE * jnp.arange(n_cmp)
    sstart = SEL_BLOCK * jnp.arange(n_sel)
    overlap = ((cstart[None, :] <= sstart[:, None] + SEL_BLOCK - 1)
               & (cstart[None, :] + CMP_BLOCK - 1 >= sstart[:, None])).astype(BF16)
    q3 = q.reshape(B, S, D_MODEL)
    kv_spec = pl.BlockSpec((1, 1, n_cmp, HEAD_DIM), lambda bi, gi, i: (bi, gi, 0, 0))
    return pl.pallas_call(
        _cmp_attn_kernel,
        grid=(B, N_KV_GROUPS, S // rows),
        in_specs=[
            pl.BlockSpec((1, rows, GROUP_Q_DIM), lambda bi, gi, i: (bi, i, gi)),
            kv_spec, kv_spec,
            pl.BlockSpec(overlap.shape, lambda bi, gi, i: (0, 0)),
        ],
        out_specs=[
            pl.BlockSpec((1, rows, GROUP_Q_DIM), lambda bi, gi, i: (bi, i, gi)),
            pl.BlockSpec((1, 1, rows, n_sel), lambda bi, gi, i: (bi, gi, i, 0)),
        ],
        out_shape=[
            jax.ShapeDtypeStruct((B, S, D_MODEL), F32),
            jax.ShapeDtypeStruct((B, N_KV_GROUPS, S, n_sel), BF16),
        ],
        scratch_shapes=[pltpu.VMEM((n_sel, rows), F32)],
        compiler_params=_params("parallel", "parallel", "arbitrary"),
    )(q3, kc, vc, overlap)


def _sel_win_attn_kernel(q_ref, sel_ref, ks_ref, vs_ref, kw_ref, vw_ref, diag_ref, band_ref, oc_ref, gate_ref,
                         o_ref, qaug_ref, m_ref, acc_ref, ow_ref):
    i = pl.program_id(2)
    tq = q_ref.shape[1]
    hg = HEADS_PER_GROUP
    half = ATT_HALF
    n_half = tq // half
    half_rows = hg * half

    for hq in range(n_half):
        for h in range(hg):
            r0 = (hq * hg + h) * half
            qaug_ref[r0:r0 + half, 0:HEAD_DIM] = q_ref[0, hq * half:(hq + 1) * half, h * HEAD_DIM:(h + 1) * HEAD_DIM]
            qaug_ref[r0:r0 + half, HEAD_DIM:] = sel_ref[0, 0, hq * half:(hq + 1) * half, :]
    qa = qaug_ref[...]

    def keys(ref, start, size):
        return ref[0, 0, pl.ds(pl.multiple_of(start, ATT_HALF), size), :]

    m_ref[...] = jnp.full(m_ref.shape, NEG_INF, F32)
    acc_ref[...] = jnp.zeros(acc_ref.shape, F32)

    def sel_step(c, bias):
        s = _dot_nt(qa, keys(ks_ref, c * ATT_K_ROWS, ATT_K_ROWS))
        if bias is not None:
            s = s + bias
        m_prev = m_ref[...]
        m_next = jnp.maximum(m_prev, jnp.max(s, axis=-1, keepdims=True))
        p = jnp.exp2(s - jnp.tile(m_next, (1, ATT_K_ROWS // LANES)))
        alpha = jnp.exp2(m_prev - m_next)
        acc_ref[...] = alpha * acc_ref[...] + _dot(p.astype(BF16), keys(vs_ref, c * ATT_K_ROWS, ATT_K_ROWS))
        m_ref[...] = m_next

    def sel_body(c, carry):
        sel_step(c, None)
        return carry

    lax.fori_loop(0, i, sel_body, 0)
    sel_step(i, diag_ref[...])

    def softmax_pv(s, v):
        p = jnp.exp2(s - jnp.max(s, axis=-1, keepdims=True))
        return _dot(p.astype(BF16), v)

    @pl.when(i == 0)
    def _():
        s = _dot_nt(qa, keys(kw_ref, 0, tq)) + diag_ref[...]
        ow_ref[...] = softmax_pv(s, keys(vw_ref, 0, tq))

    @pl.when(i > 0)
    def _():
        for hq in range(n_half):
            start = i * tq + hq * half - WINDOW
            qh = qaug_ref[hq * half_rows:(hq + 1) * half_rows, :]
            s = _dot_nt(qh, keys(kw_ref, start, WINDOW + half)) + band_ref[...]
            ow_ref[hq * half_rows:(hq + 1) * half_rows, :] = softmax_pv(s, keys(vw_ref, start, WINDOW + half))

    acc_s = acc_ref[...]
    acc_w = ow_ref[...]
    o_s = acc_s[:, 0:HEAD_DIM] / acc_s[:, HEAD_DIM:HEAD_DIM + 1]
    o_w = acc_w[:, 0:HEAD_DIM] / acc_w[:, HEAD_DIM:HEAD_DIM + 1]
    gate = gate_ref[0, 0]
    for hq in range(n_half):
        tok = slice(hq * half, (hq + 1) * half)
        for h in range(hg):
            r0 = (hq * hg + h) * half
            gc = gate[tok, N_BRANCHES * h + 0:N_BRANCHES * h + 1]
            gs = gate[tok, N_BRANCHES * h + 1:N_BRANCHES * h + 2]
            gw = gate[tok, N_BRANCHES * h + 2:N_BRANCHES * h + 3]
            o = (gc * oc_ref[0, tok, h * HEAD_DIM:(h + 1) * HEAD_DIM]
                 + gs * o_s[r0:r0 + half] + gw * o_w[r0:r0 + half])
            o_ref[0, tok, h * HEAD_DIM:(h + 1) * HEAD_DIM] = o.astype(o_ref.dtype)


def _sel_win_attn(q, sel, ks, vs, kw, vw, oc, gates, B, S):
    tq = ATT_Q_ROWS
    half = ATT_HALF
    n_sel = SEL_SLOTS
    hg = HEADS_PER_GROUP
    rows = hg * tq
    q3 = q.reshape(B, S, D_MODEL)
    gate_g = gates.reshape(B, S, LANES)[:, :, :N_GATES].reshape(B, S, N_KV_GROUPS, N_BRANCHES * hg)
    gate_g = gate_g.transpose(0, 2, 1, 3)
    t_rel = (jnp.arange(tq // half)[:, None, None] * half + jnp.arange(half)[None, None, :])
    t_rel = jnp.broadcast_to(t_rel, (tq // half, hg, half)).reshape(rows, 1)
    diag = jnp.where(jnp.arange(ATT_K_ROWS)[None, :] <= t_rel, 0.0, NEG_INF).astype(F32)
    tt = jnp.broadcast_to(jnp.arange(half)[None, :], (hg, half)).reshape(hg * half, 1)
    a = jnp.arange(WINDOW + half)[None, :]
    band = jnp.where((a > tt) & (a <= tt + WINDOW), 0.0, NEG_INF).astype(F32)
    kv_spec = pl.BlockSpec((1, 1, S, AUG_DIM), lambda bi, gi, i: (bi, gi, 0, 0))
    q_spec = pl.BlockSpec((1, tq, GROUP_Q_DIM), lambda bi, gi, i: (bi, i, gi))
    const = lambda arr: pl.BlockSpec(arr.shape, lambda bi, gi, i: (0, 0))
    return pl.pallas_call(
        _sel_win_attn_kernel,
        grid=(B, N_KV_GROUPS, S // tq),
        in_specs=[
            q_spec,
            pl.BlockSpec((1, 1, tq, n_sel), lambda bi, gi, i: (bi, gi, i, 0)),
            kv_spec, kv_spec, kv_spec, kv_spec,
            const(diag), const(band),
            q_spec,
            pl.BlockSpec((1, 1, tq, N_BRANCHES * hg), lambda bi, gi, i: (bi, gi, i, 0)),
        ],
        out_specs=q_spec,
        out_shape=jax.ShapeDtypeStruct((B, S, D_MODEL), BF16),
        scratch_shapes=[
            pltpu.VMEM((rows, AUG_DIM), BF16),
            pltpu.VMEM((rows, LANES), F32),
            pltpu.VMEM((rows, AUG_DIM), F32),
            pltpu.VMEM((rows, AUG_DIM), F32),
        ],
        compiler_params=_params("parallel", "parallel", "arbitrary"),
    )(q3, sel, ks, vs, kw, vw, diag, band, oc, gate_g)


def _out_proj_ln_kernel(o_ref, w_ref, x_ref, g_ref, b_ref, y_ref):
    h = _dot(o_ref[...], w_ref[...])
    y_ref[...] = _layer_norm(ALPHA * x_ref[...] + h, g_ref[...], b_ref[...])


def _out_proj_ln(o, w_out, x, g, b):
    T, D = x.shape
    rows = PROJ_ROWS
    return pl.pallas_call(
        _out_proj_ln_kernel,
        grid=(T // rows,),
        in_specs=[
            pl.BlockSpec((rows, D), lambda i: (i, 0)),
            pl.BlockSpec((D, D), lambda i: (0, 0)),
            pl.BlockSpec((rows, D), lambda i: (i, 0)),
            pl.BlockSpec((1, D), lambda i: (0, 0)),
            pl.BlockSpec((1, D), lambda i: (0, 0)),
        ],
        out_specs=pl.BlockSpec((rows, D), lambda i: (i, 0)),
        out_shape=jax.ShapeDtypeStruct((T, D), F32),
        compiler_params=_params("parallel"),
    )(o, w_out.astype(BF16), x, g.reshape(1, D), b.reshape(1, D))


def _route_kernel(x_ref, r_ref, tri_ref, xb_ref, info_ref, wcol_ref, cnt_ref, carry_ref):
    @pl.when(pl.program_id(0) == 0)
    def _():
        carry_ref[...] = jnp.zeros_like(carry_ref)

    x = x_ref[...]
    xb = x.astype(BF16)
    xb_ref[...] = xb
    x_lo = (x - xb.astype(F32)).astype(BF16)
    r_hi = r_ref[0]
    r_lo = r_ref[1]
    logits = _dot(xb, r_hi) + (_dot(x_lo, r_hi) + _dot(xb, r_lo))
    lane = lax.broadcasted_iota(jnp.int32, logits.shape, 1)
    logits = jnp.where(lane < N_EXPERTS, logits, -jnp.inf)
    v1 = jnp.max(logits, axis=-1, keepdims=True)
    i1 = jnp.min(jnp.where(logits == v1, lane, LANES), axis=-1, keepdims=True)
    rest = jnp.where(lane == i1, -jnp.inf, logits)
    v2 = jnp.max(rest, axis=-1, keepdims=True)
    i2 = jnp.min(jnp.where(rest == v2, lane, LANES), axis=-1, keepdims=True)
    e2 = jnp.exp(v2 - v1)
    w1 = 1.0 / (1.0 + e2)
    w2 = e2 / (1.0 + e2)
    m1 = jnp.where(lane == i1, 1.0, 0.0)
    m2 = jnp.where(lane == i2, 1.0, 0.0)
    routed = m1 + m2
    before = _dot(tri_ref[...], routed.astype(BF16)) + carry_ref[0:1, :]
    rank1 = jnp.sum(m1 * before, axis=-1, keepdims=True)
    rank2 = jnp.sum(m2 * before, axis=-1, keepdims=True)
    cnt = jnp.sum(routed, axis=0, keepdims=True)
    carry_ref[...] = carry_ref[...] + cnt
    cnt_ref[0] = jnp.broadcast_to(cnt, cnt_ref.shape[1:])
    info = jnp.where(lane == 0, i1.astype(F32),
                     jnp.where(lane == 1, i2.astype(F32),
                               jnp.where(lane == 2, rank1, jnp.where(lane == 3, rank2, 0.0))))
    info_ref[...] = info
    w1_hi = w1.astype(BF16).astype(F32)
    w2_hi = w2.astype(BF16).astype(F32)
    lo_half = (lane & 1) == 1
    wcol = (jnp.where((lane >> 1) == i1, jnp.where(lo_half, w1 - w1_hi, w1_hi), 0.0)
            + jnp.where((lane >> 1) == i2, jnp.where(lo_half, w2 - w2_hi, w2_hi), 0.0))
    wcol_ref[...] = wcol.astype(BF16)


def _route(x, router):
    T, D = x.shape
    rows = MOE_ROWS
    r = jnp.pad(router, ((0, 0), (0, LANES - router.shape[1])))
    r_hi = r.astype(BF16)
    r_lo = (r - r_hi.astype(F32)).astype(BF16)
    r2 = jnp.stack([r_hi, r_lo])
    tri = (jnp.arange(rows)[None, :] < jnp.arange(rows)[:, None]).astype(BF16)
    return pl.pallas_call(
        _route_kernel,
        grid=(T // rows,),
        in_specs=[
            pl.BlockSpec((rows, D), lambda i: (i, 0)),
            pl.BlockSpec(r2.shape, lambda i: (0, 0, 0)),
            pl.BlockSpec(tri.shape, lambda i: (0, 0)),
        ],
        out_specs=[
            pl.BlockSpec((rows, D), lambda i: (i, 0)),
            pl.BlockSpec((rows, LANES), lambda i: (i, 0)),
            pl.BlockSpec((rows, LANES), lambda i: (i, 0)),
            pl.BlockSpec((1, 8, LANES), lambda i: (i, 0, 0)),
        ],
        out_shape=[
            jax.ShapeDtypeStruct((T, D), BF16),
            jax.ShapeDtypeStruct((T, LANES), F32),
            jax.ShapeDtypeStruct((T, LANES), BF16),
            jax.ShapeDtypeStruct((T // rows, 8, LANES), F32),
        ],
        scratch_shapes=[pltpu.VMEM((8, LANES), F32)],
        compiler_params=_params("arbitrary"),
    )(x, r2, tri)


def _moe_tables(cnt, n_blocks_max, n_items_max):
    R = MOE_ROWS
    C, E = cnt.shape
    i32 = jnp.int32
    count_le = lambda sorted_v, q: jnp.sum(sorted_v[None, :] <= q[:, None], axis=1).astype(i32)
    cum = jnp.concatenate([jnp.zeros((1, E), i32), jnp.cumsum(cnt, axis=0)], axis=0)
    tot = cum[-1]
    nb = (tot + R - 1) // R
    nb_end = jnp.cumsum(nb)
    blk_start = nb_end - nb
    n_blocks = nb_end[-1]
    b = jnp.minimum(jnp.arange(n_blocks_max, dtype=i32), n_blocks - 1)
    bexp = jnp.minimum(count_le(nb_end, b), E - 1)
    bvalid = jnp.arange(n_blocks_max, dtype=i32) < n_blocks
    lb = b - blk_start[bexp]
    rho0 = lb * R
    rho1 = jnp.minimum((lb + 1) * R, tot[bexp]) - 1
    cum_b = cum[1:, :][:, bexp].T
    lo = jnp.minimum(jnp.sum(cum_b <= rho0[:, None], axis=1).astype(i32), C - 1)
    hi = jnp.minimum(jnp.sum(cum_b <= rho1[:, None], axis=1).astype(i32), C - 1)
    nit = jnp.where(bvalid, hi - lo + 1, 0)
    it_end = jnp.cumsum(nit)
    it_start = it_end - nit
    n_items = it_end[-1]
    i = jnp.arange(n_items_max, dtype=i32)
    ic = jnp.minimum(i, n_items - 1)
    d_blk = jnp.minimum(count_le(it_end, ic), n_blocks_max - 1)
    d_chk = lo[d_blk] + ic - it_start[d_blk]
    d_valid = i < n_items
    d_first = d_valid & (ic == it_start[d_blk])
    d_exp = bexp[d_blk]
    d_ra = jnp.maximum(rho0[d_blk], cum[d_chk, d_exp]) - rho0[d_blk]
    d_rb = jnp.minimum(rho1[d_blk], cum[d_chk + 1, d_exp] - 1) - rho0[d_blk]
    d_sub_lo = jnp.where(d_valid & (d_rb >= d_ra), d_ra // DISPATCH_SUB, 1)
    d_sub_hi = jnp.where(d_valid & (d_rb >= d_ra), d_rb // DISPATCH_SUB, 0)
    bl_lo = (blk_start[None, :] + cum[:-1] // R).reshape(-1)
    bl_hi = (blk_start[None, :] + (cum[1:] - 1) // R).reshape(-1)
    npair = jnp.where(cnt.reshape(-1) > 0, bl_hi - bl_lo + 1, 0)
    p_end = jnp.cumsum(npair)
    p_start = p_end - npair
    n_items2 = p_end[-1]
    jc = jnp.minimum(i, n_items2 - 1)
    pair = jnp.minimum(count_le(p_end, jc), C * E - 1)
    c_blk = bl_lo[pair] + jc - p_start[pair]
    c_chk = pair // E
    c_valid = i < n_items2
    prev_chk = jnp.concatenate([jnp.full((1,), -1, i32), c_chk[:-1]])
    next_chk = jnp.concatenate([c_chk[1:], jnp.full((1,), -1, i32)])
    next_valid = jnp.concatenate([c_valid[1:], jnp.zeros((1,), bool)])
    c_first = c_valid & (c_chk != prev_chk)
    c_last = c_valid & ((c_chk != next_chk) | ~next_valid)
    c_exp = pair % E
    c_base = (c_blk - blk_start[c_exp]) * R
    c_ra = jnp.maximum(cum[c_chk, c_exp] - c_base, 0)
    c_rb = jnp.minimum(cum[c_chk + 1, c_exp] - 1 - c_base, R - 1)
    c_sub_lo = jnp.where(c_valid, c_ra // COMBINE_SUB, 1)
    c_sub_hi = jnp.where(c_valid, c_rb // COMBINE_SUB, 0)
    as_i32 = lambda v: v.astype(i32)
    return dict(base=blk_start * R, bexp=bexp, bvalid=as_i32(bvalid),
                d_blk=d_blk, d_chk=d_chk, d_first=as_i32(d_first), d_sub_lo=d_sub_lo, d_sub_hi=d_sub_hi,
                c_blk=c_blk, c_chk=c_chk, c_first=as_i32(c_first), c_last=as_i32(c_last),
                c_sub_lo=c_sub_lo, c_sub_hi=c_sub_hi)


def _dispatch_kernel(blk_ref, chk_ref, first_ref, lo_ref, hi_ref, xb_ref, pos_ref, wcol_ref,
                     xs_ref, ws_ref, acc_ref, wacc_ref):
    i = pl.program_id(0)
    R = xs_ref.shape[0]

    @pl.when(first_ref[i] == 1)
    def _():
        acc_ref[...] = jnp.zeros_like(acc_ref)
        wacc_ref[...] = jnp.zeros_like(wacc_ref)

    pos1 = pos_ref[0, 0:1, :]
    pos2 = pos_ref[0, 1:2, :]

    def sub_tile(k, carry):
        r0 = pl.multiple_of(k * DISPATCH_SUB, DISPATCH_SUB)
        rows = (blk_ref[i] * R + r0 + lax.broadcasted_iota(jnp.int32, (DISPATCH_SUB, 1), 0)).astype(F32)
        hit = jnp.where((pos1 - rows) * (pos2 - rows) == 0.0, 1.0, 0.0).astype(BF16)
        acc_ref[pl.ds(r0, DISPATCH_SUB), :] += _dot(hit, xb_ref[...])
        wacc_ref[pl.ds(r0, DISPATCH_SUB), :] += _dot(hit, wcol_ref[...])
        return carry

    lax.fori_loop(lo_ref[i], hi_ref[i] + 1, sub_tile, 0)
    xs_ref[...] = acc_ref[...].astype(xs_ref.dtype)
    ws_ref[...] = wacc_ref[...]


def _dispatch(xb, posrow, wcol, tab, n_blocks_max, n_items_max):
    T, D = xb.shape
    R = MOE_ROWS
    chunk_map = lambda i, blk, chk, first, lo, hi: (chk[i], 0)
    block_map = lambda i, blk, chk, first, lo, hi: (blk[i], 0)
    grid_spec = pltpu.PrefetchScalarGridSpec(
        num_scalar_prefetch=5,
        grid=(n_items_max,),
        in_specs=[
            pl.BlockSpec((R, D), chunk_map),
            pl.BlockSpec((1, 8, R), lambda i, blk, chk, first, lo, hi: (chk[i], 0, 0)),
            pl.BlockSpec((R, LANES), chunk_map),
        ],
        out_specs=[pl.BlockSpec((R, D), block_map), pl.BlockSpec((R, LANES), block_map)],
        scratch_shapes=[pltpu.VMEM((R, D), F32), pltpu.VMEM((R, LANES), F32)],
    )
    return pl.pallas_call(
        _dispatch_kernel,
        grid_spec=grid_spec,
        out_shape=[
            jax.ShapeDtypeStruct((n_blocks_max * R, D), BF16),
            jax.ShapeDtypeStruct((n_blocks_max * R, LANES), F32),
        ],
        compiler_params=_params("arbitrary"),
    )(tab["d_blk"], tab["d_chk"], tab["d_first"], tab["d_sub_lo"], tab["d_sub_hi"], xb, posrow, wcol)


def _expert_ffn_kernel(bexp_ref, bvalid_ref, xs_ref, ws_ref, wg_ref, wu_ref, wd_ref, ys_ref, acc_ref):
    b = pl.program_id(0)
    f = pl.program_id(1)

    @pl.when(f == 0)
    def _():
        acc_ref[...] = jnp.zeros_like(acc_ref)

    @pl.when(bvalid_ref[b] == 1)
    def _():
        xb = xs_ref[...]
        a = _silu(_dot(xb, wg_ref[0])) * _dot(xb, wu_ref[0])
        acc_ref[...] += _dot(a.astype(BF16), wd_ref[0])

    @pl.when(f == pl.num_programs(1) - 1)
    def _():
        ws = ws_ref[...]
        lane = lax.broadcasted_iota(jnp.int32, ws.shape, 1)
        w = jnp.sum(jnp.where((lane >> 1) == bexp_ref[b], ws, 0.0), axis=-1, keepdims=True)
        ys_ref[...] = (acc_ref[...] * w).astype(ys_ref.dtype)


def _expert_ffn(xs, ws, w_gu, w_down, tab, n_blocks_max):
    D = xs.shape[1]
    R = MOE_ROWS
    d_ff = w_down.shape[1]
    fc = _ff_chunk(d_ff, 1792)
    nf = d_ff // fc
    f_eff = lambda b, f, bvalid: f * bvalid[b] + (nf - 1) * (1 - bvalid[b])
    grid_spec = pltpu.PrefetchScalarGridSpec(
        num_scalar_prefetch=2,
        grid=(n_blocks_max, nf),
        in_specs=[
            pl.BlockSpec((R, D), lambda b, f, bexp, bvalid: (b, 0)),
            pl.BlockSpec((R, LANES), lambda b, f, bexp, bvalid: (b, 0)),
            pl.BlockSpec((1, D, fc), lambda b, f, bexp, bvalid: (bexp[b], 0, f_eff(b, f, bvalid))),
            pl.BlockSpec((1, D, fc), lambda b, f, bexp, bvalid: (bexp[b], 0, nf + f_eff(b, f, bvalid))),
            pl.BlockSpec((1, fc, D), lambda b, f, bexp, bvalid: (bexp[b], f_eff(b, f, bvalid), 0)),
        ],
        out_specs=pl.BlockSpec((R, D), lambda b, f, bexp, bvalid: (b, 0)),
        scratch_shapes=[pltpu.VMEM((R, D), F32)],
    )
    return pl.pallas_call(
        _expert_ffn_kernel,
        grid_spec=grid_spec,
        out_shape=jax.ShapeDtypeStruct(xs.shape, BF16),
        compiler_params=_params("parallel", "arbitrary"),
    )(tab["bexp"], tab["bvalid"], xs, ws, w_gu, w_gu, w_down)


def _combine_ln_kernel(chk_ref, blk_ref, first_ref, last_ref, lo_ref, hi_ref,
                       ys_ref, pc1_ref, pc2_ref, x_ref, g_ref, b_ref, o_ref, acc_ref):
    i = pl.program_id(0)
    R = ys_ref.shape[0]

    @pl.when(first_ref[i] == 1)
    def _():
        acc_ref[...] = jnp.zeros_like(acc_ref)

    def sub_tile(k, carry):
        r0 = pl.multiple_of(k * COMBINE_SUB, COMBINE_SUB)
        pc1 = pc1_ref[...]
        pc2 = pc2_ref[...]
        lane = lax.broadcasted_iota(jnp.int32, pc1.shape, 1)
        parts = []
        for t in range(COMBINE_SUB // LANES):
            col = (blk_ref[i] * R + r0 + t * LANES + lane).astype(F32)
            parts.append(jnp.where((pc1 - col) * (pc2 - col) == 0.0, 1.0, 0.0).astype(BF16))
        acc_ref[...] += _dot(jnp.concatenate(parts, axis=1), ys_ref[pl.ds(r0, COMBINE_SUB), :])
        return carry

    lax.fori_loop(lo_ref[i], hi_ref[i] + 1, sub_tile, 0)

    @pl.when(last_ref[i] == 1)
    def _():
        o_ref[...] = _layer_norm(ALPHA * x_ref[...] + acc_ref[...], g_ref[...], b_ref[...])


def _combine_ln(ys, poscol1, poscol2, x, g, b, tab, n_items_max):
    T, D = x.shape
    R = MOE_ROWS
    chunk_map = lambda i, chk, blk, first, last, lo, hi: (chk[i], 0)
    const_map = lambda i, chk, blk, first, last, lo, hi: (0, 0)
    grid_spec = pltpu.PrefetchScalarGridSpec(
        num_scalar_prefetch=6,
        grid=(n_items_max,),
        in_specs=[
            pl.BlockSpec((R, D), lambda i, chk, blk, first, last, lo, hi: (blk[i], 0)),
            pl.BlockSpec((R, LANES), chunk_map),
            pl.BlockSpec((R, LANES), chunk_map),
            pl.BlockSpec((R, D), chunk_map),
            pl.BlockSpec((1, D), const_map),
            pl.BlockSpec((1, D), const_map),
        ],
        out_specs=pl.BlockSpec((R, D), chunk_map),
        scratch_shapes=[pltpu.VMEM((R, D), F32)],
    )
    return pl.pallas_call(
        _combine_ln_kernel,
        grid_spec=grid_spec,
        out_shape=jax.ShapeDtypeStruct((T, D), F32),
        compiler_params=_params("arbitrary"),
    )(tab["c_chk"], tab["c_blk"], tab["c_first"], tab["c_last"], tab["c_sub_lo"], tab["c_sub_hi"],
      ys, poscol1, poscol2, x, g.reshape(1, D), b.reshape(1, D))


def _moe_ln(x, router, w_gu, w_down, g, b):
    T, D = x.shape
    R = MOE_ROWS
    n_exp = w_down.shape[0]
    n_chunks = T // R
    n_blocks_max = (2 * T) // R + n_exp
    n_items_max = n_blocks_max + n_exp * (n_chunks - 1)
    xb, info, wcol, cnt = _route(x, router)
    tab = _moe_tables(cnt[:, 0, :n_exp].astype(jnp.int32), n_blocks_max, n_items_max)
    e1 = info[:, 0].astype(jnp.int32)
    e2 = info[:, 1].astype(jnp.int32)
    pos1 = tab["base"][e1] + info[:, 2].astype(jnp.int32)
    pos2 = tab["base"][e2] + info[:, 3].astype(jnp.int32)
    pos1 = pos1.astype(F32)
    pos2 = pos2.astype(F32)
    posrow = jnp.stack([pos1.reshape(n_chunks, R), pos2.reshape(n_chunks, R)], axis=1)
    posrow = jnp.pad(posrow, ((0, 0), (0, 6), (0, 0)), constant_values=-1.0)
    poscol1 = jnp.broadcast_to(pos1[:, None], (T, LANES))
    poscol2 = jnp.broadcast_to(pos2[:, None], (T, LANES))
    xs, ws = _dispatch(xb, posrow, wcol, tab, n_blocks_max, n_items_max)
    ys = _expert_ffn(xs, ws, w_gu, w_down, tab, n_blocks_max)
    return _combine_ln(ys, poscol1, poscol2, x, g, b, tab, n_items_max)


def kernel(x, ln_g, ln_b, pool_w, pool_scale, nsa_w_in, nsa_pe_k, nsa_w1_k, nsa_w2_k, nsa_pe_v, nsa_w1_v,
           nsa_w2_v, nsa_w_out, ffn_w_gu, ffn_w_down, moe_router, moe_w_gu, moe_w_down):
    B, S, D = x.shape
    T = B * S
    xa = _pool_ln(x, pool_w[0], pool_scale[0], ln_g[0, 0], ln_b[0, 0]).reshape(T, D)
    x1 = _ffn_ln(xa, ffn_w_gu[0].astype(BF16), ffn_w_down[0].astype(BF16), ln_g[0, 1], ln_b[0, 1])
    q, kcv, ks, vs, kw, vw, gates = _in_proj(x1, nsa_w_in[0], B, S)
    kc, vc = _compress(kcv, nsa_w1_k[0], nsa_w2_k[0], nsa_pe_k[0], nsa_w1_v[0], nsa_w2_v[0], nsa_pe_v[0], B, S)
    oc, sel = _cmp_attn(q, kc, vc, B, S)
    o = _sel_win_attn(q, sel, ks, vs, kw, vw, oc, gates, B, S).reshape(T, D)
    x2 = _out_proj_ln(o, nsa_w_out[0], x1, ln_g[1, 0], ln_b[1, 0])
    y = _moe_ln(x2, moe_router[0], moe_w_gu[0].astype(BF16), moe_w_down[0].astype(BF16), ln_g[1, 1], ln_b[1, 1])
    return y.reshape(B, S, D)
```

```python
import functools

import jax
import jax.numpy as jnp
from jax import lax
from jax.experimental import pallas as pl
from jax.experimental.pallas import tpu as pltpu

D_MODEL = 1024
DEPTH = 2
POOL_WINDOWS = (2, 4, 8, 16)
POOL_GROUP_DIM = D_MODEL // len(POOL_WINDOWS)
POOL_HALO = 16
N_HEADS = 16
N_KV_GROUPS = 4
HEADS_PER_GROUP = N_HEADS // N_KV_GROUPS
HEAD_DIM = D_MODEL // N_HEADS
HALF_DIM = HEAD_DIM // 2
GROUP_Q_DIM = HEADS_PER_GROUP * HEAD_DIM
KV_DIM = N_KV_GROUPS * HEAD_DIM
N_BRANCHES = 3
N_GATES = N_BRANCHES * N_HEADS
CMP_STRIDE = 16
CMP_BLOCK = 2 * CMP_STRIDE
CMP_HIDDEN = 2 * HEAD_DIM
SEL_BLOCK = 64
SEL_SHIFT = SEL_BLOCK.bit_length() - 1
SEL_TOP_N = 16
WINDOW = 512
FORCE_BONUS = 1.0e3
NEG_INF = -1.0e30
ROPE_THETA = 10000.0
ATTN_SCALE = HEAD_DIM ** -0.5
LOG2E = 1.4426950408889634
Q_SCALE = ATTN_SCALE * LOG2E
AUG_DIM = 2 * HEAD_DIM
SEL_SLOTS = AUG_DIM - HEAD_DIM
VT_ROWS = HEAD_DIM + 16
GATE_SLOTS = 16
GATE_ROWS = N_KV_GROUPS * GATE_SLOTS
N_EXPERTS = 8
LN_EPS = 1e-5
ALPHA = (2 * DEPTH) ** 0.25

LANES = 128
VMEM_LIMIT_BYTES = 56 * 1024 * 1024

POOL_ROWS = 512
FFN_ROWS = 512
PROJ_ROWS = 512
PROJ_COLS = 256
CMP_Q_ROWS = 512
ATT_Q_ROWS = 512
ATT_K_ROWS = 512
ATT_HALF = 256
ATT_COLS = 512
ATT_AHEAD = 3
MOE_ROWS = 512
DISPATCH_SUB = 128
COMBINE_SUB = 256

F32 = jnp.float32
BF16 = jnp.bfloat16


def _dot(a, b):
    return jnp.dot(a, b, preferred_element_type=F32)


def _dot_nt(a, b):
    return lax.dot_general(a, b, (((1,), (1,)), ((), ())), preferred_element_type=F32)


def _layer_norm(z, g, b):
    mu = jnp.mean(z, axis=-1, keepdims=True)
    zc = z - mu
    var = jnp.mean(zc * zc, axis=-1, keepdims=True)
    return zc * lax.rsqrt(var + LN_EPS) * g + b


def _silu(x):
    return x / (1.0 + jnp.exp(-x))


def _params(*semantics):
    return pltpu.CompilerParams(dimension_semantics=semantics, vmem_limit_bytes=VMEM_LIMIT_BYTES)


def _pool_ln_kernel(x_ref, halo_ref, w_ref, scale_ref, g_ref, b_ref, o_ref, ext_ref):
    i = pl.program_id(1)
    rows = x_ref.shape[1]
    x = x_ref[0]
    ext_ref[0:POOL_HALO, :] = jnp.where(i > 0, halo_ref[0], 0.0)
    ext_ref[POOL_HALO:, :] = x
    pos = i * rows + lax.broadcasted_iota(jnp.int32, (rows, 1), 0)
    ys = []
    for gi, w in enumerate(POOL_WINDOWS):
        c0 = gi * POOL_GROUP_DIM
        xg = x[:, c0:c0 + POOL_GROUP_DIM]
        acc = xg
        for k in range(1, w):
            acc = acc + ext_ref[POOL_HALO - k:POOL_HALO - k + rows, c0:c0 + POOL_GROUP_DIM]
        cnt = jnp.minimum(pos + 1, w).astype(F32)
        diff = acc / cnt - xg
        ys.append(_dot(diff.astype(BF16), w_ref[gi]))
    h = jnp.concatenate(ys, axis=1) * scale_ref[...]
    o_ref[0] = _layer_norm(ALPHA * x + h, g_ref[...], b_ref[...])


def _pool_ln(x, w, scale, g, b):
    B, S, D = x.shape
    rows = POOL_ROWS
    halo_blocks = rows // POOL_HALO
    row2 = lambda v: v.reshape(1, D)
    return pl.pallas_call(
        _pool_ln_kernel,
        grid=(B, S // rows),
        in_specs=[
            pl.BlockSpec((1, rows, D), lambda bi, i: (bi, i, 0)),
            pl.BlockSpec((1, POOL_HALO, D), lambda bi, i: (bi, jnp.maximum(i * halo_blocks - 1, 0), 0)),
            pl.BlockSpec(w.shape, lambda bi, i: (0, 0, 0)),
            pl.BlockSpec((1, D), lambda bi, i: (0, 0)),
            pl.BlockSpec((1, D), lambda bi, i: (0, 0)),
            pl.BlockSpec((1, D), lambda bi, i: (0, 0)),
        ],
        out_specs=pl.BlockSpec((1, rows, D), lambda bi, i: (bi, i, 0)),
        out_shape=jax.ShapeDtypeStruct((B, S, D), F32),
        scratch_shapes=[pltpu.VMEM((rows + POOL_HALO, D), F32)],
        compiler_params=_params("parallel", "arbitrary"),
    )(x, x, w.astype(BF16), row2(scale), row2(g), row2(b))


def _ffn_ln_kernel(x_ref, wg_ref, wu_ref, wd_ref, g_ref, b_ref, o_ref, xb_ref, acc_ref):
    f = pl.program_id(1)

    @pl.when(f == 0)
    def _():
        xb_ref[...] = x_ref[...].astype(BF16)
        acc_ref[...] = jnp.zeros_like(acc_ref)

    xb = xb_ref[...]
    a = _silu(_dot(xb, wg_ref[...])) * _dot(xb, wu_ref[...])
    acc_ref[...] += _dot(a.astype(BF16), wd_ref[...])

    @pl.when(f == pl.num_programs(1) - 1)
    def _():
        o_ref[...] = _layer_norm(ALPHA * x_ref[...] + acc_ref[...], g_ref[...], b_ref[...])


def _ff_chunk(d_ff, target):
    best = LANES
    for c in range(LANES, target + 1, LANES):
        if d_ff % c == 0:
            best = c
    return best


def _ffn_ln(x, w_gu, w_down, g, b):
    T, D = x.shape
    d_ff = w_down.shape[0]
    fc = _ff_chunk(d_ff, 1536)
    nf = d_ff // fc
    rows = FFN_ROWS
    return pl.pallas_call(
        _ffn_ln_kernel,
        grid=(T // rows, nf),
        in_specs=[
            pl.BlockSpec((rows, D), lambda i, f: (i, 0)),
            pl.BlockSpec((D, fc), lambda i, f: (0, f)),
            pl.BlockSpec((D, fc), lambda i, f: (0, nf + f)),
            pl.BlockSpec((fc, D), lambda i, f: (f, 0)),
            pl.BlockSpec((1, D), lambda i, f: (0, 0)),
            pl.BlockSpec((1, D), lambda i, f: (0, 0)),
        ],
        out_specs=pl.BlockSpec((rows, D), lambda i, f: (i, 0)),
        out_shape=jax.ShapeDtypeStruct((T, D), F32),
        scratch_shapes=[pltpu.VMEM((rows, D), BF16), pltpu.VMEM((rows, D), F32)],
        compiler_params=_params("parallel", "arbitrary"),
    )(x, w_gu, w_gu, w_down, g.reshape(1, D), b.reshape(1, D))


def _rope_tables(pos, reps):
    freqs = jnp.power(ROPE_THETA, -jnp.arange(HALF_DIM, dtype=F32) / HALF_DIM)
    ang = pos.astype(F32)[:, None] * freqs[None, :]
    cos, sin = jnp.cos(ang), jnp.sin(ang)
    return (jnp.tile(jnp.concatenate([cos, cos], axis=1), (1, reps)),
            jnp.tile(jnp.concatenate([-sin, sin], axis=1), (1, reps)))


def _in_proj_kernel(x_ref, w_ref, wt_ref, cos_ref, sin_ref, cost_ref, sint_ref,
                    qt_ref, kcv_ref, ks_ref, vst_ref, kw_ref, vwt_ref, gatet_ref, *, steps_per_seq):
    rows = x_ref.shape[0]
    xb = x_ref[...].astype(BF16)
    cos = cos_ref[...]
    sin = sin_ref[...]
    lane = lax.broadcasted_iota(jnp.int32, cos.shape, 1)
    first_half = (lane & (HEAD_DIM - 1)) < HALF_DIM
    seq_step = pl.program_id(0) % steps_per_seq
    pos = seq_step * rows + lax.broadcasted_iota(jnp.int32, (rows, HEAD_DIM), 0)
    col = lax.broadcasted_iota(jnp.int32, (rows, HEAD_DIM), 1)
    blk_onehot = jnp.where((pos >> SEL_SHIFT) == col, 1.0, 0.0)
    zeros = jnp.zeros((rows, HEAD_DIM), F32)

    def col_tile(j):
        return _dot(xb, w_ref[:, j * PROJ_COLS:(j + 1) * PROJ_COLS])

    def row_tile(r0, n):
        return _dot_nt(wt_ref[r0:r0 + n, :], xb)

    def rope(y):
        rot = jnp.where(first_half,
                        pltpu.roll(y, PROJ_COLS - HALF_DIM, 1),
                        pltpu.roll(y, HALF_DIM, 1))
        return y * cos + rot * sin

    def rope_t(yt):
        pieces = []
        for h in range(yt.shape[0] // HEAD_DIM):
            pieces.append(yt[h * HEAD_DIM + HALF_DIM:(h + 1) * HEAD_DIM])
            pieces.append(yt[h * HEAD_DIM:h * HEAD_DIM + HALF_DIM])
        return yt * cost_ref[...] + jnp.concatenate(pieces, axis=0) * sint_ref[...]

    def store_keys(ref, y, extra):
        for gi in range(N_KV_GROUPS):
            ref[0, gi, :, 0:HEAD_DIM] = y[:, gi * HEAD_DIM:(gi + 1) * HEAD_DIM].astype(ref.dtype)
            ref[0, gi, :, HEAD_DIM:] = extra.astype(ref.dtype)

    def store_values_t(ref, yt):
        ones = jnp.ones((VT_ROWS - HEAD_DIM, rows), ref.dtype)
        for gi in range(N_KV_GROUPS):
            ref[0, gi, 0, 0:HEAD_DIM, :] = yt[gi * HEAD_DIM:(gi + 1) * HEAD_DIM].astype(ref.dtype)
            ref[0, gi, 0, HEAD_DIM:, :] = ones

    for j in range(D_MODEL // PROJ_COLS):
        qt = rope_t(row_tile(j * PROJ_COLS, PROJ_COLS)) * Q_SCALE
        qt_ref[0, j * PROJ_COLS:(j + 1) * PROJ_COLS, :] = qt.astype(qt_ref.dtype)
    kcv_ref[:, 0:KV_DIM] = col_tile(0)
    kcv_ref[:, KV_DIM:2 * KV_DIM] = col_tile(1)
    store_keys(ks_ref, rope(col_tile(2)), blk_onehot)
    store_keys(kw_ref, rope(col_tile(3)), zeros)
    store_values_t(vst_ref, row_tile(D_MODEL, KV_DIM))
    store_values_t(vwt_ref, row_tile(D_MODEL + KV_DIM, KV_DIM))
    logits_t = row_tile(D_MODEL + 2 * KV_DIM, GATE_ROWS)
    gatet_ref[0] = 1.0 / (1.0 + jnp.exp(-logits_t))


def _in_proj(x, w_in, B, S):
    T, D = x.shape
    rows = PROJ_ROWS
    assert rows == ATT_K_ROWS
    steps_per_seq = S // rows
    sec = lambda k: w_in[:, D_MODEL + k * KV_DIM:D_MODEL + (k + 1) * KV_DIM]
    w = jnp.concatenate([sec(0), sec(1), sec(2), sec(4)], axis=1).astype(BF16)
    wg = w_in[:, D_MODEL + 6 * KV_DIM:].reshape(D, N_KV_GROUPS, N_BRANCHES * HEADS_PER_GROUP)
    wg = jnp.pad(wg, ((0, 0), (0, 0), (0, GATE_SLOTS - N_BRANCHES * HEADS_PER_GROUP))).reshape(D, GATE_ROWS)
    wt = jnp.concatenate([w_in[:, :D_MODEL], sec(3), sec(5), wg], axis=1).T.astype(BF16)
    cos, sin = _rope_tables(jnp.arange(S), PROJ_COLS // HEAD_DIM)
    k_shape = jax.ShapeDtypeStruct((B, N_KV_GROUPS, S, AUG_DIM), BF16)
    k_spec = pl.BlockSpec((1, N_KV_GROUPS, rows, AUG_DIM),
                          lambda i: (i // steps_per_seq, 0, i % steps_per_seq, 0))
    vt_shape = jax.ShapeDtypeStruct((B, N_KV_GROUPS, steps_per_seq, VT_ROWS, rows), BF16)
    vt_spec = pl.BlockSpec((1, N_KV_GROUPS, 1, VT_ROWS, rows),
                           lambda i: (i // steps_per_seq, 0, i % steps_per_seq, 0, 0))
    tok_map = lambda i: (i, 0)
    seq_map = lambda i: (i % steps_per_seq, 0)
    feat_map = lambda i: (i // steps_per_seq, 0, i % steps_per_seq)
    return pl.pallas_call(
        functools.partial(_in_proj_kernel, steps_per_seq=steps_per_seq),
        grid=(T // rows,),
        in_specs=[
            pl.BlockSpec((rows, D), tok_map),
            pl.BlockSpec(w.shape, lambda i: (0, 0)),
            pl.BlockSpec(wt.shape, lambda i: (0, 0)),
            pl.BlockSpec((rows, PROJ_COLS), seq_map),
            pl.BlockSpec((rows, PROJ_COLS), seq_map),
            pl.BlockSpec((PROJ_COLS, rows), lambda i: (0, i % steps_per_seq)),
            pl.BlockSpec((PROJ_COLS, rows), lambda i: (0, i % steps_per_seq)),
        ],
        out_specs=[
            pl.BlockSpec((1, D_MODEL, rows), feat_map),
            pl.BlockSpec((rows, 2 * KV_DIM), tok_map),
            k_spec, vt_spec, k_spec, vt_spec,
            pl.BlockSpec((1, GATE_ROWS, rows), feat_map),
        ],
        out_shape=[
            jax.ShapeDtypeStruct((B, D_MODEL, S), BF16),
            jax.ShapeDtypeStruct((T, 2 * KV_DIM), F32),
            k_shape, vt_shape, k_shape, vt_shape,
            jax.ShapeDtypeStruct((B, GATE_ROWS, S), F32),
        ],
        compiler_params=_params("parallel"),
    )(x, w, wt, cos, sin, cos.T, sin.T)


def _compress_kernel(ch_ref, w1_ref, pe_ref, w2_ref, w2t_ref, cos_ref, sin_ref, kc_ref, vct_ref):
    half = CMP_STRIDE * HEAD_DIM
    hidden = []
    for which in range(2):
        w1 = w1_ref[which]
        r = _dot(ch_ref[which, 0, 0], w1)
        pe = pe_ref[which]
        bias = (_dot(pe[:, 0:half], w1[:, 0:CMP_HIDDEN])
                + _dot(pe[:, half:2 * half], w1[:, CMP_HIDDEN:2 * CMP_HIDDEN]))[0:1, :]
        nxt = pltpu.roll(r[:, CMP_HIDDEN:2 * CMP_HIDDEN], r.shape[0] - 1, 0)
        hidden.append(_silu(r[:, 0:CMP_HIDDEN] + nxt + bias).astype(BF16))
    kc = _dot(hidden[0], w2_ref[...])
    rot = jnp.concatenate([kc[:, HALF_DIM:], kc[:, :HALF_DIM]], axis=1)
    kc_ref[0, 0] = (kc * cos_ref[...] + rot * sin_ref[...]).astype(kc_ref.dtype)
    vct_ref[0, 0] = _dot_nt(w2t_ref[...], hidden[1]).astype(vct_ref.dtype)


def _compress(kcv, w1_k, w2_k, pe_k, w1_v, w2_v, pe_v, B, S):
    n_chunks = S // CMP_STRIDE
    half = CMP_STRIDE * HEAD_DIM
    ch = kcv.reshape(B, n_chunks, CMP_STRIDE, 2, N_KV_GROUPS, HEAD_DIM)
    ch = ch.transpose(3, 0, 4, 1, 2, 5).reshape(2, B, N_KV_GROUPS, n_chunks, half).astype(BF16)
    cat = lambda w1: jnp.concatenate([w1[:half], w1[half:]], axis=1)
    w1 = jnp.stack([cat(w1_k), cat(w1_v)]).astype(BF16)
    pe = jnp.stack([pe_k.reshape(1, -1), pe_v.reshape(1, -1)])
    pe = jnp.broadcast_to(pe, (2, 8, 2 * half)).astype(BF16)
    cos, sin = _rope_tables(CMP_STRIDE * jnp.arange(n_chunks) + CMP_BLOCK - 1, 1)
    const = lambda a: pl.BlockSpec(a.shape, lambda bi, gi: (0,) * a.ndim)
    w2k = w2_k.astype(BF16)
    w2vt = w2_v.T.astype(BF16)
    return pl.pallas_call(
        _compress_kernel,
        grid=(B, N_KV_GROUPS),
        in_specs=[
            pl.BlockSpec((2, 1, 1, n_chunks, half), lambda bi, gi: (0, bi, gi, 0, 0)),
            const(w1), const(pe), const(w2k), const(w2vt), const(cos), const(sin),
        ],
        out_specs=[pl.BlockSpec((1, 1, n_chunks, HEAD_DIM), lambda bi, gi: (bi, gi, 0, 0)),
                   pl.BlockSpec((1, 1, HEAD_DIM, n_chunks), lambda bi, gi: (bi, gi, 0, 0))],
        out_shape=[jax.ShapeDtypeStruct((B, N_KV_GROUPS, n_chunks, HEAD_DIM), BF16),
                   jax.ShapeDtypeStruct((B, N_KV_GROUPS, HEAD_DIM, n_chunks), BF16)],
        compiler_params=_params("parallel", "parallel"),
    )(ch, w1, pe, w2k, w2vt, cos, sin)


def _cmp_attn_kernel(qt_ref, kc_ref, vct_ref, ovl_ref, oct_ref, selt_ref, score_ref):
    i = pl.program_id(2)
    cols = qt_ref.shape[2]
    n_cmp = kc_ref.shape[2]
    n_sel = ovl_ref.shape[0]
    kc = kc_ref[0, 0]
    vct = vct_ref[0, 0]
    t = i * cols + lax.broadcasted_iota(jnp.int32, (1, cols), 1)
    cend = CMP_STRIDE * lax.broadcasted_iota(jnp.int32, (n_cmp, 1), 0) + (CMP_BLOCK - 1)
    cvalid = cend <= t
    any_valid = jnp.where(t >= CMP_BLOCK - 1, 1.0, 0.0)
    pc_sum = jnp.zeros((n_cmp, cols), F32)
    for h in range(HEADS_PER_GROUP):
        qh = qt_ref[0, h * HEAD_DIM:(h + 1) * HEAD_DIM, :]
        s = jnp.where(cvalid, _dot(kc, qh), NEG_INF)
        e = jnp.exp2(s - jnp.max(s, axis=0, keepdims=True))
        pc = e * (any_valid / jnp.sum(e, axis=0, keepdims=True))
        oct_ref[0, h * HEAD_DIM:(h + 1) * HEAD_DIM, :] = _dot(vct, pc.astype(BF16))
        pc_sum = pc_sum + pc
    imp = _dot(ovl_ref[...], pc_sum.astype(BF16))
    j = lax.broadcasted_iota(jnp.int32, (n_sel, 1), 0)
    blk_t = t >> SEL_SHIFT
    bvalid = j <= blk_t
    forced = (j == 0) | (j == blk_t) | (j == blk_t - 1)
    score = jnp.where(bvalid, imp + jnp.where(forced, FORCE_BONUS, 0.0), -1.0)
    score_ref[...] = score
    def rank_group(grp, rank):
        base = pl.multiple_of(grp * 8, 8)
        others = score_ref[pl.ds(base, 8), :]
        for r in range(8):
            other = others[r:r + 1, :]
            ge = jnp.where(other >= score, 1.0, 0.0)
            gt = jnp.where(other > score, 1.0, 0.0)
            rank = rank + jnp.where(j > base + r, ge, gt)
        return rank

    n_groups = jnp.minimum(((i + 1) * cols) // (8 * SEL_BLOCK), n_sel // 8)
    rank = lax.fori_loop(0, n_groups, rank_group, jnp.zeros((n_sel, cols), F32))
    selected = (rank < float(SEL_TOP_N)) & bvalid
    selt_ref[0, 0] = jnp.where(selected, 0.0, NEG_INF).astype(selt_ref.dtype)


def _cmp_attn(qt, kc, vct, B, S):
    cols = CMP_Q_ROWS
    n_cmp = S // CMP_STRIDE
    n_sel = SEL_SLOTS
    assert S // SEL_BLOCK <= SEL_SLOTS
    cstart = CMP_STRIDE * jnp.arange(n_cmp)
    sstart = SEL_BLOCK * jnp.arange(n_sel)
    overlap = ((cstart[None, :] <= sstart[:, None] + SEL_BLOCK - 1)
               & (cstart[None, :] + CMP_BLOCK - 1 >= sstart[:, None])).astype(BF16)
    q_spec = pl.BlockSpec((1, GROUP_Q_DIM, cols), lambda bi, gi, i: (bi, gi, i))
    return pl.pallas_call(
        _cmp_attn_kernel,
        grid=(B, N_KV_GROUPS, S // cols),
        in_specs=[
            q_spec,
            pl.BlockSpec((1, 1, n_cmp, HEAD_DIM), lambda bi, gi, i: (bi, gi, 0, 0)),
            pl.BlockSpec((1, 1, HEAD_DIM, n_cmp), lambda bi, gi, i: (bi, gi, 0, 0)),
            pl.BlockSpec(overlap.shape, lambda bi, gi, i: (0, 0)),
        ],
        out_specs=[
            q_spec,
            pl.BlockSpec((1, 1, n_sel, cols), lambda bi, gi, i: (bi, gi, 0, i)),
        ],
        out_shape=[
            jax.ShapeDtypeStruct((B, D_MODEL, S), F32),
            jax.ShapeDtypeStruct((B, N_KV_GROUPS, n_sel, S), BF16),
        ],
        scratch_shapes=[pltpu.VMEM((n_sel, cols), F32)],
        compiler_params=_params("parallel", "parallel", "arbitrary"),
    )(qt, kc, vct, overlap)


def _sel_win_attn_kernel(qt_ref, selt_ref, ks_ref, vst_ref, kw_ref, vwt_ref, diag_ref, band_ref, oct_ref, gatet_ref,
                         o_ref, qaug_ref, m_ref, acc_ref, ow_ref):
    i = pl.program_id(2)
    tq = qt_ref.shape[2]
    hg = HEADS_PER_GROUP
    half = ATT_HALF
    n_half = tq // half
    half_cols = hg * half

    for hq in range(n_half):
        for h in range(hg):
            c0 = (hq * hg + h) * half
            qaug_ref[0:HEAD_DIM, c0:c0 + half] = qt_ref[0, h * HEAD_DIM:(h + 1) * HEAD_DIM, hq * half:(hq + 1) * half]
            qaug_ref[HEAD_DIM:, c0:c0 + half] = selt_ref[0, 0, :, hq * half:(hq + 1) * half]
    n_tiles = (hg * tq) // ATT_COLS

    def keys(ref, start, size):
        return ref[0, 0, pl.ds(pl.multiple_of(start, ATT_HALF), size), :]

    m_ref[...] = jnp.full(m_ref.shape, NEG_INF, F32)
    acc_ref[...] = jnp.zeros(acc_ref.shape, F32)

    def sel_step(c, bias_ref):
        k = keys(ks_ref, c * ATT_K_ROWS, ATT_K_ROWS)
        v = vst_ref[0, 0, c]
        new_m, new_acc = [], []
        tile = lambda ct: slice(ct * ATT_COLS, (ct + 1) * ATT_COLS)
        scores = [_dot(k, qaug_ref[:, tile(ct)]) for ct in range(ATT_AHEAD)]
        for ct in range(n_tiles):
            cs = tile(ct)
            if ct + ATT_AHEAD < n_tiles:
                scores.append(_dot(k, qaug_ref[:, tile(ct + ATT_AHEAD)]))
            s = scores[ct]
            if bias_ref is not None:
                s = s + bias_ref[:, cs]
            m_prev = m_ref[:, cs]
            m_next = jnp.maximum(m_prev, jnp.max(s, axis=0, keepdims=True))
            p = jnp.exp2(s - m_next)
            alpha = jnp.exp2(m_prev - m_next)
            new_acc.append(alpha * acc_ref[:, cs] + _dot(v, p.astype(BF16)))
            new_m.append(m_next)
        for ct in range(n_tiles):
            cs = slice(ct * ATT_COLS, (ct + 1) * ATT_COLS)
            acc_ref[:, cs] = new_acc[ct]
            m_ref[:, cs] = new_m[ct]

    def sel_body(c, carry):
        sel_step(c, None)
        return carry

    lax.fori_loop(0, i, sel_body, 0)
    sel_step(i, diag_ref)

    def probs(s):
        return jnp.exp2(s - jnp.max(s, axis=0, keepdims=True)).astype(BF16)

    @pl.when(i == 0)
    def _():
        k = keys(kw_ref, 0, tq)
        v = vwt_ref[0, 0, 0]
        for ct in range(n_tiles):
            cs = slice(ct * ATT_COLS, (ct + 1) * ATT_COLS)
            ow_ref[:, cs] = _dot(v, probs(_dot(k, qaug_ref[:, cs]) + diag_ref[:, cs]))

    @pl.when(i > 0)
    def _():
        v_prev = vwt_ref[0, 0, i - 1]
        v_here = vwt_ref[0, 0, i]
        for hq in range(n_half):
            k = keys(kw_ref, i * tq + hq * half - WINDOW, WINDOW + half)
            n_prev = tq - hq * half
            for ct in range(half_cols // ATT_COLS):
                cs = slice(hq * half_cols + ct * ATT_COLS, hq * half_cols + (ct + 1) * ATT_COLS)
                p = probs(_dot(k, qaug_ref[:, cs]) + band_ref[:, ct * ATT_COLS:(ct + 1) * ATT_COLS])
                ow_ref[:, cs] = (_dot(v_prev[:, tq - n_prev:], p[0:n_prev])
                                 + _dot(v_here[:, 0:WINDOW + half - n_prev], p[n_prev:]))

    acc_s = acc_ref[...]
    acc_w = ow_ref[...]
    o_s = acc_s[0:HEAD_DIM] / acc_s[HEAD_DIM:HEAD_DIM + 1]
    o_w = acc_w[0:HEAD_DIM] / acc_w[HEAD_DIM:HEAD_DIM + 1]
    for hq in range(n_half):
        tok = slice(hq * half, (hq + 1) * half)
        for h in range(hg):
            c0 = (hq * hg + h) * half
            gc = gatet_ref[0, N_BRANCHES * h + 0:N_BRANCHES * h + 1, tok]
            gs = gatet_ref[0, N_BRANCHES * h + 1:N_BRANCHES * h + 2, tok]
            gw = gatet_ref[0, N_BRANCHES * h + 2:N_BRANCHES * h + 3, tok]
            o = (gc * oct_ref[0, h * HEAD_DIM:(h + 1) * HEAD_DIM, tok]
                 + gs * o_s[:, c0:c0 + half] + gw * o_w[:, c0:c0 + half])
            o_ref[0, tok, h * HEAD_DIM:(h + 1) * HEAD_DIM] = o.T.astype(o_ref.dtype)


def _sel_win_attn(qt, selt, ks, vst, kw, vwt, oct, gatet, B, S):
    tq = ATT_Q_ROWS
    half = ATT_HALF
    hg = HEADS_PER_GROUP
    cols = hg * tq
    n_chunks = S // ATT_K_ROWS
    t_rel = (jnp.arange(tq // half)[:, None, None] * half + jnp.arange(half)[None, None, :])
    t_rel = jnp.broadcast_to(t_rel, (tq // half, hg, half)).reshape(1, cols)
    diag = jnp.where(jnp.arange(ATT_K_ROWS)[:, None] <= t_rel, 0.0, NEG_INF).astype(F32)
    tt = jnp.broadcast_to(jnp.arange(half)[None, :], (hg, half)).reshape(1, hg * half)
    a = jnp.arange(WINDOW + half)[:, None]
    band = jnp.where((a > tt) & (a <= tt + WINDOW), 0.0, NEG_INF).astype(F32)
    k_spec = pl.BlockSpec((1, 1, S, AUG_DIM), lambda bi, gi, i: (bi, gi, 0, 0))
    vt_spec = pl.BlockSpec((1, 1, n_chunks, VT_ROWS, ATT_K_ROWS), lambda bi, gi, i: (bi, gi, 0, 0, 0))
    q_spec = pl.BlockSpec((1, GROUP_Q_DIM, tq), lambda bi, gi, i: (bi, gi, i))
    const = lambda arr: pl.BlockSpec(arr.shape, lambda bi, gi, i: (0, 0))
    return pl.pallas_call(
        _sel_win_attn_kernel,
        grid=(B, N_KV_GROUPS, S // tq),
        in_specs=[
            q_spec,
            pl.BlockSpec((1, 1, SEL_SLOTS, tq), lambda bi, gi, i: (bi, gi, 0, i)),
            k_spec, vt_spec, k_spec, vt_spec,
            const(diag), const(band),
            q_spec,
            pl.BlockSpec((1, GATE_SLOTS, tq), lambda bi, gi, i: (bi, gi, i)),
        ],
        out_specs=pl.BlockSpec((1, tq, GROUP_Q_DIM), lambda bi, gi, i: (bi, i, gi)),
        out_shape=jax.ShapeDtypeStruct((B, S, D_MODEL), BF16),
        scratch_shapes=[
            pltpu.VMEM((AUG_DIM, cols), BF16),
            pltpu.VMEM((1, cols), F32),
            pltpu.VMEM((VT_ROWS, cols), F32),
            pltpu.VMEM((VT_ROWS, cols), F32),
        ],
        compiler_params=_params("parallel", "parallel", "arbitrary"),
    )(qt, selt, ks, vst, kw, vwt, diag, band, oct, gatet)


def _out_proj_ln_kernel(o_ref, w_ref, x_ref, g_ref, b_ref, y_ref):
    h = _dot(o_ref[...], w_ref[...])
    y_ref[...] = _layer_norm(ALPHA * x_ref[...] + h, g_ref[...], b_ref[...])


def _out_proj_ln(o, w_out, x, g, b):
    T, D = x.shape
    rows = PROJ_ROWS
    return pl.pallas_call(
        _out_proj_ln_kernel,
        grid=(T // rows,),
        in_specs=[
            pl.BlockSpec((rows, D), lambda i: (i, 0)),
            pl.BlockSpec((D, D), lambda i: (0, 0)),
            pl.BlockSpec((rows, D), lambda i: (i, 0)),
            pl.BlockSpec((1, D), lambda i: (0, 0)),
            pl.BlockSpec((1, D), lambda i: (0, 0)),
        ],
        out_specs=pl.BlockSpec((rows, D), lambda i: (i, 0)),
        out_shape=jax.ShapeDtypeStruct((T, D), F32),
        compiler_params=_params("parallel"),
    )(o, w_out.astype(BF16), x, g.reshape(1, D), b.reshape(1, D))


def _route_kernel(x_ref, r_ref, tri_ref, xb_ref, info_ref, wcol_ref, cnt_ref, carry_ref):
    @pl.when(pl.program_id(0) == 0)
    def _():
        carry_ref[...] = jnp.zeros_like(carry_ref)

    x = x_ref[...]
    xb = x.astype(BF16)
    xb_ref[...] = xb
    x_lo = (x - xb.astype(F32)).astype(BF16)
    r_hi = r_ref[0]
    r_lo = r_ref[1]
    logits = _dot(xb, r_hi) + (_dot(x_lo, r_hi) + _dot(xb, r_lo))
    lane = lax.broadcasted_iota(jnp.int32, logits.shape, 1)
    logits = jnp.where(lane < N_EXPERTS, logits, -jnp.inf)
    v1 = jnp.max(logits, axis=-1, keepdims=True)
    i1 = jnp.min(jnp.where(logits == v1, lane, LANES), axis=-1, keepdims=True)
    rest = jnp.where(lane == i1, -jnp.inf, logits)
    v2 = jnp.max(rest, axis=-1, keepdims=True)
    i2 = jnp.min(jnp.where(rest == v2, lane, LANES), axis=-1, keepdims=True)
    e2 = jnp.exp(v2 - v1)
    w1 = 1.0 / (1.0 + e2)
    w2 = e2 / (1.0 + e2)
    m1 = jnp.where(lane == i1, 1.0, 0.0)
    m2 = jnp.where(lane == i2, 1.0, 0.0)
    routed = m1 + m2
    before = _dot(tri_ref[...], routed.astype(BF16)) + carry_ref[0:1, :]
    rank1 = jnp.sum(m1 * before, axis=-1, keepdims=True)
    rank2 = jnp.sum(m2 * before, axis=-1, keepdims=True)
    cnt = jnp.sum(routed, axis=0, keepdims=True)
    carry_ref[...] = carry_ref[...] + cnt
    cnt_ref[0] = jnp.broadcast_to(cnt, cnt_ref.shape[1:])
    info = jnp.where(lane == 0, i1.astype(F32),
                     jnp.where(lane == 1, i2.astype(F32),
                               jnp.where(lane == 2, rank1, jnp.where(lane == 3, rank2, 0.0))))
    info_ref[...] = info
    w1_hi = w1.astype(BF16).astype(F32)
    w2_hi = w2.astype(BF16).astype(F32)
    lo_half = (lane & 1) == 1
    wcol = (jnp.where((lane >> 1) == i1, jnp.where(lo_half, w1 - w1_hi, w1_hi), 0.0)
            + jnp.where((lane >> 1) == i2, jnp.where(lo_half, w2 - w2_hi, w2_hi), 0.0))
    wcol_ref[...] = wcol.astype(BF16)


def _route(x, router):
    T, D = x.shape
    rows = MOE_ROWS
    r = jnp.pad(router, ((0, 0), (0, LANES - router.shape[1])))
    r_hi = r.astype(BF16)
    r_lo = (r - r_hi.astype(F32)).astype(BF16)
    r2 = jnp.stack([r_hi, r_lo])
    tri = (jnp.arange(rows)[None, :] < jnp.arange(rows)[:, None]).astype(BF16)
    return pl.pallas_call(
        _route_kernel,
        grid=(T // rows,),
        in_specs=[
            pl.BlockSpec((rows, D), lambda i: (i, 0)),
            pl.BlockSpec(r2.shape, lambda i: (0, 0, 0)),
            pl.BlockSpec(tri.shape, lambda i: (0, 0)),
        ],
        out_specs=[
            pl.BlockSpec((rows, D), lambda i: (i, 0)),
            pl.BlockSpec((rows, LANES), lambda i: (i, 0)),
            pl.BlockSpec((rows, LANES), lambda i: (i, 0)),
            pl.BlockSpec((1, 8, LANES), lambda i: (i, 0, 0)),
        ],
        out_shape=[
            jax.ShapeDtypeStruct((T, D), BF16),
            jax.ShapeDtypeStruct((T, LANES), F32),
            jax.ShapeDtypeStruct((T, LANES), BF16),
            jax.ShapeDtypeStruct((T // rows, 8, LANES), F32),
        ],
        scratch_shapes=[pltpu.VMEM((8, LANES), F32)],
        compiler_params=_params("arbitrary"),
    )(x, r2, tri)


def _moe_tables(cnt, n_blocks_max, n_items_max):
    R = MOE_ROWS
    C, E = cnt.shape
    i32 = jnp.int32
    count_le = lambda sorted_v, q: jnp.sum(sorted_v[None, :] <= q[:, None], axis=1).astype(i32)
    cum = jnp.concatenate([jnp.zeros((1, E), i32), jnp.cumsum(cnt, axis=0)], axis=0)
    tot = cum[-1]
    nb = (tot + R - 1) // R
    nb_end = jnp.cumsum(nb)
    blk_start = nb_end - nb
    n_blocks = nb_end[-1]
    b = jnp.minimum(jnp.arange(n_blocks_max, dtype=i32), n_blocks - 1)
    bexp = jnp.minimum(count_le(nb_end, b), E - 1)
    bvalid = jnp.arange(n_blocks_max, dtype=i32) < n_blocks
    lb = b - blk_start[bexp]
    rho0 = lb * R
    rho1 = jnp.minimum((lb + 1) * R, tot[bexp]) - 1
    cum_b = cum[1:, :][:, bexp].T
    lo = jnp.minimum(jnp.sum(cum_b <= rho0[:, None], axis=1).astype(i32), C - 1)
    hi = jnp.minimum(jnp.sum(cum_b <= rho1[:, None], axis=1).astype(i32), C - 1)
    nit = jnp.where(bvalid, hi - lo + 1, 0)
    it_end = jnp.cumsum(nit)
    it_start = it_end - nit
    n_items = it_end[-1]
    i = jnp.arange(n_items_max, dtype=i32)
    ic = jnp.minimum(i, n_items - 1)
    d_blk = jnp.minimum(count_le(it_end, ic), n_blocks_max - 1)
    d_chk = lo[d_blk] + ic - it_start[d_blk]
    d_valid = i < n_items
    d_first = d_valid & (ic == it_start[d_blk])
    d_last = d_valid & (ic == it_end[d_blk] - 1)
    d_exp = bexp[d_blk]
    d_ra = jnp.maximum(rho0[d_blk], cum[d_chk, d_exp]) - rho0[d_blk]
    d_rb = jnp.minimum(rho1[d_blk], cum[d_chk + 1, d_exp] - 1) - rho0[d_blk]
    d_sub_lo = jnp.where(d_valid & (d_rb >= d_ra), d_ra // DISPATCH_SUB, 1)
    d_sub_hi = jnp.where(d_valid & (d_rb >= d_ra), d_rb // DISPATCH_SUB, 0)
    bl_lo = (blk_start[None, :] + cum[:-1] // R).reshape(-1)
    bl_hi = (blk_start[None, :] + (cum[1:] - 1) // R).reshape(-1)
    npair = jnp.where(cnt.reshape(-1) > 0, bl_hi - bl_lo + 1, 0)
    p_end = jnp.cumsum(npair)
    p_start = p_end - npair
    n_items2 = p_end[-1]
    jc = jnp.minimum(i, n_items2 - 1)
    pair = jnp.minimum(count_le(p_end, jc), C * E - 1)
    c_blk = bl_lo[pair] + jc - p_start[pair]
    c_chk = pair // E
    c_valid = i < n_items2
    prev_chk = jnp.concatenate([jnp.full((1,), -1, i32), c_chk[:-1]])
    next_chk = jnp.concatenate([c_chk[1:], jnp.full((1,), -1, i32)])
    next_valid = jnp.concatenate([c_valid[1:], jnp.zeros((1,), bool)])
    c_first = c_valid & (c_chk != prev_chk)
    c_last = c_valid & ((c_chk != next_chk) | ~next_valid)
    c_exp = pair % E
    c_base = (c_blk - blk_start[c_exp]) * R
    c_ra = jnp.maximum(cum[c_chk, c_exp] - c_base, 0)
    c_rb = jnp.minimum(cum[c_chk + 1, c_exp] - 1 - c_base, R - 1)
    c_sub_lo = jnp.where(c_valid, c_ra // COMBINE_SUB, 1)
    c_sub_hi = jnp.where(c_valid, c_rb // COMBINE_SUB, 0)
    as_i32 = lambda v: v.astype(i32)
    return dict(base=blk_start * R, bexp=bexp, bvalid=as_i32(bvalid),
                d_blk=d_blk, d_chk=d_chk, d_first=as_i32(d_first), d_last=as_i32(d_last), d_sub_lo=d_sub_lo, d_sub_hi=d_sub_hi,
                c_blk=c_blk, c_chk=c_chk, c_first=as_i32(c_first), c_last=as_i32(c_last),
                c_sub_lo=c_sub_lo, c_sub_hi=c_sub_hi)


def _dispatch_kernel(blk_ref, chk_ref, first_ref, last_ref, lo_ref, hi_ref, xb_ref, pos_ref, wcol_ref,
                     xs_ref, ws_ref, acc_ref, wacc_ref):
    i = pl.program_id(0)
    R = xs_ref.shape[0]

    @pl.when(first_ref[i] == 1)
    def _():
        acc_ref[...] = jnp.zeros_like(acc_ref)
        wacc_ref[...] = jnp.zeros_like(wacc_ref)

    pos1 = pos_ref[0, 0:1, :]
    pos2 = pos_ref[0, 1:2, :]

    def sub_tile(k, carry):
        r0 = pl.multiple_of(k * DISPATCH_SUB, DISPATCH_SUB)
        rows = (blk_ref[i] * R + r0 + lax.broadcasted_iota(jnp.int32, (DISPATCH_SUB, 1), 0)).astype(F32)
        hit = jnp.where((pos1 - rows) * (pos2 - rows) == 0.0, 1.0, 0.0).astype(BF16)
        acc_ref[pl.ds(r0, DISPATCH_SUB), :] += _dot(hit, xb_ref[...])
        wacc_ref[pl.ds(r0, DISPATCH_SUB), :] += _dot(hit, wcol_ref[...])
        return carry

    lax.fori_loop(lo_ref[i], hi_ref[i] + 1, sub_tile, 0)

    @pl.when(last_ref[i] == 1)
    def _():
        xs_ref[...] = acc_ref[...].astype(xs_ref.dtype)
        ws_ref[...] = wacc_ref[...]


def _dispatch(xb, posrow, wcol, tab, n_blocks_max, n_items_max):
    T, D = xb.shape
    R = MOE_ROWS
    chunk_map = lambda i, blk, chk, first, last, lo, hi: (chk[i], 0)
    block_map = lambda i, blk, chk, first, last, lo, hi: (blk[i], 0)
    grid_spec = pltpu.PrefetchScalarGridSpec(
        num_scalar_prefetch=6,
        grid=(n_items_max,),
        in_specs=[
            pl.BlockSpec((R, D), chunk_map),
            pl.BlockSpec((1, 8, R), lambda i, blk, chk, first, last, lo, hi: (chk[i], 0, 0)),
            pl.BlockSpec((R, LANES), chunk_map),
        ],
        out_specs=[pl.BlockSpec((R, D), block_map), pl.BlockSpec((R, LANES), block_map)],
        scratch_shapes=[pltpu.VMEM((R, D), F32), pltpu.VMEM((R, LANES), F32)],
    )
    return pl.pallas_call(
        _dispatch_kernel,
        grid_spec=grid_spec,
        out_shape=[
            jax.ShapeDtypeStruct((n_blocks_max * R, D), BF16),
            jax.ShapeDtypeStruct((n_blocks_max * R, LANES), F32),
        ],
        compiler_params=_params("arbitrary"),
    )(tab["d_blk"], tab["d_chk"], tab["d_first"], tab["d_last"], tab["d_sub_lo"], tab["d_sub_hi"], xb, posrow, wcol)


def _expert_ffn_kernel(bexp_ref, bvalid_ref, xs_ref, ws_ref, wg_ref, wu_ref, wd_ref, ys_ref, acc_ref):
    b = pl.program_id(0)
    f = pl.program_id(1)

    @pl.when(f == 0)
    def _():
        acc_ref[...] = jnp.zeros_like(acc_ref)

    @pl.when(bvalid_ref[b] == 1)
    def _():
        xb = xs_ref[...]
        a = _silu(_dot(xb, wg_ref[0])) * _dot(xb, wu_ref[0])
        acc_ref[...] += _dot(a.astype(BF16), wd_ref[0])

    @pl.when(f == pl.num_programs(1) - 1)
    def _():
        ws = ws_ref[...]
        lane = lax.broadcasted_iota(jnp.int32, ws.shape, 1)
        w = jnp.sum(jnp.where((lane >> 1) == bexp_ref[b], ws, 0.0), axis=-1, keepdims=True)
        ys_ref[...] = (acc_ref[...] * w).astype(ys_ref.dtype)


def _expert_ffn(xs, ws, w_gu, w_down, tab, n_blocks_max):
    D = xs.shape[1]
    R = MOE_ROWS
    d_ff = w_down.shape[1]
    fc = _ff_chunk(d_ff, 1792)
    nf = d_ff // fc
    f_eff = lambda b, f, bvalid: f * bvalid[b] + (nf - 1) * (1 - bvalid[b])
    grid_spec = pltpu.PrefetchScalarGridSpec(
        num_scalar_prefetch=2,
        grid=(n_blocks_max, nf),
        in_specs=[
            pl.BlockSpec((R, D), lambda b, f, bexp, bvalid: (b, 0)),
            pl.BlockSpec((R, LANES), lambda b, f, bexp, bvalid: (b, 0)),
            pl.BlockSpec((1, D, fc), lambda b, f, bexp, bvalid: (bexp[b], 0, f_eff(b, f, bvalid))),
            pl.BlockSpec((1, D, fc), lambda b, f, bexp, bvalid: (bexp[b], 0, nf + f_eff(b, f, bvalid))),
            pl.BlockSpec((1, fc, D), lambda b, f, bexp, bvalid: (bexp[b], f_eff(b, f, bvalid), 0)),
        ],
        out_specs=pl.BlockSpec((R, D), lambda b, f, bexp, bvalid: (b, 0)),
        scratch_shapes=[pltpu.VMEM((R, D), F32)],
    )
    return pl.pallas_call(
        _expert_ffn_kernel,
        grid_spec=grid_spec,
        out_shape=jax.ShapeDtypeStruct(xs.shape, BF16),
        compiler_params=_params("parallel", "arbitrary"),
    )(tab["bexp"], tab["bvalid"], xs, ws, w_gu, w_gu, w_down)


def _combine_ln_kernel(chk_ref, blk_ref, first_ref, last_ref, lo_ref, hi_ref,
                       ys_ref, pc1_ref, pc2_ref, x_ref, g_ref, b_ref, o_ref, acc_ref):
    i = pl.program_id(0)
    R = ys_ref.shape[0]

    @pl.when(first_ref[i] == 1)
    def _():
        acc_ref[...] = jnp.zeros_like(acc_ref)

    def sub_tile(k, carry):
        r0 = pl.multiple_of(k * COMBINE_SUB, COMBINE_SUB)
        pc1 = pc1_ref[...]
        pc2 = pc2_ref[...]
        lane = lax.broadcasted_iota(jnp.int32, pc1.shape, 1)
        parts = []
        for t in range(COMBINE_SUB // LANES):
            col = (blk_ref[i] * R + r0 + t * LANES + lane).astype(F32)
            parts.append(jnp.where((pc1 - col) * (pc2 - col) == 0.0, 1.0, 0.0).astype(BF16))
        acc_ref[...] += _dot(jnp.concatenate(parts, axis=1), ys_ref[pl.ds(r0, COMBINE_SUB), :])
        return carry

    lax.fori_loop(lo_ref[i], hi_ref[i] + 1, sub_tile, 0)

    @pl.when(last_ref[i] == 1)
    def _():
        o_ref[...] = _layer_norm(ALPHA * x_ref[...] + acc_ref[...], g_ref[...], b_ref[...])


def _combine_ln(ys, poscol1, poscol2, x, g, b, tab, n_items_max):
    T, D = x.shape
    R = MOE_ROWS
    chunk_map = lambda i, chk, blk, first, last, lo, hi: (chk[i], 0)
    const_map = lambda i, chk, blk, first, last, lo, hi: (0, 0)
    grid_spec = pltpu.PrefetchScalarGridSpec(
        num_scalar_prefetch=6,
        grid=(n_items_max,),
        in_specs=[
            pl.BlockSpec((R, D), lambda i, chk, blk, first, last, lo, hi: (blk[i], 0)),
            pl.BlockSpec((R, LANES), chunk_map),
            pl.BlockSpec((R, LANES), chunk_map),
            pl.BlockSpec((R, D), chunk_map),
            pl.BlockSpec((1, D), const_map),
            pl.BlockSpec((1, D), const_map),
        ],
        out_specs=pl.BlockSpec((R, D), chunk_map),
        scratch_shapes=[pltpu.VMEM((R, D), F32)],
    )
    return pl.pallas_call(
        _combine_ln_kernel,
        grid_spec=grid_spec,
        out_shape=jax.ShapeDtypeStruct((T, D), F32),
        compiler_params=_params("arbitrary"),
    )(tab["c_chk"], tab["c_blk"], tab["c_first"], tab["c_last"], tab["c_sub_lo"], tab["c_sub_hi"],
      ys, poscol1, poscol2, x, g.reshape(1, D), b.reshape(1, D))


def _moe_ln(x, router, w_gu, w_down, g, b):
    T, D = x.shape
    R = MOE_ROWS
    n_exp = w_down.shape[0]
    n_chunks = T // R
    n_blocks_max = (2 * T) // R + n_exp
    n_items_max = n_blocks_max + n_exp * (n_chunks - 1)
    xb, info, wcol, cnt = _route(x, router)
    tab = _moe_tables(cnt[:, 0, :n_exp].astype(jnp.int32), n_blocks_max, n_items_max)
    e1 = info[:, 0].astype(jnp.int32)
    e2 = info[:, 1].astype(jnp.int32)
    pos1 = tab["base"][e1] + info[:, 2].astype(jnp.int32)
    pos2 = tab["base"][e2] + info[:, 3].astype(jnp.int32)
    pos1 = pos1.astype(F32)
    pos2 = pos2.astype(F32)
    posrow = jnp.stack([pos1.reshape(n_chunks, R), pos2.reshape(n_chunks, R)], axis=1)
    posrow = jnp.pad(posrow, ((0, 0), (0, 6), (0, 0)), constant_values=-1.0)
    poscol1 = jnp.broadcast_to(pos1[:, None], (T, LANES))
    poscol2 = jnp.broadcast_to(pos2[:, None], (T, LANES))
    xs, ws = _dispatch(xb, posrow, wcol, tab, n_blocks_max, n_items_max)
    ys = _expert_ffn(xs, ws, w_gu, w_down, tab, n_blocks_max)
    return _combine_ln(ys, poscol1, poscol2, x, g, b, tab, n_items_max)


def kernel(x, ln_g, ln_b, pool_w, pool_scale, nsa_w_in, nsa_pe_k, nsa_w1_k, nsa_w2_k, nsa_pe_v, nsa_w1_v,
           nsa_w2_v, nsa_w_out, ffn_w_gu, ffn_w_down, moe_router, moe_w_gu, moe_w_down):
    B, S, D = x.shape
    T = B * S
    xa = _pool_ln(x, pool_w[0], pool_scale[0], ln_g[0, 0], ln_b[0, 0]).reshape(T, D)
    x1 = _ffn_ln(xa, ffn_w_gu[0].astype(BF16), ffn_w_down[0].astype(BF16), ln_g[0, 1], ln_b[0, 1])
    qt, kcv, ks, vst, kw, vwt, gatet = _in_proj(x1, nsa_w_in[0], B, S)
    kc, vct = _compress(kcv, nsa_w1_k[0], nsa_w2_k[0], nsa_pe_k[0], nsa_w1_v[0], nsa_w2_v[0], nsa_pe_v[0], B, S)
    oct, selt = _cmp_attn(qt, kc, vct, B, S)
    o = _sel_win_attn(qt, selt, ks, vst, kw, vwt, oct, gatet, B, S).reshape(T, D)
    x2 = _out_proj_ln(o, nsa_w_out[0], x1, ln_g[1, 0], ln_b[1, 0])
    y = _moe_ln(x2, moe_router[0], moe_w_gu[0].astype(BF16), moe_w_down[0].astype(BF16), ln_g[1, 1], ln_b[1, 1])
    return y.reshape(B, S, D)
```

```python
import functools

import jax
import jax.numpy as jnp
from jax import lax
from jax.experimental import pallas as pl
from jax.experimental.pallas import tpu as pltpu

D_MODEL = 1024
DEPTH = 2
POOL_WINDOWS = (2, 4, 8, 16)
POOL_GROUP_DIM = D_MODEL // len(POOL_WINDOWS)
POOL_HALO = 16
N_HEADS = 16
N_KV_GROUPS = 4
HEADS_PER_GROUP = N_HEADS // N_KV_GROUPS
HEAD_DIM = D_MODEL // N_HEADS
HALF_DIM = HEAD_DIM // 2
GROUP_Q_DIM = HEADS_PER_GROUP * HEAD_DIM
KV_DIM = N_KV_GROUPS * HEAD_DIM
N_BRANCHES = 3
N_GATES = N_BRANCHES * N_HEADS
CMP_STRIDE = 16
CMP_BLOCK = 2 * CMP_STRIDE
CMP_HIDDEN = 2 * HEAD_DIM
SEL_BLOCK = 64
SEL_SHIFT = SEL_BLOCK.bit_length() - 1
SEL_TOP_N = 16
WINDOW = 512
FORCE_BONUS = 1.0e3
NEG_INF = -1.0e30
ROPE_THETA = 10000.0
ATTN_SCALE = HEAD_DIM ** -0.5
LOG2E = 1.4426950408889634
Q_SCALE = ATTN_SCALE * LOG2E
AUG_DIM = 2 * HEAD_DIM
SEL_SLOTS = AUG_DIM - HEAD_DIM
VT_ROWS = HEAD_DIM + 16
GATE_SLOTS = 16
GATE_ROWS = N_KV_GROUPS * GATE_SLOTS
N_EXPERTS = 8
LN_EPS = 1e-5
ALPHA = (2 * DEPTH) ** 0.25

LANES = 128
VMEM_LIMIT_BYTES = 56 * 1024 * 1024

POOL_ROWS = 512
FFN_ROWS = 512
PROJ_ROWS = 512
PROJ_COLS = 256
CMP_Q_ROWS = 512
ATT_Q_ROWS = 512
ATT_K_ROWS = 512
ATT_HALF = 256
ATT_COLS = 512
ATT_AHEAD = 3
MOE_ROWS = 512
DISPATCH_SUB = 128
COMBINE_SUB = 256

F32 = jnp.float32
BF16 = jnp.bfloat16


def _dot(a, b):
    return jnp.dot(a, b, preferred_element_type=F32)


def _dot_nt(a, b):
    return lax.dot_general(a, b, (((1,), (1,)), ((), ())), preferred_element_type=F32)


def _layer_norm(z, g, b):
    mu = jnp.mean(z, axis=-1, keepdims=True)
    zc = z - mu
    var = jnp.mean(zc * zc, axis=-1, keepdims=True)
    return zc * lax.rsqrt(var + LN_EPS) * g + b


def _silu(x):
    return x / (1.0 + jnp.exp(-x))


def _params(*semantics):
    return pltpu.CompilerParams(dimension_semantics=semantics, vmem_limit_bytes=VMEM_LIMIT_BYTES)


def _pool_ln_kernel(x_ref, halo_ref, w_ref, scale_ref, g_ref, b_ref, o_ref, ext_ref):
    i = pl.program_id(1)
    rows = x_ref.shape[1]
    x = x_ref[0]
    ext_ref[0:POOL_HALO, :] = jnp.where(i > 0, halo_ref[0], 0.0)
    ext_ref[POOL_HALO:, :] = x
    pos = i * rows + lax.broadcasted_iota(jnp.int32, (rows, 1), 0)
    ys = []
    run = ext_ref[...]
    span = 1
    for gi, w in enumerate(POOL_WINDOWS):
        while span < w:
            run = run + pltpu.roll(run, span, 0)
            span *= 2
        xg = x[:, gi * POOL_GROUP_DIM:(gi + 1) * POOL_GROUP_DIM]
        cnt = jnp.minimum(pos + 1, w).astype(F32)
        diff = run[POOL_HALO:, 0:POOL_GROUP_DIM] / cnt - xg
        ys.append(_dot(diff.astype(BF16), w_ref[gi]))
        run = run[:, POOL_GROUP_DIM:]
    h = jnp.concatenate(ys, axis=1) * scale_ref[...]
    o_ref[0] = _layer_norm(ALPHA * x + h, g_ref[...], b_ref[...])


def _pool_ln(x, w, scale, g, b):
    B, S, D = x.shape
    rows = POOL_ROWS
    assert all(a < b_ for a, b_ in zip(POOL_WINDOWS, POOL_WINDOWS[1:]))
    assert all(w_ & (w_ - 1) == 0 for w_ in POOL_WINDOWS) and POOL_WINDOWS[-1] <= POOL_HALO
    halo_blocks = rows // POOL_HALO
    row2 = lambda v: v.reshape(1, D)
    return pl.pallas_call(
        _pool_ln_kernel,
        grid=(B, S // rows),
        in_specs=[
            pl.BlockSpec((1, rows, D), lambda bi, i: (bi, i, 0)),
            pl.BlockSpec((1, POOL_HALO, D), lambda bi, i: (bi, jnp.maximum(i * halo_blocks - 1, 0), 0)),
            pl.BlockSpec(w.shape, lambda bi, i: (0, 0, 0)),
            pl.BlockSpec((1, D), lambda bi, i: (0, 0)),
            pl.BlockSpec((1, D), lambda bi, i: (0, 0)),
            pl.BlockSpec((1, D), lambda bi, i: (0, 0)),
        ],
        out_specs=pl.BlockSpec((1, rows, D), lambda bi, i: (bi, i, 0)),
        out_shape=jax.ShapeDtypeStruct((B, S, D), F32),
        scratch_shapes=[pltpu.VMEM((rows + POOL_HALO, D), F32)],
        compiler_params=_params("parallel", "arbitrary"),
    )(x, x, w.astype(BF16), row2(scale), row2(g), row2(b))


def _ffn_ln_kernel(x_ref, wgu_ref, wd_ref, g_ref, b_ref, o_ref, xb_ref, acc_ref):
    f = pl.program_id(1)
    fc = wd_ref.shape[0]

    @pl.when(f == 0)
    def _():
        xb_ref[...] = x_ref[...].astype(BF16)
        acc_ref[...] = jnp.zeros_like(acc_ref)

    gu = _dot(xb_ref[...], wgu_ref[...])
    a = _silu(gu[:, 0:fc]) * gu[:, fc:]
    acc_ref[...] += _dot(a.astype(BF16), wd_ref[...])

    @pl.when(f == pl.num_programs(1) - 1)
    def _():
        o_ref[...] = _layer_norm(ALPHA * x_ref[...] + acc_ref[...], g_ref[...], b_ref[...])


def _ff_chunk(d_ff, target):
    best = LANES
    for c in range(LANES, target + 1, LANES):
        if d_ff % c == 0:
            best = c
    return best


def _ffn_ln(x, w_gu, w_down, g, b):
    T, D = x.shape
    d_ff = w_down.shape[0]
    fc = _ff_chunk(d_ff, 1536)
    nf = d_ff // fc
    rows = FFN_ROWS
    w_gu = w_gu.reshape(D, 2, nf, fc).transpose(0, 2, 1, 3).reshape(D, 2 * d_ff)
    return pl.pallas_call(
        _ffn_ln_kernel,
        grid=(T // rows, nf),
        in_specs=[
            pl.BlockSpec((rows, D), lambda i, f: (i, 0)),
            pl.BlockSpec((D, 2 * fc), lambda i, f: (0, f)),
            pl.BlockSpec((fc, D), lambda i, f: (f, 0)),
            pl.BlockSpec((1, D), lambda i, f: (0, 0)),
            pl.BlockSpec((1, D), lambda i, f: (0, 0)),
        ],
        out_specs=pl.BlockSpec((rows, D), lambda i, f: (i, 0)),
        out_shape=jax.ShapeDtypeStruct((T, D), F32),
        scratch_shapes=[pltpu.VMEM((rows, D), BF16), pltpu.VMEM((rows, D), F32)],
        compiler_params=_params("parallel", "arbitrary"),
    )(x, w_gu, w_down, g.reshape(1, D), b.reshape(1, D))


def _rope_tables(pos, reps):
    freqs = jnp.power(ROPE_THETA, -jnp.arange(HALF_DIM, dtype=F32) / HALF_DIM)
    ang = pos.astype(F32)[:, None] * freqs[None, :]
    cos, sin = jnp.cos(ang), jnp.sin(ang)
    return (jnp.tile(jnp.concatenate([cos, cos], axis=1), (1, reps)),
            jnp.tile(jnp.concatenate([-sin, sin], axis=1), (1, reps)))


def _in_proj_kernel(x_ref, w_ref, wt_ref, cos_ref, sin_ref, cost_ref, sint_ref,
                    qt_ref, kcv_ref, ks_ref, vst_ref, kw_ref, vwt_ref, gatet_ref, *, steps_per_seq):
    rows = x_ref.shape[0]
    xb = x_ref[...].astype(BF16)
    cos = cos_ref[...]
    sin = sin_ref[...]
    lane = lax.broadcasted_iota(jnp.int32, cos.shape, 1)
    first_half = (lane & (HEAD_DIM - 1)) < HALF_DIM
    seq_step = pl.program_id(0) % steps_per_seq
    pos = seq_step * rows + lax.broadcasted_iota(jnp.int32, (rows, HEAD_DIM), 0)
    col = lax.broadcasted_iota(jnp.int32, (rows, HEAD_DIM), 1)
    blk_onehot = jnp.where((pos >> SEL_SHIFT) == col, 1.0, 0.0)
    zeros = jnp.zeros((rows, HEAD_DIM), F32)

    def col_tile(j):
        return _dot(xb, w_ref[:, j * PROJ_COLS:(j + 1) * PROJ_COLS])

    def row_tile(r0, n):
        return _dot_nt(wt_ref[r0:r0 + n, :], xb)

    def rope(y):
        rot = jnp.where(first_half,
                        pltpu.roll(y, PROJ_COLS - HALF_DIM, 1),
                        pltpu.roll(y, HALF_DIM, 1))
        return y * cos + rot * sin

    def rope_t(yt):
        pieces = []
        for h in range(yt.shape[0] // HEAD_DIM):
            pieces.append(yt[h * HEAD_DIM + HALF_DIM:(h + 1) * HEAD_DIM])
            pieces.append(yt[h * HEAD_DIM:h * HEAD_DIM + HALF_DIM])
        return yt * cost_ref[...] + jnp.concatenate(pieces, axis=0) * sint_ref[...]

    def store_keys(ref, y, extra):
        for gi in range(N_KV_GROUPS):
            ref[0, gi, :, 0:HEAD_DIM] = y[:, gi * HEAD_DIM:(gi + 1) * HEAD_DIM].astype(ref.dtype)
            ref[0, gi, :, HEAD_DIM:] = extra.astype(ref.dtype)

    def store_values_t(ref, yt):
        ones = jnp.ones((VT_ROWS - HEAD_DIM, rows), ref.dtype)
        for gi in range(N_KV_GROUPS):
            ref[0, gi, 0, 0:HEAD_DIM, :] = yt[gi * HEAD_DIM:(gi + 1) * HEAD_DIM].astype(ref.dtype)
            ref[0, gi, 0, HEAD_DIM:, :] = ones

    for j in range(D_MODEL // PROJ_COLS):
        qt = rope_t(row_tile(j * PROJ_COLS, PROJ_COLS)) * Q_SCALE
        qt_ref[0, j * PROJ_COLS:(j + 1) * PROJ_COLS, :] = qt.astype(qt_ref.dtype)
    raw_k = col_tile(0)
    raw_v = col_tile(1)
    for gi in range(N_KV_GROUPS):
        kcv_ref[0, gi, :, 0:HEAD_DIM] = raw_k[:, gi * HEAD_DIM:(gi + 1) * HEAD_DIM]
        kcv_ref[0, gi, :, HEAD_DIM:] = raw_v[:, gi * HEAD_DIM:(gi + 1) * HEAD_DIM]
    store_keys(ks_ref, rope(col_tile(2)), blk_onehot)
    store_keys(kw_ref, rope(col_tile(3)), zeros)
    store_values_t(vst_ref, row_tile(D_MODEL, KV_DIM))
    store_values_t(vwt_ref, row_tile(D_MODEL + KV_DIM, KV_DIM))
    logits_t = row_tile(D_MODEL + 2 * KV_DIM, GATE_ROWS)
    gatet_ref[0] = 1.0 / (1.0 + jnp.exp(-logits_t))


def _in_proj(x, w_in, B, S):
    T, D = x.shape
    rows = PROJ_ROWS
    assert rows == ATT_K_ROWS
    steps_per_seq = S // rows
    sec = lambda k: w_in[:, D_MODEL + k * KV_DIM:D_MODEL + (k + 1) * KV_DIM]
    w = jnp.concatenate([sec(0), sec(1), sec(2), sec(4)], axis=1).astype(BF16)
    wg = w_in[:, D_MODEL + 6 * KV_DIM:].reshape(D, N_KV_GROUPS, N_BRANCHES * HEADS_PER_GROUP)
    wg = jnp.pad(wg, ((0, 0), (0, 0), (0, GATE_SLOTS - N_BRANCHES * HEADS_PER_GROUP))).reshape(D, GATE_ROWS)
    wt = jnp.concatenate([w_in[:, :D_MODEL], sec(3), sec(5), wg], axis=1).T.astype(BF16)
    cos, sin = _rope_tables(jnp.arange(S), PROJ_COLS // HEAD_DIM)
    k_shape = jax.ShapeDtypeStruct((B, N_KV_GROUPS, S, AUG_DIM), BF16)
    k_spec = pl.BlockSpec((1, N_KV_GROUPS, rows, AUG_DIM),
                          lambda i: (i // steps_per_seq, 0, i % steps_per_seq, 0))
    vt_shape = jax.ShapeDtypeStruct((B, N_KV_GROUPS, steps_per_seq, VT_ROWS, rows), BF16)
    vt_spec = pl.BlockSpec((1, N_KV_GROUPS, 1, VT_ROWS, rows),
                           lambda i: (i // steps_per_seq, 0, i % steps_per_seq, 0, 0))
    tok_map = lambda i: (i, 0)
    seq_map = lambda i: (i % steps_per_seq, 0)
    feat_map = lambda i: (i // steps_per_seq, 0, i % steps_per_seq)
    return pl.pallas_call(
        functools.partial(_in_proj_kernel, steps_per_seq=steps_per_seq),
        grid=(T // rows,),
        in_specs=[
            pl.BlockSpec((rows, D), tok_map),
            pl.BlockSpec(w.shape, lambda i: (0, 0)),
            pl.BlockSpec(wt.shape, lambda i: (0, 0)),
            pl.BlockSpec((rows, PROJ_COLS), seq_map),
            pl.BlockSpec((rows, PROJ_COLS), seq_map),
            pl.BlockSpec((PROJ_COLS, rows), lambda i: (0, i % steps_per_seq)),
            pl.BlockSpec((PROJ_COLS, rows), lambda i: (0, i % steps_per_seq)),
        ],
        out_specs=[
            pl.BlockSpec((1, D_MODEL, rows), feat_map),
            k_spec,
            k_spec, vt_spec, k_spec, vt_spec,
            pl.BlockSpec((1, GATE_ROWS, rows), feat_map),
        ],
        out_shape=[
            jax.ShapeDtypeStruct((B, D_MODEL, S), BF16),
            jax.ShapeDtypeStruct((B, N_KV_GROUPS, S, AUG_DIM), F32),
            k_shape, vt_shape, k_shape, vt_shape,
            jax.ShapeDtypeStruct((B, GATE_ROWS, S), F32),
        ],
        compiler_params=_params("parallel"),
    )(x, w, wt, cos, sin, cos.T, sin.T)


def _compress_kernel(kv_ref, w1_ref, pea_ref, peb_ref, w2_ref, w2t_ref, cos_ref, sin_ref, kc_ref, vct_ref):
    n = kc_ref.shape[2]
    hid = CMP_HIDDEN
    r = jnp.zeros((n, 4 * hid), F32)
    bias_a = jnp.zeros((8, 4 * hid), F32)
    bias_b = jnp.zeros((8, 4 * hid), F32)
    for p in range(CMP_STRIDE):
        slab = kv_ref[0, 0, pl.ds(p, n, stride=CMP_STRIDE), :].astype(BF16)
        r = r + _dot(slab, w1_ref[p])
        bias_a = bias_a + _dot(pea_ref[p], w1_ref[p])
        bias_b = bias_b + _dot(peb_ref[p], w1_ref[p])
    hidden = []
    for which in range(2):
        c0 = which * 2 * hid
        nxt = pltpu.roll(r[:, c0 + hid:c0 + 2 * hid], n - 1, 0)
        bias = bias_a[0:1, c0:c0 + hid] + bias_b[0:1, c0 + hid:c0 + 2 * hid]
        hidden.append(_silu(r[:, c0:c0 + hid] + nxt + bias).astype(BF16))
    kc = _dot(hidden[0], w2_ref[...])
    rot = jnp.concatenate([kc[:, HALF_DIM:], kc[:, :HALF_DIM]], axis=1)
    kc_ref[0, 0] = (kc * cos_ref[...] + rot * sin_ref[...]).astype(kc_ref.dtype)
    vct_ref[0, 0] = _dot_nt(w2t_ref[...], hidden[1]).astype(vct_ref.dtype)


def _compress(kv_raw, w1_k, w2_k, pe_k, w1_v, w2_v, pe_v, B, S):
    n_chunks = S // CMP_STRIDE
    hid = CMP_HIDDEN
    split = lambda w1: w1.reshape(2, CMP_STRIDE, HEAD_DIM, hid).transpose(1, 2, 0, 3).reshape(CMP_STRIDE, HEAD_DIM, 2 * hid)
    zeros = jnp.zeros((CMP_STRIDE, HEAD_DIM, 2 * hid), F32)
    w1 = jnp.concatenate([jnp.concatenate([split(w1_k), zeros], axis=2),
                          jnp.concatenate([zeros, split(w1_v)], axis=2)], axis=1).astype(BF16)
    pe = jnp.concatenate([pe_k, pe_v], axis=1)
    rows8 = lambda v: jnp.broadcast_to(v[:, None, :], (CMP_STRIDE, 8, AUG_DIM)).astype(BF16)
    pea, peb = rows8(pe[:CMP_STRIDE]), rows8(pe[CMP_STRIDE:])
    cos, sin = _rope_tables(CMP_STRIDE * jnp.arange(n_chunks) + CMP_BLOCK - 1, 1)
    const = lambda a: pl.BlockSpec(a.shape, lambda bi, gi: (0,) * a.ndim)
    w2k = w2_k.astype(BF16)
    w2vt = w2_v.T.astype(BF16)
    return pl.pallas_call(
        _compress_kernel,
        grid=(B, N_KV_GROUPS),
        in_specs=[
            pl.BlockSpec((1, 1, S, AUG_DIM), lambda bi, gi: (bi, gi, 0, 0)),
            const(w1), const(pea), const(peb), const(w2k), const(w2vt), const(cos), const(sin),
        ],
        out_specs=[pl.BlockSpec((1, 1, n_chunks, HEAD_DIM), lambda bi, gi: (bi, gi, 0, 0)),
                   pl.BlockSpec((1, 1, HEAD_DIM, n_chunks), lambda bi, gi: (bi, gi, 0, 0))],
        out_shape=[jax.ShapeDtypeStruct((B, N_KV_GROUPS, n_chunks, HEAD_DIM), BF16),
                   jax.ShapeDtypeStruct((B, N_KV_GROUPS, HEAD_DIM, n_chunks), BF16)],
        compiler_params=_params("parallel", "parallel"),
    )(kv_raw, w1, pea, peb, w2k, w2vt, cos, sin)


def _cmp_attn_kernel(qt_ref, kc_ref, vct_ref, ovl_ref, oct_ref, selt_ref, score_ref):
    i = pl.program_id(2)
    cols = qt_ref.shape[2]
    n_cmp = kc_ref.shape[2]
    n_sel = ovl_ref.shape[0]
    kc = kc_ref[0, 0]
    vct = vct_ref[0, 0]
    t = i * cols + lax.broadcasted_iota(jnp.int32, (1, cols), 1)
    cend = CMP_STRIDE * lax.broadcasted_iota(jnp.int32, (n_cmp, 1), 0) + (CMP_BLOCK - 1)
    cvalid = cend <= t
    any_valid = jnp.where(t >= CMP_BLOCK - 1, 1.0, 0.0)
    pc_sum = jnp.zeros((n_cmp, cols), F32)
    for h in range(HEADS_PER_GROUP):
        qh = qt_ref[0, h * HEAD_DIM:(h + 1) * HEAD_DIM, :]
        s = jnp.where(cvalid, _dot(kc, qh), NEG_INF)
        e = jnp.exp2(s - jnp.max(s, axis=0, keepdims=True))
        pc = e * (any_valid / jnp.sum(e, axis=0, keepdims=True))
        oct_ref[0, h * HEAD_DIM:(h + 1) * HEAD_DIM, :] = _dot(vct, pc.astype(BF16))
        pc_sum = pc_sum + pc
    imp = _dot(ovl_ref[...], pc_sum.astype(BF16))
    j = lax.broadcasted_iota(jnp.int32, (n_sel, 1), 0)
    blk_t = t >> SEL_SHIFT
    bvalid = j <= blk_t
    forced = (j == 0) | (j == blk_t) | (j == blk_t - 1)
    score = jnp.where(bvalid, imp + jnp.where(forced, FORCE_BONUS, 0.0), -1.0)
    score_ref[...] = score
    def rank_group(grp, rank):
        base = pl.multiple_of(grp * 8, 8)
        others = score_ref[pl.ds(base, 8), :]
        for r in range(8):
            other = others[r:r + 1, :]
            ge = jnp.where(other >= score, 1.0, 0.0)
            gt = jnp.where(other > score, 1.0, 0.0)
            rank = rank + jnp.where(j > base + r, ge, gt)
        return rank

    n_groups = jnp.minimum(((i + 1) * cols) // (8 * SEL_BLOCK), n_sel // 8)
    rank = lax.fori_loop(0, n_groups, rank_group, jnp.zeros((n_sel, cols), F32))
    selected = (rank < float(SEL_TOP_N)) & bvalid
    selt_ref[0, 0] = jnp.where(selected, 0.0, NEG_INF).astype(selt_ref.dtype)


def _cmp_attn(qt, kc, vct, B, S):
    cols = CMP_Q_ROWS
    n_cmp = S // CMP_STRIDE
    n_sel = SEL_SLOTS
    assert S // SEL_BLOCK <= SEL_SLOTS
    cstart = CMP_STRIDE * jnp.arange(n_cmp)
    sstart = SEL_BLOCK * jnp.arange(n_sel)
    overlap = ((cstart[None, :] <= sstart[:, None] + SEL_BLOCK - 1)
               & (cstart[None, :] + CMP_BLOCK - 1 >= sstart[:, None])).astype(BF16)
    q_spec = pl.BlockSpec((1, GROUP_Q_DIM, cols), lambda bi, gi, i: (bi, gi, i))
    return pl.pallas_call(
        _cmp_attn_kernel,
        grid=(B, N_KV_GROUPS, S // cols),
        in_specs=[
            q_spec,
            pl.BlockSpec((1, 1, n_cmp, HEAD_DIM), lambda bi, gi, i: (bi, gi, 0, 0)),
            pl.BlockSpec((1, 1, HEAD_DIM, n_cmp), lambda bi, gi, i: (bi, gi, 0, 0)),
            pl.BlockSpec(overlap.shape, lambda bi, gi, i: (0, 0)),
        ],
        out_specs=[
            q_spec,
            pl.BlockSpec((1, 1, n_sel, cols), lambda bi, gi, i: (bi, gi, 0, i)),
        ],
        out_shape=[
            jax.ShapeDtypeStruct((B, D_MODEL, S), F32),
            jax.ShapeDtypeStruct((B, N_KV_GROUPS, n_sel, S), BF16),
        ],
        scratch_shapes=[pltpu.VMEM((n_sel, cols), F32)],
        compiler_params=_params("parallel", "parallel", "arbitrary"),
    )(qt, kc, vct, overlap)


def _sel_win_attn_kernel(qt_ref, selt_ref, ks_ref, vst_ref, kw_ref, vwt_ref, diag_ref, band_ref, oct_ref, gatet_ref,
                         o_ref, qaug_ref, m_ref, acc_ref, ow_ref):
    i = pl.program_id(2)
    tq = qt_ref.shape[2]
    hg = HEADS_PER_GROUP
    half = ATT_HALF
    n_half = tq // half
    half_cols = hg * half

    for hq in range(n_half):
        for h in range(hg):
            c0 = (hq * hg + h) * half
            qaug_ref[0:HEAD_DIM, c0:c0 + half] = qt_ref[0, h * HEAD_DIM:(h + 1) * HEAD_DIM, hq * half:(hq + 1) * half]
            qaug_ref[HEAD_DIM:, c0:c0 + half] = selt_ref[0, 0, :, hq * half:(hq + 1) * half]
    n_tiles = (hg * tq) // ATT_COLS

    def keys(ref, start, size):
        return ref[0, 0, pl.ds(pl.multiple_of(start, ATT_HALF), size), :]

    m_ref[...] = jnp.full(m_ref.shape, NEG_INF, F32)
    acc_ref[...] = jnp.zeros(acc_ref.shape, F32)

    def sel_step(c, bias_ref):
        k = keys(ks_ref, c * ATT_K_ROWS, ATT_K_ROWS)
        v = vst_ref[0, 0, c]
        new_m, new_acc = [], []
        tile = lambda ct: slice(ct * ATT_COLS, (ct + 1) * ATT_COLS)
        scores = [_dot(k, qaug_ref[:, tile(ct)]) for ct in range(ATT_AHEAD)]
        for ct in range(n_tiles):
            cs = tile(ct)
            if ct + ATT_AHEAD < n_tiles:
                scores.append(_dot(k, qaug_ref[:, tile(ct + ATT_AHEAD)]))
            s = scores[ct]
            if bias_ref is not None:
                s = s + bias_ref[:, cs]
            m_prev = m_ref[:, cs]
            m_next = jnp.maximum(m_prev, jnp.max(s, axis=0, keepdims=True))
            p = jnp.exp2(s - m_next)
            alpha = jnp.exp2(m_prev - m_next)
            new_acc.append(alpha * acc_ref[:, cs] + _dot(v, p.astype(BF16)))
            new_m.append(m_next)
        for ct in range(n_tiles):
            cs = slice(ct * ATT_COLS, (ct + 1) * ATT_COLS)
            acc_ref[:, cs] = new_acc[ct]
            m_ref[:, cs] = new_m[ct]

    def sel_body(c, carry):
        sel_step(c, None)
        return carry

    lax.fori_loop(0, i, sel_body, 0)
    sel_step(i, diag_ref)

    def probs(s):
        return jnp.exp2(s - jnp.max(s, axis=0, keepdims=True)).astype(BF16)

    @pl.when(i == 0)
    def _():
        k = keys(kw_ref, 0, tq)
        v = vwt_ref[0, 0, 0]
        for ct in range(n_tiles):
            cs = slice(ct * ATT_COLS, (ct + 1) * ATT_COLS)
            ow_ref[:, cs] = _dot(v, probs(_dot(k, qaug_ref[:, cs]) + diag_ref[:, cs]))

    @pl.when(i > 0)
    def _():
        v_prev = vwt_ref[0, 0, i - 1]
        v_here = vwt_ref[0, 0, i]
        for hq in range(n_half):
            k = keys(kw_ref, i * tq + hq * half - WINDOW, WINDOW + half)
            n_prev = tq - hq * half
            for ct in range(half_cols // ATT_COLS):
                cs = slice(hq * half_cols + ct * ATT_COLS, hq * half_cols + (ct + 1) * ATT_COLS)
                p = probs(_dot(k, qaug_ref[:, cs]) + band_ref[:, ct * ATT_COLS:(ct + 1) * ATT_COLS])
                ow_ref[:, cs] = (_dot(v_prev[:, tq - n_prev:], p[0:n_prev])
                                 + _dot(v_here[:, 0:WINDOW + half - n_prev], p[n_prev:]))

    acc_s = acc_ref[...]
    acc_w = ow_ref[...]
    o_s = acc_s[0:HEAD_DIM] / acc_s[HEAD_DIM:HEAD_DIM + 1]
    o_w = acc_w[0:HEAD_DIM] / acc_w[HEAD_DIM:HEAD_DIM + 1]
    for hq in range(n_half):
        tok = slice(hq * half, (hq + 1) * half)
        for h in range(hg):
            c0 = (hq * hg + h) * half
            gc = gatet_ref[0, N_BRANCHES * h + 0:N_BRANCHES * h + 1, tok]
            gs = gatet_ref[0, N_BRANCHES * h + 1:N_BRANCHES * h + 2, tok]
            gw = gatet_ref[0, N_BRANCHES * h + 2:N_BRANCHES * h + 3, tok]
            o = (gc * oct_ref[0, h * HEAD_DIM:(h + 1) * HEAD_DIM, tok]
                 + gs * o_s[:, c0:c0 + half] + gw * o_w[:, c0:c0 + half])
            o_ref[0, tok, h * HEAD_DIM:(h + 1) * HEAD_DIM] = o.T.astype(o_ref.dtype)


def _sel_win_attn(qt, selt, ks, vst, kw, vwt, oct, gatet, B, S):
    tq = ATT_Q_ROWS
    half = ATT_HALF
    hg = HEADS_PER_GROUP
    cols = hg * tq
    n_chunks = S // ATT_K_ROWS
    t_rel = (jnp.arange(tq // half)[:, None, None] * half + jnp.arange(half)[None, None, :])
    t_rel = jnp.broadcast_to(t_rel, (tq // half, hg, half)).reshape(1, cols)
    diag = jnp.where(jnp.arange(ATT_K_ROWS)[:, None] <= t_rel, 0.0, NEG_INF).astype(F32)
    tt = jnp.broadcast_to(jnp.arange(half)[None, :], (hg, half)).reshape(1, hg * half)
    a = jnp.arange(WINDOW + half)[:, None]
    band = jnp.where((a > tt) & (a <= tt + WINDOW), 0.0, NEG_INF).astype(F32)
    k_spec = pl.BlockSpec((1, 1, S, AUG_DIM), lambda bi, gi, i: (bi, gi, 0, 0))
    vt_spec = pl.BlockSpec((1, 1, n_chunks, VT_ROWS, ATT_K_ROWS), lambda bi, gi, i: (bi, gi, 0, 0, 0))
    q_spec = pl.BlockSpec((1, GROUP_Q_DIM, tq), lambda bi, gi, i: (bi, gi, i))
    const = lambda arr: pl.BlockSpec(arr.shape, lambda bi, gi, i: (0, 0))
    return pl.pallas_call(
        _sel_win_attn_kernel,
        grid=(B, N_KV_GROUPS, S // tq),
        in_specs=[
            q_spec,
            pl.BlockSpec((1, 1, SEL_SLOTS, tq), lambda bi, gi, i: (bi, gi, 0, i)),
            k_spec, vt_spec, k_spec, vt_spec,
            const(diag), const(band),
            q_spec,
            pl.BlockSpec((1, GATE_SLOTS, tq), lambda bi, gi, i: (bi, gi, i)),
        ],
        out_specs=pl.BlockSpec((1, tq, GROUP_Q_DIM), lambda bi, gi, i: (bi, i, gi)),
        out_shape=jax.ShapeDtypeStruct((B, S, D_MODEL), BF16),
        scratch_shapes=[
            pltpu.VMEM((AUG_DIM, cols), BF16),
            pltpu.VMEM((1, cols), F32),
            pltpu.VMEM((VT_ROWS, cols), F32),
            pltpu.VMEM((VT_ROWS, cols), F32),
        ],
        compiler_params=_params("parallel", "parallel", "arbitrary"),
    )(qt, selt, ks, vst, kw, vwt, diag, band, oct, gatet)


def _out_proj_ln_kernel(o_ref, w_ref, x_ref, g_ref, b_ref, y_ref):
    h = _dot(o_ref[...], w_ref[...])
    y_ref[...] = _layer_norm(ALPHA * x_ref[...] + h, g_ref[...], b_ref[...])


def _out_proj_ln(o, w_out, x, g, b):
    T, D = x.shape
    rows = PROJ_ROWS
    return pl.pallas_call(
        _out_proj_ln_kernel,
        grid=(T // rows,),
        in_specs=[
            pl.BlockSpec((rows, D), lambda i: (i, 0)),
            pl.BlockSpec((D, D), lambda i: (0, 0)),
            pl.BlockSpec((rows, D), lambda i: (i, 0)),
            pl.BlockSpec((1, D), lambda i: (0, 0)),
            pl.BlockSpec((1, D), lambda i: (0, 0)),
        ],
        out_specs=pl.BlockSpec((rows, D), lambda i: (i, 0)),
        out_shape=jax.ShapeDtypeStruct((T, D), F32),
        compiler_params=_params("parallel"),
    )(o, w_out.astype(BF16), x, g.reshape(1, D), b.reshape(1, D))


def _route_kernel(x_ref, r_ref, tri_ref, xb_ref, info_ref, wcol_ref, cnt_ref, carry_ref):
    @pl.when(pl.program_id(0) == 0)
    def _():
        carry_ref[...] = jnp.zeros_like(carry_ref)

    x = x_ref[...]
    xb = x.astype(BF16)
    xb_ref[...] = xb
    x_lo = (x - xb.astype(F32)).astype(BF16)
    r_hi = r_ref[0]
    r_lo = r_ref[1]
    logits = _dot(xb, r_hi) + (_dot(x_lo, r_hi) + _dot(xb, r_lo))
    lane = lax.broadcasted_iota(jnp.int32, logits.shape, 1)
    logits = jnp.where(lane < N_EXPERTS, logits, -jnp.inf)
    v1 = jnp.max(logits, axis=-1, keepdims=True)
    i1 = jnp.min(jnp.where(logits == v1, lane, LANES), axis=-1, keepdims=True)
    rest = jnp.where(lane == i1, -jnp.inf, logits)
    v2 = jnp.max(rest, axis=-1, keepdims=True)
    i2 = jnp.min(jnp.where(rest == v2, lane, LANES), axis=-1, keepdims=True)
    e2 = jnp.exp(v2 - v1)
    w1 = 1.0 / (1.0 + e2)
    w2 = e2 / (1.0 + e2)
    m1 = jnp.where(lane == i1, 1.0, 0.0)
    m2 = jnp.where(lane == i2, 1.0, 0.0)
    routed = m1 + m2
    before = _dot(tri_ref[...], routed.astype(BF16)) + carry_ref[0:1, :]
    rank1 = jnp.sum(m1 * before, axis=-1, keepdims=True)
    rank2 = jnp.sum(m2 * before, axis=-1, keepdims=True)
    cnt = jnp.sum(routed, axis=0, keepdims=True)
    carry_ref[...] = carry_ref[...] + cnt
    cnt_ref[0] = jnp.broadcast_to(cnt, cnt_ref.shape[1:])
    info = jnp.where(lane == 0, i1.astype(F32),
                     jnp.where(lane == 1, i2.astype(F32),
                               jnp.where(lane == 2, rank1, jnp.where(lane == 3, rank2, 0.0))))
    info_ref[...] = info
    w1_hi = w1.astype(BF16).astype(F32)
    w2_hi = w2.astype(BF16).astype(F32)
    lo_half = (lane & 1) == 1
    wcol = (jnp.where((lane >> 1) == i1, jnp.where(lo_half, w1 - w1_hi, w1_hi), 0.0)
            + jnp.where((lane >> 1) == i2, jnp.where(lo_half, w2 - w2_hi, w2_hi), 0.0))
    wcol_ref[...] = wcol.astype(BF16)


def _route(x, router):
    T, D = x.shape
    rows = MOE_ROWS
    r = jnp.pad(router, ((0, 0), (0, LANES - router.shape[1])))
    r_hi = r.astype(BF16)
    r_lo = (r - r_hi.astype(F32)).astype(BF16)
    r2 = jnp.stack([r_hi, r_lo])
    tri = (jnp.arange(rows)[None, :] < jnp.arange(rows)[:, None]).astype(BF16)
    return pl.pallas_call(
        _route_kernel,
        grid=(T // rows,),
        in_specs=[
            pl.BlockSpec((rows, D), lambda i: (i, 0)),
            pl.BlockSpec(r2.shape, lambda i: (0, 0, 0)),
            pl.BlockSpec(tri.shape, lambda i: (0, 0)),
        ],
        out_specs=[
            pl.BlockSpec((rows, D), lambda i: (i, 0)),
            pl.BlockSpec((rows, LANES), lambda i: (i, 0)),
            pl.BlockSpec((rows, LANES), lambda i: (i, 0)),
            pl.BlockSpec((1, 8, LANES), lambda i: (i, 0, 0)),
        ],
        out_shape=[
            jax.ShapeDtypeStruct((T, D), BF16),
            jax.ShapeDtypeStruct((T, LANES), F32),
            jax.ShapeDtypeStruct((T, LANES), BF16),
            jax.ShapeDtypeStruct((T // rows, 8, LANES), F32),
        ],
        scratch_shapes=[pltpu.VMEM((8, LANES), F32)],
        compiler_params=_params("arbitrary"),
    )(x, r2, tri)


def _moe_tables(cnt, n_blocks_max, n_items_max):
    R = MOE_ROWS
    C, E = cnt.shape
    i32 = jnp.int32
    count_le = lambda sorted_v, q: jnp.sum(sorted_v[None, :] <= q[:, None], axis=1).astype(i32)
    cum = jnp.concatenate([jnp.zeros((1, E), i32), jnp.cumsum(cnt, axis=0)], axis=0)
    tot = cum[-1]
    nb = (tot + R - 1) // R
    nb_end = jnp.cumsum(nb)
    blk_start = nb_end - nb
    n_blocks = nb_end[-1]
    b = jnp.minimum(jnp.arange(n_blocks_max, dtype=i32), n_blocks - 1)
    bexp = jnp.minimum(count_le(nb_end, b), E - 1)
    bvalid = jnp.arange(n_blocks_max, dtype=i32) < n_blocks
    lb = b - blk_start[bexp]
    rho0 = lb * R
    rho1 = jnp.minimum((lb + 1) * R, tot[bexp]) - 1
    cum_b = cum[1:, :][:, bexp].T
    lo = jnp.minimum(jnp.sum(cum_b <= rho0[:, None], axis=1).astype(i32), C - 1)
    hi = jnp.minimum(jnp.sum(cum_b <= rho1[:, None], axis=1).astype(i32), C - 1)
    nit = jnp.where(bvalid, hi - lo + 1, 0)
    it_end = jnp.cumsum(nit)
    it_start = it_end - nit
    n_items = it_end[-1]
    i = jnp.arange(n_items_max, dtype=i32)
    ic = jnp.minimum(i, n_items - 1)
    d_blk = jnp.minimum(count_le(it_end, ic), n_blocks_max - 1)
    d_chk = lo[d_blk] + ic - it_start[d_blk]
    d_valid = i < n_items
    d_first = d_valid & (ic == it_start[d_blk])
    d_last = d_valid & (ic == it_end[d_blk] - 1)
    d_exp = bexp[d_blk]
    d_ra = jnp.maximum(rho0[d_blk], cum[d_chk, d_exp]) - rho0[d_blk]
    d_rb = jnp.minimum(rho1[d_blk], cum[d_chk + 1, d_exp] - 1) - rho0[d_blk]
    d_sub_lo = jnp.where(d_valid & (d_rb >= d_ra), d_ra // DISPATCH_SUB, 1)
    d_sub_hi = jnp.where(d_valid & (d_rb >= d_ra), d_rb // DISPATCH_SUB, 0)
    bl_lo = (blk_start[None, :] + cum[:-1] // R).reshape(-1)
    bl_hi = (blk_start[None, :] + (cum[1:] - 1) // R).reshape(-1)
    npair = jnp.where(cnt.reshape(-1) > 0, bl_hi - bl_lo + 1, 0)
    p_end = jnp.cumsum(npair)
    p_start = p_end - npair
    n_items2 = p_end[-1]
    jc = jnp.minimum(i, n_items2 - 1)
    pair = jnp.minimum(count_le(p_end, jc), C * E - 1)
    c_blk = bl_lo[pair] + jc - p_start[pair]
    c_chk = pair // E
    c_valid = i < n_items2
    prev_chk = jnp.concatenate([jnp.full((1,), -1, i32), c_chk[:-1]])
    next_chk = jnp.concatenate([c_chk[1:], jnp.full((1,), -1, i32)])
    next_valid = jnp.concatenate([c_valid[1:], jnp.zeros((1,), bool)])
    c_first = c_valid & (c_chk != prev_chk)
    c_last = c_valid & ((c_chk != next_chk) | ~next_valid)
    c_exp = pair % E
    c_base = (c_blk - blk_start[c_exp]) * R
    c_ra = jnp.maximum(cum[c_chk, c_exp] - c_base, 0)
    c_rb = jnp.minimum(cum[c_chk + 1, c_exp] - 1 - c_base, R - 1)
    c_sub_lo = jnp.where(c_valid, c_ra // COMBINE_SUB, 1)
    c_sub_hi = jnp.where(c_valid, c_rb // COMBINE_SUB, 0)
    as_i32 = lambda v: v.astype(i32)
    return dict(base=blk_start * R, bexp=bexp, bvalid=as_i32(bvalid),
                d_blk=d_blk, d_chk=d_chk, d_first=as_i32(d_first), d_last=as_i32(d_last), d_sub_lo=d_sub_lo, d_sub_hi=d_sub_hi,
                c_blk=c_blk, c_chk=c_chk, c_first=as_i32(c_first), c_last=as_i32(c_last),
                c_sub_lo=c_sub_lo, c_sub_hi=c_sub_hi)


def _dispatch_kernel(blk_ref, chk_ref, first_ref, last_ref, lo_ref, hi_ref, xb_ref, pos_ref, wcol_ref,
                     xs_ref, ws_ref, acc_ref, wacc_ref):
    i = pl.program_id(0)
    R = xs_ref.shape[0]

    @pl.when(first_ref[i] == 1)
    def _():
        acc_ref[...] = jnp.zeros_like(acc_ref)
        wacc_ref[...] = jnp.zeros_like(wacc_ref)

    pos1 = pos_ref[0, 0:1, :]
    pos2 = pos_ref[0, 1:2, :]

    def sub_tile(k, carry):
        r0 = pl.multiple_of(k * DISPATCH_SUB, DISPATCH_SUB)
        rows = (blk_ref[i] * R + r0 + lax.broadcasted_iota(jnp.int32, (DISPATCH_SUB, 1), 0)).astype(F32)
        hit = jnp.where((pos1 - rows) * (pos2 - rows) == 0.0, 1.0, 0.0).astype(BF16)
        acc_ref[pl.ds(r0, DISPATCH_SUB), :] += _dot(hit, xb_ref[...])
        wacc_ref[pl.ds(r0, DISPATCH_SUB), :] += _dot(hit, wcol_ref[...])
        return carry

    lax.fori_loop(lo_ref[i], hi_ref[i] + 1, sub_tile, 0)

    @pl.when(last_ref[i] == 1)
    def _():
        xs_ref[...] = acc_ref[...].astype(xs_ref.dtype)
        ws_ref[...] = wacc_ref[...]


def _dispatch(xb, posrow, wcol, tab, n_blocks_max, n_items_max):
    T, D = xb.shape
    R = MOE_ROWS
    chunk_map = lambda i, blk, chk, first, last, lo, hi: (chk[i], 0)
    block_map = lambda i, blk, chk, first, last, lo, hi: (blk[i], 0)
    grid_spec = pltpu.PrefetchScalarGridSpec(
        num_scalar_prefetch=6,
        grid=(n_items_max,),
        in_specs=[
            pl.BlockSpec((R, D), chunk_map),
            pl.BlockSpec((1, 8, R), lambda i, blk, chk, first, last, lo, hi: (chk[i], 0, 0)),
            pl.BlockSpec((R, LANES), chunk_map),
        ],
        out_specs=[pl.BlockSpec((R, D), block_map), pl.BlockSpec((R, LANES), block_map)],
        scratch_shapes=[pltpu.VMEM((R, D), F32), pltpu.VMEM((R, LANES), F32)],
    )
    return pl.pallas_call(
        _dispatch_kernel,
        grid_spec=grid_spec,
        out_shape=[
            jax.ShapeDtypeStruct((n_blocks_max * R, D), BF16),
            jax.ShapeDtypeStruct((n_blocks_max * R, LANES), F32),
        ],
        compiler_params=_params("arbitrary"),
    )(tab["d_blk"], tab["d_chk"], tab["d_first"], tab["d_last"], tab["d_sub_lo"], tab["d_sub_hi"], xb, posrow, wcol)


def _expert_ffn_kernel(bexp_ref, bvalid_ref, xs_ref, ws_ref, wg_ref, wu_ref, wd_ref, ys_ref, acc_ref):
    b = pl.program_id(0)
    f = pl.program_id(1)

    @pl.when(f == 0)
    def _():
        acc_ref[...] = jnp.zeros_like(acc_ref)

    @pl.when(bvalid_ref[b] == 1)
    def _():
        xb = xs_ref[...]
        a = _silu(_dot(xb, wg_ref[0])) * _dot(xb, wu_ref[0])
        acc_ref[...] += _dot(a.astype(BF16), wd_ref[0])

    @pl.when(f == pl.num_programs(1) - 1)
    def _():
        ws = ws_ref[...]
        lane = lax.broadcasted_iota(jnp.int32, ws.shape, 1)
        w = jnp.sum(jnp.where((lane >> 1) == bexp_ref[b], ws, 0.0), axis=-1, keepdims=True)
        ys_ref[...] = (acc_ref[...] * w).astype(ys_ref.dtype)


def _expert_ffn(xs, ws, w_gu, w_down, tab, n_blocks_max):
    D = xs.shape[1]
    R = MOE_ROWS
    d_ff = w_down.shape[1]
    fc = _ff_chunk(d_ff, 1792)
    nf = d_ff // fc
    f_eff = lambda b, f, bvalid: f * bvalid[b] + (nf - 1) * (1 - bvalid[b])
    grid_spec = pltpu.PrefetchScalarGridSpec(
        num_scalar_prefetch=2,
        grid=(n_blocks_max, nf),
        in_specs=[
            pl.BlockSpec((R, D), lambda b, f, bexp, bvalid: (b, 0)),
            pl.BlockSpec((R, LANES), lambda b, f, bexp, bvalid: (b, 0)),
            pl.BlockSpec((1, D, fc), lambda b, f, bexp, bvalid: (bexp[b], 0, f_eff(b, f, bvalid))),
            pl.BlockSpec((1, D, fc), lambda b, f, bexp, bvalid: (bexp[b], 0, nf + f_eff(b, f, bvalid))),
            pl.BlockSpec((1, fc, D), lambda b, f, bexp, bvalid: (bexp[b], f_eff(b, f, bvalid), 0)),
        ],
        out_specs=pl.BlockSpec((R, D), lambda b, f, bexp, bvalid: (b, 0)),
        scratch_shapes=[pltpu.VMEM((R, D), F32)],
    )
    return pl.pallas_call(
        _expert_ffn_kernel,
        grid_spec=grid_spec,
        out_shape=jax.ShapeDtypeStruct(xs.shape, BF16),
        compiler_params=_params("parallel", "arbitrary"),
    )(tab["bexp"], tab["bvalid"], xs, ws, w_gu, w_gu, w_down)


def _combine_ln_kernel(chk_ref, blk_ref, first_ref, last_ref, lo_ref, hi_ref,
                       ys_ref, pc1_ref, pc2_ref, x_ref, g_ref, b_ref, o_ref, acc_ref):
    i = pl.program_id(0)
    R = ys_ref.shape[0]

    @pl.when(first_ref[i] == 1)
    def _():
        acc_ref[...] = jnp.zeros_like(acc_ref)

    def sub_tile(k, carry):
        r0 = pl.multiple_of(k * COMBINE_SUB, COMBINE_SUB)
        pc1 = pc1_ref[...]
        pc2 = pc2_ref[...]
        lane = lax.broadcasted_iota(jnp.int32, pc1.shape, 1)
        parts = []
        for t in range(COMBINE_SUB // LANES):
            col = (blk_ref[i] * R + r0 + t * LANES + lane).astype(F32)
            parts.append(jnp.where((pc1 - col) * (pc2 - col) == 0.0, 1.0, 0.0).astype(BF16))
        acc_ref[...] += _dot(jnp.concatenate(parts, axis=1), ys_ref[pl.ds(r0, COMBINE_SUB), :])
        return carry

    lax.fori_loop(lo_ref[i], hi_ref[i] + 1, sub_tile, 0)

    @pl.when(last_ref[i] == 1)
    def _():
        o_ref[...] = _layer_norm(ALPHA * x_ref[...] + acc_ref[...], g_ref[...], b_ref[...])


def _combine_ln(ys, poscol1, poscol2, x, g, b, tab, n_items_max):
    T, D = x.shape
    R = MOE_ROWS
    chunk_map = lambda i, chk, blk, first, last, lo, hi: (chk[i], 0)
    const_map = lambda i, chk, blk, first, last, lo, hi: (0, 0)
    grid_spec = pltpu.PrefetchScalarGridSpec(
        num_scalar_prefetch=6,
        grid=(n_items_max,),
        in_specs=[
            pl.BlockSpec((R, D), lambda i, chk, blk, first, last, lo, hi: (blk[i], 0)),
            pl.BlockSpec((R, LANES), chunk_map),
            pl.BlockSpec((R, LANES), chunk_map),
            pl.BlockSpec((R, D), chunk_map),
            pl.BlockSpec((1, D), const_map),
            pl.BlockSpec((1, D), const_map),
        ],
        out_specs=pl.BlockSpec((R, D), chunk_map),
        scratch_shapes=[pltpu.VMEM((R, D), F32)],
    )
    return pl.pallas_call(
        _combine_ln_kernel,
        grid_spec=grid_spec,
        out_shape=jax.ShapeDtypeStruct((T, D), F32),
        compiler_params=_params("arbitrary"),
    )(tab["c_chk"], tab["c_blk"], tab["c_first"], tab["c_last"], tab["c_sub_lo"], tab["c_sub_hi"],
      ys, poscol1, poscol2, x, g.reshape(1, D), b.reshape(1, D))


def _moe_ln(x, router, w_gu, w_down, g, b):
    T, D = x.shape
    R = MOE_ROWS
    n_exp = w_down.shape[0]
    n_chunks = T // R
    n_blocks_max = (2 * T) // R + n_exp
    n_items_max = n_blocks_max + n_exp * (n_chunks - 1)
    xb, info, wcol, cnt = _route(x, router)
    tab = _moe_tables(cnt[:, 0, :n_exp].astype(jnp.int32), n_blocks_max, n_items_max)
    e1 = info[:, 0].astype(jnp.int32)
    e2 = info[:, 1].astype(jnp.int32)
    pos1 = tab["base"][e1] + info[:, 2].astype(jnp.int32)
    pos2 = tab["base"][e2] + info[:, 3].astype(jnp.int32)
    pos1 = pos1.astype(F32)
    pos2 = pos2.astype(F32)
    posrow = jnp.stack([pos1.reshape(n_chunks, R), pos2.reshape(n_chunks, R)], axis=1)
    posrow = jnp.pad(posrow, ((0, 0), (0, 6), (0, 0)), constant_values=-1.0)
    poscol1 = jnp.broadcast_to(pos1[:, None], (T, LANES))
    poscol2 = jnp.broadcast_to(pos2[:, None], (T, LANES))
    xs, ws = _dispatch(xb, posrow, wcol, tab, n_blocks_max, n_items_max)
    ys = _expert_ffn(xs, ws, w_gu, w_down, tab, n_blocks_max)
    return _combine_ln(ys, poscol1, poscol2, x, g, b, tab, n_items_max)


def kernel(x, ln_g, ln_b, pool_w, pool_scale, nsa_w_in, nsa_pe_k, nsa_w1_k, nsa_w2_k, nsa_pe_v, nsa_w1_v,
           nsa_w2_v, nsa_w_out, ffn_w_gu, ffn_w_down, moe_router, moe_w_gu, moe_w_down):
    B, S, D = x.shape
    T = B * S
    xa = _pool_ln(x, pool_w[0], pool_scale[0], ln_g[0, 0], ln_b[0, 0]).reshape(T, D)
    x1 = _ffn_ln(xa, ffn_w_gu[0].astype(BF16), ffn_w_down[0].astype(BF16), ln_g[0, 1], ln_b[0, 1])
    qt, kcv, ks, vst, kw, vwt, gatet = _in_proj(x1, nsa_w_in[0], B, S)
    kc, vct = _compress(kcv, nsa_w1_k[0], nsa_w2_k[0], nsa_pe_k[0], nsa_w1_v[0], nsa_w2_v[0], nsa_pe_v[0], B, S)
    oct, selt = _cmp_attn(qt, kc, vct, B, S)
    o = _sel_win_attn(qt, selt, ks, vst, kw, vwt, oct, gatet, B, S).reshape(T, D)
    x2 = _out_proj_ln(o, nsa_w_out[0], x1, ln_g[1, 0], ln_b[1, 0])
    y = _moe_ln(x2, moe_router[0], moe_w_gu[0].astype(BF16), moe_w_down[0].astype(BF16), ln_g[1, 1], ln_b[1, 1])
    return y.reshape(B, S, D)
```

```python
import functools

import jax
import jax.numpy as jnp
from jax import lax
from jax.experimental import pallas as pl
from jax.experimental.pallas import tpu as pltpu

D_MODEL = 1024
DEPTH = 2
POOL_WINDOWS = (2, 4, 8, 16)
POOL_GROUP_DIM = D_MODEL // len(POOL_WINDOWS)
POOL_HALO = 16
N_HEADS = 16
N_KV_GROUPS = 4
HEADS_PER_GROUP = N_HEADS // N_KV_GROUPS
HEAD_DIM = D_MODEL // N_HEADS
HALF_DIM = HEAD_DIM // 2
GROUP_Q_DIM = HEADS_PER_GROUP * HEAD_DIM
KV_DIM = N_KV_GROUPS * HEAD_DIM
N_BRANCHES = 3
N_GATES = N_BRANCHES * N_HEADS
CMP_STRIDE = 16
CMP_BLOCK = 2 * CMP_STRIDE
CMP_HIDDEN = 2 * HEAD_DIM
SEL_BLOCK = 64
SEL_SHIFT = SEL_BLOCK.bit_length() - 1
SEL_TOP_N = 16
WINDOW = 512
FORCE_BONUS = 1.0e3
NEG_INF = -1.0e30
ROPE_THETA = 10000.0
ATTN_SCALE = HEAD_DIM ** -0.5
LOG2E = 1.4426950408889634
Q_SCALE = ATTN_SCALE * LOG2E
AUG_DIM = 2 * HEAD_DIM
SEL_SLOTS = AUG_DIM - HEAD_DIM
VT_ROWS = HEAD_DIM + 16
GATE_SLOTS = 16
GATE_ROWS = N_KV_GROUPS * GATE_SLOTS
N_EXPERTS = 8
LN_EPS = 1e-5
ALPHA = (2 * DEPTH) ** 0.25

LANES = 128
VMEM_LIMIT_BYTES = 56 * 1024 * 1024

POOL_ROWS = 512
FFN_ROWS = 512
PROJ_ROWS = 512
PROJ_COLS = 256
CMP_Q_ROWS = 512
ATT_Q_ROWS = 512
ATT_K_ROWS = 512
ATT_HALF = 256
ATT_COLS = 512
ATT_AHEAD = 3
MOE_ROWS = 512
DISPATCH_SUB = 128
COMBINE_SUB = 256

F32 = jnp.float32
BF16 = jnp.bfloat16


def _dot(a, b):
    return jnp.dot(a, b, preferred_element_type=F32)


def _dot_nt(a, b):
    return lax.dot_general(a, b, (((1,), (1,)), ((), ())), preferred_element_type=F32)


def _layer_norm(z, g, b):
    mu = jnp.mean(z, axis=-1, keepdims=True)
    zc = z - mu
    var = jnp.mean(zc * zc, axis=-1, keepdims=True)
    return zc * lax.rsqrt(var + LN_EPS) * g + b


def _silu(x):
    return x / (1.0 + jnp.exp(-x))


def _params(*semantics):
    return pltpu.CompilerParams(dimension_semantics=semantics, vmem_limit_bytes=VMEM_LIMIT_BYTES)


def _pool_ln_kernel(x_ref, halo_ref, w_ref, scale_ref, g_ref, b_ref, o_ref, ext_ref):
    i = pl.program_id(1)
    rows = x_ref.shape[1]
    x = x_ref[0]
    ext_ref[0:POOL_HALO, :] = jnp.where(i > 0, halo_ref[0], 0.0)
    ext_ref[POOL_HALO:, :] = x
    pos = i * rows + lax.broadcasted_iota(jnp.int32, (rows, 1), 0)
    ys = []
    run = ext_ref[...]
    span = 1
    for gi, w in enumerate(POOL_WINDOWS):
        while span < w:
            run = run + pltpu.roll(run, span, 0)
            span *= 2
        xg = x[:, gi * POOL_GROUP_DIM:(gi + 1) * POOL_GROUP_DIM]
        cnt = jnp.minimum(pos + 1, w).astype(F32)
        diff = run[POOL_HALO:, 0:POOL_GROUP_DIM] / cnt - xg
        ys.append(_dot(diff.astype(BF16), w_ref[gi]))
        run = run[:, POOL_GROUP_DIM:]
    h = jnp.concatenate(ys, axis=1) * scale_ref[...]
    o_ref[0] = _layer_norm(ALPHA * x + h, g_ref[...], b_ref[...])


def _pool_ln(x, w, scale, g, b):
    B, S, D = x.shape
    rows = POOL_ROWS
    assert all(a < b_ for a, b_ in zip(POOL_WINDOWS, POOL_WINDOWS[1:]))
    assert all(w_ & (w_ - 1) == 0 for w_ in POOL_WINDOWS) and POOL_WINDOWS[-1] <= POOL_HALO
    halo_blocks = rows // POOL_HALO
    row2 = lambda v: v.reshape(1, D)
    return pl.pallas_call(
        _pool_ln_kernel,
        grid=(B, S // rows),
        in_specs=[
            pl.BlockSpec((1, rows, D), lambda bi, i: (bi, i, 0)),
            pl.BlockSpec((1, POOL_HALO, D), lambda bi, i: (bi, jnp.maximum(i * halo_blocks - 1, 0), 0)),
            pl.BlockSpec(w.shape, lambda bi, i: (0, 0, 0)),
            pl.BlockSpec((1, D), lambda bi, i: (0, 0)),
            pl.BlockSpec((1, D), lambda bi, i: (0, 0)),
            pl.BlockSpec((1, D), lambda bi, i: (0, 0)),
        ],
        out_specs=pl.BlockSpec((1, rows, D), lambda bi, i: (bi, i, 0)),
        out_shape=jax.ShapeDtypeStruct((B, S, D), F32),
        scratch_shapes=[pltpu.VMEM((rows + POOL_HALO, D), F32)],
        compiler_params=_params("parallel", "arbitrary"),
    )(x, x, w.astype(BF16), row2(scale), row2(g), row2(b))


def _ffn_ln_kernel(x_ref, wg_ref, wu_ref, wd_ref, g_ref, b_ref, o_ref, xb_ref, acc_ref):
    f = pl.program_id(1)

    @pl.when(f == 0)
    def _():
        xb_ref[...] = x_ref[...].astype(BF16)
        acc_ref[...] = jnp.zeros_like(acc_ref)

    xb = xb_ref[...]
    a = _silu(_dot(xb, wg_ref[...])) * _dot(xb, wu_ref[...])
    acc_ref[...] += _dot(a.astype(BF16), wd_ref[...])

    @pl.when(f == pl.num_programs(1) - 1)
    def _():
        o_ref[...] = _layer_norm(ALPHA * x_ref[...] + acc_ref[...], g_ref[...], b_ref[...])


def _ff_chunk(d_ff, target):
    best = LANES
    for c in range(LANES, target + 1, LANES):
        if d_ff % c == 0:
            best = c
    return best


def _ffn_ln(x, w_gu, w_down, g, b):
    T, D = x.shape
    d_ff = w_down.shape[0]
    fc = _ff_chunk(d_ff, 1536)
    nf = d_ff // fc
    rows = FFN_ROWS
    return pl.pallas_call(
        _ffn_ln_kernel,
        grid=(T // rows, nf),
        in_specs=[
            pl.BlockSpec((rows, D), lambda i, f: (i, 0)),
            pl.BlockSpec((D, fc), lambda i, f: (0, f)),
            pl.BlockSpec((D, fc), lambda i, f: (0, nf + f)),
            pl.BlockSpec((fc, D), lambda i, f: (f, 0)),
            pl.BlockSpec((1, D), lambda i, f: (0, 0)),
            pl.BlockSpec((1, D), lambda i, f: (0, 0)),
        ],
        out_specs=pl.BlockSpec((rows, D), lambda i, f: (i, 0)),
        out_shape=jax.ShapeDtypeStruct((T, D), F32),
        scratch_shapes=[pltpu.VMEM((rows, D), BF16), pltpu.VMEM((rows, D), F32)],
        compiler_params=_params("parallel", "arbitrary"),
    )(x, w_gu, w_gu, w_down, g.reshape(1, D), b.reshape(1, D))


def _rope_tables(pos, reps):
    freqs = jnp.power(ROPE_THETA, -jnp.arange(HALF_DIM, dtype=F32) / HALF_DIM)
    ang = pos.astype(F32)[:, None] * freqs[None, :]
    cos, sin = jnp.cos(ang), jnp.sin(ang)
    return (jnp.tile(jnp.concatenate([cos, cos], axis=1), (1, reps)),
            jnp.tile(jnp.concatenate([-sin, sin], axis=1), (1, reps)))


def _in_proj_kernel(x_ref, w_ref, wt_ref, cos_ref, sin_ref, cost_ref, sint_ref,
                    qt_ref, kcv_ref, ks_ref, vst_ref, kw_ref, vwt_ref, gatet_ref, *, steps_per_seq):
    rows = x_ref.shape[0]
    xb = x_ref[...].astype(BF16)
    cos = cos_ref[...]
    sin = sin_ref[...]
    lane = lax.broadcasted_iota(jnp.int32, cos.shape, 1)
    first_half = (lane & (HEAD_DIM - 1)) < HALF_DIM
    seq_step = pl.program_id(0) % steps_per_seq
    pos = seq_step * rows + lax.broadcasted_iota(jnp.int32, (rows, HEAD_DIM), 0)
    col = lax.broadcasted_iota(jnp.int32, (rows, HEAD_DIM), 1)
    blk_onehot = jnp.where((pos >> SEL_SHIFT) == col, 1.0, 0.0)
    zeros = jnp.zeros((rows, HEAD_DIM), F32)

    def col_tile(j):
        return _dot(xb, w_ref[:, j * PROJ_COLS:(j + 1) * PROJ_COLS])

    def row_tile(r0, n):
        return _dot_nt(wt_ref[r0:r0 + n, :], xb)

    def rope(y):
        rot = jnp.where(first_half,
                        pltpu.roll(y, PROJ_COLS - HALF_DIM, 1),
                        pltpu.roll(y, HALF_DIM, 1))
        return y * cos + rot * sin

    def rope_t(yt):
        pieces = []
        for h in range(yt.shape[0] // HEAD_DIM):
            pieces.append(yt[h * HEAD_DIM + HALF_DIM:(h + 1) * HEAD_DIM])
            pieces.append(yt[h * HEAD_DIM:h * HEAD_DIM + HALF_DIM])
        return yt * cost_ref[...] + jnp.concatenate(pieces, axis=0) * sint_ref[...]

    def store_keys(ref, y, extra):
        for gi in range(N_KV_GROUPS):
            ref[0, gi, :, 0:HEAD_DIM] = y[:, gi * HEAD_DIM:(gi + 1) * HEAD_DIM].astype(ref.dtype)
            ref[0, gi, :, HEAD_DIM:] = extra.astype(ref.dtype)

    def store_values_t(ref, yt):
        ones = jnp.ones((VT_ROWS - HEAD_DIM, rows), ref.dtype)
        for gi in range(N_KV_GROUPS):
            ref[0, gi, 0, 0:HEAD_DIM, :] = yt[gi * HEAD_DIM:(gi + 1) * HEAD_DIM].astype(ref.dtype)
            ref[0, gi, 0, HEAD_DIM:, :] = ones

    for j in range(D_MODEL // PROJ_COLS):
        qt = rope_t(row_tile(j * PROJ_COLS, PROJ_COLS)) * Q_SCALE
        qt_ref[0, j * PROJ_COLS:(j + 1) * PROJ_COLS, :] = qt.astype(qt_ref.dtype)
    raw_k = col_tile(0)
    raw_v = col_tile(1)
    for gi in range(N_KV_GROUPS):
        kcv_ref[0, gi, :, 0:HEAD_DIM] = raw_k[:, gi * HEAD_DIM:(gi + 1) * HEAD_DIM]
        kcv_ref[0, gi, :, HEAD_DIM:] = raw_v[:, gi * HEAD_DIM:(gi + 1) * HEAD_DIM]
    store_keys(ks_ref, rope(col_tile(2)), blk_onehot)
    store_keys(kw_ref, rope(col_tile(3)), zeros)
    store_values_t(vst_ref, row_tile(D_MODEL, KV_DIM))
    store_values_t(vwt_ref, row_tile(D_MODEL + KV_DIM, KV_DIM))
    logits_t = row_tile(D_MODEL + 2 * KV_DIM, GATE_ROWS)
    gatet_ref[0] = 1.0 / (1.0 + jnp.exp(-logits_t))


def _in_proj(x, w_in, B, S):
    T, D = x.shape
    rows = PROJ_ROWS
    assert rows == ATT_K_ROWS
    steps_per_seq = S // rows
    sec = lambda k: w_in[:, D_MODEL + k * KV_DIM:D_MODEL + (k + 1) * KV_DIM]
    w = jnp.concatenate([sec(0), sec(1), sec(2), sec(4)], axis=1).astype(BF16)
    wg = w_in[:, D_MODEL + 6 * KV_DIM:].reshape(D, N_KV_GROUPS, N_BRANCHES * HEADS_PER_GROUP)
    wg = jnp.pad(wg, ((0, 0), (0, 0), (0, GATE_SLOTS - N_BRANCHES * HEADS_PER_GROUP))).reshape(D, GATE_ROWS)
    wt = jnp.concatenate([w_in[:, :D_MODEL], sec(3), sec(5), wg], axis=1).T.astype(BF16)
    cos, sin = _rope_tables(jnp.arange(S), PROJ_COLS // HEAD_DIM)
    k_shape = jax.ShapeDtypeStruct((B, N_KV_GROUPS, S, AUG_DIM), BF16)
    k_spec = pl.BlockSpec((1, N_KV_GROUPS, rows, AUG_DIM),
                          lambda i: (i // steps_per_seq, 0, i % steps_per_seq, 0))
    vt_shape = jax.ShapeDtypeStruct((B, N_KV_GROUPS, steps_per_seq, VT_ROWS, rows), BF16)
    vt_spec = pl.BlockSpec((1, N_KV_GROUPS, 1, VT_ROWS, rows),
                           lambda i: (i // steps_per_seq, 0, i % steps_per_seq, 0, 0))
    tok_map = lambda i: (i, 0)
    seq_map = lambda i: (i % steps_per_seq, 0)
    feat_map = lambda i: (i // steps_per_seq, 0, i % steps_per_seq)
    return pl.pallas_call(
        functools.partial(_in_proj_kernel, steps_per_seq=steps_per_seq),
        grid=(T // rows,),
        in_specs=[
            pl.BlockSpec((rows, D), tok_map),
            pl.BlockSpec(w.shape, lambda i: (0, 0)),
            pl.BlockSpec(wt.shape, lambda i: (0, 0)),
            pl.BlockSpec((rows, PROJ_COLS), seq_map),
            pl.BlockSpec((rows, PROJ_COLS), seq_map),
            pl.BlockSpec((PROJ_COLS, rows), lambda i: (0, i % steps_per_seq)),
            pl.BlockSpec((PROJ_COLS, rows), lambda i: (0, i % steps_per_seq)),
        ],
        out_specs=[
            pl.BlockSpec((1, D_MODEL, rows), feat_map),
            k_spec,
            k_spec, vt_spec, k_spec, vt_spec,
            pl.BlockSpec((1, GATE_ROWS, rows), feat_map),
        ],
        out_shape=[
            jax.ShapeDtypeStruct((B, D_MODEL, S), BF16),
            jax.ShapeDtypeStruct((B, N_KV_GROUPS, S, AUG_DIM), F32),
            k_shape, vt_shape, k_shape, vt_shape,
            jax.ShapeDtypeStruct((B, GATE_ROWS, S), F32),
        ],
        compiler_params=_params("parallel"),
    )(x, w, wt, cos, sin, cos.T, sin.T)


def _compress_kernel(kv_ref, w1_ref, pea_ref, peb_ref, w2_ref, w2t_ref, cos_ref, sin_ref, kc_ref, vct_ref):
    n = kc_ref.shape[2]
    hid = CMP_HIDDEN
    r = jnp.zeros((n, 4 * hid), F32)
    bias_a = jnp.zeros((8, 4 * hid), F32)
    bias_b = jnp.zeros((8, 4 * hid), F32)
    for p in range(CMP_STRIDE):
        slab = kv_ref[0, 0, pl.ds(p, n, stride=CMP_STRIDE), :].astype(BF16)
        r = r + _dot(slab, w1_ref[p])
        bias_a = bias_a + _dot(pea_ref[p], w1_ref[p])
        bias_b = bias_b + _dot(peb_ref[p], w1_ref[p])
    hidden = []
    for which in range(2):
        c0 = which * 2 * hid
        nxt = pltpu.roll(r[:, c0 + hid:c0 + 2 * hid], n - 1, 0)
        bias = bias_a[0:1, c0:c0 + hid] + bias_b[0:1, c0 + hid:c0 + 2 * hid]
        hidden.append(_silu(r[:, c0:c0 + hid] + nxt + bias).astype(BF16))
    kc = _dot(hidden[0], w2_ref[...])
    rot = jnp.concatenate([kc[:, HALF_DIM:], kc[:, :HALF_DIM]], axis=1)
    kc_ref[0, 0] = (kc * cos_ref[...] + rot * sin_ref[...]).astype(kc_ref.dtype)
    vct_ref[0, 0] = _dot_nt(w2t_ref[...], hidden[1]).astype(vct_ref.dtype)


def _compress(kv_raw, w1_k, w2_k, pe_k, w1_v, w2_v, pe_v, B, S):
    n_chunks = S // CMP_STRIDE
    hid = CMP_HIDDEN
    split = lambda w1: w1.reshape(2, CMP_STRIDE, HEAD_DIM, hid).transpose(1, 2, 0, 3).reshape(CMP_STRIDE, HEAD_DIM, 2 * hid)
    zeros = jnp.zeros((CMP_STRIDE, HEAD_DIM, 2 * hid), F32)
    w1 = jnp.concatenate([jnp.concatenate([split(w1_k), zeros], axis=2),
                          jnp.concatenate([zeros, split(w1_v)], axis=2)], axis=1).astype(BF16)
    pe = jnp.concatenate([pe_k, pe_v], axis=1)
    rows8 = lambda v: jnp.broadcast_to(v[:, None, :], (CMP_STRIDE, 8, AUG_DIM)).astype(BF16)
    pea, peb = rows8(pe[:CMP_STRIDE]), rows8(pe[CMP_STRIDE:])
    cos, sin = _rope_tables(CMP_STRIDE * jnp.arange(n_chunks) + CMP_BLOCK - 1, 1)
    const = lambda a: pl.BlockSpec(a.shape, lambda bi, gi: (0,) * a.ndim)
    w2k = w2_k.astype(BF16)
    w2vt = w2_v.T.astype(BF16)
    return pl.pallas_call(
        _compress_kernel,
        grid=(B, N_KV_GROUPS),
        in_specs=[
            pl.BlockSpec((1, 1, S, AUG_DIM), lambda bi, gi: (bi, gi, 0, 0)),
            const(w1), const(pea), const(peb), const(w2k), const(w2vt), const(cos), const(sin),
        ],
        out_specs=[pl.BlockSpec((1, 1, n_chunks, HEAD_DIM), lambda bi, gi: (bi, gi, 0, 0)),
                   pl.BlockSpec((1, 1, HEAD_DIM, n_chunks), lambda bi, gi: (bi, gi, 0, 0))],
        out_shape=[jax.ShapeDtypeStruct((B, N_KV_GROUPS, n_chunks, HEAD_DIM), BF16),
                   jax.ShapeDtypeStruct((B, N_KV_GROUPS, HEAD_DIM, n_chunks), BF16)],
        compiler_params=_params("parallel", "parallel"),
    )(kv_raw, w1, pea, peb, w2k, w2vt, cos, sin)


def _cmp_attn_kernel(qt_ref, kc_ref, vct_ref, ovl_ref, oct_ref, selt_ref, score_ref):
    i = pl.program_id(2)
    cols = qt_ref.shape[2]
    n_cmp = kc_ref.shape[2]
    n_sel = ovl_ref.shape[0]
    kc = kc_ref[0, 0]
    vct = vct_ref[0, 0]
    t = i * cols + lax.broadcasted_iota(jnp.int32, (1, cols), 1)
    cend = CMP_STRIDE * lax.broadcasted_iota(jnp.int32, (n_cmp, 1), 0) + (CMP_BLOCK - 1)
    cvalid = cend <= t
    any_valid = jnp.where(t >= CMP_BLOCK - 1, 1.0, 0.0)
    pc_sum = jnp.zeros((n_cmp, cols), F32)
    for h in range(HEADS_PER_GROUP):
        qh = qt_ref[0, h * HEAD_DIM:(h + 1) * HEAD_DIM, :]
        s = jnp.where(cvalid, _dot(kc, qh), NEG_INF)
        e = jnp.exp2(s - jnp.max(s, axis=0, keepdims=True))
        pc = e * (any_valid / jnp.sum(e, axis=0, keepdims=True))
        oct_ref[0, h * HEAD_DIM:(h + 1) * HEAD_DIM, :] = _dot(vct, pc.astype(BF16))
        pc_sum = pc_sum + pc
    imp = _dot(ovl_ref[...], pc_sum.astype(BF16))
    j = lax.broadcasted_iota(jnp.int32, (n_sel, 1), 0)
    blk_t = t >> SEL_SHIFT
    bvalid = j <= blk_t
    forced = (j == 0) | (j == blk_t) | (j == blk_t - 1)
    score = jnp.where(bvalid, imp + jnp.where(forced, FORCE_BONUS, 0.0), -1.0)
    key = pltpu.bitcast(score, jnp.int32)
    key_next = key + 1
    score_ref[...] = key
    def rank_group(grp, rank):
        base = pl.multiple_of(grp * 8, 8)
        others = score_ref[pl.ds(base, 8), :]
        for r in range(8):
            ahead = others[r:r + 1, :] >= jnp.where(j > base + r, key, key_next)
            rank = rank + jnp.where(ahead, 1.0, 0.0)
        return rank

    n_groups = jnp.minimum(((i + 1) * cols) // (8 * SEL_BLOCK), n_sel // 8)
    rank = lax.fori_loop(0, n_groups, rank_group, jnp.zeros((n_sel, cols), F32))
    selected = (rank < float(SEL_TOP_N)) & bvalid
    selt_ref[0, 0] = jnp.where(selected, 0.0, NEG_INF).astype(selt_ref.dtype)


def _cmp_attn(qt, kc, vct, B, S):
    cols = CMP_Q_ROWS
    n_cmp = S // CMP_STRIDE
    n_sel = SEL_SLOTS
    assert S // SEL_BLOCK <= SEL_SLOTS
    cstart = CMP_STRIDE * jnp.arange(n_cmp)
    sstart = SEL_BLOCK * jnp.arange(n_sel)
    overlap = ((cstart[None, :] <= sstart[:, None] + SEL_BLOCK - 1)
               & (cstart[None, :] + CMP_BLOCK - 1 >= sstart[:, None])).astype(BF16)
    q_spec = pl.BlockSpec((1, GROUP_Q_DIM, cols), lambda bi, gi, i: (bi, gi, i))
    return pl.pallas_call(
        _cmp_attn_kernel,
        grid=(B, N_KV_GROUPS, S // cols),
        in_specs=[
            q_spec,
            pl.BlockSpec((1, 1, n_cmp, HEAD_DIM), lambda bi, gi, i: (bi, gi, 0, 0)),
            pl.BlockSpec((1, 1, HEAD_DIM, n_cmp), lambda bi, gi, i: (bi, gi, 0, 0)),
            pl.BlockSpec(overlap.shape, lambda bi, gi, i: (0, 0)),
        ],
        out_specs=[
            q_spec,
            pl.BlockSpec((1, 1, n_sel, cols), lambda bi, gi, i: (bi, gi, 0, i)),
        ],
        out_shape=[
            jax.ShapeDtypeStruct((B, D_MODEL, S), F32),
            jax.ShapeDtypeStruct((B, N_KV_GROUPS, n_sel, S), BF16),
        ],
        scratch_shapes=[pltpu.VMEM((n_sel, cols), jnp.int32)],
        compiler_params=_params("parallel", "parallel", "arbitrary"),
    )(qt, kc, vct, overlap)


def _sel_win_attn_kernel(qt_ref, selt_ref, ks_ref, vst_ref, kw_ref, vwt_ref, diag_ref, band_ref, oct_ref, gatet_ref,
                         o_ref, qaug_ref, m_ref, acc_ref, ow_ref):
    i = pl.program_id(2)
    tq = qt_ref.shape[2]
    hg = HEADS_PER_GROUP
    half = ATT_HALF
    n_half = tq // half
    half_cols = hg * half

    for hq in range(n_half):
        for h in range(hg):
            c0 = (hq * hg + h) * half
            qaug_ref[0:HEAD_DIM, c0:c0 + half] = qt_ref[0, h * HEAD_DIM:(h + 1) * HEAD_DIM, hq * half:(hq + 1) * half]
            qaug_ref[HEAD_DIM:, c0:c0 + half] = selt_ref[0, 0, :, hq * half:(hq + 1) * half]
    n_tiles = (hg * tq) // ATT_COLS

    def keys(ref, start, size):
        return ref[0, 0, pl.ds(pl.multiple_of(start, ATT_HALF), size), :]

    m_ref[...] = jnp.full(m_ref.shape, NEG_INF, F32)
    acc_ref[...] = jnp.zeros(acc_ref.shape, F32)

    def sel_step(c, bias_ref):
        k = keys(ks_ref, c * ATT_K_ROWS, ATT_K_ROWS)
        v = vst_ref[0, 0, c]
        new_m, new_acc = [], []
        tile = lambda ct: slice(ct * ATT_COLS, (ct + 1) * ATT_COLS)
        scores = [_dot(k, qaug_ref[:, tile(ct)]) for ct in range(ATT_AHEAD)]
        for ct in range(n_tiles):
            cs = tile(ct)
            if ct + ATT_AHEAD < n_tiles:
                scores.append(_dot(k, qaug_ref[:, tile(ct + ATT_AHEAD)]))
            s = scores[ct]
            if bias_ref is not None:
                s = s + bias_ref[:, cs]
            m_prev = m_ref[:, cs]
            m_next = jnp.maximum(m_prev, jnp.max(s, axis=0, keepdims=True))
            p = jnp.exp2(s - m_next)
            alpha = jnp.exp2(m_prev - m_next)
            new_acc.append(alpha * acc_ref[:, cs] + _dot(v, p.astype(BF16)))
            new_m.append(m_next)
        for ct in range(n_tiles):
            cs = slice(ct * ATT_COLS, (ct + 1) * ATT_COLS)
            acc_ref[:, cs] = new_acc[ct]
            m_ref[:, cs] = new_m[ct]

    def sel_body(c, carry):
        sel_step(c, None)
        return carry

    lax.fori_loop(0, i, sel_body, 0)
    sel_step(i, diag_ref)

    def probs(s):
        return jnp.exp2(s - jnp.max(s, axis=0, keepdims=True)).astype(BF16)

    @pl.when(i == 0)
    def _():
        k = keys(kw_ref, 0, tq)
        v = vwt_ref[0, 0, 0]
        for ct in range(n_tiles):
            cs = slice(ct * ATT_COLS, (ct + 1) * ATT_COLS)
            ow_ref[:, cs] = _dot(v, probs(_dot(k, qaug_ref[:, cs]) + diag_ref[:, cs]))

    @pl.when(i > 0)
    def _():
        v_prev = vwt_ref[0, 0, i - 1]
        v_here = vwt_ref[0, 0, i]
        for hq in range(n_half):
            k = keys(kw_ref, i * tq + hq * half - WINDOW, WINDOW + half)
            n_prev = tq - hq * half
            for ct in range(half_cols // ATT_COLS):
                cs = slice(hq * half_cols + ct * ATT_COLS, hq * half_cols + (ct + 1) * ATT_COLS)
                p = probs(_dot(k, qaug_ref[:, cs]) + band_ref[:, ct * ATT_COLS:(ct + 1) * ATT_COLS])
                ow_ref[:, cs] = (_dot(v_prev[:, tq - n_prev:], p[0:n_prev])
                                 + _dot(v_here[:, 0:WINDOW + half - n_prev], p[n_prev:]))

    acc_s = acc_ref[...]
    acc_w = ow_ref[...]
    o_s = acc_s[0:HEAD_DIM] / acc_s[HEAD_DIM:HEAD_DIM + 1]
    o_w = acc_w[0:HEAD_DIM] / acc_w[HEAD_DIM:HEAD_DIM + 1]
    for hq in range(n_half):
        tok = slice(hq * half, (hq + 1) * half)
        for h in range(hg):
            c0 = (hq * hg + h) * half
            gc = gatet_ref[0, N_BRANCHES * h + 0:N_BRANCHES * h + 1, tok]
            gs = gatet_ref[0, N_BRANCHES * h + 1:N_BRANCHES * h + 2, tok]
            gw = gatet_ref[0, N_BRANCHES * h + 2:N_BRANCHES * h + 3, tok]
            o = (gc * oct_ref[0, h * HEAD_DIM:(h + 1) * HEAD_DIM, tok]
                 + gs * o_s[:, c0:c0 + half] + gw * o_w[:, c0:c0 + half])
            o_ref[0, tok, h * HEAD_DIM:(h + 1) * HEAD_DIM] = o.T.astype(o_ref.dtype)


def _sel_win_attn(qt, selt, ks, vst, kw, vwt, oct, gatet, B, S):
    tq = ATT_Q_ROWS
    half = ATT_HALF
    hg = HEADS_PER_GROUP
    cols = hg * tq
    n_chunks = S // ATT_K_ROWS
    t_rel = (jnp.arange(tq // half)[:, None, None] * half + jnp.arange(half)[None, None, :])
    t_rel = jnp.broadcast_to(t_rel, (tq // half, hg, half)).reshape(1, cols)
    diag = jnp.where(jnp.arange(ATT_K_ROWS)[:, None] <= t_rel, 0.0, NEG_INF).astype(F32)
    tt = jnp.broadcast_to(jnp.arange(half)[None, :], (hg, half)).reshape(1, hg * half)
    a = jnp.arange(WINDOW + half)[:, None]
    band = jnp.where((a > tt) & (a <= tt + WINDOW), 0.0, NEG_INF).astype(F32)
    k_spec = pl.BlockSpec((1, 1, S, AUG_DIM), lambda bi, gi, i: (bi, gi, 0, 0))
    vt_spec = pl.BlockSpec((1, 1, n_chunks, VT_ROWS, ATT_K_ROWS), lambda bi, gi, i: (bi, gi, 0, 0, 0))
    q_spec = pl.BlockSpec((1, GROUP_Q_DIM, tq), lambda bi, gi, i: (bi, gi, i))
    const = lambda arr: pl.BlockSpec(arr.shape, lambda bi, gi, i: (0, 0))
    return pl.pallas_call(
        _sel_win_attn_kernel,
        grid=(B, N_KV_GROUPS, S // tq),
        in_specs=[
            q_spec,
            pl.BlockSpec((1, 1, SEL_SLOTS, tq), lambda bi, gi, i: (bi, gi, 0, i)),
            k_spec, vt_spec, k_spec, vt_spec,
            const(diag), const(band),
            q_spec,
            pl.BlockSpec((1, GATE_SLOTS, tq), lambda bi, gi, i: (bi, gi, i)),
        ],
        out_specs=pl.BlockSpec((1, tq, GROUP_Q_DIM), lambda bi, gi, i: (bi, i, gi)),
        out_shape=jax.ShapeDtypeStruct((B, S, D_MODEL), BF16),
        scratch_shapes=[
            pltpu.VMEM((AUG_DIM, cols), BF16),
            pltpu.VMEM((1, cols), F32),
            pltpu.VMEM((VT_ROWS, cols), F32),
            pltpu.VMEM((VT_ROWS, cols), F32),
        ],
        compiler_params=_params("parallel", "parallel", "arbitrary"),
    )(qt, selt, ks, vst, kw, vwt, diag, band, oct, gatet)


def _out_proj_ln_kernel(o_ref, w_ref, x_ref, g_ref, b_ref, y_ref):
    h = _dot(o_ref[...], w_ref[...])
    y_ref[...] = _layer_norm(ALPHA * x_ref[...] + h, g_ref[...], b_ref[...])


def _out_proj_ln(o, w_out, x, g, b):
    T, D = x.shape
    rows = PROJ_ROWS
    return pl.pallas_call(
        _out_proj_ln_kernel,
        grid=(T // rows,),
        in_specs=[
            pl.BlockSpec((rows, D), lambda i: (i, 0)),
            pl.BlockSpec((D, D), lambda i: (0, 0)),
            pl.BlockSpec((rows, D), lambda i: (i, 0)),
            pl.BlockSpec((1, D), lambda i: (0, 0)),
            pl.BlockSpec((1, D), lambda i: (0, 0)),
        ],
        out_specs=pl.BlockSpec((rows, D), lambda i: (i, 0)),
        out_shape=jax.ShapeDtypeStruct((T, D), F32),
        compiler_params=_params("parallel"),
    )(o, w_out.astype(BF16), x, g.reshape(1, D), b.reshape(1, D))


def _route_kernel(x_ref, r_ref, tri_ref, xb_ref, info_ref, wcol_ref, cnt_ref, carry_ref):
    @pl.when(pl.program_id(0) == 0)
    def _():
        carry_ref[...] = jnp.zeros_like(carry_ref)

    x = x_ref[...]
    xb = x.astype(BF16)
    xb_ref[...] = xb
    x_lo = (x - xb.astype(F32)).astype(BF16)
    r_hi = r_ref[0]
    r_lo = r_ref[1]
    logits = _dot(xb, r_hi) + (_dot(x_lo, r_hi) + _dot(xb, r_lo))
    lane = lax.broadcasted_iota(jnp.int32, logits.shape, 1)
    logits = jnp.where(lane < N_EXPERTS, logits, -jnp.inf)
    v1 = jnp.max(logits, axis=-1, keepdims=True)
    i1 = jnp.min(jnp.where(logits == v1, lane, LANES), axis=-1, keepdims=True)
    rest = jnp.where(lane == i1, -jnp.inf, logits)
    v2 = jnp.max(rest, axis=-1, keepdims=True)
    i2 = jnp.min(jnp.where(rest == v2, lane, LANES), axis=-1, keepdims=True)
    e2 = jnp.exp(v2 - v1)
    w1 = 1.0 / (1.0 + e2)
    w2 = e2 / (1.0 + e2)
    m1 = jnp.where(lane == i1, 1.0, 0.0)
    m2 = jnp.where(lane == i2, 1.0, 0.0)
    routed = m1 + m2
    before = _dot(tri_ref[...], routed.astype(BF16)) + carry_ref[0:1, :]
    rank1 = jnp.sum(m1 * before, axis=-1, keepdims=True)
    rank2 = jnp.sum(m2 * before, axis=-1, keepdims=True)
    cnt = jnp.sum(routed, axis=0, keepdims=True)
    carry_ref[...] = carry_ref[...] + cnt
    cnt_ref[0] = jnp.broadcast_to(cnt, cnt_ref.shape[1:])
    info = jnp.where(lane == 0, i1.astype(F32),
                     jnp.where(lane == 1, i2.astype(F32),
                               jnp.where(lane == 2, rank1, jnp.where(lane == 3, rank2, 0.0))))
    info_ref[0] = info.T[0:8, :]
    w1_hi = w1.astype(BF16).astype(F32)
    w2_hi = w2.astype(BF16).astype(F32)
    lo_half = (lane & 1) == 1
    wcol = (jnp.where((lane >> 1) == i1, jnp.where(lo_half, w1 - w1_hi, w1_hi), 0.0)
            + jnp.where((lane >> 1) == i2, jnp.where(lo_half, w2 - w2_hi, w2_hi), 0.0))
    wcol_ref[...] = wcol.astype(BF16)


def _route(x, router):
    T, D = x.shape
    rows = MOE_ROWS
    r = jnp.pad(router, ((0, 0), (0, LANES - router.shape[1])))
    r_hi = r.astype(BF16)
    r_lo = (r - r_hi.astype(F32)).astype(BF16)
    r2 = jnp.stack([r_hi, r_lo])
    tri = (jnp.arange(rows)[None, :] < jnp.arange(rows)[:, None]).astype(BF16)
    return pl.pallas_call(
        _route_kernel,
        grid=(T // rows,),
        in_specs=[
            pl.BlockSpec((rows, D), lambda i: (i, 0)),
            pl.BlockSpec(r2.shape, lambda i: (0, 0, 0)),
            pl.BlockSpec(tri.shape, lambda i: (0, 0)),
        ],
        out_specs=[
            pl.BlockSpec((rows, D), lambda i: (i, 0)),
            pl.BlockSpec((1, 8, rows), lambda i: (i, 0, 0)),
            pl.BlockSpec((rows, LANES), lambda i: (i, 0)),
            pl.BlockSpec((1, 8, LANES), lambda i: (i, 0, 0)),
        ],
        out_shape=[
            jax.ShapeDtypeStruct((T, D), BF16),
            jax.ShapeDtypeStruct((T // rows, 8, rows), F32),
            jax.ShapeDtypeStruct((T, LANES), BF16),
            jax.ShapeDtypeStruct((T // rows, 8, LANES), F32),
        ],
        scratch_shapes=[pltpu.VMEM((8, LANES), F32)],
        compiler_params=_params("arbitrary"),
    )(x, r2, tri)


def _moe_tables(cnt, n_blocks_max, n_items_max):
    R = MOE_ROWS
    C, E = cnt.shape
    i32 = jnp.int32
    count_le = lambda sorted_v, q: jnp.sum(sorted_v[None, :] <= q[:, None], axis=1).astype(i32)
    cum = jnp.concatenate([jnp.zeros((1, E), i32), jnp.cumsum(cnt, axis=0)], axis=0)
    tot = cum[-1]
    nb = (tot + R - 1) // R
    nb_end = jnp.cumsum(nb)
    blk_start = nb_end - nb
    n_blocks = nb_end[-1]
    b = jnp.minimum(jnp.arange(n_blocks_max, dtype=i32), n_blocks - 1)
    bexp = jnp.minimum(count_le(nb_end, b), E - 1)
    bvalid = jnp.arange(n_blocks_max, dtype=i32) < n_blocks
    lb = b - blk_start[bexp]
    rho0 = lb * R
    rho1 = jnp.minimum((lb + 1) * R, tot[bexp]) - 1
    cum_b = cum[1:, :][:, bexp].T
    lo = jnp.minimum(jnp.sum(cum_b <= rho0[:, None], axis=1).astype(i32), C - 1)
    hi = jnp.minimum(jnp.sum(cum_b <= rho1[:, None], axis=1).astype(i32), C - 1)
    nit = jnp.where(bvalid, hi - lo + 1, 0)
    it_end = jnp.cumsum(nit)
    it_start = it_end - nit
    n_items = it_end[-1]
    i = jnp.arange(n_items_max, dtype=i32)
    ic = jnp.minimum(i, n_items - 1)
    d_blk = jnp.minimum(count_le(it_end, ic), n_blocks_max - 1)
    d_chk = lo[d_blk] + ic - it_start[d_blk]
    d_valid = i < n_items
    d_first = d_valid & (ic == it_start[d_blk])
    d_last = d_valid & (ic == it_end[d_blk] - 1)
    d_exp = bexp[d_blk]
    d_ra = jnp.maximum(rho0[d_blk], cum[d_chk, d_exp]) - rho0[d_blk]
    d_rb = jnp.minimum(rho1[d_blk], cum[d_chk + 1, d_exp] - 1) - rho0[d_blk]
    d_sub_lo = jnp.where(d_valid & (d_rb >= d_ra), d_ra // DISPATCH_SUB, 1)
    d_sub_hi = jnp.where(d_valid & (d_rb >= d_ra), d_rb // DISPATCH_SUB, 0)
    bl_lo = (blk_start[None, :] + cum[:-1] // R).reshape(-1)
    bl_hi = (blk_start[None, :] + (cum[1:] - 1) // R).reshape(-1)
    npair = jnp.where(cnt.reshape(-1) > 0, bl_hi - bl_lo + 1, 0)
    p_end = jnp.cumsum(npair)
    p_start = p_end - npair
    n_items2 = p_end[-1]
    jc = jnp.minimum(i, n_items2 - 1)
    pair = jnp.minimum(count_le(p_end, jc), C * E - 1)
    c_blk = bl_lo[pair] + jc - p_start[pair]
    c_chk = pair // E
    c_valid = i < n_items2
    prev_chk = jnp.concatenate([jnp.full((1,), -1, i32), c_chk[:-1]])
    next_chk = jnp.concatenate([c_chk[1:], jnp.full((1,), -1, i32)])
    next_valid = jnp.concatenate([c_valid[1:], jnp.zeros((1,), bool)])
    c_first = c_valid & (c_chk != prev_chk)
    c_last = c_valid & ((c_chk != next_chk) | ~next_valid)
    c_exp = pair % E
    c_base = (c_blk - blk_start[c_exp]) * R
    c_ra = jnp.maximum(cum[c_chk, c_exp] - c_base, 0)
    c_rb = jnp.minimum(cum[c_chk + 1, c_exp] - 1 - c_base, R - 1)
    c_sub_lo = jnp.where(c_valid, c_ra // COMBINE_SUB, 1)
    c_sub_hi = jnp.where(c_valid, c_rb // COMBINE_SUB, 0)
    as_i32 = lambda v: v.astype(i32)
    return dict(base=blk_start * R, bexp=bexp, bvalid=as_i32(bvalid),
                d_blk=d_blk, d_chk=d_chk, d_first=as_i32(d_first), d_last=as_i32(d_last), d_sub_lo=d_sub_lo, d_sub_hi=d_sub_hi,
                c_blk=c_blk, c_chk=c_chk, c_first=as_i32(c_first), c_last=as_i32(c_last),
                c_sub_lo=c_sub_lo, c_sub_hi=c_sub_hi)


def _dispatch_kernel(blk_ref, chk_ref, first_ref, last_ref, lo_ref, hi_ref, xb_ref, pos_ref, wcol_ref,
                     xs_ref, ws_ref, acc_ref, wacc_ref):
    i = pl.program_id(0)
    R = xs_ref.shape[0]

    @pl.when(first_ref[i] == 1)
    def _():
        acc_ref[...] = jnp.zeros_like(acc_ref)
        wacc_ref[...] = jnp.zeros_like(wacc_ref)

    pos1 = pos_ref[0, 0:1, :]
    pos2 = pos_ref[0, 1:2, :]

    def sub_tile(k, carry):
        r0 = pl.multiple_of(k * DISPATCH_SUB, DISPATCH_SUB)
        rows = (blk_ref[i] * R + r0 + lax.broadcasted_iota(jnp.int32, (DISPATCH_SUB, 1), 0)).astype(F32)
        hit = jnp.where((pos1 - rows) * (pos2 - rows) == 0.0, 1.0, 0.0).astype(BF16)
        acc_ref[pl.ds(r0, DISPATCH_SUB), :] += _dot(hit, xb_ref[...])
        wacc_ref[pl.ds(r0, DISPATCH_SUB), :] += _dot(hit, wcol_ref[...])
        return carry

    lax.fori_loop(lo_ref[i], hi_ref[i] + 1, sub_tile, 0)

    @pl.when(last_ref[i] == 1)
    def _():
        xs_ref[...] = acc_ref[...].astype(xs_ref.dtype)
        ws_ref[...] = wacc_ref[...]


def _dispatch(xb, posrow, wcol, tab, n_blocks_max, n_items_max):
    T, D = xb.shape
    R = MOE_ROWS
    chunk_map = lambda i, blk, chk, first, last, lo, hi: (chk[i], 0)
    block_map = lambda i, blk, chk, first, last, lo, hi: (blk[i], 0)
    grid_spec = pltpu.PrefetchScalarGridSpec(
        num_scalar_prefetch=6,
        grid=(n_items_max,),
        in_specs=[
            pl.BlockSpec((R, D), chunk_map),
            pl.BlockSpec((1, 8, R), lambda i, blk, chk, first, last, lo, hi: (chk[i], 0, 0)),
            pl.BlockSpec((R, LANES), chunk_map),
        ],
        out_specs=[pl.BlockSpec((R, D), block_map), pl.BlockSpec((R, LANES), block_map)],
        scratch_shapes=[pltpu.VMEM((R, D), F32), pltpu.VMEM((R, LANES), F32)],
    )
    return pl.pallas_call(
        _dispatch_kernel,
        grid_spec=grid_spec,
        out_shape=[
            jax.ShapeDtypeStruct((n_blocks_max * R, D), BF16),
            jax.ShapeDtypeStruct((n_blocks_max * R, LANES), F32),
        ],
        compiler_params=_params("arbitrary"),
    )(tab["d_blk"], tab["d_chk"], tab["d_first"], tab["d_last"], tab["d_sub_lo"], tab["d_sub_hi"], xb, posrow, wcol)


def _expert_ffn_kernel(bexp_ref, bvalid_ref, xs_ref, ws_ref, wg_ref, wu_ref, wd_ref, ys_ref, acc_ref):
    b = pl.program_id(0)
    f = pl.program_id(1)

    @pl.when(f == 0)
    def _():
        acc_ref[...] = jnp.zeros_like(acc_ref)

    @pl.when(bvalid_ref[b] == 1)
    def _():
        xb = xs_ref[...]
        a = _silu(_dot(xb, wg_ref[0])) * _dot(xb, wu_ref[0])
        acc_ref[...] += _dot(a.astype(BF16), wd_ref[0])

    @pl.when(f == pl.num_programs(1) - 1)
    def _():
        ws = ws_ref[...]
        lane = lax.broadcasted_iota(jnp.int32, ws.shape, 1)
        w = jnp.sum(jnp.where((lane >> 1) == bexp_ref[b], ws, 0.0), axis=-1, keepdims=True)
        ys_ref[...] = (acc_ref[...] * w).astype(ys_ref.dtype)


def _expert_ffn(xs, ws, w_gu, w_down, tab, n_blocks_max):
    D = xs.shape[1]
    R = MOE_ROWS
    d_ff = w_down.shape[1]
    fc = _ff_chunk(d_ff, 1792)
    nf = d_ff // fc
    f_eff = lambda b, f, bvalid: f * bvalid[b] + (nf - 1) * (1 - bvalid[b])
    grid_spec = pltpu.PrefetchScalarGridSpec(
        num_scalar_prefetch=2,
        grid=(n_blocks_max, nf),
        in_specs=[
            pl.BlockSpec((R, D), lambda b, f, bexp, bvalid: (b, 0)),
            pl.BlockSpec((R, LANES), lambda b, f, bexp, bvalid: (b, 0)),
            pl.BlockSpec((1, D, fc), lambda b, f, bexp, bvalid: (bexp[b], 0, f_eff(b, f, bvalid))),
            pl.BlockSpec((1, D, fc), lambda b, f, bexp, bvalid: (bexp[b], 0, nf + f_eff(b, f, bvalid))),
            pl.BlockSpec((1, fc, D), lambda b, f, bexp, bvalid: (bexp[b], f_eff(b, f, bvalid), 0)),
        ],
        out_specs=pl.BlockSpec((R, D), lambda b, f, bexp, bvalid: (b, 0)),
        scratch_shapes=[pltpu.VMEM((R, D), F32)],
    )
    return pl.pallas_call(
        _expert_ffn_kernel,
        grid_spec=grid_spec,
        out_shape=jax.ShapeDtypeStruct(xs.shape, BF16),
        compiler_params=_params("parallel", "arbitrary"),
    )(tab["bexp"], tab["bvalid"], xs, ws, w_gu, w_gu, w_down)


def _combine_ln_kernel(chk_ref, blk_ref, first_ref, last_ref, lo_ref, hi_ref,
                       ys_ref, pc1_ref, pc2_ref, x_ref, g_ref, b_ref, o_ref, acc_ref):
    i = pl.program_id(0)
    R = ys_ref.shape[0]

    @pl.when(first_ref[i] == 1)
    def _():
        acc_ref[...] = jnp.zeros_like(acc_ref)

    def sub_tile(k, carry):
        r0 = pl.multiple_of(k * COMBINE_SUB, COMBINE_SUB)
        pc1 = pc1_ref[...]
        pc2 = pc2_ref[...]
        lane = lax.broadcasted_iota(jnp.int32, pc1.shape, 1)
        parts = []
        for t in range(COMBINE_SUB // LANES):
            col = (blk_ref[i] * R + r0 + t * LANES + lane).astype(F32)
            parts.append(jnp.where((pc1 - col) * (pc2 - col) == 0.0, 1.0, 0.0).astype(BF16))
        acc_ref[...] += _dot(jnp.concatenate(parts, axis=1), ys_ref[pl.ds(r0, COMBINE_SUB), :])
        return carry

    lax.fori_loop(lo_ref[i], hi_ref[i] + 1, sub_tile, 0)

    @pl.when(last_ref[i] == 1)
    def _():
        o_ref[...] = _layer_norm(ALPHA * x_ref[...] + acc_ref[...], g_ref[...], b_ref[...])


def _combine_ln(ys, poscol1, poscol2, x, g, b, tab, n_items_max):
    T, D = x.shape
    R = MOE_ROWS
    chunk_map = lambda i, chk, blk, first, last, lo, hi: (chk[i], 0)
    const_map = lambda i, chk, blk, first, last, lo, hi: (0, 0)
    grid_spec = pltpu.PrefetchScalarGridSpec(
        num_scalar_prefetch=6,
        grid=(n_items_max,),
        in_specs=[
            pl.BlockSpec((R, D), lambda i, chk, blk, first, last, lo, hi: (blk[i], 0)),
            pl.BlockSpec((R, LANES), chunk_map),
            pl.BlockSpec((R, LANES), chunk_map),
            pl.BlockSpec((R, D), chunk_map),
            pl.BlockSpec((1, D), const_map),
            pl.BlockSpec((1, D), const_map),
        ],
        out_specs=pl.BlockSpec((R, D), chunk_map),
        scratch_shapes=[pltpu.VMEM((R, D), F32)],
    )
    return pl.pallas_call(
        _combine_ln_kernel,
        grid_spec=grid_spec,
        out_shape=jax.ShapeDtypeStruct((T, D), F32),
        compiler_params=_params("arbitrary"),
    )(tab["c_chk"], tab["c_blk"], tab["c_first"], tab["c_last"], tab["c_sub_lo"], tab["c_sub_hi"],
      ys, poscol1, poscol2, x, g.reshape(1, D), b.reshape(1, D))


def _moe_ln(x, router, w_gu, w_down, g, b):
    T, D = x.shape
    R = MOE_ROWS
    n_exp = w_down.shape[0]
    n_chunks = T // R
    n_blocks_max = (2 * T) // R + n_exp
    n_items_max = n_blocks_max + n_exp * (n_chunks - 1)
    xb, info, wcol, cnt = _route(x, router)
    tab = _moe_tables(cnt[:, 0, :n_exp].astype(jnp.int32), n_blocks_max, n_items_max)
    field = lambda k: info[:, k, :].reshape(T).astype(jnp.int32)
    pos1 = tab["base"][field(0)] + field(2)
    pos2 = tab["base"][field(1)] + field(3)
    pos1 = pos1.astype(F32)
    pos2 = pos2.astype(F32)
    posrow = jnp.stack([pos1.reshape(n_chunks, R), pos2.reshape(n_chunks, R)], axis=1)
    posrow = jnp.pad(posrow, ((0, 0), (0, 6), (0, 0)), constant_values=-1.0)
    poscol1 = jnp.broadcast_to(pos1[:, None], (T, LANES))
    poscol2 = jnp.broadcast_to(pos2[:, None], (T, LANES))
    xs, ws = _dispatch(xb, posrow, wcol, tab, n_blocks_max, n_items_max)
    ys = _expert_ffn(xs, ws, w_gu, w_down, tab, n_blocks_max)
    return _combine_ln(ys, poscol1, poscol2, x, g, b, tab, n_items_max)


def kernel(x, ln_g, ln_b, pool_w, pool_scale, nsa_w_in, nsa_pe_k, nsa_w1_k, nsa_w2_k, nsa_pe_v, nsa_w1_v,
           nsa_w2_v, nsa_w_out, ffn_w_gu, ffn_w_down, moe_router, moe_w_gu, moe_w_down):
    B, S, D = x.shape
    T = B * S
    xa = _pool_ln(x, pool_w[0], pool_scale[0], ln_g[0, 0], ln_b[0, 0]).reshape(T, D)
    x1 = _ffn_ln(xa, ffn_w_gu[0].astype(BF16), ffn_w_down[0].astype(BF16), ln_g[0, 1], ln_b[0, 1])
    qt, kcv, ks, vst, kw, vwt, gatet = _in_proj(x1, nsa_w_in[0], B, S)
    kc, vct = _compress(kcv, nsa_w1_k[0], nsa_w2_k[0], nsa_pe_k[0], nsa_w1_v[0], nsa_w2_v[0], nsa_pe_v[0], B, S)
    oct, selt = _cmp_attn(qt, kc, vct, B, S)
    o = _sel_win_attn(qt, selt, ks, vst, kw, vwt, oct, gatet, B, S).reshape(T, D)
    x2 = _out_proj_ln(o, nsa_w_out[0], x1, ln_g[1, 0], ln_b[1, 0])
    y = _moe_ln(x2, moe_router[0], moe_w_gu[0].astype(BF16), moe_w_down[0].astype(BF16), ln_g[1, 1], ln_b[1, 1])
    return y.reshape(B, S, D)
```

```python
import functools

import jax
import jax.numpy as jnp
from jax import lax
from jax.experimental import pallas as pl
from jax.experimental.pallas import tpu as pltpu

D_MODEL = 1024
DEPTH = 2
POOL_WINDOWS = (2, 4, 8, 16)
POOL_GROUP_DIM = D_MODEL // len(POOL_WINDOWS)
POOL_HALO = 16
N_HEADS = 16
N_KV_GROUPS = 4
HEADS_PER_GROUP = N_HEADS // N_KV_GROUPS
HEAD_DIM = D_MODEL // N_HEADS
HALF_DIM = HEAD_DIM // 2
GROUP_Q_DIM = HEADS_PER_GROUP * HEAD_DIM
KV_DIM = N_KV_GROUPS * HEAD_DIM
N_BRANCHES = 3
N_GATES = N_BRANCHES * N_HEADS
CMP_STRIDE = 16
CMP_BLOCK = 2 * CMP_STRIDE
CMP_HIDDEN = 2 * HEAD_DIM
SEL_BLOCK = 64
SEL_SHIFT = SEL_BLOCK.bit_length() - 1
SEL_TOP_N = 16
WINDOW = 512
FORCE_BONUS = 1.0e3
NEG_INF = -1.0e30
ROPE_THETA = 10000.0
ATTN_SCALE = HEAD_DIM ** -0.5
LOG2E = 1.4426950408889634
Q_SCALE = ATTN_SCALE * LOG2E
AUG_DIM = 2 * HEAD_DIM
SEL_SLOTS = AUG_DIM - HEAD_DIM
VT_ROWS = HEAD_DIM + 16
GATE_SLOTS = 16
GATE_ROWS = N_KV_GROUPS * GATE_SLOTS
N_EXPERTS = 8
LN_EPS = 1e-5
ALPHA = (2 * DEPTH) ** 0.25

LANES = 128
VMEM_LIMIT_BYTES = 56 * 1024 * 1024

POOL_ROWS = 512
FFN_ROWS = 512
PROJ_ROWS = 512
PROJ_COLS = 256
CMP_Q_ROWS = 512
ATT_Q_ROWS = 512
ATT_K_ROWS = 512
ATT_HALF = 128
ATT_COLS = 512
ATT_AHEAD = 3
MOE_ROWS = 512
MOE_WIN = 256
MOE_ALIGN = 16

F32 = jnp.float32
BF16 = jnp.bfloat16


def _dot(a, b):
    return jnp.dot(a, b, preferred_element_type=F32)


def _dot_nt(a, b):
    return lax.dot_general(a, b, (((1,), (1,)), ((), ())), preferred_element_type=F32)


def _layer_norm(z, g, b):
    mu = jnp.mean(z, axis=-1, keepdims=True)
    zc = z - mu
    var = jnp.mean(zc * zc, axis=-1, keepdims=True)
    return zc * lax.rsqrt(var + LN_EPS) * g + b


def _silu(x):
    return x / (1.0 + jnp.exp(-x))


def _params(*semantics):
    return pltpu.CompilerParams(dimension_semantics=semantics, vmem_limit_bytes=VMEM_LIMIT_BYTES)


def _pool_ln_kernel(x_ref, halo_ref, w_ref, scale_ref, g_ref, b_ref, o_ref, ext_ref):
    i = pl.program_id(1)
    rows = x_ref.shape[1]
    x = x_ref[0]
    ext_ref[0:POOL_HALO, :] = jnp.where(i > 0, halo_ref[0], 0.0)
    ext_ref[POOL_HALO:, :] = x
    pos = i * rows + lax.broadcasted_iota(jnp.int32, (rows, 1), 0)
    ys = []
    run = ext_ref[...]
    span = 1
    for gi, w in enumerate(POOL_WINDOWS):
        while span < w:
            run = run + pltpu.roll(run, span, 0)
            span *= 2
        xg = x[:, gi * POOL_GROUP_DIM:(gi + 1) * POOL_GROUP_DIM]
        cnt = jnp.minimum(pos + 1, w).astype(F32)
        diff = run[POOL_HALO:, 0:POOL_GROUP_DIM] / cnt - xg
        ys.append(_dot(diff.astype(BF16), w_ref[gi]))
        run = run[:, POOL_GROUP_DIM:]
    h = jnp.concatenate(ys, axis=1) * scale_ref[...]
    o_ref[0] = _layer_norm(ALPHA * x + h, g_ref[...], b_ref[...])


def _pool_ln(x, w, scale, g, b):
    B, S, D = x.shape
    rows = POOL_ROWS
    assert all(a < b_ for a, b_ in zip(POOL_WINDOWS, POOL_WINDOWS[1:]))
    assert all(w_ & (w_ - 1) == 0 for w_ in POOL_WINDOWS) and POOL_WINDOWS[-1] <= POOL_HALO
    halo_blocks = rows // POOL_HALO
    row2 = lambda v: v.reshape(1, D)
    return pl.pallas_call(
        _pool_ln_kernel,
        grid=(B, S // rows),
        in_specs=[
            pl.BlockSpec((1, rows, D), lambda bi, i: (bi, i, 0)),
            pl.BlockSpec((1, POOL_HALO, D), lambda bi, i: (bi, jnp.maximum(i * halo_blocks - 1, 0), 0)),
            pl.BlockSpec(w.shape, lambda bi, i: (0, 0, 0)),
            pl.BlockSpec((1, D), lambda bi, i: (0, 0)),
            pl.BlockSpec((1, D), lambda bi, i: (0, 0)),
            pl.BlockSpec((1, D), lambda bi, i: (0, 0)),
        ],
        out_specs=pl.BlockSpec((1, rows, D), lambda bi, i: (bi, i, 0)),
        out_shape=jax.ShapeDtypeStruct((B, S, D), F32),
        scratch_shapes=[pltpu.VMEM((rows + POOL_HALO, D), F32)],
        compiler_params=_params("parallel", "arbitrary"),
    )(x, x, w.astype(BF16), row2(scale), row2(g), row2(b))


def _ffn_ln_kernel(x_ref, wg_ref, wu_ref, wd_ref, g_ref, b_ref, o_ref, xb_ref, acc_ref):
    f = pl.program_id(1)

    @pl.when(f == 0)
    def _():
        xb_ref[...] = x_ref[...].astype(BF16)
        acc_ref[...] = jnp.zeros_like(acc_ref)

    xb = xb_ref[...]
    a = _silu(_dot(xb, wg_ref[...])) * _dot(xb, wu_ref[...])
    acc_ref[...] += _dot(a.astype(BF16), wd_ref[...])

    @pl.when(f == pl.num_programs(1) - 1)
    def _():
        o_ref[...] = _layer_norm(ALPHA * x_ref[...] + acc_ref[...], g_ref[...], b_ref[...])


def _ff_chunk(d_ff, target):
    best = LANES
    for c in range(LANES, target + 1, LANES):
        if d_ff % c == 0:
            best = c
    return best


def _ffn_ln(x, w_gu, w_down, g, b):
    T, D = x.shape
    d_ff = w_down.shape[0]
    fc = _ff_chunk(d_ff, 1536)
    nf = d_ff // fc
    rows = FFN_ROWS
    return pl.pallas_call(
        _ffn_ln_kernel,
        grid=(T // rows, nf),
        in_specs=[
            pl.BlockSpec((rows, D), lambda i, f: (i, 0)),
            pl.BlockSpec((D, fc), lambda i, f: (0, f)),
            pl.BlockSpec((D, fc), lambda i, f: (0, nf + f)),
            pl.BlockSpec((fc, D), lambda i, f: (f, 0)),
            pl.BlockSpec((1, D), lambda i, f: (0, 0)),
            pl.BlockSpec((1, D), lambda i, f: (0, 0)),
        ],
        out_specs=pl.BlockSpec((rows, D), lambda i, f: (i, 0)),
        out_shape=jax.ShapeDtypeStruct((T, D), F32),
        scratch_shapes=[pltpu.VMEM((rows, D), BF16), pltpu.VMEM((rows, D), F32)],
        compiler_params=_params("parallel", "arbitrary"),
    )(x, w_gu, w_gu, w_down, g.reshape(1, D), b.reshape(1, D))


def _rope_tables(pos, reps):
    freqs = jnp.power(ROPE_THETA, -jnp.arange(HALF_DIM, dtype=F32) / HALF_DIM)
    ang = pos.astype(F32)[:, None] * freqs[None, :]
    cos, sin = jnp.cos(ang), jnp.sin(ang)
    return (jnp.tile(jnp.concatenate([cos, cos], axis=1), (1, reps)),
            jnp.tile(jnp.concatenate([-sin, sin], axis=1), (1, reps)))


def _in_proj_kernel(x_ref, w_ref, wt_ref, cos_ref, sin_ref, cost_ref, sint_ref,
                    qt_ref, kcv_ref, ks_ref, vst_ref, kw_ref, vwt_ref, gatet_ref, *, steps_per_seq):
    rows = x_ref.shape[0]
    xb = x_ref[...].astype(BF16)
    cos = cos_ref[...]
    sin = sin_ref[...]
    lane = lax.broadcasted_iota(jnp.int32, cos.shape, 1)
    first_half = (lane & (HEAD_DIM - 1)) < HALF_DIM
    seq_step = pl.program_id(0) % steps_per_seq
    pos = seq_step * rows + lax.broadcasted_iota(jnp.int32, (rows, HEAD_DIM), 0)
    col = lax.broadcasted_iota(jnp.int32, (rows, HEAD_DIM), 1)
    blk_onehot = jnp.where((pos >> SEL_SHIFT) == col, 1.0, 0.0)
    zeros = jnp.zeros((rows, HEAD_DIM), F32)

    def col_tile(j):
        return _dot(xb, w_ref[:, j * PROJ_COLS:(j + 1) * PROJ_COLS])

    def row_tile(r0, n):
        return _dot_nt(wt_ref[r0:r0 + n, :], xb)

    def rope(y):
        rot = jnp.where(first_half,
                        pltpu.roll(y, PROJ_COLS - HALF_DIM, 1),
                        pltpu.roll(y, HALF_DIM, 1))
        return y * cos + rot * sin

    def rope_t(yt):
        pieces = []
        for h in range(yt.shape[0] // HEAD_DIM):
            pieces.append(yt[h * HEAD_DIM + HALF_DIM:(h + 1) * HEAD_DIM])
            pieces.append(yt[h * HEAD_DIM:h * HEAD_DIM + HALF_DIM])
        return yt * cost_ref[...] + jnp.concatenate(pieces, axis=0) * sint_ref[...]

    def store_keys(ref, y, extra):
        for gi in range(N_KV_GROUPS):
            ref[0, gi, :, 0:HEAD_DIM] = y[:, gi * HEAD_DIM:(gi + 1) * HEAD_DIM].astype(ref.dtype)
            ref[0, gi, :, HEAD_DIM:] = extra.astype(ref.dtype)

    def store_values_t(ref, yt):
        ones = jnp.ones((VT_ROWS - HEAD_DIM, rows), ref.dtype)
        for gi in range(N_KV_GROUPS):
            ref[0, gi, 0, 0:HEAD_DIM, :] = yt[gi * HEAD_DIM:(gi + 1) * HEAD_DIM].astype(ref.dtype)
            ref[0, gi, 0, HEAD_DIM:, :] = ones

    for j in range(D_MODEL // PROJ_COLS):
        qt = rope_t(row_tile(j * PROJ_COLS, PROJ_COLS)) * Q_SCALE
        qt_ref[0, j * PROJ_COLS:(j + 1) * PROJ_COLS, :] = qt.astype(qt_ref.dtype)
    raw_k = col_tile(0)
    raw_v = col_tile(1)
    for gi in range(N_KV_GROUPS):
        kcv_ref[0, gi, :, 0:HEAD_DIM] = raw_k[:, gi * HEAD_DIM:(gi + 1) * HEAD_DIM]
        kcv_ref[0, gi, :, HEAD_DIM:] = raw_v[:, gi * HEAD_DIM:(gi + 1) * HEAD_DIM]
    store_keys(ks_ref, rope(col_tile(2)), blk_onehot)
    store_keys(kw_ref, rope(col_tile(3)), zeros)
    store_values_t(vst_ref, row_tile(D_MODEL, KV_DIM))
    store_values_t(vwt_ref, row_tile(D_MODEL + KV_DIM, KV_DIM))
    logits_t = row_tile(D_MODEL + 2 * KV_DIM, GATE_ROWS)
    gatet_ref[0] = 1.0 / (1.0 + jnp.exp(-logits_t))


def _in_proj(x, w_in, B, S):
    T, D = x.shape
    rows = PROJ_ROWS
    assert rows == ATT_K_ROWS
    steps_per_seq = S // rows
    sec = lambda k: w_in[:, D_MODEL + k * KV_DIM:D_MODEL + (k + 1) * KV_DIM]
    w = jnp.concatenate([sec(0), sec(1), sec(2), sec(4)], axis=1).astype(BF16)
    wg = w_in[:, D_MODEL + 6 * KV_DIM:].reshape(D, N_KV_GROUPS, N_BRANCHES * HEADS_PER_GROUP)
    wg = jnp.pad(wg, ((0, 0), (0, 0), (0, GATE_SLOTS - N_BRANCHES * HEADS_PER_GROUP))).reshape(D, GATE_ROWS)
    wt = jnp.concatenate([w_in[:, :D_MODEL], sec(3), sec(5), wg], axis=1).T.astype(BF16)
    cos, sin = _rope_tables(jnp.arange(S), PROJ_COLS // HEAD_DIM)
    k_shape = jax.ShapeDtypeStruct((B, N_KV_GROUPS, S, AUG_DIM), BF16)
    k_spec = pl.BlockSpec((1, N_KV_GROUPS, rows, AUG_DIM),
                          lambda i: (i // steps_per_seq, 0, i % steps_per_seq, 0))
    vt_shape = jax.ShapeDtypeStruct((B, N_KV_GROUPS, steps_per_seq, VT_ROWS, rows), BF16)
    vt_spec = pl.BlockSpec((1, N_KV_GROUPS, 1, VT_ROWS, rows),
                           lambda i: (i // steps_per_seq, 0, i % steps_per_seq, 0, 0))
    tok_map = lambda i: (i, 0)
    seq_map = lambda i: (i % steps_per_seq, 0)
    feat_map = lambda i: (i // steps_per_seq, 0, i % steps_per_seq)
    return pl.pallas_call(
        functools.partial(_in_proj_kernel, steps_per_seq=steps_per_seq),
        grid=(T // rows,),
        in_specs=[
            pl.BlockSpec((rows, D), tok_map),
            pl.BlockSpec(w.shape, lambda i: (0, 0)),
            pl.BlockSpec(wt.shape, lambda i: (0, 0)),
            pl.BlockSpec((rows, PROJ_COLS), seq_map),
            pl.BlockSpec((rows, PROJ_COLS), seq_map),
            pl.BlockSpec((PROJ_COLS, rows), lambda i: (0, i % steps_per_seq)),
            pl.BlockSpec((PROJ_COLS, rows), lambda i: (0, i % steps_per_seq)),
        ],
        out_specs=[
            pl.BlockSpec((1, D_MODEL, rows), feat_map),
            k_spec,
            k_spec, vt_spec, k_spec, vt_spec,
            pl.BlockSpec((1, GATE_ROWS, rows), feat_map),
        ],
        out_shape=[
            jax.ShapeDtypeStruct((B, D_MODEL, S), BF16),
            jax.ShapeDtypeStruct((B, N_KV_GROUPS, S, AUG_DIM), F32),
            k_shape, vt_shape, k_shape, vt_shape,
            jax.ShapeDtypeStruct((B, GATE_ROWS, S), F32),
        ],
        compiler_params=_params("parallel"),
    )(x, w, wt, cos, sin, cos.T, sin.T)


def _compress_kernel(kv_ref, w1_ref, pea_ref, peb_ref, w2_ref, w2t_ref, cos_ref, sin_ref, kc_ref, vct_ref):
    n = kc_ref.shape[2]
    hid = CMP_HIDDEN
    r = jnp.zeros((n, 4 * hid), F32)
    bias_a = jnp.zeros((8, 4 * hid), F32)
    bias_b = jnp.zeros((8, 4 * hid), F32)
    for p in range(CMP_STRIDE):
        slab = kv_ref[0, 0, pl.ds(p, n, stride=CMP_STRIDE), :].astype(BF16)
        r = r + _dot(slab, w1_ref[p])
        bias_a = bias_a + _dot(pea_ref[p], w1_ref[p])
        bias_b = bias_b + _dot(peb_ref[p], w1_ref[p])
    hidden = []
    for which in range(2):
        c0 = which * 2 * hid
        nxt = pltpu.roll(r[:, c0 + hid:c0 + 2 * hid], n - 1, 0)
        bias = bias_a[0:1, c0:c0 + hid] + bias_b[0:1, c0 + hid:c0 + 2 * hid]
        hidden.append(_silu(r[:, c0:c0 + hid] + nxt + bias).astype(BF16))
    kc = _dot(hidden[0], w2_ref[...])
    rot = jnp.concatenate([kc[:, HALF_DIM:], kc[:, :HALF_DIM]], axis=1)
    kc_ref[0, 0] = (kc * cos_ref[...] + rot * sin_ref[...]).astype(kc_ref.dtype)
    vct_ref[0, 0] = _dot_nt(w2t_ref[...], hidden[1]).astype(vct_ref.dtype)


def _compress(kv_raw, w1_k, w2_k, pe_k, w1_v, w2_v, pe_v, B, S):
    n_chunks = S // CMP_STRIDE
    hid = CMP_HIDDEN
    split = lambda w1: w1.reshape(2, CMP_STRIDE, HEAD_DIM, hid).transpose(1, 2, 0, 3).reshape(CMP_STRIDE, HEAD_DIM, 2 * hid)
    zeros = jnp.zeros((CMP_STRIDE, HEAD_DIM, 2 * hid), F32)
    w1 = jnp.concatenate([jnp.concatenate([split(w1_k), zeros], axis=2),
                          jnp.concatenate([zeros, split(w1_v)], axis=2)], axis=1).astype(BF16)
    pe = jnp.concatenate([pe_k, pe_v], axis=1)
    rows8 = lambda v: jnp.broadcast_to(v[:, None, :], (CMP_STRIDE, 8, AUG_DIM)).astype(BF16)
    pea, peb = rows8(pe[:CMP_STRIDE]), rows8(pe[CMP_STRIDE:])
    cos, sin = _rope_tables(CMP_STRIDE * jnp.arange(n_chunks) + CMP_BLOCK - 1, 1)
    const = lambda a: pl.BlockSpec(a.shape, lambda bi, gi: (0,) * a.ndim)
    w2k = w2_k.astype(BF16)
    w2vt = w2_v.T.astype(BF16)
    return pl.pallas_call(
        _compress_kernel,
        grid=(B, N_KV_GROUPS),
        in_specs=[
            pl.BlockSpec((1, 1, S, AUG_DIM), lambda bi, gi: (bi, gi, 0, 0)),
            const(w1), const(pea), const(peb), const(w2k), const(w2vt), const(cos), const(sin),
        ],
        out_specs=[pl.BlockSpec((1, 1, n_chunks, HEAD_DIM), lambda bi, gi: (bi, gi, 0, 0)),
                   pl.BlockSpec((1, 1, HEAD_DIM, n_chunks), lambda bi, gi: (bi, gi, 0, 0))],
        out_shape=[jax.ShapeDtypeStruct((B, N_KV_GROUPS, n_chunks, HEAD_DIM), BF16),
                   jax.ShapeDtypeStruct((B, N_KV_GROUPS, HEAD_DIM, n_chunks), BF16)],
        compiler_params=_params("parallel", "parallel"),
    )(kv_raw, w1, pea, peb, w2k, w2vt, cos, sin)


def _cmp_attn_kernel(qt_ref, kc_ref, vct_ref, ovl_ref, oct_ref, selt_ref, score_ref):
    i = pl.program_id(2)
    cols = qt_ref.shape[2]
    n_cmp = kc_ref.shape[2]
    n_sel = ovl_ref.shape[0]
    kc = kc_ref[0, 0]
    vct = vct_ref[0, 0]
    t = i * cols + lax.broadcasted_iota(jnp.int32, (1, cols), 1)
    cend = CMP_STRIDE * lax.broadcasted_iota(jnp.int32, (n_cmp, 1), 0) + (CMP_BLOCK - 1)
    cvalid = cend <= t
    any_valid = jnp.where(t >= CMP_BLOCK - 1, 1.0, 0.0)
    pc_sum = jnp.zeros((n_cmp, cols), F32)
    for h in range(HEADS_PER_GROUP):
        qh = qt_ref[0, h * HEAD_DIM:(h + 1) * HEAD_DIM, :]
        s = jnp.where(cvalid, _dot(kc, qh), NEG_INF)
        e = jnp.exp2(s - jnp.max(s, axis=0, keepdims=True))
        pc = e * (any_valid / jnp.sum(e, axis=0, keepdims=True))
        oct_ref[0, h * HEAD_DIM:(h + 1) * HEAD_DIM, :] = _dot(vct, pc.astype(BF16))
        pc_sum = pc_sum + pc
    imp = _dot(ovl_ref[...], pc_sum.astype(BF16))
    j = lax.broadcasted_iota(jnp.int32, (n_sel, 1), 0)
    blk_t = t >> SEL_SHIFT
    bvalid = j <= blk_t
    forced = (j == 0) | (j == blk_t) | (j == blk_t - 1)
    score = jnp.where(bvalid, imp + jnp.where(forced, FORCE_BONUS, 0.0), -1.0)
    key = pltpu.bitcast(score, jnp.int32)
    key_next = key + 1
    score_ref[...] = key
    def rank_group(grp, rank):
        base = pl.multiple_of(grp * 8, 8)
        others = score_ref[pl.ds(base, 8), :]
        for r in range(8):
            ahead = others[r:r + 1, :] >= jnp.where(j > base + r, key, key_next)
            rank = rank + jnp.where(ahead, 1.0, 0.0)
        return rank

    n_groups = jnp.minimum(((i + 1) * cols) // (8 * SEL_BLOCK), n_sel // 8)
    rank = lax.fori_loop(0, n_groups, rank_group, jnp.zeros((n_sel, cols), F32))
    selected = (rank < float(SEL_TOP_N)) & bvalid
    selt_ref[0, 0] = jnp.where(selected, 0.0, NEG_INF).astype(selt_ref.dtype)


def _cmp_attn(qt, kc, vct, B, S):
    cols = CMP_Q_ROWS
    n_cmp = S // CMP_STRIDE
    n_sel = SEL_SLOTS
    assert S // SEL_BLOCK <= SEL_SLOTS
    cstart = CMP_STRIDE * jnp.arange(n_cmp)
    sstart = SEL_BLOCK * jnp.arange(n_sel)
    overlap = ((cstart[None, :] <= sstart[:, None] + SEL_BLOCK - 1)
               & (cstart[None, :] + CMP_BLOCK - 1 >= sstart[:, None])).astype(BF16)
    q_spec = pl.BlockSpec((1, GROUP_Q_DIM, cols), lambda bi, gi, i: (bi, gi, i))
    return pl.pallas_call(
        _cmp_attn_kernel,
        grid=(B, N_KV_GROUPS, S // cols),
        in_specs=[
            q_spec,
            pl.BlockSpec((1, 1, n_cmp, HEAD_DIM), lambda bi, gi, i: (bi, gi, 0, 0)),
            pl.BlockSpec((1, 1, HEAD_DIM, n_cmp), lambda bi, gi, i: (bi, gi, 0, 0)),
            pl.BlockSpec(overlap.shape, lambda bi, gi, i: (0, 0)),
        ],
        out_specs=[
            q_spec,
            pl.BlockSpec((1, 1, n_sel, cols), lambda bi, gi, i: (bi, gi, 0, i)),
        ],
        out_shape=[
            jax.ShapeDtypeStruct((B, D_MODEL, S), F32),
            jax.ShapeDtypeStruct((B, N_KV_GROUPS, n_sel, S), BF16),
        ],
        scratch_shapes=[pltpu.VMEM((n_sel, cols), jnp.int32)],
        compiler_params=_params("parallel", "parallel", "arbitrary"),
    )(qt, kc, vct, overlap)


def _sel_win_attn_kernel(qt_ref, selt_ref, ks_ref, vst_ref, kw_ref, vwt_ref, diag_ref, band_ref, oct_ref, gatet_ref,
                         o_ref, qaug_ref, m_ref, acc_ref, ow_ref):
    i = pl.program_id(2)
    tq = qt_ref.shape[2]
    hg = HEADS_PER_GROUP
    half = ATT_HALF
    n_half = tq // half
    half_cols = hg * half

    for hq in range(n_half):
        for h in range(hg):
            c0 = (hq * hg + h) * half
            qaug_ref[0:HEAD_DIM, c0:c0 + half] = qt_ref[0, h * HEAD_DIM:(h + 1) * HEAD_DIM, hq * half:(hq + 1) * half]
            qaug_ref[HEAD_DIM:, c0:c0 + half] = selt_ref[0, 0, :, hq * half:(hq + 1) * half]
    n_tiles = (hg * tq) // ATT_COLS

    def keys(ref, start, size):
        return ref[0, 0, pl.ds(pl.multiple_of(start, ATT_HALF), size), :]

    m_ref[...] = jnp.full(m_ref.shape, NEG_INF, F32)
    acc_ref[...] = jnp.zeros(acc_ref.shape, F32)

    def sel_step(c, bias_ref):
        k = keys(ks_ref, c * ATT_K_ROWS, ATT_K_ROWS)
        v = vst_ref[0, 0, c]
        new_m, new_acc = [], []
        tile = lambda ct: slice(ct * ATT_COLS, (ct + 1) * ATT_COLS)
        scores = [_dot(k, qaug_ref[:, tile(ct)]) for ct in range(ATT_AHEAD)]
        for ct in range(n_tiles):
            cs = tile(ct)
            if ct + ATT_AHEAD < n_tiles:
                scores.append(_dot(k, qaug_ref[:, tile(ct + ATT_AHEAD)]))
            s = scores[ct]
            if bias_ref is not None:
                s = s + bias_ref[:, cs]
            m_prev = m_ref[:, cs]
            m_next = jnp.maximum(m_prev, jnp.max(s, axis=0, keepdims=True))
            p = jnp.exp2(s - m_next)
            alpha = jnp.exp2(m_prev - m_next)
            new_acc.append(alpha * acc_ref[:, cs] + _dot(v, p.astype(BF16)))
            new_m.append(m_next)
        for ct in range(n_tiles):
            cs = slice(ct * ATT_COLS, (ct + 1) * ATT_COLS)
            acc_ref[:, cs] = new_acc[ct]
            m_ref[:, cs] = new_m[ct]

    def sel_body(c, carry):
        sel_step(c, None)
        return carry

    lax.fori_loop(0, i, sel_body, 0)
    sel_step(i, diag_ref)

    def probs(s):
        return jnp.exp2(s - jnp.max(s, axis=0, keepdims=True)).astype(BF16)

    @pl.when(i == 0)
    def _():
        k = keys(kw_ref, 0, tq)
        v = vwt_ref[0, 0, 0]
        for ct in range(n_tiles):
            cs = slice(ct * ATT_COLS, (ct + 1) * ATT_COLS)
            ow_ref[:, cs] = _dot(v, probs(_dot(k, qaug_ref[:, cs]) + diag_ref[:, cs]))

    @pl.when(i > 0)
    def _():
        v_prev = vwt_ref[0, 0, i - 1]
        v_here = vwt_ref[0, 0, i]
        for hq in range(n_half):
            k = keys(kw_ref, i * tq + hq * half - WINDOW, WINDOW + half)
            n_prev = tq - hq * half
            for ct in range(half_cols // ATT_COLS):
                cs = slice(hq * half_cols + ct * ATT_COLS, hq * half_cols + (ct + 1) * ATT_COLS)
                p = probs(_dot(k, qaug_ref[:, cs]) + band_ref[:, ct * ATT_COLS:(ct + 1) * ATT_COLS])
                ow_ref[:, cs] = (_dot(v_prev[:, tq - n_prev:], p[0:n_prev])
                                 + _dot(v_here[:, 0:WINDOW + half - n_prev], p[n_prev:]))

    acc_s = acc_ref[...]
    acc_w = ow_ref[...]
    o_s = acc_s[0:HEAD_DIM] / acc_s[HEAD_DIM:HEAD_DIM + 1]
    o_w = acc_w[0:HEAD_DIM] / acc_w[HEAD_DIM:HEAD_DIM + 1]
    for hq in range(n_half):
        tok = slice(hq * half, (hq + 1) * half)
        for h in range(hg):
            c0 = (hq * hg + h) * half
            gc = gatet_ref[0, N_BRANCHES * h + 0:N_BRANCHES * h + 1, tok]
            gs = gatet_ref[0, N_BRANCHES * h + 1:N_BRANCHES * h + 2, tok]
            gw = gatet_ref[0, N_BRANCHES * h + 2:N_BRANCHES * h + 3, tok]
            o = (gc * oct_ref[0, h * HEAD_DIM:(h + 1) * HEAD_DIM, tok]
                 + gs * o_s[:, c0:c0 + half] + gw * o_w[:, c0:c0 + half])
            o_ref[0, tok, h * HEAD_DIM:(h + 1) * HEAD_DIM] = o.T.astype(o_ref.dtype)


def _sel_win_attn(qt, selt, ks, vst, kw, vwt, oct, gatet, B, S):
    tq = ATT_Q_ROWS
    half = ATT_HALF
    hg = HEADS_PER_GROUP
    cols = hg * tq
    n_chunks = S // ATT_K_ROWS
    t_rel = (jnp.arange(tq // half)[:, None, None] * half + jnp.arange(half)[None, None, :])
    t_rel = jnp.broadcast_to(t_rel, (tq // half, hg, half)).reshape(1, cols)
    diag = jnp.where(jnp.arange(ATT_K_ROWS)[:, None] <= t_rel, 0.0, NEG_INF).astype(F32)
    tt = jnp.broadcast_to(jnp.arange(half)[None, :], (hg, half)).reshape(1, hg * half)
    a = jnp.arange(WINDOW + half)[:, None]
    band = jnp.where((a > tt) & (a <= tt + WINDOW), 0.0, NEG_INF).astype(F32)
    k_spec = pl.BlockSpec((1, 1, S, AUG_DIM), lambda bi, gi, i: (bi, gi, 0, 0))
    vt_spec = pl.BlockSpec((1, 1, n_chunks, VT_ROWS, ATT_K_ROWS), lambda bi, gi, i: (bi, gi, 0, 0, 0))
    q_spec = pl.BlockSpec((1, GROUP_Q_DIM, tq), lambda bi, gi, i: (bi, gi, i))
    const = lambda arr: pl.BlockSpec(arr.shape, lambda bi, gi, i: (0, 0))
    return pl.pallas_call(
        _sel_win_attn_kernel,
        grid=(B, N_KV_GROUPS, S // tq),
        in_specs=[
            q_spec,
            pl.BlockSpec((1, 1, SEL_SLOTS, tq), lambda bi, gi, i: (bi, gi, 0, i)),
            k_spec, vt_spec, k_spec, vt_spec,
            const(diag), const(band),
            q_spec,
            pl.BlockSpec((1, GATE_SLOTS, tq), lambda bi, gi, i: (bi, gi, i)),
        ],
        out_specs=pl.BlockSpec((1, tq, GROUP_Q_DIM), lambda bi, gi, i: (bi, i, gi)),
        out_shape=jax.ShapeDtypeStruct((B, S, D_MODEL), BF16),
        scratch_shapes=[
            pltpu.VMEM((AUG_DIM, cols), BF16),
            pltpu.VMEM((1, cols), F32),
            pltpu.VMEM((VT_ROWS, cols), F32),
            pltpu.VMEM((VT_ROWS, cols), F32),
        ],
        compiler_params=_params("parallel", "parallel", "arbitrary"),
    )(qt, selt, ks, vst, kw, vwt, diag, band, oct, gatet)


def _out_proj_ln_kernel(o_ref, w_ref, x_ref, g_ref, b_ref, y_ref):
    h = _dot(o_ref[...], w_ref[...])
    y_ref[...] = _layer_norm(ALPHA * x_ref[...] + h, g_ref[...], b_ref[...])


def _out_proj_ln(o, w_out, x, g, b):
    T, D = x.shape
    rows = PROJ_ROWS
    return pl.pallas_call(
        _out_proj_ln_kernel,
        grid=(T // rows,),
        in_specs=[
            pl.BlockSpec((rows, D), lambda i: (i, 0)),
            pl.BlockSpec((D, D), lambda i: (0, 0)),
            pl.BlockSpec((rows, D), lambda i: (i, 0)),
            pl.BlockSpec((1, D), lambda i: (0, 0)),
            pl.BlockSpec((1, D), lambda i: (0, 0)),
        ],
        out_specs=pl.BlockSpec((rows, D), lambda i: (i, 0)),
        out_shape=jax.ShapeDtypeStruct((T, D), F32),
        compiler_params=_params("parallel"),
    )(o, w_out.astype(BF16), x, g.reshape(1, D), b.reshape(1, D))


def _route_kernel(x_ref, r_ref, tri_ref, xb_ref, info_ref, wcol_ref, cnt_ref, carry_ref):
    @pl.when(pl.program_id(0) == 0)
    def _():
        carry_ref[...] = jnp.zeros_like(carry_ref)

    x = x_ref[...]
    xb = x.astype(BF16)
    xb_ref[...] = xb
    x_lo = (x - xb.astype(F32)).astype(BF16)
    r_hi = r_ref[0]
    r_lo = r_ref[1]
    logits = _dot(xb, r_hi) + (_dot(x_lo, r_hi) + _dot(xb, r_lo))
    lane = lax.broadcasted_iota(jnp.int32, logits.shape, 1)
    logits = jnp.where(lane < N_EXPERTS, logits, -jnp.inf)
    v1 = jnp.max(logits, axis=-1, keepdims=True)
    i1 = jnp.min(jnp.where(logits == v1, lane, LANES), axis=-1, keepdims=True)
    rest = jnp.where(lane == i1, -jnp.inf, logits)
    v2 = jnp.max(rest, axis=-1, keepdims=True)
    i2 = jnp.min(jnp.where(rest == v2, lane, LANES), axis=-1, keepdims=True)
    e2 = jnp.exp(v2 - v1)
    w1 = 1.0 / (1.0 + e2)
    w2 = e2 / (1.0 + e2)
    m1 = jnp.where(lane == i1, 1.0, 0.0)
    m2 = jnp.where(lane == i2, 1.0, 0.0)
    routed = m1 + m2
    before = _dot(tri_ref[...], routed.astype(BF16)) + carry_ref[0:1, :]
    rank1 = jnp.sum(m1 * before, axis=-1, keepdims=True)
    rank2 = jnp.sum(m2 * before, axis=-1, keepdims=True)
    cnt = jnp.sum(routed, axis=0, keepdims=True)
    carry_ref[...] = carry_ref[...] + cnt
    cnt_ref[0] = jnp.broadcast_to(cnt, cnt_ref.shape[1:])
    info = jnp.where(lane == 0, i1.astype(F32),
                     jnp.where(lane == 1, i2.astype(F32),
                               jnp.where(lane == 2, rank1, jnp.where(lane == 3, rank2, 0.0))))
    info_ref[0] = info.T[0:8, :]
    w1_hi = w1.astype(BF16).astype(F32)
    w2_hi = w2.astype(BF16).astype(F32)
    lo_half = (lane & 1) == 1
    wcol = (jnp.where((lane >> 1) == i1, jnp.where(lo_half, w1 - w1_hi, w1_hi), 0.0)
            + jnp.where((lane >> 1) == i2, jnp.where(lo_half, w2 - w2_hi, w2_hi), 0.0))
    wcol_ref[...] = wcol.astype(BF16)


def _route(x, router):
    T, D = x.shape
    rows = MOE_ROWS
    r = jnp.pad(router, ((0, 0), (0, LANES - router.shape[1])))
    r_hi = r.astype(BF16)
    r_lo = (r - r_hi.astype(F32)).astype(BF16)
    r2 = jnp.stack([r_hi, r_lo])
    tri = (jnp.arange(rows)[None, :] < jnp.arange(rows)[:, None]).astype(BF16)
    return pl.pallas_call(
        _route_kernel,
        grid=(T // rows,),
        in_specs=[
            pl.BlockSpec((rows, D), lambda i: (i, 0)),
            pl.BlockSpec(r2.shape, lambda i: (0, 0, 0)),
            pl.BlockSpec(tri.shape, lambda i: (0, 0)),
        ],
        out_specs=[
            pl.BlockSpec((rows, D), lambda i: (i, 0)),
            pl.BlockSpec((1, 8, rows), lambda i: (i, 0, 0)),
            pl.BlockSpec((rows, LANES), lambda i: (i, 0)),
            pl.BlockSpec((1, 8, LANES), lambda i: (i, 0, 0)),
        ],
        out_shape=[
            jax.ShapeDtypeStruct((T, D), BF16),
            jax.ShapeDtypeStruct((T // rows, 8, rows), F32),
            jax.ShapeDtypeStruct((T, LANES), BF16),
            jax.ShapeDtypeStruct((T // rows, 8, LANES), F32),
        ],
        scratch_shapes=[pltpu.VMEM((8, LANES), F32)],
        compiler_params=_params("arbitrary"),
    )(x, r2, tri)


def _moe_tables(cnt, n_blocks_max, n_items_max):
    R = MOE_ROWS
    C, E = cnt.shape
    i32 = jnp.int32
    count_le = lambda sorted_v, q: jnp.sum(sorted_v[None, :] <= q[:, None], axis=1).astype(i32)

    def windows(ra, rb, live):
        near = jnp.minimum((ra // MOE_ALIGN) * MOE_ALIGN, R - MOE_WIN)
        fits = rb - near < MOE_WIN
        start = jnp.where(fits, near, (ra // MOE_WIN) * MOE_WIN)
        n = jnp.where(fits, 1, rb // MOE_WIN - ra // MOE_WIN + 1)
        return jnp.where(live, start, 0), jnp.where(live, n, 0)

    cum = jnp.concatenate([jnp.zeros((1, E), i32), jnp.cumsum(cnt, axis=0)], axis=0)
    tot = cum[-1]
    nb = (tot + R - 1) // R
    nb_end = jnp.cumsum(nb)
    blk_start = nb_end - nb
    n_blocks = nb_end[-1]
    b = jnp.minimum(jnp.arange(n_blocks_max, dtype=i32), n_blocks - 1)
    bexp = jnp.minimum(count_le(nb_end, b), E - 1)
    bvalid = jnp.arange(n_blocks_max, dtype=i32) < n_blocks
    lb = b - blk_start[bexp]
    rho0 = lb * R
    rho1 = jnp.minimum((lb + 1) * R, tot[bexp]) - 1
    cum_b = cum[1:, :][:, bexp].T
    lo = jnp.minimum(jnp.sum(cum_b <= rho0[:, None], axis=1).astype(i32), C - 1)
    hi = jnp.minimum(jnp.sum(cum_b <= rho1[:, None], axis=1).astype(i32), C - 1)
    nit = jnp.where(bvalid, hi - lo + 1, 0)
    it_end = jnp.cumsum(nit)
    it_start = it_end - nit
    n_items = it_end[-1]
    i = jnp.arange(n_items_max, dtype=i32)
    ic = jnp.minimum(i, n_items - 1)
    d_blk = jnp.minimum(count_le(it_end, ic), n_blocks_max - 1)
    d_chk = lo[d_blk] + ic - it_start[d_blk]
    d_valid = i < n_items
    d_first = d_valid & (ic == it_start[d_blk])
    d_last = d_valid & (ic == it_end[d_blk] - 1)
    d_exp = bexp[d_blk]
    d_ra = jnp.maximum(rho0[d_blk], cum[d_chk, d_exp]) - rho0[d_blk]
    d_rb = jnp.minimum(rho1[d_blk], cum[d_chk + 1, d_exp] - 1) - rho0[d_blk]
    d_start, d_nwin = windows(d_ra, d_rb, d_valid & (d_rb >= d_ra))
    bl_lo = (blk_start[None, :] + cum[:-1] // R).reshape(-1)
    bl_hi = (blk_start[None, :] + (cum[1:] - 1) // R).reshape(-1)
    npair = jnp.where(cnt.reshape(-1) > 0, bl_hi - bl_lo + 1, 0)
    p_end = jnp.cumsum(npair)
    p_start = p_end - npair
    n_items2 = p_end[-1]
    jc = jnp.minimum(i, n_items2 - 1)
    pair = jnp.minimum(count_le(p_end, jc), C * E - 1)
    c_blk = bl_lo[pair] + jc - p_start[pair]
    c_chk = pair // E
    c_valid = i < n_items2
    prev_chk = jnp.concatenate([jnp.full((1,), -1, i32), c_chk[:-1]])
    next_chk = jnp.concatenate([c_chk[1:], jnp.full((1,), -1, i32)])
    next_valid = jnp.concatenate([c_valid[1:], jnp.zeros((1,), bool)])
    c_first = c_valid & (c_chk != prev_chk)
    c_last = c_valid & ((c_chk != next_chk) | ~next_valid)
    c_exp = pair % E
    c_base = (c_blk - blk_start[c_exp]) * R
    c_ra = jnp.maximum(cum[c_chk, c_exp] - c_base, 0)
    c_rb = jnp.minimum(cum[c_chk + 1, c_exp] - 1 - c_base, R - 1)
    c_start, c_nwin = windows(c_ra, c_rb, c_valid)
    as_i32 = lambda v: v.astype(i32)
    return dict(base=blk_start * R, bexp=bexp, bvalid=as_i32(bvalid),
                d_blk=d_blk, d_chk=d_chk, d_first=as_i32(d_first), d_last=as_i32(d_last), d_start=d_start, d_nwin=d_nwin,
                c_blk=c_blk, c_chk=c_chk, c_first=as_i32(c_first), c_last=as_i32(c_last),
                c_start=c_start, c_nwin=c_nwin)


def _dispatch_kernel(blk_ref, chk_ref, first_ref, last_ref, start_ref, nwin_ref, xb_ref, pos_ref, wcol_ref,
                     xs_ref, ws_ref, acc_ref, wacc_ref):
    i = pl.program_id(0)
    R = xs_ref.shape[0]

    @pl.when(first_ref[i] == 1)
    def _():
        acc_ref[...] = jnp.zeros_like(acc_ref)
        wacc_ref[...] = jnp.zeros_like(wacc_ref)

    pos1 = pos_ref[0, 0:1, :]
    pos2 = pos_ref[0, 1:2, :]

    def window(k, carry):
        r0 = pl.multiple_of(start_ref[i] + k * MOE_WIN, MOE_ALIGN)
        rows = (blk_ref[i] * R + r0 + lax.broadcasted_iota(jnp.int32, (MOE_WIN, 1), 0)).astype(F32)
        hit = jnp.where((pos1 - rows) * (pos2 - rows) == 0.0, 1.0, 0.0).astype(BF16)
        acc_ref[pl.ds(r0, MOE_WIN), :] += _dot(hit, xb_ref[...])
        wacc_ref[pl.ds(r0, MOE_WIN), :] += _dot(hit, wcol_ref[...])
        return carry

    lax.fori_loop(0, nwin_ref[i], window, 0)

    @pl.when(last_ref[i] == 1)
    def _():
        xs_ref[...] = acc_ref[...].astype(xs_ref.dtype)
        ws_ref[...] = wacc_ref[...]


def _dispatch(xb, posrow, wcol, tab, n_blocks_max, n_items_max):
    T, D = xb.shape
    R = MOE_ROWS
    chunk_map = lambda i, blk, chk, first, last, start, nwin: (chk[i], 0)
    block_map = lambda i, blk, chk, first, last, start, nwin: (blk[i], 0)
    grid_spec = pltpu.PrefetchScalarGridSpec(
        num_scalar_prefetch=6,
        grid=(n_items_max,),
        in_specs=[
            pl.BlockSpec((R, D), chunk_map),
            pl.BlockSpec((1, 8, R), lambda i, blk, chk, first, last, start, nwin: (chk[i], 0, 0)),
            pl.BlockSpec((R, LANES), chunk_map),
        ],
        out_specs=[pl.BlockSpec((R, D), block_map), pl.BlockSpec((R, LANES), block_map)],
        scratch_shapes=[pltpu.VMEM((R, D), F32), pltpu.VMEM((R, LANES), F32)],
    )
    return pl.pallas_call(
        _dispatch_kernel,
        grid_spec=grid_spec,
        out_shape=[
            jax.ShapeDtypeStruct((n_blocks_max * R, D), BF16),
            jax.ShapeDtypeStruct((n_blocks_max * R, LANES), F32),
        ],
        compiler_params=_params("arbitrary"),
    )(tab["d_blk"], tab["d_chk"], tab["d_first"], tab["d_last"], tab["d_start"], tab["d_nwin"], xb, posrow, wcol)


def _expert_ffn_kernel(bexp_ref, bvalid_ref, xs_ref, ws_ref, wg_ref, wu_ref, wd_ref, ys_ref, acc_ref):
    b = pl.program_id(0)
    f = pl.program_id(1)

    @pl.when(f == 0)
    def _():
        acc_ref[...] = jnp.zeros_like(acc_ref)

    @pl.when(bvalid_ref[b] == 1)
    def _():
        xb = xs_ref[...]
        a = _silu(_dot(xb, wg_ref[0])) * _dot(xb, wu_ref[0])
        acc_ref[...] += _dot(a.astype(BF16), wd_ref[0])

    @pl.when(f == pl.num_programs(1) - 1)
    def _():
        ws = ws_ref[...]
        lane = lax.broadcasted_iota(jnp.int32, ws.shape, 1)
        w = jnp.sum(jnp.where((lane >> 1) == bexp_ref[b], ws, 0.0), axis=-1, keepdims=True)
        ys_ref[...] = (acc_ref[...] * w).astype(ys_ref.dtype)


def _expert_ffn(xs, ws, w_gu, w_down, tab, n_blocks_max):
    D = xs.shape[1]
    R = MOE_ROWS
    d_ff = w_down.shape[1]
    fc = _ff_chunk(d_ff, 1792)
    nf = d_ff // fc
    f_eff = lambda b, f, bvalid: f * bvalid[b] + (nf - 1) * (1 - bvalid[b])
    grid_spec = pltpu.PrefetchScalarGridSpec(
        num_scalar_prefetch=2,
        grid=(n_blocks_max, nf),
        in_specs=[
            pl.BlockSpec((R, D), lambda b, f, bexp, bvalid: (b, 0)),
            pl.BlockSpec((R, LANES), lambda b, f, bexp, bvalid: (b, 0)),
            pl.BlockSpec((1, D, fc), lambda b, f, bexp, bvalid: (bexp[b], 0, f_eff(b, f, bvalid))),
            pl.BlockSpec((1, D, fc), lambda b, f, bexp, bvalid: (bexp[b], 0, nf + f_eff(b, f, bvalid))),
            pl.BlockSpec((1, fc, D), lambda b, f, bexp, bvalid: (bexp[b], f_eff(b, f, bvalid), 0)),
        ],
        out_specs=pl.BlockSpec((R, D), lambda b, f, bexp, bvalid: (b, 0)),
        scratch_shapes=[pltpu.VMEM((R, D), F32)],
    )
    return pl.pallas_call(
        _expert_ffn_kernel,
        grid_spec=grid_spec,
        out_shape=jax.ShapeDtypeStruct(xs.shape, BF16),
        compiler_params=_params("parallel", "arbitrary"),
    )(tab["bexp"], tab["bvalid"], xs, ws, w_gu, w_gu, w_down)


def _combine_ln_kernel(chk_ref, blk_ref, first_ref, last_ref, start_ref, nwin_ref,
                       ys_ref, pc1_ref, pc2_ref, x_ref, g_ref, b_ref, o_ref, acc_ref):
    i = pl.program_id(0)
    R = ys_ref.shape[0]

    @pl.when(first_ref[i] == 1)
    def _():
        acc_ref[...] = jnp.zeros_like(acc_ref)

    def window(k, carry):
        r0 = pl.multiple_of(start_ref[i] + k * MOE_WIN, MOE_ALIGN)
        pc1 = pc1_ref[...]
        pc2 = pc2_ref[...]
        lane = lax.broadcasted_iota(jnp.int32, pc1.shape, 1)
        parts = []
        for t in range(MOE_WIN // LANES):
            col = (blk_ref[i] * R + r0 + t * LANES + lane).astype(F32)
            parts.append(jnp.where((pc1 - col) * (pc2 - col) == 0.0, 1.0, 0.0).astype(BF16))
        acc_ref[...] += _dot(jnp.concatenate(parts, axis=1), ys_ref[pl.ds(r0, MOE_WIN), :])
        return carry

    lax.fori_loop(0, nwin_ref[i], window, 0)

    @pl.when(last_ref[i] == 1)
    def _():
        o_ref[...] = _layer_norm(ALPHA * x_ref[...] + acc_ref[...], g_ref[...], b_ref[...])


def _combine_ln(ys, poscol1, poscol2, x, g, b, tab, n_items_max):
    T, D = x.shape
    R = MOE_ROWS
    chunk_map = lambda i, chk, blk, first, last, start, nwin: (chk[i], 0)
    const_map = lambda i, chk, blk, first, last, start, nwin: (0, 0)
    grid_spec = pltpu.PrefetchScalarGridSpec(
        num_scalar_prefetch=6,
        grid=(n_items_max,),
        in_specs=[
            pl.BlockSpec((R, D), lambda i, chk, blk, first, last, start, nwin: (blk[i], 0)),
            pl.BlockSpec((R, LANES), chunk_map),
            pl.BlockSpec((R, LANES), chunk_map),
            pl.BlockSpec((R, D), chunk_map),
            pl.BlockSpec((1, D), const_map),
            pl.BlockSpec((1, D), const_map),
        ],
        out_specs=pl.BlockSpec((R, D), chunk_map),
        scratch_shapes=[pltpu.VMEM((R, D), F32)],
    )
    return pl.pallas_call(
        _combine_ln_kernel,
        grid_spec=grid_spec,
        out_shape=jax.ShapeDtypeStruct((T, D), F32),
        compiler_params=_params("arbitrary"),
    )(tab["c_chk"], tab["c_blk"], tab["c_first"], tab["c_last"], tab["c_start"], tab["c_nwin"],
      ys, poscol1, poscol2, x, g.reshape(1, D), b.reshape(1, D))


def _moe_ln(x, router, w_gu, w_down, g, b):
    T, D = x.shape
    R = MOE_ROWS
    n_exp = w_down.shape[0]
    n_chunks = T // R
    n_blocks_max = (2 * T) // R + n_exp
    n_items_max = n_blocks_max + n_exp * (n_chunks - 1)
    xb, info, wcol, cnt = _route(x, router)
    tab = _moe_tables(cnt[:, 0, :n_exp].astype(jnp.int32), n_blocks_max, n_items_max)
    field = lambda k: info[:, k, :].reshape(T).astype(jnp.int32)
    pos1 = tab["base"][field(0)] + field(2)
    pos2 = tab["base"][field(1)] + field(3)
    pos1 = pos1.astype(F32)
    pos2 = pos2.astype(F32)
    posrow = jnp.stack([pos1.reshape(n_chunks, R), pos2.reshape(n_chunks, R)], axis=1)
    posrow = jnp.pad(posrow, ((0, 0), (0, 6), (0, 0)), constant_values=-1.0)
    poscol1 = jnp.broadcast_to(pos1[:, None], (T, LANES))
    poscol2 = jnp.broadcast_to(pos2[:, None], (T, LANES))
    xs, ws = _dispatch(xb, posrow, wcol, tab, n_blocks_max, n_items_max)
    ys = _expert_ffn(xs, ws, w_gu, w_down, tab, n_blocks_max)
    return _combine_ln(ys, poscol1, poscol2, x, g, b, tab, n_items_max)


def kernel(x, ln_g, ln_b, pool_w, pool_scale, nsa_w_in, nsa_pe_k, nsa_w1_k, nsa_w2_k, nsa_pe_v, nsa_w1_v,
           nsa_w2_v, nsa_w_out, ffn_w_gu, ffn_w_down, moe_router, moe_w_gu, moe_w_down):
    B, S, D = x.shape
    T = B * S
    xa = _pool_ln(x, pool_w[0], pool_scale[0], ln_g[0, 0], ln_b[0, 0]).reshape(T, D)
    x1 = _ffn_ln(xa, ffn_w_gu[0].astype(BF16), ffn_w_down[0].astype(BF16), ln_g[0, 1], ln_b[0, 1])
    qt, kcv, ks, vst, kw, vwt, gatet = _in_proj(x1, nsa_w_in[0], B, S)
    kc, vct = _compress(kcv, nsa_w1_k[0], nsa_w2_k[0], nsa_pe_k[0], nsa_w1_v[0], nsa_w2_v[0], nsa_pe_v[0], B, S)
    oct, selt = _cmp_attn(qt, kc, vct, B, S)
    o = _sel_win_attn(qt, selt, ks, vst, kw, vwt, oct, gatet, B, S).reshape(T, D)
    x2 = _out_proj_ln(o, nsa_w_out[0], x1, ln_g[1, 0], ln_b[1, 0])
    y = _moe_ln(x2, moe_router[0], moe_w_gu[0].astype(BF16), moe_w_down[0].astype(BF16), ln_g[1, 1], ln_b[1, 1])
    return y.reshape(B, S, D)
```

```python
import functools

import jax
import jax.numpy as jnp
from jax import lax
from jax.experimental import pallas as pl
from jax.experimental.pallas import tpu as pltpu

D_MODEL = 1024
DEPTH = 2
POOL_WINDOWS = (2, 4, 8, 16)
POOL_GROUP_DIM = D_MODEL // len(POOL_WINDOWS)
POOL_HALO = 16
N_HEADS = 16
N_KV_GROUPS = 4
HEADS_PER_GROUP = N_HEADS // N_KV_GROUPS
HEAD_DIM = D_MODEL // N_HEADS
HALF_DIM = HEAD_DIM // 2
GROUP_Q_DIM = HEADS_PER_GROUP * HEAD_DIM
KV_DIM = N_KV_GROUPS * HEAD_DIM
N_BRANCHES = 3
N_GATES = N_BRANCHES * N_HEADS
CMP_STRIDE = 16
CMP_BLOCK = 2 * CMP_STRIDE
CMP_HIDDEN = 2 * HEAD_DIM
SEL_BLOCK = 64
SEL_SHIFT = SEL_BLOCK.bit_length() - 1
SEL_TOP_N = 16
WINDOW = 512
FORCE_BONUS = 1.0e3
NEG_INF = -1.0e30
ROPE_THETA = 10000.0
ATTN_SCALE = HEAD_DIM ** -0.5
LOG2E = 1.4426950408889634
Q_SCALE = ATTN_SCALE * LOG2E
AUG_DIM = 2 * HEAD_DIM
SEL_SLOTS = AUG_DIM - HEAD_DIM
VT_ROWS = HEAD_DIM + 16
GATE_SLOTS = 16
GATE_ROWS = N_KV_GROUPS * GATE_SLOTS
N_EXPERTS = 8
LN_EPS = 1e-5
ALPHA = (2 * DEPTH) ** 0.25

LANES = 128
VMEM_LIMIT_BYTES = 56 * 1024 * 1024

POOL_ROWS = 512
FFN_ROWS = 512
PROJ_ROWS = 512
PROJ_COLS = 256
CMP_Q_ROWS = 512
ATT_Q_ROWS = 512
ATT_K_ROWS = 512
ATT_HALF = 128
ATT_COLS = 512
ATT_AHEAD = 3
MOE_ROWS = 512
MOE_WIN = 256
MOE_ALIGN = 16

F32 = jnp.float32
BF16 = jnp.bfloat16


def _dot(a, b):
    return jnp.dot(a, b, preferred_element_type=F32)


def _dot_nt(a, b):
    return lax.dot_general(a, b, (((1,), (1,)), ((), ())), preferred_element_type=F32)


def _layer_norm(z, g, b):
    mu = jnp.mean(z, axis=-1, keepdims=True)
    zc = z - mu
    var = jnp.mean(zc * zc, axis=-1, keepdims=True)
    return zc * lax.rsqrt(var + LN_EPS) * g + b


def _silu(x):
    return x / (1.0 + jnp.exp(-x))


def _params(*semantics):
    return pltpu.CompilerParams(dimension_semantics=semantics, vmem_limit_bytes=VMEM_LIMIT_BYTES)


def _pool_ln_kernel(x_ref, halo_ref, w_ref, scale_ref, g_ref, b_ref, o_ref, ext_ref):
    i = pl.program_id(1)
    rows = x_ref.shape[1]
    x = x_ref[0]
    ext_ref[0:POOL_HALO, :] = jnp.where(i > 0, halo_ref[0], 0.0)
    ext_ref[POOL_HALO:, :] = x
    pos = i * rows + lax.broadcasted_iota(jnp.int32, (rows, 1), 0)
    ys = []
    run = ext_ref[...]
    span = 1
    for gi, w in enumerate(POOL_WINDOWS):
        while span < w:
            run = run + pltpu.roll(run, span, 0)
            span *= 2
        xg = x[:, gi * POOL_GROUP_DIM:(gi + 1) * POOL_GROUP_DIM]
        cnt = jnp.minimum(pos + 1, w).astype(F32)
        diff = run[POOL_HALO:, 0:POOL_GROUP_DIM] / cnt - xg
        ys.append(_dot(diff.astype(BF16), w_ref[gi]))
        run = run[:, POOL_GROUP_DIM:]
    h = jnp.concatenate(ys, axis=1) * scale_ref[...]
    o_ref[0] = _layer_norm(ALPHA * x + h, g_ref[...], b_ref[...])


def _pool_ln(x, w, scale, g, b):
    B, S, D = x.shape
    rows = POOL_ROWS
    assert all(a < b_ for a, b_ in zip(POOL_WINDOWS, POOL_WINDOWS[1:]))
    assert all(w_ & (w_ - 1) == 0 for w_ in POOL_WINDOWS) and POOL_WINDOWS[-1] <= POOL_HALO
    halo_blocks = rows // POOL_HALO
    row2 = lambda v: v.reshape(1, D)
    return pl.pallas_call(
        _pool_ln_kernel,
        grid=(B, S // rows),
        in_specs=[
            pl.BlockSpec((1, rows, D), lambda bi, i: (bi, i, 0)),
            pl.BlockSpec((1, POOL_HALO, D), lambda bi, i: (bi, jnp.maximum(i * halo_blocks - 1, 0), 0)),
            pl.BlockSpec(w.shape, lambda bi, i: (0, 0, 0)),
            pl.BlockSpec((1, D), lambda bi, i: (0, 0)),
            pl.BlockSpec((1, D), lambda bi, i: (0, 0)),
            pl.BlockSpec((1, D), lambda bi, i: (0, 0)),
        ],
        out_specs=pl.BlockSpec((1, rows, D), lambda bi, i: (bi, i, 0)),
        out_shape=jax.ShapeDtypeStruct((B, S, D), F32),
        scratch_shapes=[pltpu.VMEM((rows + POOL_HALO, D), F32)],
        compiler_params=_params("parallel", "arbitrary"),
    )(x, x, w.astype(BF16), row2(scale), row2(g), row2(b))


def _ffn_ln_kernel(x_ref, wg_ref, wu_ref, wd_ref, g_ref, b_ref, o_ref, xb_ref, acc_ref):
    f = pl.program_id(1)

    @pl.when(f == 0)
    def _():
        xb_ref[...] = x_ref[...].astype(BF16)
        acc_ref[...] = jnp.zeros_like(acc_ref)

    xb = xb_ref[...]
    a = _silu(_dot(xb, wg_ref[...])) * _dot(xb, wu_ref[...])
    acc_ref[...] += _dot(a.astype(BF16), wd_ref[...])

    @pl.when(f == pl.num_programs(1) - 1)
    def _():
        o_ref[...] = _layer_norm(ALPHA * x_ref[...] + acc_ref[...], g_ref[...], b_ref[...])


def _ff_chunk(d_ff, target):
    best = LANES
    for c in range(LANES, target + 1, LANES):
        if d_ff % c == 0:
            best = c
    return best


def _ffn_ln(x, w_gu, w_down, g, b):
    T, D = x.shape
    d_ff = w_down.shape[0]
    fc = _ff_chunk(d_ff, 1536)
    nf = d_ff // fc
    rows = FFN_ROWS
    return pl.pallas_call(
        _ffn_ln_kernel,
        grid=(T // rows, nf),
        in_specs=[
            pl.BlockSpec((rows, D), lambda i, f: (i, 0)),
            pl.BlockSpec((D, fc), lambda i, f: (0, f)),
            pl.BlockSpec((D, fc), lambda i, f: (0, nf + f)),
            pl.BlockSpec((fc, D), lambda i, f: (f, 0)),
            pl.BlockSpec((1, D), lambda i, f: (0, 0)),
            pl.BlockSpec((1, D), lambda i, f: (0, 0)),
        ],
        out_specs=pl.BlockSpec((rows, D), lambda i, f: (i, 0)),
        out_shape=jax.ShapeDtypeStruct((T, D), F32),
        scratch_shapes=[pltpu.VMEM((rows, D), BF16), pltpu.VMEM((rows, D), F32)],
        compiler_params=_params("parallel", "arbitrary"),
    )(x, w_gu, w_gu, w_down, g.reshape(1, D), b.reshape(1, D))


def _rope_tables(pos, reps):
    freqs = jnp.power(ROPE_THETA, -jnp.arange(HALF_DIM, dtype=F32) / HALF_DIM)
    ang = pos.astype(F32)[:, None] * freqs[None, :]
    cos, sin = jnp.cos(ang), jnp.sin(ang)
    return (jnp.tile(jnp.concatenate([cos, cos], axis=1), (1, reps)),
            jnp.tile(jnp.concatenate([-sin, sin], axis=1), (1, reps)))


def _in_proj_kernel(x_ref, w_ref, wt_ref, cos_ref, sin_ref, cost_ref, sint_ref,
                    qt_ref, kcv_ref, ks_ref, vst_ref, kw_ref, vwt_ref, gatet_ref, *, steps_per_seq):
    rows = x_ref.shape[0]
    xb = x_ref[...].astype(BF16)
    cos = cos_ref[...]
    sin = sin_ref[...]
    lane = lax.broadcasted_iota(jnp.int32, cos.shape, 1)
    first_half = (lane & (HEAD_DIM - 1)) < HALF_DIM
    seq_step = pl.program_id(0) % steps_per_seq
    pos = seq_step * rows + lax.broadcasted_iota(jnp.int32, (rows, HEAD_DIM), 0)
    col = lax.broadcasted_iota(jnp.int32, (rows, HEAD_DIM), 1)
    blk_onehot = jnp.where((pos >> SEL_SHIFT) == col, 1.0, 0.0)
    zeros = jnp.zeros((rows, HEAD_DIM), F32)

    def col_tile(j):
        return _dot(xb, w_ref[:, j * PROJ_COLS:(j + 1) * PROJ_COLS])

    def row_tile(r0, n):
        return _dot_nt(wt_ref[r0:r0 + n, :], xb)

    def rope(y):
        rot = jnp.where(first_half,
                        pltpu.roll(y, PROJ_COLS - HALF_DIM, 1),
                        pltpu.roll(y, HALF_DIM, 1))
        return y * cos + rot * sin

    def rope_t(yt):
        pieces = []
        for h in range(yt.shape[0] // HEAD_DIM):
            pieces.append(yt[h * HEAD_DIM + HALF_DIM:(h + 1) * HEAD_DIM])
            pieces.append(yt[h * HEAD_DIM:h * HEAD_DIM + HALF_DIM])
        return yt * cost_ref[...] + jnp.concatenate(pieces, axis=0) * sint_ref[...]

    def store_keys(ref, y, extra):
        for gi in range(N_KV_GROUPS):
            ref[0, gi, :, 0:HEAD_DIM] = y[:, gi * HEAD_DIM:(gi + 1) * HEAD_DIM].astype(ref.dtype)
            ref[0, gi, :, HEAD_DIM:] = extra.astype(ref.dtype)

    def store_values_t(ref, yt):
        ones = jnp.ones((VT_ROWS - HEAD_DIM, rows), ref.dtype)
        for gi in range(N_KV_GROUPS):
            ref[0, gi, 0, 0:HEAD_DIM, :] = yt[gi * HEAD_DIM:(gi + 1) * HEAD_DIM].astype(ref.dtype)
            ref[0, gi, 0, HEAD_DIM:, :] = ones

    for j in range(D_MODEL // PROJ_COLS):
        qt = rope_t(row_tile(j * PROJ_COLS, PROJ_COLS)) * Q_SCALE
        qt_ref[0, j * PROJ_COLS:(j + 1) * PROJ_COLS, :] = qt.astype(qt_ref.dtype)
    raw_k = col_tile(0)
    raw_v = col_tile(1)
    for gi in range(N_KV_GROUPS):
        kcv_ref[0, gi, :, 0:HEAD_DIM] = raw_k[:, gi * HEAD_DIM:(gi + 1) * HEAD_DIM]
        kcv_ref[0, gi, :, HEAD_DIM:] = raw_v[:, gi * HEAD_DIM:(gi + 1) * HEAD_DIM]
    store_keys(ks_ref, rope(col_tile(2)), blk_onehot)
    store_keys(kw_ref, rope(col_tile(3)), zeros)
    store_values_t(vst_ref, row_tile(D_MODEL, KV_DIM))
    store_values_t(vwt_ref, row_tile(D_MODEL + KV_DIM, KV_DIM))
    logits_t = row_tile(D_MODEL + 2 * KV_DIM, GATE_ROWS)
    gatet_ref[0] = 1.0 / (1.0 + jnp.exp(-logits_t))


def _in_proj(x, w_in, B, S):
    T, D = x.shape
    rows = PROJ_ROWS
    assert rows == ATT_K_ROWS
    steps_per_seq = S // rows
    sec = lambda k: w_in[:, D_MODEL + k * KV_DIM:D_MODEL + (k + 1) * KV_DIM]
    w = jnp.concatenate([sec(0), sec(1), sec(2), sec(4)], axis=1).astype(BF16)
    wg = w_in[:, D_MODEL + 6 * KV_DIM:].reshape(D, N_KV_GROUPS, N_BRANCHES * HEADS_PER_GROUP)
    wg = jnp.pad(wg, ((0, 0), (0, 0), (0, GATE_SLOTS - N_BRANCHES * HEADS_PER_GROUP))).reshape(D, GATE_ROWS)
    wt = jnp.concatenate([w_in[:, :D_MODEL], sec(3), sec(5), wg], axis=1).T.astype(BF16)
    cos, sin = _rope_tables(jnp.arange(S), PROJ_COLS // HEAD_DIM)
    k_shape = jax.ShapeDtypeStruct((B, N_KV_GROUPS, S, AUG_DIM), BF16)
    k_spec = pl.BlockSpec((1, N_KV_GROUPS, rows, AUG_DIM),
                          lambda i: (i // steps_per_seq, 0, i % steps_per_seq, 0))
    vt_shape = jax.ShapeDtypeStruct((B, N_KV_GROUPS, steps_per_seq, VT_ROWS, rows), BF16)
    vt_spec = pl.BlockSpec((1, N_KV_GROUPS, 1, VT_ROWS, rows),
                           lambda i: (i // steps_per_seq, 0, i % steps_per_seq, 0, 0))
    tok_map = lambda i: (i, 0)
    seq_map = lambda i: (i % steps_per_seq, 0)
    feat_map = lambda i: (i // steps_per_seq, 0, i % steps_per_seq)
    return pl.pallas_call(
        functools.partial(_in_proj_kernel, steps_per_seq=steps_per_seq),
        grid=(T // rows,),
        in_specs=[
            pl.BlockSpec((rows, D), tok_map),
            pl.BlockSpec(w.shape, lambda i: (0, 0)),
            pl.BlockSpec(wt.shape, lambda i: (0, 0)),
            pl.BlockSpec((rows, PROJ_COLS), seq_map),
            pl.BlockSpec((rows, PROJ_COLS), seq_map),
            pl.BlockSpec((PROJ_COLS, rows), lambda i: (0, i % steps_per_seq)),
            pl.BlockSpec((PROJ_COLS, rows), lambda i: (0, i % steps_per_seq)),
        ],
        out_specs=[
            pl.BlockSpec((1, D_MODEL, rows), feat_map),
            k_spec,
            k_spec, vt_spec, k_spec, vt_spec,
            pl.BlockSpec((1, GATE_ROWS, rows), feat_map),
        ],
        out_shape=[
            jax.ShapeDtypeStruct((B, D_MODEL, S), BF16),
            jax.ShapeDtypeStruct((B, N_KV_GROUPS, S, AUG_DIM), F32),
            k_shape, vt_shape, k_shape, vt_shape,
            jax.ShapeDtypeStruct((B, GATE_ROWS, S), F32),
        ],
        compiler_params=_params("parallel"),
    )(x, w, wt, cos, sin, cos.T, sin.T)


def _compress_kernel(kv_ref, w1_ref, pea_ref, peb_ref, w2_ref, w2t_ref, cos_ref, sin_ref, kc_ref, vct_ref):
    n = kc_ref.shape[2]
    hid = CMP_HIDDEN
    r = jnp.zeros((n, 4 * hid), F32)
    bias_a = jnp.zeros((8, 4 * hid), F32)
    bias_b = jnp.zeros((8, 4 * hid), F32)
    for p in range(CMP_STRIDE):
        slab = kv_ref[0, 0, pl.ds(p, n, stride=CMP_STRIDE), :].astype(BF16)
        r = r + _dot(slab, w1_ref[p])
        bias_a = bias_a + _dot(pea_ref[p], w1_ref[p])
        bias_b = bias_b + _dot(peb_ref[p], w1_ref[p])
    hidden = []
    for which in range(2):
        c0 = which * 2 * hid
        nxt = pltpu.roll(r[:, c0 + hid:c0 + 2 * hid], n - 1, 0)
        bias = bias_a[0:1, c0:c0 + hid] + bias_b[0:1, c0 + hid:c0 + 2 * hid]
        hidden.append(_silu(r[:, c0:c0 + hid] + nxt + bias).astype(BF16))
    kc = _dot(hidden[0], w2_ref[...])
    rot = jnp.concatenate([kc[:, HALF_DIM:], kc[:, :HALF_DIM]], axis=1)
    kc_ref[0, 0] = (kc * cos_ref[...] + rot * sin_ref[...]).astype(kc_ref.dtype)
    vct_ref[0, 0] = _dot_nt(w2t_ref[...], hidden[1]).astype(vct_ref.dtype)


def _compress(kv_raw, w1_k, w2_k, pe_k, w1_v, w2_v, pe_v, B, S):
    n_chunks = S // CMP_STRIDE
    hid = CMP_HIDDEN
    split = lambda w1: w1.reshape(2, CMP_STRIDE, HEAD_DIM, hid).transpose(1, 2, 0, 3).reshape(CMP_STRIDE, HEAD_DIM, 2 * hid)
    zeros = jnp.zeros((CMP_STRIDE, HEAD_DIM, 2 * hid), F32)
    w1 = jnp.concatenate([jnp.concatenate([split(w1_k), zeros], axis=2),
                          jnp.concatenate([zeros, split(w1_v)], axis=2)], axis=1).astype(BF16)
    pe = jnp.concatenate([pe_k, pe_v], axis=1)
    rows8 = lambda v: jnp.broadcast_to(v[:, None, :], (CMP_STRIDE, 8, AUG_DIM)).astype(BF16)
    pea, peb = rows8(pe[:CMP_STRIDE]), rows8(pe[CMP_STRIDE:])
    cos, sin = _rope_tables(CMP_STRIDE * jnp.arange(n_chunks) + CMP_BLOCK - 1, 1)
    const = lambda a: pl.BlockSpec(a.shape, lambda bi, gi: (0,) * a.ndim)
    w2k = w2_k.astype(BF16)
    w2vt = w2_v.T.astype(BF16)
    return pl.pallas_call(
        _compress_kernel,
        grid=(B, N_KV_GROUPS),
        in_specs=[
            pl.BlockSpec((1, 1, S, AUG_DIM), lambda bi, gi: (bi, gi, 0, 0)),
            const(w1), const(pea), const(peb), const(w2k), const(w2vt), const(cos), const(sin),
        ],
        out_specs=[pl.BlockSpec((1, 1, n_chunks, HEAD_DIM), lambda bi, gi: (bi, gi, 0, 0)),
                   pl.BlockSpec((1, 1, HEAD_DIM, n_chunks), lambda bi, gi: (bi, gi, 0, 0))],
        out_shape=[jax.ShapeDtypeStruct((B, N_KV_GROUPS, n_chunks, HEAD_DIM), BF16),
                   jax.ShapeDtypeStruct((B, N_KV_GROUPS, HEAD_DIM, n_chunks), BF16)],
        compiler_params=_params("parallel", "parallel"),
    )(kv_raw, w1, pea, peb, w2k, w2vt, cos, sin)


def _cmp_attn_kernel(qt_ref, kc_ref, vct_ref, ovl_ref, oct_ref, selt_ref, score_ref):
    i = pl.program_id(2)
    cols = qt_ref.shape[2]
    n_cmp = kc_ref.shape[2]
    n_sel = ovl_ref.shape[0]
    kc = kc_ref[0, 0]
    vct = vct_ref[0, 0]
    t = i * cols + lax.broadcasted_iota(jnp.int32, (1, cols), 1)
    cend = CMP_STRIDE * lax.broadcasted_iota(jnp.int32, (n_cmp, 1), 0) + (CMP_BLOCK - 1)
    cvalid = cend <= t
    any_valid = jnp.where(t >= CMP_BLOCK - 1, 1.0, 0.0)
    pc_sum = jnp.zeros((n_cmp, cols), F32)
    for h in range(HEADS_PER_GROUP):
        qh = qt_ref[0, h * HEAD_DIM:(h + 1) * HEAD_DIM, :]
        s = jnp.where(cvalid, _dot(kc, qh), NEG_INF)
        e = jnp.exp2(s - jnp.max(s, axis=0, keepdims=True))
        pc = e * (any_valid / jnp.sum(e, axis=0, keepdims=True))
        oct_ref[0, h * HEAD_DIM:(h + 1) * HEAD_DIM, :] = _dot(vct, pc.astype(BF16))
        pc_sum = pc_sum + pc
    imp = _dot(ovl_ref[...], pc_sum.astype(BF16))
    j = lax.broadcasted_iota(jnp.int32, (n_sel, 1), 0)
    blk_t = t >> SEL_SHIFT
    bvalid = j <= blk_t
    forced = (j == 0) | (j == blk_t) | (j == blk_t - 1)
    score = jnp.where(bvalid, imp + jnp.where(forced, FORCE_BONUS, 0.0), -1.0)
    key = pltpu.bitcast(score, jnp.int32)
    key_next = key + 1
    score_ref[...] = key
    def rank_group(grp, rank):
        base = pl.multiple_of(grp * 8, 8)
        others = score_ref[pl.ds(base, 8), :]
        for r in range(8):
            ahead = others[r:r + 1, :] >= jnp.where(j > base + r, key, key_next)
            rank = rank + jnp.where(ahead, 1.0, 0.0)
        return rank

    n_groups = jnp.minimum(((i + 1) * cols) // (8 * SEL_BLOCK), n_sel // 8)
    rank = lax.fori_loop(0, n_groups, rank_group, jnp.zeros((n_sel, cols), F32))
    selected = (rank < float(SEL_TOP_N)) & bvalid
    selt_ref[0, 0] = jnp.where(selected, 0.0, NEG_INF).astype(selt_ref.dtype)


def _cmp_attn(qt, kc, vct, B, S):
    cols = CMP_Q_ROWS
    n_cmp = S // CMP_STRIDE
    n_sel = SEL_SLOTS
    assert S // SEL_BLOCK <= SEL_SLOTS
    cstart = CMP_STRIDE * jnp.arange(n_cmp)
    sstart = SEL_BLOCK * jnp.arange(n_sel)
    overlap = ((cstart[None, :] <= sstart[:, None] + SEL_BLOCK - 1)
               & (cstart[None, :] + CMP_BLOCK - 1 >= sstart[:, None])).astype(BF16)
    q_spec = pl.BlockSpec((1, GROUP_Q_DIM, cols), lambda bi, gi, i: (bi, gi, i))
    return pl.pallas_call(
        _cmp_attn_kernel,
        grid=(B, N_KV_GROUPS, S // cols),
        in_specs=[
            q_spec,
            pl.BlockSpec((1, 1, n_cmp, HEAD_DIM), lambda bi, gi, i: (bi, gi, 0, 0)),
            pl.BlockSpec((1, 1, HEAD_DIM, n_cmp), lambda bi, gi, i: (bi, gi, 0, 0)),
            pl.BlockSpec(overlap.shape, lambda bi, gi, i: (0, 0)),
        ],
        out_specs=[
            q_spec,
            pl.BlockSpec((1, 1, n_sel, cols), lambda bi, gi, i: (bi, gi, 0, i)),
        ],
        out_shape=[
            jax.ShapeDtypeStruct((B, D_MODEL, S), F32),
            jax.ShapeDtypeStruct((B, N_KV_GROUPS, n_sel, S), BF16),
        ],
        scratch_shapes=[pltpu.VMEM((n_sel, cols), jnp.int32)],
        compiler_params=_params("parallel", "parallel", "arbitrary"),
    )(qt, kc, vct, overlap)


def _sel_win_attn_kernel(qt_ref, selt_ref, ks_ref, vst_ref, kw_ref, vwt_ref, diag_ref, band_ref, oct_ref, gatet_ref,
                         o_ref, qaug_ref, m_ref, acc_ref, ow_ref):
    i = pl.program_id(2)
    tq = qt_ref.shape[2]
    hg = HEADS_PER_GROUP
    half = ATT_HALF
    n_half = tq // half
    half_cols = hg * half

    for hq in range(n_half):
        for h in range(hg):
            c0 = (hq * hg + h) * half
            qaug_ref[0:HEAD_DIM, c0:c0 + half] = qt_ref[0, h * HEAD_DIM:(h + 1) * HEAD_DIM, hq * half:(hq + 1) * half]
            qaug_ref[HEAD_DIM:, c0:c0 + half] = selt_ref[0, 0, :, hq * half:(hq + 1) * half]
    n_tiles = (hg * tq) // ATT_COLS

    def keys(ref, start, size):
        return ref[0, 0, pl.ds(pl.multiple_of(start, ATT_HALF), size), :]

    m_ref[...] = jnp.full(m_ref.shape, NEG_INF, F32)
    acc_ref[...] = jnp.zeros(acc_ref.shape, F32)

    def sel_step(c, bias_ref):
        k = keys(ks_ref, c * ATT_K_ROWS, ATT_K_ROWS)
        v = vst_ref[0, 0, c]
        new_m, new_acc = [], []
        tile = lambda ct: slice(ct * ATT_COLS, (ct + 1) * ATT_COLS)
        last_part = lambda ct: ((ct + 1) * ATT_COLS - 1) // half_cols
        n_keys = lambda ct: ATT_K_ROWS if bias_ref is None else min(ATT_K_ROWS, (last_part(ct) + 1) * half)
        score = lambda ct: _dot(k[0:n_keys(ct)], qaug_ref[:, tile(ct)])
        scores = [score(ct) for ct in range(ATT_AHEAD)]
        for ct in range(n_tiles):
            cs = tile(ct)
            if ct + ATT_AHEAD < n_tiles:
                scores.append(score(ct + ATT_AHEAD))
            s = scores[ct]
            if bias_ref is not None:
                s = s + bias_ref[0:n_keys(ct), cs]
            m_prev = m_ref[:, cs]
            m_next = jnp.maximum(m_prev, jnp.max(s, axis=0, keepdims=True))
            p = jnp.exp2(s - m_next)
            alpha = jnp.exp2(m_prev - m_next)
            new_acc.append(alpha * acc_ref[:, cs] + _dot(v[:, 0:n_keys(ct)], p.astype(BF16)))
            new_m.append(m_next)
        for ct in range(n_tiles):
            cs = slice(ct * ATT_COLS, (ct + 1) * ATT_COLS)
            acc_ref[:, cs] = new_acc[ct]
            m_ref[:, cs] = new_m[ct]

    def sel_body(c, carry):
        sel_step(c, None)
        return carry

    lax.fori_loop(0, i, sel_body, 0)
    sel_step(i, diag_ref)

    def probs(s):
        return jnp.exp2(s - jnp.max(s, axis=0, keepdims=True)).astype(BF16)

    @pl.when(i == 0)
    def _():
        k = keys(kw_ref, 0, tq)
        v = vwt_ref[0, 0, 0]
        for ct in range(n_tiles):
            cs = slice(ct * ATT_COLS, (ct + 1) * ATT_COLS)
            ow_ref[:, cs] = _dot(v, probs(_dot(k, qaug_ref[:, cs]) + diag_ref[:, cs]))

    @pl.when(i > 0)
    def _():
        v_prev = vwt_ref[0, 0, i - 1]
        v_here = vwt_ref[0, 0, i]
        for hq in range(n_half):
            k = keys(kw_ref, i * tq + hq * half - WINDOW, WINDOW + half)
            n_prev = tq - hq * half
            for ct in range(half_cols // ATT_COLS):
                cs = slice(hq * half_cols + ct * ATT_COLS, hq * half_cols + (ct + 1) * ATT_COLS)
                p = probs(_dot(k, qaug_ref[:, cs]) + band_ref[:, ct * ATT_COLS:(ct + 1) * ATT_COLS])
                ow_ref[:, cs] = (_dot(v_prev[:, tq - n_prev:], p[0:n_prev])
                                 + _dot(v_here[:, 0:WINDOW + half - n_prev], p[n_prev:]))

    acc_s = acc_ref[...]
    acc_w = ow_ref[...]
    o_s = acc_s[0:HEAD_DIM] / acc_s[HEAD_DIM:HEAD_DIM + 1]
    o_w = acc_w[0:HEAD_DIM] / acc_w[HEAD_DIM:HEAD_DIM + 1]
    for hq in range(n_half):
        tok = slice(hq * half, (hq + 1) * half)
        for h in range(hg):
            c0 = (hq * hg + h) * half
            gc = gatet_ref[0, N_BRANCHES * h + 0:N_BRANCHES * h + 1, tok]
            gs = gatet_ref[0, N_BRANCHES * h + 1:N_BRANCHES * h + 2, tok]
            gw = gatet_ref[0, N_BRANCHES * h + 2:N_BRANCHES * h + 3, tok]
            o = (gc * oct_ref[0, h * HEAD_DIM:(h + 1) * HEAD_DIM, tok]
                 + gs * o_s[:, c0:c0 + half] + gw * o_w[:, c0:c0 + half])
            o_ref[0, tok, h * HEAD_DIM:(h + 1) * HEAD_DIM] = o.T.astype(o_ref.dtype)


def _sel_win_attn(qt, selt, ks, vst, kw, vwt, oct, gatet, B, S):
    tq = ATT_Q_ROWS
    half = ATT_HALF
    hg = HEADS_PER_GROUP
    cols = hg * tq
    n_chunks = S // ATT_K_ROWS
    t_rel = (jnp.arange(tq // half)[:, None, None] * half + jnp.arange(half)[None, None, :])
    t_rel = jnp.broadcast_to(t_rel, (tq // half, hg, half)).reshape(1, cols)
    diag = jnp.where(jnp.arange(ATT_K_ROWS)[:, None] <= t_rel, 0.0, NEG_INF).astype(F32)
    tt = jnp.broadcast_to(jnp.arange(half)[None, :], (hg, half)).reshape(1, hg * half)
    a = jnp.arange(WINDOW + half)[:, None]
    band = jnp.where((a > tt) & (a <= tt + WINDOW), 0.0, NEG_INF).astype(F32)
    k_spec = pl.BlockSpec((1, 1, S, AUG_DIM), lambda bi, gi, i: (bi, gi, 0, 0))
    vt_spec = pl.BlockSpec((1, 1, n_chunks, VT_ROWS, ATT_K_ROWS), lambda bi, gi, i: (bi, gi, 0, 0, 0))
    q_spec = pl.BlockSpec((1, GROUP_Q_DIM, tq), lambda bi, gi, i: (bi, gi, i))
    const = lambda arr: pl.BlockSpec(arr.shape, lambda bi, gi, i: (0, 0))
    return pl.pallas_call(
        _sel_win_attn_kernel,
        grid=(B, N_KV_GROUPS, S // tq),
        in_specs=[
            q_spec,
            pl.BlockSpec((1, 1, SEL_SLOTS, tq), lambda bi, gi, i: (bi, gi, 0, i)),
            k_spec, vt_spec, k_spec, vt_spec,
            const(diag), const(band),
            q_spec,
            pl.BlockSpec((1, GATE_SLOTS, tq), lambda bi, gi, i: (bi, gi, i)),
        ],
        out_specs=pl.BlockSpec((1, tq, GROUP_Q_DIM), lambda bi, gi, i: (bi, i, gi)),
        out_shape=jax.ShapeDtypeStruct((B, S, D_MODEL), BF16),
        scratch_shapes=[
            pltpu.VMEM((AUG_DIM, cols), BF16),
            pltpu.VMEM((1, cols), F32),
            pltpu.VMEM((VT_ROWS, cols), F32),
            pltpu.VMEM((VT_ROWS, cols), F32),
        ],
        compiler_params=_params("parallel", "parallel", "arbitrary"),
    )(qt, selt, ks, vst, kw, vwt, diag, band, oct, gatet)


def _out_proj_ln_kernel(o_ref, w_ref, x_ref, g_ref, b_ref, y_ref):
    h = _dot(o_ref[...], w_ref[...])
    y_ref[...] = _layer_norm(ALPHA * x_ref[...] + h, g_ref[...], b_ref[...])


def _out_proj_ln(o, w_out, x, g, b):
    T, D = x.shape
    rows = PROJ_ROWS
    return pl.pallas_call(
        _out_proj_ln_kernel,
        grid=(T // rows,),
        in_specs=[
            pl.BlockSpec((rows, D), lambda i: (i, 0)),
            pl.BlockSpec((D, D), lambda i: (0, 0)),
            pl.BlockSpec((rows, D), lambda i: (i, 0)),
            pl.BlockSpec((1, D), lambda i: (0, 0)),
            pl.BlockSpec((1, D), lambda i: (0, 0)),
        ],
        out_specs=pl.BlockSpec((rows, D), lambda i: (i, 0)),
        out_shape=jax.ShapeDtypeStruct((T, D), F32),
        compiler_params=_params("parallel"),
    )(o, w_out.astype(BF16), x, g.reshape(1, D), b.reshape(1, D))


def _route_kernel(x_ref, r_ref, tri_ref, xb_ref, info_ref, wcol_ref, cnt_ref, carry_ref):
    @pl.when(pl.program_id(0) == 0)
    def _():
        carry_ref[...] = jnp.zeros_like(carry_ref)

    x = x_ref[...]
    xb = x.astype(BF16)
    xb_ref[...] = xb
    x_lo = (x - xb.astype(F32)).astype(BF16)
    r_hi = r_ref[0]
    r_lo = r_ref[1]
    logits = _dot(xb, r_hi) + (_dot(x_lo, r_hi) + _dot(xb, r_lo))
    lane = lax.broadcasted_iota(jnp.int32, logits.shape, 1)
    logits = jnp.where(lane < N_EXPERTS, logits, -jnp.inf)
    v1 = jnp.max(logits, axis=-1, keepdims=True)
    i1 = jnp.min(jnp.where(logits == v1, lane, LANES), axis=-1, keepdims=True)
    rest = jnp.where(lane == i1, -jnp.inf, logits)
    v2 = jnp.max(rest, axis=-1, keepdims=True)
    i2 = jnp.min(jnp.where(rest == v2, lane, LANES), axis=-1, keepdims=True)
    e2 = jnp.exp(v2 - v1)
    w1 = 1.0 / (1.0 + e2)
    w2 = e2 / (1.0 + e2)
    m1 = jnp.where(lane == i1, 1.0, 0.0)
    m2 = jnp.where(lane == i2, 1.0, 0.0)
    routed = m1 + m2
    before = _dot(tri_ref[...], routed.astype(BF16)) + carry_ref[0:1, :]
    rank1 = jnp.sum(m1 * before, axis=-1, keepdims=True)
    rank2 = jnp.sum(m2 * before, axis=-1, keepdims=True)
    cnt = jnp.sum(routed, axis=0, keepdims=True)
    carry_ref[...] = carry_ref[...] + cnt
    cnt_ref[0] = jnp.broadcast_to(cnt, cnt_ref.shape[1:])
    info = jnp.where(lane == 0, i1.astype(F32),
                     jnp.where(lane == 1, i2.astype(F32),
                               jnp.where(lane == 2, rank1, jnp.where(lane == 3, rank2, 0.0))))
    info_ref[0] = info.T[0:8, :]
    w1_hi = w1.astype(BF16).astype(F32)
    w2_hi = w2.astype(BF16).astype(F32)
    lo_half = (lane & 1) == 1
    wcol = (jnp.where((lane >> 1) == i1, jnp.where(lo_half, w1 - w1_hi, w1_hi), 0.0)
            + jnp.where((lane >> 1) == i2, jnp.where(lo_half, w2 - w2_hi, w2_hi), 0.0))
    wcol_ref[...] = wcol.astype(BF16)


def _route(x, router):
    T, D = x.shape
    rows = MOE_ROWS
    r = jnp.pad(router, ((0, 0), (0, LANES - router.shape[1])))
    r_hi = r.astype(BF16)
    r_lo = (r - r_hi.astype(F32)).astype(BF16)
    r2 = jnp.stack([r_hi, r_lo])
    tri = (jnp.arange(rows)[None, :] < jnp.arange(rows)[:, None]).astype(BF16)
    return pl.pallas_call(
        _route_kernel,
        grid=(T // rows,),
        in_specs=[
            pl.BlockSpec((rows, D), lambda i: (i, 0)),
            pl.BlockSpec(r2.shape, lambda i: (0, 0, 0)),
            pl.BlockSpec(tri.shape, lambda i: (0, 0)),
        ],
        out_specs=[
            pl.BlockSpec((rows, D), lambda i: (i, 0)),
            pl.BlockSpec((1, 8, rows), lambda i: (i, 0, 0)),
            pl.BlockSpec((rows, LANES), lambda i: (i, 0)),
            pl.BlockSpec((1, 8, LANES), lambda i: (i, 0, 0)),
        ],
        out_shape=[
            jax.ShapeDtypeStruct((T, D), BF16),
            jax.ShapeDtypeStruct((T // rows, 8, rows), F32),
            jax.ShapeDtypeStruct((T, LANES), BF16),
            jax.ShapeDtypeStruct((T // rows, 8, LANES), F32),
        ],
        scratch_shapes=[pltpu.VMEM((8, LANES), F32)],
        compiler_params=_params("arbitrary"),
    )(x, r2, tri)


def _moe_tables(cnt, n_blocks_max, n_items_max):
    R = MOE_ROWS
    C, E = cnt.shape
    i32 = jnp.int32
    count_le = lambda sorted_v, q: jnp.sum(sorted_v[None, :] <= q[:, None], axis=1).astype(i32)

    def windows(ra, rb, live):
        near = jnp.minimum((ra // MOE_ALIGN) * MOE_ALIGN, R - MOE_WIN)
        fits = rb - near < MOE_WIN
        start = jnp.where(fits, near, (ra // MOE_WIN) * MOE_WIN)
        n = jnp.where(fits, 1, rb // MOE_WIN - ra // MOE_WIN + 1)
        return jnp.where(live, start, 0), jnp.where(live, n, 0)

    cum = jnp.concatenate([jnp.zeros((1, E), i32), jnp.cumsum(cnt, axis=0)], axis=0)
    tot = cum[-1]
    nb = (tot + R - 1) // R
    nb_end = jnp.cumsum(nb)
    blk_start = nb_end - nb
    n_blocks = nb_end[-1]
    b = jnp.minimum(jnp.arange(n_blocks_max, dtype=i32), n_blocks - 1)
    bexp = jnp.minimum(count_le(nb_end, b), E - 1)
    bvalid = jnp.arange(n_blocks_max, dtype=i32) < n_blocks
    lb = b - blk_start[bexp]
    rho0 = lb * R
    rho1 = jnp.minimum((lb + 1) * R, tot[bexp]) - 1
    cum_b = cum[1:, :][:, bexp].T
    lo = jnp.minimum(jnp.sum(cum_b <= rho0[:, None], axis=1).astype(i32), C - 1)
    hi = jnp.minimum(jnp.sum(cum_b <= rho1[:, None], axis=1).astype(i32), C - 1)
    nit = jnp.where(bvalid, hi - lo + 1, 0)
    it_end = jnp.cumsum(nit)
    it_start = it_end - nit
    n_items = it_end[-1]
    i = jnp.arange(n_items_max, dtype=i32)
    ic = jnp.minimum(i, n_items - 1)
    d_blk = jnp.minimum(count_le(it_end, ic), n_blocks_max - 1)
    d_chk = lo[d_blk] + ic - it_start[d_blk]
    d_valid = i < n_items
    d_first = d_valid & (ic == it_start[d_blk])
    d_last = d_valid & (ic == it_end[d_blk] - 1)
    d_exp = bexp[d_blk]
    d_ra = jnp.maximum(rho0[d_blk], cum[d_chk, d_exp]) - rho0[d_blk]
    d_rb = jnp.minimum(rho1[d_blk], cum[d_chk + 1, d_exp] - 1) - rho0[d_blk]
    d_start, d_nwin = windows(d_ra, d_rb, d_valid & (d_rb >= d_ra))
    bl_lo = (blk_start[None, :] + cum[:-1] // R).reshape(-1)
    bl_hi = (blk_start[None, :] + (cum[1:] - 1) // R).reshape(-1)
    npair = jnp.where(cnt.reshape(-1) > 0, bl_hi - bl_lo + 1, 0)
    p_end = jnp.cumsum(npair)
    p_start = p_end - npair
    n_items2 = p_end[-1]
    jc = jnp.minimum(i, n_items2 - 1)
    pair = jnp.minimum(count_le(p_end, jc), C * E - 1)
    c_blk = bl_lo[pair] + jc - p_start[pair]
    c_chk = pair // E
    c_valid = i < n_items2
    prev_chk = jnp.concatenate([jnp.full((1,), -1, i32), c_chk[:-1]])
    next_chk = jnp.concatenate([c_chk[1:], jnp.full((1,), -1, i32)])
    next_valid = jnp.concatenate([c_valid[1:], jnp.zeros((1,), bool)])
    c_first = c_valid & (c_chk != prev_chk)
    c_last = c_valid & ((c_chk != next_chk) | ~next_valid)
    c_exp = pair % E
    c_base = (c_blk - blk_start[c_exp]) * R
    c_ra = jnp.maximum(cum[c_chk, c_exp] - c_base, 0)
    c_rb = jnp.minimum(cum[c_chk + 1, c_exp] - 1 - c_base, R - 1)
    c_start, c_nwin = windows(c_ra, c_rb, c_valid)
    as_i32 = lambda v: v.astype(i32)
    return dict(base=blk_start * R, bexp=bexp, bvalid=as_i32(bvalid),
                d_blk=d_blk, d_chk=d_chk, d_first=as_i32(d_first), d_last=as_i32(d_last), d_start=d_start, d_nwin=d_nwin,
                c_blk=c_blk, c_chk=c_chk, c_first=as_i32(c_first), c_last=as_i32(c_last),
                c_start=c_start, c_nwin=c_nwin)


def _dispatch_kernel(blk_ref, chk_ref, first_ref, last_ref, start_ref, nwin_ref, xb_ref, pos_ref, wcol_ref,
                     xs_ref, ws_ref, acc_ref, wacc_ref):
    i = pl.program_id(0)
    R = xs_ref.shape[0]

    @pl.when(first_ref[i] == 1)
    def _():
        acc_ref[...] = jnp.zeros_like(acc_ref)
        wacc_ref[...] = jnp.zeros_like(wacc_ref)

    pos1 = pos_ref[0, 0:1, :]
    pos2 = pos_ref[0, 1:2, :]

    def window(k, carry):
        r0 = pl.multiple_of(start_ref[i] + k * MOE_WIN, MOE_ALIGN)
        rows = (blk_ref[i] * R + r0 + lax.broadcasted_iota(jnp.int32, (MOE_WIN, 1), 0)).astype(F32)
        hit = jnp.where((pos1 - rows) * (pos2 - rows) == 0.0, 1.0, 0.0).astype(BF16)
        acc_ref[pl.ds(r0, MOE_WIN), :] += _dot(hit, xb_ref[...])
        wacc_ref[pl.ds(r0, MOE_WIN), :] += _dot(hit, wcol_ref[...])
        return carry

    lax.fori_loop(0, nwin_ref[i], window, 0)

    @pl.when(last_ref[i] == 1)
    def _():
        xs_ref[...] = acc_ref[...].astype(xs_ref.dtype)
        ws_ref[...] = wacc_ref[...]


def _dispatch(xb, posrow, wcol, tab, n_blocks_max, n_items_max):
    T, D = xb.shape
    R = MOE_ROWS
    chunk_map = lambda i, blk, chk, first, last, start, nwin: (chk[i], 0)
    block_map = lambda i, blk, chk, first, last, start, nwin: (blk[i], 0)
    grid_spec = pltpu.PrefetchScalarGridSpec(
        num_scalar_prefetch=6,
        grid=(n_items_max,),
        in_specs=[
            pl.BlockSpec((R, D), chunk_map),
            pl.BlockSpec((1, 8, R), lambda i, blk, chk, first, last, start, nwin: (chk[i], 0, 0)),
            pl.BlockSpec((R, LANES), chunk_map),
        ],
        out_specs=[pl.BlockSpec((R, D), block_map), pl.BlockSpec((R, LANES), block_map)],
        scratch_shapes=[pltpu.VMEM((R, D), F32), pltpu.VMEM((R, LANES), F32)],
    )
    return pl.pallas_call(
        _dispatch_kernel,
        grid_spec=grid_spec,
        out_shape=[
            jax.ShapeDtypeStruct((n_blocks_max * R, D), BF16),
            jax.ShapeDtypeStruct((n_blocks_max * R, LANES), F32),
        ],
        compiler_params=_params("arbitrary"),
    )(tab["d_blk"], tab["d_chk"], tab["d_first"], tab["d_last"], tab["d_start"], tab["d_nwin"], xb, posrow, wcol)


def _expert_ffn_kernel(bexp_ref, bvalid_ref, xs_ref, ws_ref, wg_ref, wu_ref, wd_ref, ys_ref, acc_ref):
    b = pl.program_id(0)
    f = pl.program_id(1)

    @pl.when(f == 0)
    def _():
        acc_ref[...] = jnp.zeros_like(acc_ref)

    @pl.when(bvalid_ref[b] == 1)
    def _():
        xb = xs_ref[...]
        a = _silu(_dot(xb, wg_ref[0])) * _dot(xb, wu_ref[0])
        acc_ref[...] += _dot(a.astype(BF16), wd_ref[0])

    @pl.when(f == pl.num_programs(1) - 1)
    def _():
        ws = ws_ref[...]
        lane = lax.broadcasted_iota(jnp.int32, ws.shape, 1)
        w = jnp.sum(jnp.where((lane >> 1) == bexp_ref[b], ws, 0.0), axis=-1, keepdims=True)
        ys_ref[...] = (acc_ref[...] * w).astype(ys_ref.dtype)


def _expert_ffn(xs, ws, w_gu, w_down, tab, n_blocks_max):
    D = xs.shape[1]
    R = MOE_ROWS
    d_ff = w_down.shape[1]
    fc = _ff_chunk(d_ff, 1792)
    nf = d_ff // fc
    f_eff = lambda b, f, bvalid: f * bvalid[b] + (nf - 1) * (1 - bvalid[b])
    grid_spec = pltpu.PrefetchScalarGridSpec(
        num_scalar_prefetch=2,
        grid=(n_blocks_max, nf),
        in_specs=[
            pl.BlockSpec((R, D), lambda b, f, bexp, bvalid: (b, 0)),
            pl.BlockSpec((R, LANES), lambda b, f, bexp, bvalid: (b, 0)),
            pl.BlockSpec((1, D, fc), lambda b, f, bexp, bvalid: (bexp[b], 0, f_eff(b, f, bvalid))),
            pl.BlockSpec((1, D, fc), lambda b, f, bexp, bvalid: (bexp[b], 0, nf + f_eff(b, f, bvalid))),
            pl.BlockSpec((1, fc, D), lambda b, f, bexp, bvalid: (bexp[b], f_eff(b, f, bvalid), 0)),
        ],
        out_specs=pl.BlockSpec((R, D), lambda b, f, bexp, bvalid: (b, 0)),
        scratch_shapes=[pltpu.VMEM((R, D), F32)],
    )
    return pl.pallas_call(
        _expert_ffn_kernel,
        grid_spec=grid_spec,
        out_shape=jax.ShapeDtypeStruct(xs.shape, BF16),
        compiler_params=_params("parallel", "arbitrary"),
    )(tab["bexp"], tab["bvalid"], xs, ws, w_gu, w_gu, w_down)


def _combine_ln_kernel(chk_ref, blk_ref, first_ref, last_ref, start_ref, nwin_ref,
                       ys_ref, pc1_ref, pc2_ref, x_ref, g_ref, b_ref, o_ref, acc_ref):
    i = pl.program_id(0)
    R = ys_ref.shape[0]

    @pl.when(first_ref[i] == 1)
    def _():
        acc_ref[...] = jnp.zeros_like(acc_ref)

    def window(k, carry):
        r0 = pl.multiple_of(start_ref[i] + k * MOE_WIN, MOE_ALIGN)
        pc1 = pc1_ref[...]
        pc2 = pc2_ref[...]
        lane = lax.broadcasted_iota(jnp.int32, pc1.shape, 1)
        parts = []
        for t in range(MOE_WIN // LANES):
            col = (blk_ref[i] * R + r0 + t * LANES + lane).astype(F32)
            parts.append(jnp.where((pc1 - col) * (pc2 - col) == 0.0, 1.0, 0.0).astype(BF16))
        acc_ref[...] += _dot(jnp.concatenate(parts, axis=1), ys_ref[pl.ds(r0, MOE_WIN), :])
        return carry

    lax.fori_loop(0, nwin_ref[i], window, 0)

    @pl.when(last_ref[i] == 1)
    def _():
        o_ref[...] = _layer_norm(ALPHA * x_ref[...] + acc_ref[...], g_ref[...], b_ref[...])


def _combine_ln(ys, poscol1, poscol2, x, g, b, tab, n_items_max):
    T, D = x.shape
    R = MOE_ROWS
    chunk_map = lambda i, chk, blk, first, last, start, nwin: (chk[i], 0)
    const_map = lambda i, chk, blk, first, last, start, nwin: (0, 0)
    grid_spec = pltpu.PrefetchScalarGridSpec(
        num_scalar_prefetch=6,
        grid=(n_items_max,),
        in_specs=[
            pl.BlockSpec((R, D), lambda i, chk, blk, first, last, start, nwin: (blk[i], 0)),
            pl.BlockSpec((R, LANES), chunk_map),
            pl.BlockSpec((R, LANES), chunk_map),
            pl.BlockSpec((R, D), chunk_map),
            pl.BlockSpec((1, D), const_map),
            pl.BlockSpec((1, D), const_map),
        ],
        out_specs=pl.BlockSpec((R, D), chunk_map),
        scratch_shapes=[pltpu.VMEM((R, D), F32)],
    )
    return pl.pallas_call(
        _combine_ln_kernel,
        grid_spec=grid_spec,
        out_shape=jax.ShapeDtypeStruct((T, D), F32),
        compiler_params=_params("arbitrary"),
    )(tab["c_chk"], tab["c_blk"], tab["c_first"], tab["c_last"], tab["c_start"], tab["c_nwin"],
      ys, poscol1, poscol2, x, g.reshape(1, D), b.reshape(1, D))


def _moe_ln(x, router, w_gu, w_down, g, b):
    T, D = x.shape
    R = MOE_ROWS
    n_exp = w_down.shape[0]
    n_chunks = T // R
    n_blocks_max = (2 * T) // R + n_exp
    n_items_max = n_blocks_max + n_exp * (n_chunks - 1)
    xb, info, wcol, cnt = _route(x, router)
    tab = _moe_tables(cnt[:, 0, :n_exp].astype(jnp.int32), n_blocks_max, n_items_max)
    field = lambda k: info[:, k, :].reshape(T).astype(jnp.int32)
    pos1 = tab["base"][field(0)] + field(2)
    pos2 = tab["base"][field(1)] + field(3)
    pos1 = pos1.astype(F32)
    pos2 = pos2.astype(F32)
    posrow = jnp.stack([pos1.reshape(n_chunks, R), pos2.reshape(n_chunks, R)], axis=1)
    posrow = jnp.pad(posrow, ((0, 0), (0, 6), (0, 0)), constant_values=-1.0)
    poscol1 = jnp.broadcast_to(pos1[:, None], (T, LANES))
    poscol2 = jnp.broadcast_to(pos2[:, None], (T, LANES))
    xs, ws = _dispatch(xb, posrow, wcol, tab, n_blocks_max, n_items_max)
    ys = _expert_ffn(xs, ws, w_gu, w_down, tab, n_blocks_max)
    return _combine_ln(ys, poscol1, poscol2, x, g, b, tab, n_items_max)


def kernel(x, ln_g, ln_b, pool_w, pool_scale, nsa_w_in, nsa_pe_k, nsa_w1_k, nsa_w2_k, nsa_pe_v, nsa_w1_v,
           nsa_w2_v, nsa_w_out, ffn_w_gu, ffn_w_down, moe_router, moe_w_gu, moe_w_down):
    B, S, D = x.shape
    T = B * S
    xa = _pool_ln(x, pool_w[0], pool_scale[0], ln_g[0, 0], ln_b[0, 0]).reshape(T, D)
    x1 = _ffn_ln(xa, ffn_w_gu[0].astype(BF16), ffn_w_down[0].astype(BF16), ln_g[0, 1], ln_b[0, 1])
    qt, kcv, ks, vst, kw, vwt, gatet = _in_proj(x1, nsa_w_in[0], B, S)
    kc, vct = _compress(kcv, nsa_w1_k[0], nsa_w2_k[0], nsa_pe_k[0], nsa_w1_v[0], nsa_w2_v[0], nsa_pe_v[0], B, S)
    oct, selt = _cmp_attn(qt, kc, vct, B, S)
    o = _sel_win_attn(qt, selt, ks, vst, kw, vwt, oct, gatet, B, S).reshape(T, D)
    x2 = _out_proj_ln(o, nsa_w_out[0], x1, ln_g[1, 0], ln_b[1, 0])
    y = _moe_ln(x2, moe_router[0], moe_w_gu[0].astype(BF16), moe_w_down[0].astype(BF16), ln_g[1, 1], ln_b[1, 1])
    return y.reshape(B, S, D)
```

```python
import functools

import jax
import jax.numpy as jnp
from jax import lax
from jax.experimental import pallas as pl
from jax.experimental.pallas import tpu as pltpu

D_MODEL = 1024
DEPTH = 2
POOL_WINDOWS = (2, 4, 8, 16)
POOL_GROUP_DIM = D_MODEL // len(POOL_WINDOWS)
POOL_HALO = 16
N_HEADS = 16
N_KV_GROUPS = 4
HEADS_PER_GROUP = N_HEADS // N_KV_GROUPS
HEAD_DIM = D_MODEL // N_HEADS
HALF_DIM = HEAD_DIM // 2
GROUP_Q_DIM = HEADS_PER_GROUP * HEAD_DIM
KV_DIM = N_KV_GROUPS * HEAD_DIM
N_BRANCHES = 3
N_GATES = N_BRANCHES * N_HEADS
CMP_STRIDE = 16
CMP_BLOCK = 2 * CMP_STRIDE
CMP_HIDDEN = 2 * HEAD_DIM
SEL_BLOCK = 64
SEL_SHIFT = SEL_BLOCK.bit_length() - 1
SEL_TOP_N = 16
WINDOW = 512
FORCE_BONUS = 1.0e3
NEG_INF = -1.0e30
ROPE_THETA = 10000.0
ATTN_SCALE = HEAD_DIM ** -0.5
LOG2E = 1.4426950408889634
Q_SCALE = ATTN_SCALE * LOG2E
AUG_DIM = 2 * HEAD_DIM
SEL_SLOTS = AUG_DIM - HEAD_DIM
VT_ROWS = HEAD_DIM + 16
GATE_SLOTS = 16
GATE_ROWS = N_KV_GROUPS * GATE_SLOTS
N_EXPERTS = 8
LN_EPS = 1e-5
ALPHA = (2 * DEPTH) ** 0.25

LANES = 128
VMEM_LIMIT_BYTES = 56 * 1024 * 1024

POOL_ROWS = 512
FFN_ROWS = 512
PROJ_ROWS = 512
PROJ_COLS = 256
CMP_Q_ROWS = 512
ATT_Q_ROWS = 512
ATT_K_ROWS = 512
ATT_HALF = 128
ATT_COLS = 512
ATT_AHEAD = 3
MOE_ROWS = 512
MOE_WIN = 256
MOE_ALIGN = 16

F32 = jnp.float32
BF16 = jnp.bfloat16


def _dot(a, b):
    return jnp.dot(a, b, preferred_element_type=F32)


def _dot_nt(a, b):
    return lax.dot_general(a, b, (((1,), (1,)), ((), ())), preferred_element_type=F32)


def _layer_norm(z, g, b):
    mu = jnp.mean(z, axis=-1, keepdims=True)
    zc = z - mu
    var = jnp.mean(zc * zc, axis=-1, keepdims=True)
    return zc * lax.rsqrt(var + LN_EPS) * g + b


def _silu(x):
    return x / (1.0 + jnp.exp(-x))


def _params(*semantics):
    return pltpu.CompilerParams(dimension_semantics=semantics, vmem_limit_bytes=VMEM_LIMIT_BYTES)


def _pool_ln_kernel(x_ref, halo_ref, w_ref, scale_ref, g_ref, b_ref, o_ref, ext_ref):
    i = pl.program_id(1)
    rows = x_ref.shape[1]
    x = x_ref[0]
    ext_ref[0:POOL_HALO, :] = jnp.where(i > 0, halo_ref[0], 0.0)
    ext_ref[POOL_HALO:, :] = x
    pos = i * rows + lax.broadcasted_iota(jnp.int32, (rows, 1), 0)
    ys = []
    run = ext_ref[...]
    span = 1
    for gi, w in enumerate(POOL_WINDOWS):
        while span < w:
            run = run + pltpu.roll(run, span, 0)
            span *= 2
        xg = x[:, gi * POOL_GROUP_DIM:(gi + 1) * POOL_GROUP_DIM]
        cnt = jnp.minimum(pos + 1, w).astype(F32)
        diff = run[POOL_HALO:, 0:POOL_GROUP_DIM] / cnt - xg
        ys.append(_dot(diff.astype(BF16), w_ref[gi]))
        run = run[:, POOL_GROUP_DIM:]
    h = jnp.concatenate(ys, axis=1) * scale_ref[...]
    o_ref[0] = _layer_norm(ALPHA * x + h, g_ref[...], b_ref[...])


def _pool_ln(x, w, scale, g, b):
    B, S, D = x.shape
    rows = POOL_ROWS
    assert all(a < b_ for a, b_ in zip(POOL_WINDOWS, POOL_WINDOWS[1:]))
    assert all(w_ & (w_ - 1) == 0 for w_ in POOL_WINDOWS) and POOL_WINDOWS[-1] <= POOL_HALO
    halo_blocks = rows // POOL_HALO
    row2 = lambda v: v.reshape(1, D)
    return pl.pallas_call(
        _pool_ln_kernel,
        grid=(B, S // rows),
        in_specs=[
            pl.BlockSpec((1, rows, D), lambda bi, i: (bi, i, 0)),
            pl.BlockSpec((1, POOL_HALO, D), lambda bi, i: (bi, jnp.maximum(i * halo_blocks - 1, 0), 0)),
            pl.BlockSpec(w.shape, lambda bi, i: (0, 0, 0)),
            pl.BlockSpec((1, D), lambda bi, i: (0, 0)),
            pl.BlockSpec((1, D), lambda bi, i: (0, 0)),
            pl.BlockSpec((1, D), lambda bi, i: (0, 0)),
        ],
        out_specs=pl.BlockSpec((1, rows, D), lambda bi, i: (bi, i, 0)),
        out_shape=jax.ShapeDtypeStruct((B, S, D), F32),
        scratch_shapes=[pltpu.VMEM((rows + POOL_HALO, D), F32)],
        compiler_params=_params("parallel", "arbitrary"),
    )(x, x, w.astype(BF16), row2(scale), row2(g), row2(b))


def _ffn_ln_kernel(x_ref, wg_ref, wu_ref, wd_ref, g_ref, b_ref, o_ref, xb_ref, acc_ref):
    f = pl.program_id(1)

    @pl.when(f == 0)
    def _():
        xb_ref[...] = x_ref[...].astype(BF16)
        acc_ref[...] = jnp.zeros_like(acc_ref)

    xb = xb_ref[...]
    a = _silu(_dot(xb, wg_ref[...])) * _dot(xb, wu_ref[...])
    acc_ref[...] += _dot(a.astype(BF16), wd_ref[...])

    @pl.when(f == pl.num_programs(1) - 1)
    def _():
        o_ref[...] = _layer_norm(ALPHA * x_ref[...] + acc_ref[...], g_ref[...], b_ref[...])


def _ff_chunk(d_ff, target):
    best = LANES
    for c in range(LANES, target + 1, LANES):
        if d_ff % c == 0:
            best = c
    return best


def _ffn_ln(x, w_gu, w_down, g, b):
    T, D = x.shape
    d_ff = w_down.shape[0]
    fc = _ff_chunk(d_ff, 1536)
    nf = d_ff // fc
    rows = FFN_ROWS
    return pl.pallas_call(
        _ffn_ln_kernel,
        grid=(T // rows, nf),
        in_specs=[
            pl.BlockSpec((rows, D), lambda i, f: (i, 0)),
            pl.BlockSpec((D, fc), lambda i, f: (0, f)),
            pl.BlockSpec((D, fc), lambda i, f: (0, nf + f)),
            pl.BlockSpec((fc, D), lambda i, f: (f, 0)),
            pl.BlockSpec((1, D), lambda i, f: (0, 0)),
            pl.BlockSpec((1, D), lambda i, f: (0, 0)),
        ],
        out_specs=pl.BlockSpec((rows, D), lambda i, f: (i, 0)),
        out_shape=jax.ShapeDtypeStruct((T, D), F32),
        scratch_shapes=[pltpu.VMEM((rows, D), BF16), pltpu.VMEM((rows, D), F32)],
        compiler_params=_params("parallel", "arbitrary"),
    )(x, w_gu, w_gu, w_down, g.reshape(1, D), b.reshape(1, D))


def _rope_tables(pos, reps):
    freqs = jnp.power(ROPE_THETA, -jnp.arange(HALF_DIM, dtype=F32) / HALF_DIM)
    ang = pos.astype(F32)[:, None] * freqs[None, :]
    cos, sin = jnp.cos(ang), jnp.sin(ang)
    return (jnp.tile(jnp.concatenate([cos, cos], axis=1), (1, reps)),
            jnp.tile(jnp.concatenate([-sin, sin], axis=1), (1, reps)))


def _in_proj_kernel(x_ref, w_ref, wt_ref, cos_ref, sin_ref, cost_ref, sint_ref,
                    qt_ref, kcv_ref, ks_ref, vst_ref, kw_ref, vwt_ref, gatet_ref, *, steps_per_seq):
    rows = x_ref.shape[0]
    xb = x_ref[...].astype(BF16)
    cos = cos_ref[...]
    sin = sin_ref[...]
    lane = lax.broadcasted_iota(jnp.int32, cos.shape, 1)
    first_half = (lane & (HEAD_DIM - 1)) < HALF_DIM
    seq_step = pl.program_id(0) % steps_per_seq
    pos = seq_step * rows + lax.broadcasted_iota(jnp.int32, (rows, HEAD_DIM), 0)
    col = lax.broadcasted_iota(jnp.int32, (rows, HEAD_DIM), 1)
    blk_onehot = jnp.where((pos >> SEL_SHIFT) == col, 1.0, 0.0)
    zeros = jnp.zeros((rows, HEAD_DIM), F32)

    def col_tile(j):
        return _dot(xb, w_ref[:, j * PROJ_COLS:(j + 1) * PROJ_COLS])

    def row_tile(r0, n):
        return _dot_nt(wt_ref[r0:r0 + n, :], xb)

    def rope(y):
        rot = jnp.where(first_half,
                        pltpu.roll(y, PROJ_COLS - HALF_DIM, 1),
                        pltpu.roll(y, HALF_DIM, 1))
        return y * cos + rot * sin

    def rope_t(yt):
        pieces = []
        for h in range(yt.shape[0] // HEAD_DIM):
            pieces.append(yt[h * HEAD_DIM + HALF_DIM:(h + 1) * HEAD_DIM])
            pieces.append(yt[h * HEAD_DIM:h * HEAD_DIM + HALF_DIM])
        return yt * cost_ref[...] + jnp.concatenate(pieces, axis=0) * sint_ref[...]

    def store_keys(ref, y, extra):
        for gi in range(N_KV_GROUPS):
            ref[0, gi, :, 0:HEAD_DIM] = y[:, gi * HEAD_DIM:(gi + 1) * HEAD_DIM].astype(ref.dtype)
            ref[0, gi, :, HEAD_DIM:] = extra.astype(ref.dtype)

    def store_values_t(ref, yt):
        ones = jnp.ones((VT_ROWS - HEAD_DIM, rows), ref.dtype)
        for gi in range(N_KV_GROUPS):
            ref[0, gi, 0, 0:HEAD_DIM, :] = yt[gi * HEAD_DIM:(gi + 1) * HEAD_DIM].astype(ref.dtype)
            ref[0, gi, 0, HEAD_DIM:, :] = ones

    for j in range(D_MODEL // PROJ_COLS):
        qt = rope_t(row_tile(j * PROJ_COLS, PROJ_COLS)) * Q_SCALE
        qt_ref[0, j * PROJ_COLS:(j + 1) * PROJ_COLS, :] = qt.astype(qt_ref.dtype)
    raw_k = col_tile(0)
    raw_v = col_tile(1)
    for gi in range(N_KV_GROUPS):
        kcv_ref[0, gi, :, 0:HEAD_DIM] = raw_k[:, gi * HEAD_DIM:(gi + 1) * HEAD_DIM]
        kcv_ref[0, gi, :, HEAD_DIM:] = raw_v[:, gi * HEAD_DIM:(gi + 1) * HEAD_DIM]
    store_keys(ks_ref, rope(col_tile(2)), blk_onehot)
    store_keys(kw_ref, rope(col_tile(3)), zeros)
    store_values_t(vst_ref, row_tile(D_MODEL, KV_DIM))
    store_values_t(vwt_ref, row_tile(D_MODEL + KV_DIM, KV_DIM))
    logits_t = row_tile(D_MODEL + 2 * KV_DIM, GATE_ROWS)
    gatet_ref[0] = 1.0 / (1.0 + jnp.exp(-logits_t))


def _in_proj(x, w_in, B, S):
    T, D = x.shape
    rows = PROJ_ROWS
    assert rows == ATT_K_ROWS
    steps_per_seq = S // rows
    sec = lambda k: w_in[:, D_MODEL + k * KV_DIM:D_MODEL + (k + 1) * KV_DIM]
    w = jnp.concatenate([sec(0), sec(1), sec(2), sec(4)], axis=1).astype(BF16)
    wg = w_in[:, D_MODEL + 6 * KV_DIM:].reshape(D, N_KV_GROUPS, N_BRANCHES * HEADS_PER_GROUP)
    wg = jnp.pad(wg, ((0, 0), (0, 0), (0, GATE_SLOTS - N_BRANCHES * HEADS_PER_GROUP))).reshape(D, GATE_ROWS)
    wt = jnp.concatenate([w_in[:, :D_MODEL], sec(3), sec(5), wg], axis=1).T.astype(BF16)
    cos, sin = _rope_tables(jnp.arange(S), PROJ_COLS // HEAD_DIM)
    k_shape = jax.ShapeDtypeStruct((B, N_KV_GROUPS, S, AUG_DIM), BF16)
    k_spec = pl.BlockSpec((1, N_KV_GROUPS, rows, AUG_DIM),
                          lambda i: (i // steps_per_seq, 0, i % steps_per_seq, 0))
    vt_shape = jax.ShapeDtypeStruct((B, N_KV_GROUPS, steps_per_seq, VT_ROWS, rows), BF16)
    vt_spec = pl.BlockSpec((1, N_KV_GROUPS, 1, VT_ROWS, rows),
                           lambda i: (i // steps_per_seq, 0, i % steps_per_seq, 0, 0))
    tok_map = lambda i: (i, 0)
    seq_map = lambda i: (i % steps_per_seq, 0)
    feat_map = lambda i: (i // steps_per_seq, 0, i % steps_per_seq)
    return pl.pallas_call(
        functools.partial(_in_proj_kernel, steps_per_seq=steps_per_seq),
        grid=(T // rows,),
        in_specs=[
            pl.BlockSpec((rows, D), tok_map),
            pl.BlockSpec(w.shape, lambda i: (0, 0)),
            pl.BlockSpec(wt.shape, lambda i: (0, 0)),
            pl.BlockSpec((rows, PROJ_COLS), seq_map),
            pl.BlockSpec((rows, PROJ_COLS), seq_map),
            pl.BlockSpec((PROJ_COLS, rows), lambda i: (0, i % steps_per_seq)),
            pl.BlockSpec((PROJ_COLS, rows), lambda i: (0, i % steps_per_seq)),
        ],
        out_specs=[
            pl.BlockSpec((1, D_MODEL, rows), feat_map),
            k_spec,
            k_spec, vt_spec, k_spec, vt_spec,
            pl.BlockSpec((1, GATE_ROWS, rows), feat_map),
        ],
        out_shape=[
            jax.ShapeDtypeStruct((B, D_MODEL, S), BF16),
            jax.ShapeDtypeStruct((B, N_KV_GROUPS, S, AUG_DIM), F32),
            k_shape, vt_shape, k_shape, vt_shape,
            jax.ShapeDtypeStruct((B, GATE_ROWS, S), F32),
        ],
        compiler_params=_params("parallel"),
    )(x, w, wt, cos, sin, cos.T, sin.T)


def _compress_kernel(kv_ref, w1_ref, pea_ref, peb_ref, w2_ref, w2t_ref, cos_ref, sin_ref, kc_ref, vct_ref):
    n = kc_ref.shape[2]
    hid = CMP_HIDDEN
    r = jnp.zeros((n, 4 * hid), F32)
    bias_a = jnp.zeros((8, 4 * hid), F32)
    bias_b = jnp.zeros((8, 4 * hid), F32)
    for p in range(CMP_STRIDE):
        slab = kv_ref[0, 0, pl.ds(p, n, stride=CMP_STRIDE), :].astype(BF16)
        r = r + _dot(slab, w1_ref[p])
        bias_a = bias_a + _dot(pea_ref[p], w1_ref[p])
        bias_b = bias_b + _dot(peb_ref[p], w1_ref[p])
    hidden = []
    for which in range(2):
        c0 = which * 2 * hid
        nxt = pltpu.roll(r[:, c0 + hid:c0 + 2 * hid], n - 1, 0)
        bias = bias_a[0:1, c0:c0 + hid] + bias_b[0:1, c0 + hid:c0 + 2 * hid]
        hidden.append(_silu(r[:, c0:c0 + hid] + nxt + bias).astype(BF16))
    kc = _dot(hidden[0], w2_ref[...])
    rot = jnp.concatenate([kc[:, HALF_DIM:], kc[:, :HALF_DIM]], axis=1)
    kc_ref[0, 0] = (kc * cos_ref[...] + rot * sin_ref[...]).astype(kc_ref.dtype)
    vct_ref[0, 0] = _dot_nt(w2t_ref[...], hidden[1]).astype(vct_ref.dtype)


def _compress(kv_raw, w1_k, w2_k, pe_k, w1_v, w2_v, pe_v, B, S):
    n_chunks = S // CMP_STRIDE
    hid = CMP_HIDDEN
    split = lambda w1: w1.reshape(2, CMP_STRIDE, HEAD_DIM, hid).transpose(1, 2, 0, 3).reshape(CMP_STRIDE, HEAD_DIM, 2 * hid)
    zeros = jnp.zeros((CMP_STRIDE, HEAD_DIM, 2 * hid), F32)
    w1 = jnp.concatenate([jnp.concatenate([split(w1_k), zeros], axis=2),
                          jnp.concatenate([zeros, split(w1_v)], axis=2)], axis=1).astype(BF16)
    pe = jnp.concatenate([pe_k, pe_v], axis=1)
    rows8 = lambda v: jnp.broadcast_to(v[:, None, :], (CMP_STRIDE, 8, AUG_DIM)).astype(BF16)
    pea, peb = rows8(pe[:CMP_STRIDE]), rows8(pe[CMP_STRIDE:])
    cos, sin = _rope_tables(CMP_STRIDE * jnp.arange(n_chunks) + CMP_BLOCK - 1, 1)
    const = lambda a: pl.BlockSpec(a.shape, lambda bi, gi: (0,) * a.ndim)
    w2k = w2_k.astype(BF16)
    w2vt = w2_v.T.astype(BF16)
    return pl.pallas_call(
        _compress_kernel,
        grid=(B, N_KV_GROUPS),
        in_specs=[
            pl.BlockSpec((1, 1, S, AUG_DIM), lambda bi, gi: (bi, gi, 0, 0)),
            const(w1), const(pea), const(peb), const(w2k), const(w2vt), const(cos), const(sin),
        ],
        out_specs=[pl.BlockSpec((1, 1, n_chunks, HEAD_DIM), lambda bi, gi: (bi, gi, 0, 0)),
                   pl.BlockSpec((1, 1, HEAD_DIM, n_chunks), lambda bi, gi: (bi, gi, 0, 0))],
        out_shape=[jax.ShapeDtypeStruct((B, N_KV_GROUPS, n_chunks, HEAD_DIM), BF16),
                   jax.ShapeDtypeStruct((B, N_KV_GROUPS, HEAD_DIM, n_chunks), BF16)],
        compiler_params=_params("parallel", "parallel"),
    )(kv_raw, w1, pea, peb, w2k, w2vt, cos, sin)


def _cmp_attn_kernel(qt_ref, kc_ref, vct_ref, ovl_ref, oct_ref, selt_ref, score_ref):
    i = pl.program_id(2)
    cols = qt_ref.shape[2]
    n_cmp = kc_ref.shape[2]
    n_sel = ovl_ref.shape[0]
    kc = kc_ref[0, 0]
    vct = vct_ref[0, 0]
    t = i * cols + lax.broadcasted_iota(jnp.int32, (1, cols), 1)
    cend = CMP_STRIDE * lax.broadcasted_iota(jnp.int32, (n_cmp, 1), 0) + (CMP_BLOCK - 1)
    cvalid = cend <= t
    any_valid = jnp.where(t >= CMP_BLOCK - 1, 1.0, 0.0)
    pc_sum = jnp.zeros((n_cmp, cols), F32)
    for h in range(HEADS_PER_GROUP):
        qh = qt_ref[0, h * HEAD_DIM:(h + 1) * HEAD_DIM, :]
        s = jnp.where(cvalid, _dot(kc, qh), NEG_INF)
        e = jnp.exp2(s - jnp.max(s, axis=0, keepdims=True))
        pc = e * (any_valid / jnp.sum(e, axis=0, keepdims=True))
        oct_ref[0, h * HEAD_DIM:(h + 1) * HEAD_DIM, :] = _dot(vct, pc.astype(BF16))
        pc_sum = pc_sum + pc
    imp = _dot(ovl_ref[...], pc_sum.astype(BF16))
    j = lax.broadcasted_iota(jnp.int32, (n_sel, 1), 0)
    blk_t = t >> SEL_SHIFT
    bvalid = j <= blk_t
    forced = (j == 0) | (j == blk_t) | (j == blk_t - 1)
    score = jnp.where(bvalid, imp + jnp.where(forced, FORCE_BONUS, 0.0), -1.0)
    key = pltpu.bitcast(score, jnp.int32)
    key_next = key + 1
    score_ref[...] = key
    def rank_group(grp, rank):
        base = pl.multiple_of(grp * 8, 8)
        others = score_ref[pl.ds(base, 8), :]
        for r in range(8):
            ahead = others[r:r + 1, :] >= jnp.where(j > base + r, key, key_next)
            rank = rank + jnp.where(ahead, 1.0, 0.0)
        return rank

    n_groups = jnp.minimum(((i + 1) * cols) // (8 * SEL_BLOCK), n_sel // 8)
    rank = lax.fori_loop(0, n_groups, rank_group, jnp.zeros((n_sel, cols), F32))
    selected = (rank < float(SEL_TOP_N)) & bvalid
    selt_ref[0, 0] = jnp.where(selected, 0.0, NEG_INF).astype(selt_ref.dtype)


def _cmp_attn(qt, kc, vct, B, S):
    cols = CMP_Q_ROWS
    n_cmp = S // CMP_STRIDE
    n_sel = SEL_SLOTS
    assert S // SEL_BLOCK <= SEL_SLOTS
    cstart = CMP_STRIDE * jnp.arange(n_cmp)
    sstart = SEL_BLOCK * jnp.arange(n_sel)
    overlap = ((cstart[None, :] <= sstart[:, None] + SEL_BLOCK - 1)
               & (cstart[None, :] + CMP_BLOCK - 1 >= sstart[:, None])).astype(BF16)
    q_spec = pl.BlockSpec((1, GROUP_Q_DIM, cols), lambda bi, gi, i: (bi, gi, i))
    return pl.pallas_call(
        _cmp_attn_kernel,
        grid=(B, N_KV_GROUPS, S // cols),
        in_specs=[
            q_spec,
            pl.BlockSpec((1, 1, n_cmp, HEAD_DIM), lambda bi, gi, i: (bi, gi, 0, 0)),
            pl.BlockSpec((1, 1, HEAD_DIM, n_cmp), lambda bi, gi, i: (bi, gi, 0, 0)),
            pl.BlockSpec(overlap.shape, lambda bi, gi, i: (0, 0)),
        ],
        out_specs=[
            q_spec,
            pl.BlockSpec((1, 1, n_sel, cols), lambda bi, gi, i: (bi, gi, 0, i)),
        ],
        out_shape=[
            jax.ShapeDtypeStruct((B, D_MODEL, S), F32),
            jax.ShapeDtypeStruct((B, N_KV_GROUPS, n_sel, S), BF16),
        ],
        scratch_shapes=[pltpu.VMEM((n_sel, cols), jnp.int32)],
        compiler_params=_params("parallel", "parallel", "arbitrary"),
    )(qt, kc, vct, overlap)


def _sel_win_attn_kernel(qt_ref, selt_ref, ks_ref, vst_ref, kw_ref, vwt_ref, diag_ref, band_ref, oct_ref, gatet_ref,
                         o_ref, qaug_ref, m_ref, acc_ref, ow_ref):
    i = pl.program_id(2)
    tq = qt_ref.shape[2]
    hg = HEADS_PER_GROUP
    half = ATT_HALF
    n_half = tq // half
    half_cols = hg * half

    for hq in range(n_half):
        for h in range(hg):
            c0 = (hq * hg + h) * half
            qaug_ref[0:HEAD_DIM, c0:c0 + half] = qt_ref[0, h * HEAD_DIM:(h + 1) * HEAD_DIM, hq * half:(hq + 1) * half]
            qaug_ref[HEAD_DIM:, c0:c0 + half] = selt_ref[0, 0, :, hq * half:(hq + 1) * half]
    n_tiles = (hg * tq) // ATT_COLS

    def keys(ref, start, size):
        return ref[0, 0, pl.ds(pl.multiple_of(start, ATT_HALF), size), :]

    m_ref[...] = jnp.full(m_ref.shape, NEG_INF, F32)
    acc_ref[...] = jnp.zeros(acc_ref.shape, F32)

    def sel_step(c, bias_ref):
        k = keys(ks_ref, c * ATT_K_ROWS, ATT_K_ROWS)
        v = vst_ref[0, 0, c]
        new_m, new_acc = [], []
        tile = lambda ct: slice(ct * ATT_COLS, (ct + 1) * ATT_COLS)
        last_part = lambda ct: ((ct + 1) * ATT_COLS - 1) // half_cols
        n_keys = lambda ct: ATT_K_ROWS if bias_ref is None else min(ATT_K_ROWS, (last_part(ct) + 1) * half)
        score = lambda ct: _dot(k[0:n_keys(ct)], qaug_ref[:, tile(ct)])
        scores = [score(ct) for ct in range(ATT_AHEAD)]
        for ct in range(n_tiles):
            cs = tile(ct)
            if ct + ATT_AHEAD < n_tiles:
                scores.append(score(ct + ATT_AHEAD))
            s = scores[ct]
            if bias_ref is not None:
                s = s + bias_ref[0:n_keys(ct), cs]
            m_prev = m_ref[:, cs]
            m_next = jnp.maximum(m_prev, jnp.max(s, axis=0, keepdims=True))
            p = jnp.exp2(s - m_next)
            alpha = jnp.exp2(m_prev - m_next)
            new_acc.append(alpha * acc_ref[:, cs] + _dot(v[:, 0:n_keys(ct)], p.astype(BF16)))
            new_m.append(m_next)
        for ct in range(n_tiles):
            cs = slice(ct * ATT_COLS, (ct + 1) * ATT_COLS)
            acc_ref[:, cs] = new_acc[ct]
            m_ref[:, cs] = new_m[ct]

    def sel_body(c, carry):
        sel_step(c, None)
        return carry

    lax.fori_loop(0, i, sel_body, 0)
    sel_step(i, diag_ref)

    def probs(s):
        return jnp.exp2(s - jnp.max(s, axis=0, keepdims=True)).astype(BF16)

    @pl.when(i == 0)
    def _():
        k = keys(kw_ref, 0, tq)
        v = vwt_ref[0, 0, 0]
        for ct in range(n_tiles):
            cs = slice(ct * ATT_COLS, (ct + 1) * ATT_COLS)
            ow_ref[:, cs] = _dot(v, probs(_dot(k, qaug_ref[:, cs]) + diag_ref[:, cs]))

    @pl.when(i > 0)
    def _():
        v_prev = vwt_ref[0, 0, i - 1]
        v_here = vwt_ref[0, 0, i]
        for hq in range(n_half):
            k = keys(kw_ref, i * tq + hq * half - WINDOW, WINDOW + half)
            n_prev = tq - hq * half
            for ct in range(half_cols // ATT_COLS):
                cs = slice(hq * half_cols + ct * ATT_COLS, hq * half_cols + (ct + 1) * ATT_COLS)
                p = probs(_dot(k, qaug_ref[:, cs]) + band_ref[:, ct * ATT_COLS:(ct + 1) * ATT_COLS])
                ow_ref[:, cs] = (_dot(v_prev[:, tq - n_prev:], p[0:n_prev])
                                 + _dot(v_here[:, 0:WINDOW + half - n_prev], p[n_prev:]))

    acc_s = acc_ref[...]
    acc_w = ow_ref[...]
    o_s = acc_s[0:HEAD_DIM] / acc_s[HEAD_DIM:HEAD_DIM + 1]
    o_w = acc_w[0:HEAD_DIM] / acc_w[HEAD_DIM:HEAD_DIM + 1]
    for hq in range(n_half):
        tok = slice(hq * half, (hq + 1) * half)
        for h in range(hg):
            c0 = (hq * hg + h) * half
            gc = gatet_ref[0, N_BRANCHES * h + 0:N_BRANCHES * h + 1, tok]
            gs = gatet_ref[0, N_BRANCHES * h + 1:N_BRANCHES * h + 2, tok]
            gw = gatet_ref[0, N_BRANCHES * h + 2:N_BRANCHES * h + 3, tok]
            o = (gc * oct_ref[0, h * HEAD_DIM:(h + 1) * HEAD_DIM, tok]
                 + gs * o_s[:, c0:c0 + half] + gw * o_w[:, c0:c0 + half])
            o_ref[0, tok, h * HEAD_DIM:(h + 1) * HEAD_DIM] = o.T.astype(o_ref.dtype)


def _sel_win_attn(qt, selt, ks, vst, kw, vwt, oct, gatet, B, S):
    tq = ATT_Q_ROWS
    half = ATT_HALF
    hg = HEADS_PER_GROUP
    cols = hg * tq
    n_chunks = S // ATT_K_ROWS
    t_rel = (jnp.arange(tq // half)[:, None, None] * half + jnp.arange(half)[None, None, :])
    t_rel = jnp.broadcast_to(t_rel, (tq // half, hg, half)).reshape(1, cols)
    diag = jnp.where(jnp.arange(ATT_K_ROWS)[:, None] <= t_rel, 0.0, NEG_INF).astype(F32)
    tt = jnp.broadcast_to(jnp.arange(half)[None, :], (hg, half)).reshape(1, hg * half)
    a = jnp.arange(WINDOW + half)[:, None]
    band = jnp.where((a > tt) & (a <= tt + WINDOW), 0.0, NEG_INF).astype(F32)
    k_spec = pl.BlockSpec((1, 1, S, AUG_DIM), lambda bi, gi, i: (bi, gi, 0, 0))
    vt_spec = pl.BlockSpec((1, 1, n_chunks, VT_ROWS, ATT_K_ROWS), lambda bi, gi, i: (bi, gi, 0, 0, 0))
    q_spec = pl.BlockSpec((1, GROUP_Q_DIM, tq), lambda bi, gi, i: (bi, gi, i))
    const = lambda arr: pl.BlockSpec(arr.shape, lambda bi, gi, i: (0, 0))
    return pl.pallas_call(
        _sel_win_attn_kernel,
        grid=(B, N_KV_GROUPS, S // tq),
        in_specs=[
            q_spec,
            pl.BlockSpec((1, 1, SEL_SLOTS, tq), lambda bi, gi, i: (bi, gi, 0, i)),
            k_spec, vt_spec, k_spec, vt_spec,
            const(diag), const(band),
            q_spec,
            pl.BlockSpec((1, GATE_SLOTS, tq), lambda bi, gi, i: (bi, gi, i)),
        ],
        out_specs=pl.BlockSpec((1, tq, GROUP_Q_DIM), lambda bi, gi, i: (bi, i, gi)),
        out_shape=jax.ShapeDtypeStruct((B, S, D_MODEL), BF16),
        scratch_shapes=[
            pltpu.VMEM((AUG_DIM, cols), BF16),
            pltpu.VMEM((1, cols), F32),
            pltpu.VMEM((VT_ROWS, cols), F32),
            pltpu.VMEM((VT_ROWS, cols), F32),
        ],
        compiler_params=_params("parallel", "parallel", "arbitrary"),
    )(qt, selt, ks, vst, kw, vwt, diag, band, oct, gatet)


def _out_proj_ln_kernel(o_ref, w_ref, x_ref, g_ref, b_ref, y_ref):
    h = _dot(o_ref[...], w_ref[...])
    y_ref[...] = _layer_norm(ALPHA * x_ref[...] + h, g_ref[...], b_ref[...])


def _out_proj_ln(o, w_out, x, g, b):
    T, D = x.shape
    rows = PROJ_ROWS
    return pl.pallas_call(
        _out_proj_ln_kernel,
        grid=(T // rows,),
        in_specs=[
            pl.BlockSpec((rows, D), lambda i: (i, 0)),
            pl.BlockSpec((D, D), lambda i: (0, 0)),
            pl.BlockSpec((rows, D), lambda i: (i, 0)),
            pl.BlockSpec((1, D), lambda i: (0, 0)),
            pl.BlockSpec((1, D), lambda i: (0, 0)),
        ],
        out_specs=pl.BlockSpec((rows, D), lambda i: (i, 0)),
        out_shape=jax.ShapeDtypeStruct((T, D), F32),
        compiler_params=_params("parallel"),
    )(o, w_out.astype(BF16), x, g.reshape(1, D), b.reshape(1, D))


def _route_kernel(x_ref, r_ref, tri_ref, xb_ref, info_ref, wcol_ref, cnt_ref, carry_ref):
    @pl.when(pl.program_id(0) == 0)
    def _():
        carry_ref[...] = jnp.zeros_like(carry_ref)

    x = x_ref[...]
    xb = x.astype(BF16)
    xb_ref[...] = xb
    x_lo = (x - xb.astype(F32)).astype(BF16)
    r_hi = r_ref[0]
    r_lo = r_ref[1]
    logits = _dot(xb, r_hi) + (_dot(x_lo, r_hi) + _dot(xb, r_lo))
    lane = lax.broadcasted_iota(jnp.int32, logits.shape, 1)
    logits = jnp.where(lane < N_EXPERTS, logits, -jnp.inf)
    v1 = jnp.max(logits, axis=-1, keepdims=True)
    i1 = jnp.min(jnp.where(logits == v1, lane, LANES), axis=-1, keepdims=True)
    rest = jnp.where(lane == i1, -jnp.inf, logits)
    v2 = jnp.max(rest, axis=-1, keepdims=True)
    i2 = jnp.min(jnp.where(rest == v2, lane, LANES), axis=-1, keepdims=True)
    e2 = jnp.exp(v2 - v1)
    w1 = 1.0 / (1.0 + e2)
    w2 = e2 / (1.0 + e2)
    m1 = jnp.where(lane == i1, 1.0, 0.0)
    m2 = jnp.where(lane == i2, 1.0, 0.0)
    routed = m1 + m2
    before = _dot(tri_ref[...], routed.astype(BF16)) + carry_ref[0:1, :]
    rank1 = jnp.sum(m1 * before, axis=-1, keepdims=True)
    rank2 = jnp.sum(m2 * before, axis=-1, keepdims=True)
    cnt = jnp.sum(routed, axis=0, keepdims=True)
    carry_ref[...] = carry_ref[...] + cnt
    cnt_ref[0] = jnp.broadcast_to(cnt, cnt_ref.shape[1:])
    info = jnp.where(lane == 0, i1.astype(F32),
                     jnp.where(lane == 1, i2.astype(F32),
                               jnp.where(lane == 2, rank1, jnp.where(lane == 3, rank2, 0.0))))
    info_ref[0] = info.T[0:8, :]
    w1_hi = w1.astype(BF16).astype(F32)
    w2_hi = w2.astype(BF16).astype(F32)
    lo_half = (lane & 1) == 1
    wcol = (jnp.where((lane >> 1) == i1, jnp.where(lo_half, w1 - w1_hi, w1_hi), 0.0)
            + jnp.where((lane >> 1) == i2, jnp.where(lo_half, w2 - w2_hi, w2_hi), 0.0))
    wcol_ref[...] = wcol.astype(BF16)


def _route(x, router):
    T, D = x.shape
    rows = MOE_ROWS
    r = jnp.pad(router, ((0, 0), (0, LANES - router.shape[1])))
    r_hi = r.astype(BF16)
    r_lo = (r - r_hi.astype(F32)).astype(BF16)
    r2 = jnp.stack([r_hi, r_lo])
    tri = (jnp.arange(rows)[None, :] < jnp.arange(rows)[:, None]).astype(BF16)
    return pl.pallas_call(
        _route_kernel,
        grid=(T // rows,),
        in_specs=[
            pl.BlockSpec((rows, D), lambda i: (i, 0)),
            pl.BlockSpec(r2.shape, lambda i: (0, 0, 0)),
            pl.BlockSpec(tri.shape, lambda i: (0, 0)),
        ],
        out_specs=[
            pl.BlockSpec((rows, D), lambda i: (i, 0)),
            pl.BlockSpec((1, 8, rows), lambda i: (i, 0, 0)),
            pl.BlockSpec((rows, LANES), lambda i: (i, 0)),
            pl.BlockSpec((1, 8, LANES), lambda i: (i, 0, 0)),
        ],
        out_shape=[
            jax.ShapeDtypeStruct((T, D), BF16),
            jax.ShapeDtypeStruct((T // rows, 8, rows), F32),
            jax.ShapeDtypeStruct((T, LANES), BF16),
            jax.ShapeDtypeStruct((T // rows, 8, LANES), F32),
        ],
        scratch_shapes=[pltpu.VMEM((8, LANES), F32)],
        compiler_params=_params("arbitrary"),
    )(x, r2, tri)


def _moe_tables(cnt, n_blocks_max, n_items_max):
    R = MOE_ROWS
    C, E = cnt.shape
    i32 = jnp.int32
    count_le = lambda sorted_v, q: jnp.sum(sorted_v[None, :] <= q[:, None], axis=1).astype(i32)

    def windows(ra, rb, live):
        near = jnp.minimum((ra // MOE_ALIGN) * MOE_ALIGN, R - MOE_WIN)
        fits = rb - near < MOE_WIN
        start = jnp.where(fits, near, (ra // MOE_WIN) * MOE_WIN)
        n = jnp.where(fits, 1, rb // MOE_WIN - ra // MOE_WIN + 1)
        return jnp.where(live, start, 0), jnp.where(live, n, 0)

    cum = jnp.concatenate([jnp.zeros((1, E), i32), jnp.cumsum(cnt, axis=0)], axis=0)
    tot = cum[-1]
    nb = (tot + R - 1) // R
    nb_end = jnp.cumsum(nb)
    blk_start = nb_end - nb
    n_blocks = nb_end[-1]
    b = jnp.minimum(jnp.arange(n_blocks_max, dtype=i32), n_blocks - 1)
    bexp = jnp.minimum(count_le(nb_end, b), E - 1)
    bvalid = jnp.arange(n_blocks_max, dtype=i32) < n_blocks
    lb = b - blk_start[bexp]
    rho0 = lb * R
    rho1 = jnp.minimum((lb + 1) * R, tot[bexp]) - 1
    cum_b = cum[1:, :][:, bexp].T
    lo = jnp.minimum(jnp.sum(cum_b <= rho0[:, None], axis=1).astype(i32), C - 1)
    hi = jnp.minimum(jnp.sum(cum_b <= rho1[:, None], axis=1).astype(i32), C - 1)
    nit = jnp.where(bvalid, hi - lo + 1, 0)
    it_end = jnp.cumsum(nit)
    it_start = it_end - nit
    n_items = it_end[-1]
    i = jnp.arange(n_items_max, dtype=i32)
    ic = jnp.minimum(i, n_items - 1)
    d_blk = jnp.minimum(count_le(it_end, ic), n_blocks_max - 1)
    d_chk = lo[d_blk] + ic - it_start[d_blk]
    d_valid = i < n_items
    d_first = d_valid & (ic == it_start[d_blk])
    d_last = d_valid & (ic == it_end[d_blk] - 1)
    d_exp = bexp[d_blk]
    d_ra = jnp.maximum(rho0[d_blk], cum[d_chk, d_exp]) - rho0[d_blk]
    d_rb = jnp.minimum(rho1[d_blk], cum[d_chk + 1, d_exp] - 1) - rho0[d_blk]
    d_start, d_nwin = windows(d_ra, d_rb, d_valid & (d_rb >= d_ra))
    is_fill = (i >= n_items) & (i < n_items + (n_blocks_max - n_blocks))
    d_blk = jnp.where(i < n_items, d_blk, jnp.minimum(n_blocks + i - n_items, n_blocks_max - 1))
    d_first = d_first | is_fill
    d_last = d_last | is_fill
    bl_lo = (blk_start[None, :] + cum[:-1] // R).reshape(-1)
    bl_hi = (blk_start[None, :] + (cum[1:] - 1) // R).reshape(-1)
    npair = jnp.where(cnt.reshape(-1) > 0, bl_hi - bl_lo + 1, 0)
    p_end = jnp.cumsum(npair)
    p_start = p_end - npair
    n_items2 = p_end[-1]
    jc = jnp.minimum(i, n_items2 - 1)
    pair = jnp.minimum(count_le(p_end, jc), C * E - 1)
    c_blk = bl_lo[pair] + jc - p_start[pair]
    c_chk = pair // E
    c_valid = i < n_items2
    prev_chk = jnp.concatenate([jnp.full((1,), -1, i32), c_chk[:-1]])
    next_chk = jnp.concatenate([c_chk[1:], jnp.full((1,), -1, i32)])
    next_valid = jnp.concatenate([c_valid[1:], jnp.zeros((1,), bool)])
    c_first = c_valid & (c_chk != prev_chk)
    c_last = c_valid & ((c_chk != next_chk) | ~next_valid)
    c_exp = pair % E
    c_base = (c_blk - blk_start[c_exp]) * R
    c_ra = jnp.maximum(cum[c_chk, c_exp] - c_base, 0)
    c_rb = jnp.minimum(cum[c_chk + 1, c_exp] - 1 - c_base, R - 1)
    c_start, c_nwin = windows(c_ra, c_rb, c_valid)
    as_i32 = lambda v: v.astype(i32)
    return dict(base=blk_start * R, bexp=bexp, bvalid=as_i32(bvalid),
                d_blk=d_blk, d_chk=d_chk, d_first=as_i32(d_first), d_last=as_i32(d_last), d_start=d_start, d_nwin=d_nwin,
                c_blk=c_blk, c_chk=c_chk, c_first=as_i32(c_first), c_last=as_i32(c_last),
                c_start=c_start, c_nwin=c_nwin)


def _dispatch_kernel(blk_ref, chk_ref, first_ref, last_ref, start_ref, nwin_ref, xb_ref, pos_ref, wcol_ref,
                     xs_ref, ws_ref, acc_ref, wacc_ref):
    i = pl.program_id(0)
    R = xs_ref.shape[0]

    @pl.when(first_ref[i] == 1)
    def _():
        acc_ref[...] = jnp.zeros_like(acc_ref)
        wacc_ref[...] = jnp.zeros_like(wacc_ref)

    pos1 = pos_ref[0, 0:1, :]
    pos2 = pos_ref[0, 1:2, :]

    def window(k, carry):
        r0 = pl.multiple_of(start_ref[i] + k * MOE_WIN, MOE_ALIGN)
        rows = (blk_ref[i] * R + r0 + lax.broadcasted_iota(jnp.int32, (MOE_WIN, 1), 0)).astype(F32)
        hit = jnp.where((pos1 - rows) * (pos2 - rows) == 0.0, 1.0, 0.0).astype(BF16)
        acc_ref[pl.ds(r0, MOE_WIN), :] += _dot(hit, xb_ref[...])
        wacc_ref[pl.ds(r0, MOE_WIN), :] += _dot(hit, wcol_ref[...])
        return carry

    lax.fori_loop(0, nwin_ref[i], window, 0)

    @pl.when(last_ref[i] == 1)
    def _():
        xs_ref[...] = acc_ref[...].astype(xs_ref.dtype)
        ws_ref[...] = wacc_ref[...]


def _dispatch(xb, posrow, wcol, tab, n_blocks_max, n_items_max):
    T, D = xb.shape
    R = MOE_ROWS
    chunk_map = lambda i, blk, chk, first, last, start, nwin: (chk[i], 0)
    block_map = lambda i, blk, chk, first, last, start, nwin: (blk[i], 0)
    grid_spec = pltpu.PrefetchScalarGridSpec(
        num_scalar_prefetch=6,
        grid=(n_items_max,),
        in_specs=[
            pl.BlockSpec((R, D), chunk_map),
            pl.BlockSpec((1, 8, R), lambda i, blk, chk, first, last, start, nwin: (chk[i], 0, 0)),
            pl.BlockSpec((R, LANES), chunk_map),
        ],
        out_specs=[pl.BlockSpec((R, D), block_map), pl.BlockSpec((R, LANES), block_map)],
        scratch_shapes=[pltpu.VMEM((R, D), F32), pltpu.VMEM((R, LANES), F32)],
    )
    return pl.pallas_call(
        _dispatch_kernel,
        grid_spec=grid_spec,
        out_shape=[
            jax.ShapeDtypeStruct((n_blocks_max * R, D), BF16),
            jax.ShapeDtypeStruct((n_blocks_max * R, LANES), F32),
        ],
        compiler_params=_params("arbitrary"),
    )(tab["d_blk"], tab["d_chk"], tab["d_first"], tab["d_last"], tab["d_start"], tab["d_nwin"], xb, posrow, wcol)


def _expert_ffn_kernel(bexp_ref, bvalid_ref, xs_ref, ws_ref, wg_ref, wu_ref, wd_ref, ys_ref, acc_ref):
    b = pl.program_id(0)
    f = pl.program_id(1)

    @pl.when(f == 0)
    def _():
        acc_ref[...] = jnp.zeros_like(acc_ref)

    @pl.when(bvalid_ref[b] == 1)
    def _():
        xb = xs_ref[...]
        a = _silu(_dot(xb, wg_ref[0])) * _dot(xb, wu_ref[0])
        acc_ref[...] += _dot(a.astype(BF16), wd_ref[0])

    @pl.when(f == pl.num_programs(1) - 1)
    def _():
        ws = ws_ref[...]
        lane = lax.broadcasted_iota(jnp.int32, ws.shape, 1)
        w = jnp.sum(jnp.where((lane >> 1) == bexp_ref[b], ws, 0.0), axis=-1, keepdims=True)
        ys_ref[...] = (acc_ref[...] * w).astype(ys_ref.dtype)


def _expert_ffn(xs, ws, w_gu, w_down, tab, n_blocks_max):
    D = xs.shape[1]
    R = MOE_ROWS
    d_ff = w_down.shape[1]
    fc = _ff_chunk(d_ff, 1792)
    nf = d_ff // fc
    f_eff = lambda b, f, bvalid: f * bvalid[b] + (nf - 1) * (1 - bvalid[b])
    grid_spec = pltpu.PrefetchScalarGridSpec(
        num_scalar_prefetch=2,
        grid=(n_blocks_max, nf),
        in_specs=[
            pl.BlockSpec((R, D), lambda b, f, bexp, bvalid: (b, 0)),
            pl.BlockSpec((R, LANES), lambda b, f, bexp, bvalid: (b, 0)),
            pl.BlockSpec((1, D, fc), lambda b, f, bexp, bvalid: (bexp[b], 0, f_eff(b, f, bvalid))),
            pl.BlockSpec((1, D, fc), lambda b, f, bexp, bvalid: (bexp[b], 0, nf + f_eff(b, f, bvalid))),
            pl.BlockSpec((1, fc, D), lambda b, f, bexp, bvalid: (bexp[b], f_eff(b, f, bvalid), 0)),
        ],
        out_specs=pl.BlockSpec((R, D), lambda b, f, bexp, bvalid: (b, 0)),
        scratch_shapes=[pltpu.VMEM((R, D), F32)],
    )
    return pl.pallas_call(
        _expert_ffn_kernel,
        grid_spec=grid_spec,
        out_shape=jax.ShapeDtypeStruct(xs.shape, BF16),
        compiler_params=_params("parallel", "arbitrary"),
    )(tab["bexp"], tab["bvalid"], xs, ws, w_gu, w_gu, w_down)


def _combine_ln_kernel(chk_ref, blk_ref, first_ref, last_ref, start_ref, nwin_ref,
                       ys_ref, pc1_ref, pc2_ref, x_ref, g_ref, b_ref, o_ref, acc_ref):
    i = pl.program_id(0)
    R = ys_ref.shape[0]

    @pl.when(first_ref[i] == 1)
    def _():
        acc_ref[...] = jnp.zeros_like(acc_ref)

    def window(k, carry):
        r0 = pl.multiple_of(start_ref[i] + k * MOE_WIN, MOE_ALIGN)
        pc1 = pc1_ref[...]
        pc2 = pc2_ref[...]
        lane = lax.broadcasted_iota(jnp.int32, pc1.shape, 1)
        parts = []
        for t in range(MOE_WIN // LANES):
            col = (blk_ref[i] * R + r0 + t * LANES + lane).astype(F32)
            parts.append(jnp.where((pc1 - col) * (pc2 - col) == 0.0, 1.0, 0.0).astype(BF16))
        acc_ref[...] += _dot(jnp.concatenate(parts, axis=1), ys_ref[pl.ds(r0, MOE_WIN), :])
        return carry

    lax.fori_loop(0, nwin_ref[i], window, 0)

    @pl.when(last_ref[i] == 1)
    def _():
        o_ref[...] = _layer_norm(ALPHA * x_ref[...] + acc_ref[...], g_ref[...], b_ref[...])


def _combine_ln(ys, poscol1, poscol2, x, g, b, tab, n_items_max):
    T, D = x.shape
    R = MOE_ROWS
    chunk_map = lambda i, chk, blk, first, last, start, nwin: (chk[i], 0)
    const_map = lambda i, chk, blk, first, last, start, nwin: (0, 0)
    grid_spec = pltpu.PrefetchScalarGridSpec(
        num_scalar_prefetch=6,
        grid=(n_items_max,),
        in_specs=[
            pl.BlockSpec((R, D), lambda i, chk, blk, first, last, start, nwin: (blk[i], 0)),
            pl.BlockSpec((R, LANES), chunk_map),
            pl.BlockSpec((R, LANES), chunk_map),
            pl.BlockSpec((R, D), chunk_map),
            pl.BlockSpec((1, D), const_map),
            pl.BlockSpec((1, D), const_map),
        ],
        out_specs=pl.BlockSpec((R, D), chunk_map),
        scratch_shapes=[pltpu.VMEM((R, D), F32)],
    )
    return pl.pallas_call(
        _combine_ln_kernel,
        grid_spec=grid_spec,
        out_shape=jax.ShapeDtypeStruct((T, D), F32),
        compiler_params=_params("arbitrary"),
    )(tab["c_chk"], tab["c_blk"], tab["c_first"], tab["c_last"], tab["c_start"], tab["c_nwin"],
      ys, poscol1, poscol2, x, g.reshape(1, D), b.reshape(1, D))


def _moe_ln(x, router, w_gu, w_down, g, b):
    T, D = x.shape
    R = MOE_ROWS
    n_exp = w_down.shape[0]
    n_chunks = T // R
    n_blocks_max = (2 * T) // R + n_exp
    n_items_max = n_blocks_max + n_exp * (n_chunks - 1)
    xb, info, wcol, cnt = _route(x, router)
    tab = _moe_tables(cnt[:, 0, :n_exp].astype(jnp.int32), n_blocks_max, n_items_max)
    field = lambda k: info[:, k, :].reshape(T).astype(jnp.int32)
    pos1 = tab["base"][field(0)] + field(2)
    pos2 = tab["base"][field(1)] + field(3)
    pos1 = pos1.astype(F32)
    pos2 = pos2.astype(F32)
    posrow = jnp.stack([pos1.reshape(n_chunks, R), pos2.reshape(n_chunks, R)], axis=1)
    posrow = jnp.pad(posrow, ((0, 0), (0, 6), (0, 0)), constant_values=-1.0)
    poscol1 = jnp.broadcast_to(pos1[:, None], (T, LANES))
    poscol2 = jnp.broadcast_to(pos2[:, None], (T, LANES))
    xs, ws = _dispatch(xb, posrow, wcol, tab, n_blocks_max, n_items_max)
    ys = _expert_ffn(xs, ws, w_gu, w_down, tab, n_blocks_max)
    return _combine_ln(ys, poscol1, poscol2, x, g, b, tab, n_items_max)


def kernel(x, ln_g, ln_b, pool_w, pool_scale, nsa_w_in, nsa_pe_k, nsa_w1_k, nsa_w2_k, nsa_pe_v, nsa_w1_v,
           nsa_w2_v, nsa_w_out, ffn_w_gu, ffn_w_down, moe_router, moe_w_gu, moe_w_down):
    B, S, D = x.shape
    T = B * S
    xa = _pool_ln(x, pool_w[0], pool_scale[0], ln_g[0, 0], ln_b[0, 0]).reshape(T, D)
    x1 = _ffn_ln(xa, ffn_w_gu[0].astype(BF16), ffn_w_down[0].astype(BF16), ln_g[0, 1], ln_b[0, 1])
    qt, kcv, ks, vst, kw, vwt, gatet = _in_proj(x1, nsa_w_in[0], B, S)
    kc, vct = _compress(kcv, nsa_w1_k[0], nsa_w2_k[0], nsa_pe_k[0], nsa_w1_v[0], nsa_w2_v[0], nsa_pe_v[0], B, S)
    oct, selt = _cmp_attn(qt, kc, vct, B, S)
    o = _sel_win_attn(qt, selt, ks, vst, kw, vwt, oct, gatet, B, S).reshape(T, D)
    x2 = _out_proj_ln(o, nsa_w_out[0], x1, ln_g[1, 0], ln_b[1, 0])
    y = _moe_ln(x2, moe_router[0], moe_w_gu[0].astype(BF16), moe_w_down[0].astype(BF16), ln_g[1, 1], ln_b[1, 1])
    return y.reshape(B, S, D)
```

```python
import functools

import jax
import jax.numpy as jnp
from jax import lax
from jax.experimental import pallas as pl
from jax.experimental.pallas import tpu as pltpu

D_MODEL = 1024
DEPTH = 2
POOL_WINDOWS = (2, 4, 8, 16)
POOL_GROUP_DIM = D_MODEL // len(POOL_WINDOWS)
POOL_HALO = 16
N_HEADS = 16
N_KV_GROUPS = 4
HEADS_PER_GROUP = N_HEADS // N_KV_GROUPS
HEAD_DIM = D_MODEL // N_HEADS
HALF_DIM = HEAD_DIM // 2
GROUP_Q_DIM = HEADS_PER_GROUP * HEAD_DIM
KV_DIM = N_KV_GROUPS * HEAD_DIM
N_BRANCHES = 3
N_GATES = N_BRANCHES * N_HEADS
CMP_STRIDE = 16
CMP_BLOCK = 2 * CMP_STRIDE
CMP_HIDDEN = 2 * HEAD_DIM
SEL_BLOCK = 64
SEL_SHIFT = SEL_BLOCK.bit_length() - 1
SEL_TOP_N = 16
WINDOW = 512
FORCE_BONUS = 1.0e3
NEG_INF = -1.0e30
ROPE_THETA = 10000.0
ATTN_SCALE = HEAD_DIM ** -0.5
LOG2E = 1.4426950408889634
Q_SCALE = ATTN_SCALE * LOG2E
AUG_DIM = 2 * HEAD_DIM
SEL_SLOTS = AUG_DIM - HEAD_DIM
VT_ROWS = HEAD_DIM + 16
GATE_SLOTS = 16
GATE_ROWS = N_KV_GROUPS * GATE_SLOTS
N_EXPERTS = 8
LN_EPS = 1e-5
ALPHA = (2 * DEPTH) ** 0.25

LANES = 128
VMEM_LIMIT_BYTES = 56 * 1024 * 1024

POOL_ROWS = 512
FFN_ROWS = 512
PROJ_ROWS = 512
PROJ_COLS = 256
CMP_Q_ROWS = 512
ATT_Q_ROWS = 512
ATT_K_ROWS = 512
ATT_HALF = 128
ATT_COLS = 512
ATT_AHEAD = 3
MOE_ROWS = 512
MOE_WIN = 256
MOE_ALIGN = 16

F32 = jnp.float32
BF16 = jnp.bfloat16


def _dot(a, b):
    return jnp.dot(a, b, preferred_element_type=F32)


def _dot_nt(a, b):
    return lax.dot_general(a, b, (((1,), (1,)), ((), ())), preferred_element_type=F32)


def _layer_norm(z, g, b):
    mu = jnp.mean(z, axis=-1, keepdims=True)
    zc = z - mu
    var = jnp.mean(zc * zc, axis=-1, keepdims=True)
    return zc * lax.rsqrt(var + LN_EPS) * g + b


def _silu(x):
    return x / (1.0 + jnp.exp(-x))


def _params(*semantics):
    return pltpu.CompilerParams(dimension_semantics=semantics, vmem_limit_bytes=VMEM_LIMIT_BYTES)


def _pool_ln_kernel(x_ref, halo_ref, w_ref, scale_ref, g_ref, b_ref, o_ref, ext_ref):
    i = pl.program_id(1)
    rows = x_ref.shape[1]
    x = x_ref[0]
    ext_ref[0:POOL_HALO, :] = jnp.where(i > 0, halo_ref[0], 0.0)
    ext_ref[POOL_HALO:, :] = x
    pos = i * rows + lax.broadcasted_iota(jnp.int32, (rows, 1), 0)
    ys = []
    run = ext_ref[...]
    span = 1
    for gi, w in enumerate(POOL_WINDOWS):
        while span < w:
            run = run + pltpu.roll(run, span, 0)
            span *= 2
        xg = x[:, gi * POOL_GROUP_DIM:(gi + 1) * POOL_GROUP_DIM]
        cnt = jnp.minimum(pos + 1, w).astype(F32)
        diff = run[POOL_HALO:, 0:POOL_GROUP_DIM] / cnt - xg
        ys.append(_dot(diff.astype(BF16), w_ref[gi]))
        run = run[:, POOL_GROUP_DIM:]
    h = jnp.concatenate(ys, axis=1) * scale_ref[...]
    o_ref[0] = _layer_norm(ALPHA * x + h, g_ref[...], b_ref[...])


def _pool_ln(x, w, scale, g, b):
    B, S, D = x.shape
    rows = POOL_ROWS
    assert all(a < b_ for a, b_ in zip(POOL_WINDOWS, POOL_WINDOWS[1:]))
    assert all(w_ & (w_ - 1) == 0 for w_ in POOL_WINDOWS) and POOL_WINDOWS[-1] <= POOL_HALO
    halo_blocks = rows // POOL_HALO
    row2 = lambda v: v.reshape(1, D)
    return pl.pallas_call(
        _pool_ln_kernel,
        grid=(B, S // rows),
        in_specs=[
            pl.BlockSpec((1, rows, D), lambda bi, i: (bi, i, 0)),
            pl.BlockSpec((1, POOL_HALO, D), lambda bi, i: (bi, jnp.maximum(i * halo_blocks - 1, 0), 0)),
            pl.BlockSpec(w.shape, lambda bi, i: (0, 0, 0)),
            pl.BlockSpec((1, D), lambda bi, i: (0, 0)),
            pl.BlockSpec((1, D), lambda bi, i: (0, 0)),
            pl.BlockSpec((1, D), lambda bi, i: (0, 0)),
        ],
        out_specs=pl.BlockSpec((1, rows, D), lambda bi, i: (bi, i, 0)),
        out_shape=jax.ShapeDtypeStruct((B, S, D), F32),
        scratch_shapes=[pltpu.VMEM((rows + POOL_HALO, D), F32)],
        compiler_params=_params("parallel", "arbitrary"),
    )(x, x, w.astype(BF16), row2(scale), row2(g), row2(b))


def _ffn_ln_kernel(x_ref, wg_ref, wu_ref, wd_ref, g_ref, b_ref, o_ref, xb_ref, acc_ref):
    f = pl.program_id(1)

    @pl.when(f == 0)
    def _():
        xb_ref[...] = x_ref[...].astype(BF16)
        acc_ref[...] = jnp.zeros_like(acc_ref)

    xb = xb_ref[...]
    a = _silu(_dot(xb, wg_ref[...])) * _dot(xb, wu_ref[...])
    acc_ref[...] += _dot(a.astype(BF16), wd_ref[...])

    @pl.when(f == pl.num_programs(1) - 1)
    def _():
        o_ref[...] = _layer_norm(ALPHA * x_ref[...] + acc_ref[...], g_ref[...], b_ref[...])


def _ff_chunk(d_ff, target):
    best = LANES
    for c in range(LANES, target + 1, LANES):
        if d_ff % c == 0:
            best = c
    return best


def _ffn_ln(x, w_gu, w_down, g, b):
    T, D = x.shape
    d_ff = w_down.shape[0]
    fc = _ff_chunk(d_ff, 1536)
    nf = d_ff // fc
    rows = FFN_ROWS
    return pl.pallas_call(
        _ffn_ln_kernel,
        grid=(T // rows, nf),
        in_specs=[
            pl.BlockSpec((rows, D), lambda i, f: (i, 0)),
            pl.BlockSpec((D, fc), lambda i, f: (0, f)),
            pl.BlockSpec((D, fc), lambda i, f: (0, nf + f)),
            pl.BlockSpec((fc, D), lambda i, f: (f, 0)),
            pl.BlockSpec((1, D), lambda i, f: (0, 0)),
            pl.BlockSpec((1, D), lambda i, f: (0, 0)),
        ],
        out_specs=pl.BlockSpec((rows, D), lambda i, f: (i, 0)),
        out_shape=jax.ShapeDtypeStruct((T, D), F32),
        scratch_shapes=[pltpu.VMEM((rows, D), BF16), pltpu.VMEM((rows, D), F32)],
        compiler_params=_params("parallel", "arbitrary"),
    )(x, w_gu, w_gu, w_down, g.reshape(1, D), b.reshape(1, D))


def _rope_tables(pos, reps):
    freqs = jnp.power(ROPE_THETA, -jnp.arange(HALF_DIM, dtype=F32) / HALF_DIM)
    ang = pos.astype(F32)[:, None] * freqs[None, :]
    cos, sin = jnp.cos(ang), jnp.sin(ang)
    return (jnp.tile(jnp.concatenate([cos, cos], axis=1), (1, reps)),
            jnp.tile(jnp.concatenate([-sin, sin], axis=1), (1, reps)))


def _in_proj_kernel(x_ref, w_ref, wt_ref, cos_ref, sin_ref, cost_ref, sint_ref,
                    qt_ref, kcv_ref, ks_ref, vst_ref, kw_ref, vwt_ref, gatet_ref, *, steps_per_seq):
    rows = x_ref.shape[0]
    xb = x_ref[...].astype(BF16)
    cos = cos_ref[...]
    sin = sin_ref[...]
    lane = lax.broadcasted_iota(jnp.int32, cos.shape, 1)
    first_half = (lane & (HEAD_DIM - 1)) < HALF_DIM
    seq_step = pl.program_id(0) % steps_per_seq
    pos = seq_step * rows + lax.broadcasted_iota(jnp.int32, (rows, HEAD_DIM), 0)
    col = lax.broadcasted_iota(jnp.int32, (rows, HEAD_DIM), 1)
    blk_onehot = jnp.where((pos >> SEL_SHIFT) == col, 1.0, 0.0)
    zeros = jnp.zeros((rows, HEAD_DIM), F32)

    def col_tile(j):
        return _dot(xb, w_ref[:, j * PROJ_COLS:(j + 1) * PROJ_COLS])

    def row_tile(r0, n):
        return _dot_nt(wt_ref[r0:r0 + n, :], xb)

    def rope(y):
        rot = jnp.where(first_half,
                        pltpu.roll(y, PROJ_COLS - HALF_DIM, 1),
                        pltpu.roll(y, HALF_DIM, 1))
        return y * cos + rot * sin

    def rope_t(yt):
        pieces = []
        for h in range(yt.shape[0] // HEAD_DIM):
            pieces.append(yt[h * HEAD_DIM + HALF_DIM:(h + 1) * HEAD_DIM])
            pieces.append(yt[h * HEAD_DIM:h * HEAD_DIM + HALF_DIM])
        return yt * cost_ref[...] + jnp.concatenate(pieces, axis=0) * sint_ref[...]

    def store_keys(ref, y, extra):
        for gi in range(N_KV_GROUPS):
            ref[0, gi, :, 0:HEAD_DIM] = y[:, gi * HEAD_DIM:(gi + 1) * HEAD_DIM].astype(ref.dtype)
            ref[0, gi, :, HEAD_DIM:] = extra.astype(ref.dtype)

    def store_values_t(ref, yt):
        ones = jnp.ones((VT_ROWS - HEAD_DIM, rows), ref.dtype)
        for gi in range(N_KV_GROUPS):
            ref[0, gi, 0, 0:HEAD_DIM, :] = yt[gi * HEAD_DIM:(gi + 1) * HEAD_DIM].astype(ref.dtype)
            ref[0, gi, 0, HEAD_DIM:, :] = ones

    for j in range(D_MODEL // PROJ_COLS):
        qt = rope_t(row_tile(j * PROJ_COLS, PROJ_COLS)) * Q_SCALE
        qt_ref[0, j * PROJ_COLS:(j + 1) * PROJ_COLS, :] = qt.astype(qt_ref.dtype)
    raw_k = col_tile(0)
    raw_v = col_tile(1)
    for gi in range(N_KV_GROUPS):
        kcv_ref[0, gi, :, 0:HEAD_DIM] = raw_k[:, gi * HEAD_DIM:(gi + 1) * HEAD_DIM]
        kcv_ref[0, gi, :, HEAD_DIM:] = raw_v[:, gi * HEAD_DIM:(gi + 1) * HEAD_DIM]
    store_keys(ks_ref, rope(col_tile(2)), blk_onehot)
    store_keys(kw_ref, rope(col_tile(3)), zeros)
    store_values_t(vst_ref, row_tile(D_MODEL, KV_DIM))
    store_values_t(vwt_ref, row_tile(D_MODEL + KV_DIM, KV_DIM))
    logits_t = row_tile(D_MODEL + 2 * KV_DIM, GATE_ROWS)
    gatet_ref[0] = 1.0 / (1.0 + jnp.exp(-logits_t))


def _in_proj(x, w_in, B, S):
    T, D = x.shape
    rows = PROJ_ROWS
    assert rows == ATT_K_ROWS
    steps_per_seq = S // rows
    sec = lambda k: w_in[:, D_MODEL + k * KV_DIM:D_MODEL + (k + 1) * KV_DIM]
    w = jnp.concatenate([sec(0), sec(1), sec(2), sec(4)], axis=1).astype(BF16)
    wg = w_in[:, D_MODEL + 6 * KV_DIM:].reshape(D, N_KV_GROUPS, N_BRANCHES * HEADS_PER_GROUP)
    wg = jnp.pad(wg, ((0, 0), (0, 0), (0, GATE_SLOTS - N_BRANCHES * HEADS_PER_GROUP))).reshape(D, GATE_ROWS)
    wt = jnp.concatenate([w_in[:, :D_MODEL], sec(3), sec(5), wg], axis=1).T.astype(BF16)
    cos, sin = _rope_tables(jnp.arange(S), PROJ_COLS // HEAD_DIM)
    k_shape = jax.ShapeDtypeStruct((B, N_KV_GROUPS, S, AUG_DIM), BF16)
    k_spec = pl.BlockSpec((1, N_KV_GROUPS, rows, AUG_DIM),
                          lambda i: (i // steps_per_seq, 0, i % steps_per_seq, 0))
    vt_shape = jax.ShapeDtypeStruct((B, N_KV_GROUPS, steps_per_seq, VT_ROWS, rows), BF16)
    vt_spec = pl.BlockSpec((1, N_KV_GROUPS, 1, VT_ROWS, rows),
                           lambda i: (i // steps_per_seq, 0, i % steps_per_seq, 0, 0))
    tok_map = lambda i: (i, 0)
    seq_map = lambda i: (i % steps_per_seq, 0)
    feat_map = lambda i: (i // steps_per_seq, 0, i % steps_per_seq)
    return pl.pallas_call(
        functools.partial(_in_proj_kernel, steps_per_seq=steps_per_seq),
        grid=(T // rows,),
        in_specs=[
            pl.BlockSpec((rows, D), tok_map),
            pl.BlockSpec(w.shape, lambda i: (0, 0)),
            pl.BlockSpec(wt.shape, lambda i: (0, 0)),
            pl.BlockSpec((rows, PROJ_COLS), seq_map),
            pl.BlockSpec((rows, PROJ_COLS), seq_map),
            pl.BlockSpec((PROJ_COLS, rows), lambda i: (0, i % steps_per_seq)),
            pl.BlockSpec((PROJ_COLS, rows), lambda i: (0, i % steps_per_seq)),
        ],
        out_specs=[
            pl.BlockSpec((1, D_MODEL, rows), feat_map),
            k_spec,
            k_spec, vt_spec, k_spec, vt_spec,
            pl.BlockSpec((1, GATE_ROWS, rows), feat_map),
        ],
        out_shape=[
            jax.ShapeDtypeStruct((B, D_MODEL, S), BF16),
            jax.ShapeDtypeStruct((B, N_KV_GROUPS, S, AUG_DIM), F32),
            k_shape, vt_shape, k_shape, vt_shape,
            jax.ShapeDtypeStruct((B, GATE_ROWS, S), F32),
        ],
        compiler_params=_params("parallel"),
    )(x, w, wt, cos, sin, cos.T, sin.T)


def _compress_kernel(kv_ref, w1_ref, pea_ref, peb_ref, w2_ref, w2t_ref, cos_ref, sin_ref, kc_ref, vct_ref):
    n = kc_ref.shape[2]
    hid = CMP_HIDDEN
    r = jnp.zeros((n, 4 * hid), F32)
    bias_a = jnp.zeros((8, 4 * hid), F32)
    bias_b = jnp.zeros((8, 4 * hid), F32)
    for p in range(CMP_STRIDE):
        slab = kv_ref[0, 0, pl.ds(p, n, stride=CMP_STRIDE), :].astype(BF16)
        r = r + _dot(slab, w1_ref[p])
        bias_a = bias_a + _dot(pea_ref[p], w1_ref[p])
        bias_b = bias_b + _dot(peb_ref[p], w1_ref[p])
    hidden = []
    for which in range(2):
        c0 = which * 2 * hid
        nxt = pltpu.roll(r[:, c0 + hid:c0 + 2 * hid], n - 1, 0)
        bias = bias_a[0:1, c0:c0 + hid] + bias_b[0:1, c0 + hid:c0 + 2 * hid]
        hidden.append(_silu(r[:, c0:c0 + hid] + nxt + bias).astype(BF16))
    kc = _dot(hidden[0], w2_ref[...])
    rot = jnp.concatenate([kc[:, HALF_DIM:], kc[:, :HALF_DIM]], axis=1)
    kc_ref[0, 0] = (kc * cos_ref[...] + rot * sin_ref[...]).astype(kc_ref.dtype)
    vct_ref[0, 0] = _dot_nt(w2t_ref[...], hidden[1]).astype(vct_ref.dtype)


def _compress(kv_raw, w1_k, w2_k, pe_k, w1_v, w2_v, pe_v, B, S):
    n_chunks = S // CMP_STRIDE
    hid = CMP_HIDDEN
    split = lambda w1: w1.reshape(2, CMP_STRIDE, HEAD_DIM, hid).transpose(1, 2, 0, 3).reshape(CMP_STRIDE, HEAD_DIM, 2 * hid)
    zeros = jnp.zeros((CMP_STRIDE, HEAD_DIM, 2 * hid), F32)
    w1 = jnp.concatenate([jnp.concatenate([split(w1_k), zeros], axis=2),
                          jnp.concatenate([zeros, split(w1_v)], axis=2)], axis=1).astype(BF16)
    pe = jnp.concatenate([pe_k, pe_v], axis=1)
    rows8 = lambda v: jnp.broadcast_to(v[:, None, :], (CMP_STRIDE, 8, AUG_DIM)).astype(BF16)
    pea, peb = rows8(pe[:CMP_STRIDE]), rows8(pe[CMP_STRIDE:])
    cos, sin = _rope_tables(CMP_STRIDE * jnp.arange(n_chunks) + CMP_BLOCK - 1, 1)
    const = lambda a: pl.BlockSpec(a.shape, lambda bi, gi: (0,) * a.ndim)
    w2k = w2_k.astype(BF16)
    w2vt = w2_v.T.astype(BF16)
    return pl.pallas_call(
        _compress_kernel,
        grid=(B, N_KV_GROUPS),
        in_specs=[
            pl.BlockSpec((1, 1, S, AUG_DIM), lambda bi, gi: (bi, gi, 0, 0)),
            const(w1), const(pea), const(peb), const(w2k), const(w2vt), const(cos), const(sin),
        ],
        out_specs=[pl.BlockSpec((1, 1, n_chunks, HEAD_DIM), lambda bi, gi: (bi, gi, 0, 0)),
                   pl.BlockSpec((1, 1, HEAD_DIM, n_chunks), lambda bi, gi: (bi, gi, 0, 0))],
        out_shape=[jax.ShapeDtypeStruct((B, N_KV_GROUPS, n_chunks, HEAD_DIM), BF16),
                   jax.ShapeDtypeStruct((B, N_KV_GROUPS, HEAD_DIM, n_chunks), BF16)],
        compiler_params=_params("parallel", "parallel"),
    )(kv_raw, w1, pea, peb, w2k, w2vt, cos, sin)


def _cmp_attn_kernel(qt_ref, kc_ref, vct_ref, ovl_ref, oct_ref, selt_ref, score_ref, imp_ref):
    i = pl.program_id(2)
    cols = qt_ref.shape[2]
    n_cmp = kc_ref.shape[2]
    n_sel = ovl_ref.shape[0]
    t = i * cols + lax.broadcasted_iota(jnp.int32, (1, cols), 1)
    any_valid = jnp.where(t >= CMP_BLOCK - 1, 1.0, 0.0)

    def attend(n):
        kc = kc_ref[0, 0, 0:n, :]
        vct = vct_ref[0, 0, :, 0:n]
        cend = CMP_STRIDE * lax.broadcasted_iota(jnp.int32, (n, 1), 0) + (CMP_BLOCK - 1)
        cvalid = cend <= t
        pc_sum = jnp.zeros((n, cols), F32)
        for h in range(HEADS_PER_GROUP):
            qh = qt_ref[0, h * HEAD_DIM:(h + 1) * HEAD_DIM, :]
            s = jnp.where(cvalid, _dot(kc, qh), NEG_INF)
            e = jnp.exp2(s - jnp.max(s, axis=0, keepdims=True))
            pc = e * (any_valid / jnp.sum(e, axis=0, keepdims=True))
            oct_ref[0, h * HEAD_DIM:(h + 1) * HEAD_DIM, :] = _dot(vct, pc.astype(BF16))
            pc_sum = pc_sum + pc
        imp_ref[...] = _dot(ovl_ref[:, 0:n], pc_sum.astype(BF16))

    n_half = n_cmp // 2
    early = (i + 1) * cols <= n_half * CMP_STRIDE

    @pl.when(early)
    def _():
        attend(n_half)

    @pl.when(jnp.logical_not(early))
    def _():
        attend(n_cmp)

    imp = imp_ref[...]
    j = lax.broadcasted_iota(jnp.int32, (n_sel, 1), 0)
    blk_t = t >> SEL_SHIFT
    bvalid = j <= blk_t
    forced = (j == 0) | (j == blk_t) | (j == blk_t - 1)
    score = jnp.where(bvalid, imp + jnp.where(forced, FORCE_BONUS, 0.0), -1.0)
    key = pltpu.bitcast(score, jnp.int32)
    key_next = key + 1
    score_ref[...] = key
    def rank_group(grp, rank):
        base = pl.multiple_of(grp * 8, 8)
        others = score_ref[pl.ds(base, 8), :]
        for r in range(8):
            ahead = others[r:r + 1, :] >= jnp.where(j > base + r, key, key_next)
            rank = rank + jnp.where(ahead, 1.0, 0.0)
        return rank

    n_groups = jnp.minimum(((i + 1) * cols) // (8 * SEL_BLOCK), n_sel // 8)
    rank = lax.fori_loop(0, n_groups, rank_group, jnp.zeros((n_sel, cols), F32))
    selected = (rank < float(SEL_TOP_N)) & bvalid
    selt_ref[0, 0] = jnp.where(selected, 0.0, NEG_INF).astype(selt_ref.dtype)


def _cmp_attn(qt, kc, vct, B, S):
    cols = CMP_Q_ROWS
    n_cmp = S // CMP_STRIDE
    n_sel = SEL_SLOTS
    assert S // SEL_BLOCK <= SEL_SLOTS
    cstart = CMP_STRIDE * jnp.arange(n_cmp)
    sstart = SEL_BLOCK * jnp.arange(n_sel)
    overlap = ((cstart[None, :] <= sstart[:, None] + SEL_BLOCK - 1)
               & (cstart[None, :] + CMP_BLOCK - 1 >= sstart[:, None])).astype(BF16)
    q_spec = pl.BlockSpec((1, GROUP_Q_DIM, cols), lambda bi, gi, i: (bi, gi, i))
    return pl.pallas_call(
        _cmp_attn_kernel,
        grid=(B, N_KV_GROUPS, S // cols),
        in_specs=[
            q_spec,
            pl.BlockSpec((1, 1, n_cmp, HEAD_DIM), lambda bi, gi, i: (bi, gi, 0, 0)),
            pl.BlockSpec((1, 1, HEAD_DIM, n_cmp), lambda bi, gi, i: (bi, gi, 0, 0)),
            pl.BlockSpec(overlap.shape, lambda bi, gi, i: (0, 0)),
        ],
        out_specs=[
            q_spec,
            pl.BlockSpec((1, 1, n_sel, cols), lambda bi, gi, i: (bi, gi, 0, i)),
        ],
        out_shape=[
            jax.ShapeDtypeStruct((B, D_MODEL, S), F32),
            jax.ShapeDtypeStruct((B, N_KV_GROUPS, n_sel, S), BF16),
        ],
        scratch_shapes=[pltpu.VMEM((n_sel, cols), jnp.int32), pltpu.VMEM((n_sel, cols), F32)],
        compiler_params=_params("parallel", "parallel", "arbitrary"),
    )(qt, kc, vct, overlap)


def _sel_win_attn_kernel(qt_ref, selt_ref, ks_ref, vst_ref, kw_ref, vwt_ref, diag_ref, band_ref, oct_ref, gatet_ref,
                         o_ref, qaug_ref, m_ref, acc_ref, ow_ref):
    i = pl.program_id(2)
    tq = qt_ref.shape[2]
    hg = HEADS_PER_GROUP
    half = ATT_HALF
    n_half = tq // half
    half_cols = hg * half

    for hq in range(n_half):
        for h in range(hg):
            c0 = (hq * hg + h) * half
            qaug_ref[0:HEAD_DIM, c0:c0 + half] = qt_ref[0, h * HEAD_DIM:(h + 1) * HEAD_DIM, hq * half:(hq + 1) * half]
            qaug_ref[HEAD_DIM:, c0:c0 + half] = selt_ref[0, 0, :, hq * half:(hq + 1) * half]
    n_tiles = (hg * tq) // ATT_COLS

    def keys(ref, start, size):
        return ref[0, 0, pl.ds(pl.multiple_of(start, ATT_HALF), size), :]

    m_ref[...] = jnp.full(m_ref.shape, NEG_INF, F32)
    acc_ref[...] = jnp.zeros(acc_ref.shape, F32)

    def sel_step(c, bias_ref):
        k = keys(ks_ref, c * ATT_K_ROWS, ATT_K_ROWS)
        v = vst_ref[0, 0, c]
        new_m, new_acc = [], []
        tile = lambda ct: slice(ct * ATT_COLS, (ct + 1) * ATT_COLS)
        last_part = lambda ct: ((ct + 1) * ATT_COLS - 1) // half_cols
        n_keys = lambda ct: ATT_K_ROWS if bias_ref is None else min(ATT_K_ROWS, (last_part(ct) + 1) * half)
        score = lambda ct: _dot(k[0:n_keys(ct)], qaug_ref[:, tile(ct)])
        scores = [score(ct) for ct in range(ATT_AHEAD)]
        for ct in range(n_tiles):
            cs = tile(ct)
            if ct + ATT_AHEAD < n_tiles:
                scores.append(score(ct + ATT_AHEAD))
            s = scores[ct]
            if bias_ref is not None:
                s = s + bias_ref[0:n_keys(ct), cs]
            m_prev = m_ref[:, cs]
            m_next = jnp.maximum(m_prev, jnp.max(s, axis=0, keepdims=True))
            p = jnp.exp2(s - m_next)
            alpha = jnp.exp2(m_prev - m_next)
            new_acc.append(alpha * acc_ref[:, cs] + _dot(v[:, 0:n_keys(ct)], p.astype(BF16)))
            new_m.append(m_next)
        for ct in range(n_tiles):
            cs = slice(ct * ATT_COLS, (ct + 1) * ATT_COLS)
            acc_ref[:, cs] = new_acc[ct]
            m_ref[:, cs] = new_m[ct]

    def sel_body(c, carry):
        sel_step(c, None)
        return carry

    lax.fori_loop(0, i, sel_body, 0)
    sel_step(i, diag_ref)

    def probs(s):
        return jnp.exp2(s - jnp.max(s, axis=0, keepdims=True)).astype(BF16)

    @pl.when(i == 0)
    def _():
        k = keys(kw_ref, 0, tq)
        v = vwt_ref[0, 0, 0]
        for ct in range(n_tiles):
            cs = slice(ct * ATT_COLS, (ct + 1) * ATT_COLS)
            ow_ref[:, cs] = _dot(v, probs(_dot(k, qaug_ref[:, cs]) + diag_ref[:, cs]))

    @pl.when(i > 0)
    def _():
        v_prev = vwt_ref[0, 0, i - 1]
        v_here = vwt_ref[0, 0, i]
        for hq in range(n_half):
            k = keys(kw_ref, i * tq + hq * half - WINDOW, WINDOW + half)
            n_prev = tq - hq * half
            for ct in range(half_cols // ATT_COLS):
                cs = slice(hq * half_cols + ct * ATT_COLS, hq * half_cols + (ct + 1) * ATT_COLS)
                p = probs(_dot(k, qaug_ref[:, cs]) + band_ref[:, ct * ATT_COLS:(ct + 1) * ATT_COLS])
                ow_ref[:, cs] = (_dot(v_prev[:, tq - n_prev:], p[0:n_prev])
                                 + _dot(v_here[:, 0:WINDOW + half - n_prev], p[n_prev:]))

    acc_s = acc_ref[...]
    acc_w = ow_ref[...]
    o_s = acc_s[0:HEAD_DIM] / acc_s[HEAD_DIM:HEAD_DIM + 1]
    o_w = acc_w[0:HEAD_DIM] / acc_w[HEAD_DIM:HEAD_DIM + 1]
    for hq in range(n_half):
        tok = slice(hq * half, (hq + 1) * half)
        for h in range(hg):
            c0 = (hq * hg + h) * half
            gc = gatet_ref[0, N_BRANCHES * h + 0:N_BRANCHES * h + 1, tok]
            gs = gatet_ref[0, N_BRANCHES * h + 1:N_BRANCHES * h + 2, tok]
            gw = gatet_ref[0, N_BRANCHES * h + 2:N_BRANCHES * h + 3, tok]
            o = (gc * oct_ref[0, h * HEAD_DIM:(h + 1) * HEAD_DIM, tok]
                 + gs * o_s[:, c0:c0 + half] + gw * o_w[:, c0:c0 + half])
            o_ref[0, tok, h * HEAD_DIM:(h + 1) * HEAD_DIM] = o.T.astype(o_ref.dtype)


def _sel_win_attn(qt, selt, ks, vst, kw, vwt, oct, gatet, B, S):
    tq = ATT_Q_ROWS
    half = ATT_HALF
    hg = HEADS_PER_GROUP
    cols = hg * tq
    n_chunks = S // ATT_K_ROWS
    t_rel = (jnp.arange(tq // half)[:, None, None] * half + jnp.arange(half)[None, None, :])
    t_rel = jnp.broadcast_to(t_rel, (tq // half, hg, half)).reshape(1, cols)
    diag = jnp.where(jnp.arange(ATT_K_ROWS)[:, None] <= t_rel, 0.0, NEG_INF).astype(F32)
    tt = jnp.broadcast_to(jnp.arange(half)[None, :], (hg, half)).reshape(1, hg * half)
    a = jnp.arange(WINDOW + half)[:, None]
    band = jnp.where((a > tt) & (a <= tt + WINDOW), 0.0, NEG_INF).astype(F32)
    k_spec = pl.BlockSpec((1, 1, S, AUG_DIM), lambda bi, gi, i: (bi, gi, 0, 0))
    vt_spec = pl.BlockSpec((1, 1, n_chunks, VT_ROWS, ATT_K_ROWS), lambda bi, gi, i: (bi, gi, 0, 0, 0))
    q_spec = pl.BlockSpec((1, GROUP_Q_DIM, tq), lambda bi, gi, i: (bi, gi, i))
    const = lambda arr: pl.BlockSpec(arr.shape, lambda bi, gi, i: (0, 0))
    return pl.pallas_call(
        _sel_win_attn_kernel,
        grid=(B, N_KV_GROUPS, S // tq),
        in_specs=[
            q_spec,
            pl.BlockSpec((1, 1, SEL_SLOTS, tq), lambda bi, gi, i: (bi, gi, 0, i)),
            k_spec, vt_spec, k_spec, vt_spec,
            const(diag), const(band),
            q_spec,
            pl.BlockSpec((1, GATE_SLOTS, tq), lambda bi, gi, i: (bi, gi, i)),
        ],
        out_specs=pl.BlockSpec((1, tq, GROUP_Q_DIM), lambda bi, gi, i: (bi, i, gi)),
        out_shape=jax.ShapeDtypeStruct((B, S, D_MODEL), BF16),
        scratch_shapes=[
            pltpu.VMEM((AUG_DIM, cols), BF16),
            pltpu.VMEM((1, cols), F32),
            pltpu.VMEM((VT_ROWS, cols), F32),
            pltpu.VMEM((VT_ROWS, cols), F32),
        ],
        compiler_params=_params("parallel", "parallel", "arbitrary"),
    )(qt, selt, ks, vst, kw, vwt, diag, band, oct, gatet)


def _out_proj_ln_kernel(o_ref, w_ref, x_ref, g_ref, b_ref, y_ref):
    h = _dot(o_ref[...], w_ref[...])
    y_ref[...] = _layer_norm(ALPHA * x_ref[...] + h, g_ref[...], b_ref[...])


def _out_proj_ln(o, w_out, x, g, b):
    T, D = x.shape
    rows = PROJ_ROWS
    return pl.pallas_call(
        _out_proj_ln_kernel,
        grid=(T // rows,),
        in_specs=[
            pl.BlockSpec((rows, D), lambda i: (i, 0)),
            pl.BlockSpec((D, D), lambda i: (0, 0)),
            pl.BlockSpec((rows, D), lambda i: (i, 0)),
            pl.BlockSpec((1, D), lambda i: (0, 0)),
            pl.BlockSpec((1, D), lambda i: (0, 0)),
        ],
        out_specs=pl.BlockSpec((rows, D), lambda i: (i, 0)),
        out_shape=jax.ShapeDtypeStruct((T, D), F32),
        compiler_params=_params("parallel"),
    )(o, w_out.astype(BF16), x, g.reshape(1, D), b.reshape(1, D))


def _route_kernel(x_ref, r_ref, tri_ref, xb_ref, info_ref, wcol_ref, cnt_ref, carry_ref):
    @pl.when(pl.program_id(0) == 0)
    def _():
        carry_ref[...] = jnp.zeros_like(carry_ref)

    x = x_ref[...]
    xb = x.astype(BF16)
    xb_ref[...] = xb
    x_lo = (x - xb.astype(F32)).astype(BF16)
    r_hi = r_ref[0]
    r_lo = r_ref[1]
    logits = _dot(xb, r_hi) + (_dot(x_lo, r_hi) + _dot(xb, r_lo))
    lane = lax.broadcasted_iota(jnp.int32, logits.shape, 1)
    logits = jnp.where(lane < N_EXPERTS, logits, -jnp.inf)
    v1 = jnp.max(logits, axis=-1, keepdims=True)
    i1 = jnp.min(jnp.where(logits == v1, lane, LANES), axis=-1, keepdims=True)
    rest = jnp.where(lane == i1, -jnp.inf, logits)
    v2 = jnp.max(rest, axis=-1, keepdims=True)
    i2 = jnp.min(jnp.where(rest == v2, lane, LANES), axis=-1, keepdims=True)
    e2 = jnp.exp(v2 - v1)
    w1 = 1.0 / (1.0 + e2)
    w2 = e2 / (1.0 + e2)
    m1 = jnp.where(lane == i1, 1.0, 0.0)
    m2 = jnp.where(lane == i2, 1.0, 0.0)
    routed = m1 + m2
    before = _dot(tri_ref[...], routed.astype(BF16)) + carry_ref[0:1, :]
    rank1 = jnp.sum(m1 * before, axis=-1, keepdims=True)
    rank2 = jnp.sum(m2 * before, axis=-1, keepdims=True)
    cnt = jnp.sum(routed, axis=0, keepdims=True)
    carry_ref[...] = carry_ref[...] + cnt
    cnt_ref[0] = jnp.broadcast_to(cnt, cnt_ref.shape[1:])
    info = jnp.where(lane == 0, i1.astype(F32),
                     jnp.where(lane == 1, i2.astype(F32),
                               jnp.where(lane == 2, rank1, jnp.where(lane == 3, rank2, 0.0))))
    info_ref[0] = info.T[0:8, :]
    w1_hi = w1.astype(BF16).astype(F32)
    w2_hi = w2.astype(BF16).astype(F32)
    lo_half = (lane & 1) == 1
    wcol = (jnp.where((lane >> 1) == i1, jnp.where(lo_half, w1 - w1_hi, w1_hi), 0.0)
            + jnp.where((lane >> 1) == i2, jnp.where(lo_half, w2 - w2_hi, w2_hi), 0.0))
    wcol_ref[...] = wcol.astype(BF16)


def _route(x, router):
    T, D = x.shape
    rows = MOE_ROWS
    r = jnp.pad(router, ((0, 0), (0, LANES - router.shape[1])))
    r_hi = r.astype(BF16)
    r_lo = (r - r_hi.astype(F32)).astype(BF16)
    r2 = jnp.stack([r_hi, r_lo])
    tri = (jnp.arange(rows)[None, :] < jnp.arange(rows)[:, None]).astype(BF16)
    return pl.pallas_call(
        _route_kernel,
        grid=(T // rows,),
        in_specs=[
            pl.BlockSpec((rows, D), lambda i: (i, 0)),
            pl.BlockSpec(r2.shape, lambda i: (0, 0, 0)),
            pl.BlockSpec(tri.shape, lambda i: (0, 0)),
        ],
        out_specs=[
            pl.BlockSpec((rows, D), lambda i: (i, 0)),
            pl.BlockSpec((1, 8, rows), lambda i: (i, 0, 0)),
            pl.BlockSpec((rows, LANES), lambda i: (i, 0)),
            pl.BlockSpec((1, 8, LANES), lambda i: (i, 0, 0)),
        ],
        out_shape=[
            jax.ShapeDtypeStruct((T, D), BF16),
            jax.ShapeDtypeStruct((T // rows, 8, rows), F32),
            jax.ShapeDtypeStruct((T, LANES), BF16),
            jax.ShapeDtypeStruct((T // rows, 8, LANES), F32),
        ],
        scratch_shapes=[pltpu.VMEM((8, LANES), F32)],
        compiler_params=_params("arbitrary"),
    )(x, r2, tri)


def _moe_tables(cnt, n_blocks_max, n_items_max):
    R = MOE_ROWS
    C, E = cnt.shape
    i32 = jnp.int32
    count_le = lambda sorted_v, q: jnp.sum(sorted_v[None, :] <= q[:, None], axis=1).astype(i32)

    def windows(ra, rb, live):
        near = jnp.minimum((ra // MOE_ALIGN) * MOE_ALIGN, R - MOE_WIN)
        fits = rb - near < MOE_WIN
        start = jnp.where(fits, near, (ra // MOE_WIN) * MOE_WIN)
        n = jnp.where(fits, 1, rb // MOE_WIN - ra // MOE_WIN + 1)
        return jnp.where(live, start, 0), jnp.where(live, n, 0)

    cum = jnp.concatenate([jnp.zeros((1, E), i32), jnp.cumsum(cnt, axis=0)], axis=0)
    tot = cum[-1]
    nb = (tot + R - 1) // R
    nb_end = jnp.cumsum(nb)
    blk_start = nb_end - nb
    n_blocks = nb_end[-1]
    b = jnp.minimum(jnp.arange(n_blocks_max, dtype=i32), n_blocks - 1)
    bexp = jnp.minimum(count_le(nb_end, b), E - 1)
    bvalid = jnp.arange(n_blocks_max, dtype=i32) < n_blocks
    lb = b - blk_start[bexp]
    rho0 = lb * R
    rho1 = jnp.minimum((lb + 1) * R, tot[bexp]) - 1
    cum_b = cum[1:, :][:, bexp].T
    lo = jnp.minimum(jnp.sum(cum_b <= rho0[:, None], axis=1).astype(i32), C - 1)
    hi = jnp.minimum(jnp.sum(cum_b <= rho1[:, None], axis=1).astype(i32), C - 1)
    nit = jnp.where(bvalid, hi - lo + 1, 0)
    it_end = jnp.cumsum(nit)
    it_start = it_end - nit
    n_items = it_end[-1]
    i = jnp.arange(n_items_max, dtype=i32)
    ic = jnp.minimum(i, n_items - 1)
    d_blk = jnp.minimum(count_le(it_end, ic), n_blocks_max - 1)
    d_chk = lo[d_blk] + ic - it_start[d_blk]
    d_valid = i < n_items
    d_first = d_valid & (ic == it_start[d_blk])
    d_last = d_valid & (ic == it_end[d_blk] - 1)
    d_exp = bexp[d_blk]
    d_ra = jnp.maximum(rho0[d_blk], cum[d_chk, d_exp]) - rho0[d_blk]
    d_rb = jnp.minimum(rho1[d_blk], cum[d_chk + 1, d_exp] - 1) - rho0[d_blk]
    d_start, d_nwin = windows(d_ra, d_rb, d_valid & (d_rb >= d_ra))
    is_fill = (i >= n_items) & (i < n_items + (n_blocks_max - n_blocks))
    d_blk = jnp.where(i < n_items, d_blk, jnp.minimum(n_blocks + i - n_items, n_blocks_max - 1))
    d_first = d_first | is_fill
    d_last = d_last | is_fill
    bl_lo = (blk_start[None, :] + cum[:-1] // R).reshape(-1)
    bl_hi = (blk_start[None, :] + (cum[1:] - 1) // R).reshape(-1)
    npair = jnp.where(cnt.reshape(-1) > 0, bl_hi - bl_lo + 1, 0)
    p_end = jnp.cumsum(npair)
    p_start = p_end - npair
    n_items2 = p_end[-1]
    jc = jnp.minimum(i, n_items2 - 1)
    pair = jnp.minimum(count_le(p_end, jc), C * E - 1)
    c_blk = bl_lo[pair] + jc - p_start[pair]
    c_chk = pair // E
    c_valid = i < n_items2
    prev_chk = jnp.concatenate([jnp.full((1,), -1, i32), c_chk[:-1]])
    next_chk = jnp.concatenate([c_chk[1:], jnp.full((1,), -1, i32)])
    next_valid = jnp.concatenate([c_valid[1:], jnp.zeros((1,), bool)])
    c_first = c_valid & (c_chk != prev_chk)
    c_last = c_valid & ((c_chk != next_chk) | ~next_valid)
    c_exp = pair % E
    c_base = (c_blk - blk_start[c_exp]) * R
    c_ra = jnp.maximum(cum[c_chk, c_exp] - c_base, 0)
    c_rb = jnp.minimum(cum[c_chk + 1, c_exp] - 1 - c_base, R - 1)
    c_start, c_nwin = windows(c_ra, c_rb, c_valid)
    as_i32 = lambda v: v.astype(i32)
    return dict(base=blk_start * R, bexp=bexp, bvalid=as_i32(bvalid),
                d_blk=d_blk, d_chk=d_chk, d_first=as_i32(d_first), d_last=as_i32(d_last), d_start=d_start, d_nwin=d_nwin,
                c_blk=c_blk, c_chk=c_chk, c_first=as_i32(c_first), c_last=as_i32(c_last),
                c_start=c_start, c_nwin=c_nwin)


def _dispatch_kernel(blk_ref, chk_ref, first_ref, last_ref, start_ref, nwin_ref, xb_ref, pos_ref, wcol_ref,
                     xs_ref, ws_ref, acc_ref, wacc_ref):
    i = pl.program_id(0)
    R = xs_ref.shape[0]

    @pl.when(first_ref[i] == 1)
    def _():
        acc_ref[...] = jnp.zeros_like(acc_ref)
        wacc_ref[...] = jnp.zeros_like(wacc_ref)

    pos1 = pos_ref[0, 0:1, :]
    pos2 = pos_ref[0, 1:2, :]

    def window(k, carry):
        r0 = pl.multiple_of(start_ref[i] + k * MOE_WIN, MOE_ALIGN)
        rows = (blk_ref[i] * R + r0 + lax.broadcasted_iota(jnp.int32, (MOE_WIN, 1), 0)).astype(F32)
        hit = jnp.where((pos1 - rows) * (pos2 - rows) == 0.0, 1.0, 0.0).astype(BF16)
        acc_ref[pl.ds(r0, MOE_WIN), :] += _dot(hit, xb_ref[...])
        wacc_ref[pl.ds(r0, MOE_WIN), :] += _dot(hit, wcol_ref[...])
        return carry

    lax.fori_loop(0, nwin_ref[i], window, 0)

    @pl.when(last_ref[i] == 1)
    def _():
        xs_ref[...] = acc_ref[...].astype(xs_ref.dtype)
        ws_ref[...] = wacc_ref[...]


def _dispatch(xb, posrow, wcol, tab, n_blocks_max, n_items_max):
    T, D = xb.shape
    R = MOE_ROWS
    chunk_map = lambda i, blk, chk, first, last, start, nwin: (chk[i], 0)
    block_map = lambda i, blk, chk, first, last, start, nwin: (blk[i], 0)
    grid_spec = pltpu.PrefetchScalarGridSpec(
        num_scalar_prefetch=6,
        grid=(n_items_max,),
        in_specs=[
            pl.BlockSpec((R, D), chunk_map),
            pl.BlockSpec((1, 8, R), lambda i, blk, chk, first, last, start, nwin: (chk[i], 0, 0)),
            pl.BlockSpec((R, LANES), chunk_map),
        ],
        out_specs=[pl.BlockSpec((R, D), block_map), pl.BlockSpec((R, LANES), block_map)],
        scratch_shapes=[pltpu.VMEM((R, D), F32), pltpu.VMEM((R, LANES), F32)],
    )
    return pl.pallas_call(
        _dispatch_kernel,
        grid_spec=grid_spec,
        out_shape=[
            jax.ShapeDtypeStruct((n_blocks_max * R, D), BF16),
            jax.ShapeDtypeStruct((n_blocks_max * R, LANES), F32),
        ],
        compiler_params=_params("arbitrary"),
    )(tab["d_blk"], tab["d_chk"], tab["d_first"], tab["d_last"], tab["d_start"], tab["d_nwin"], xb, posrow, wcol)


def _expert_ffn_kernel(bexp_ref, bvalid_ref, xs_ref, ws_ref, wg_ref, wu_ref, wd_ref, ys_ref, acc_ref):
    b = pl.program_id(0)
    f = pl.program_id(1)

    @pl.when(f == 0)
    def _():
        acc_ref[...] = jnp.zeros_like(acc_ref)

    @pl.when(bvalid_ref[b] == 1)
    def _():
        xb = xs_ref[...]
        a = _silu(_dot(xb, wg_ref[0])) * _dot(xb, wu_ref[0])
        acc_ref[...] += _dot(a.astype(BF16), wd_ref[0])

    @pl.when(f == pl.num_programs(1) - 1)
    def _():
        ws = ws_ref[...]
        lane = lax.broadcasted_iota(jnp.int32, ws.shape, 1)
        w = jnp.sum(jnp.where((lane >> 1) == bexp_ref[b], ws, 0.0), axis=-1, keepdims=True)
        ys_ref[...] = (acc_ref[...] * w).astype(ys_ref.dtype)


def _expert_ffn(xs, ws, w_gu, w_down, tab, n_blocks_max):
    D = xs.shape[1]
    R = MOE_ROWS
    d_ff = w_down.shape[1]
    fc = _ff_chunk(d_ff, 1792)
    nf = d_ff // fc
    f_eff = lambda b, f, bvalid: f * bvalid[b] + (nf - 1) * (1 - bvalid[b])
    grid_spec = pltpu.PrefetchScalarGridSpec(
        num_scalar_prefetch=2,
        grid=(n_blocks_max, nf),
        in_specs=[
            pl.BlockSpec((R, D), lambda b, f, bexp, bvalid: (b, 0)),
            pl.BlockSpec((R, LANES), lambda b, f, bexp, bvalid: (b, 0)),
            pl.BlockSpec((1, D, fc), lambda b, f, bexp, bvalid: (bexp[b], 0, f_eff(b, f, bvalid))),
            pl.BlockSpec((1, D, fc), lambda b, f, bexp, bvalid: (bexp[b], 0, nf + f_eff(b, f, bvalid))),
            pl.BlockSpec((1, fc, D), lambda b, f, bexp, bvalid: (bexp[b], f_eff(b, f, bvalid), 0)),
        ],
        out_specs=pl.BlockSpec((R, D), lambda b, f, bexp, bvalid: (b, 0)),
        scratch_shapes=[pltpu.VMEM((R, D), F32)],
    )
    return pl.pallas_call(
        _expert_ffn_kernel,
        grid_spec=grid_spec,
        out_shape=jax.ShapeDtypeStruct(xs.shape, BF16),
        compiler_params=_params("parallel", "arbitrary"),
    )(tab["bexp"], tab["bvalid"], xs, ws, w_gu, w_gu, w_down)


def _combine_ln_kernel(chk_ref, blk_ref, first_ref, last_ref, start_ref, nwin_ref,
                       ys_ref, pc1_ref, pc2_ref, x_ref, g_ref, b_ref, o_ref, acc_ref):
    i = pl.program_id(0)
    R = ys_ref.shape[0]

    @pl.when(first_ref[i] == 1)
    def _():
        acc_ref[...] = jnp.zeros_like(acc_ref)

    def window(k, carry):
        r0 = pl.multiple_of(start_ref[i] + k * MOE_WIN, MOE_ALIGN)
        pc1 = pc1_ref[...]
        pc2 = pc2_ref[...]
        lane = lax.broadcasted_iota(jnp.int32, pc1.shape, 1)
        parts = []
        for t in range(MOE_WIN // LANES):
            col = (blk_ref[i] * R + r0 + t * LANES + lane).astype(F32)
            parts.append(jnp.where((pc1 - col) * (pc2 - col) == 0.0, 1.0, 0.0).astype(BF16))
        acc_ref[...] += _dot(jnp.concatenate(parts, axis=1), ys_ref[pl.ds(r0, MOE_WIN), :])
        return carry

    lax.fori_loop(0, nwin_ref[i], window, 0)

    @pl.when(last_ref[i] == 1)
    def _():
        o_ref[...] = _layer_norm(ALPHA * x_ref[...] + acc_ref[...], g_ref[...], b_ref[...])


def _combine_ln(ys, poscol1, poscol2, x, g, b, tab, n_items_max):
    T, D = x.shape
    R = MOE_ROWS
    chunk_map = lambda i, chk, blk, first, last, start, nwin: (chk[i], 0)
    const_map = lambda i, chk, blk, first, last, start, nwin: (0, 0)
    grid_spec = pltpu.PrefetchScalarGridSpec(
        num_scalar_prefetch=6,
        grid=(n_items_max,),
        in_specs=[
            pl.BlockSpec((R, D), lambda i, chk, blk, first, last, start, nwin: (blk[i], 0)),
            pl.BlockSpec((R, LANES), chunk_map),
            pl.BlockSpec((R, LANES), chunk_map),
            pl.BlockSpec((R, D), chunk_map),
            pl.BlockSpec((1, D), const_map),
            pl.BlockSpec((1, D), const_map),
        ],
        out_specs=pl.BlockSpec((R, D), chunk_map),
        scratch_shapes=[pltpu.VMEM((R, D), F32)],
    )
    return pl.pallas_call(
        _combine_ln_kernel,
        grid_spec=grid_spec,
        out_shape=jax.ShapeDtypeStruct((T, D), F32),
        compiler_params=_params("arbitrary"),
    )(tab["c_chk"], tab["c_blk"], tab["c_first"], tab["c_last"], tab["c_start"], tab["c_nwin"],
      ys, poscol1, poscol2, x, g.reshape(1, D), b.reshape(1, D))


def _moe_ln(x, router, w_gu, w_down, g, b):
    T, D = x.shape
    R = MOE_ROWS
    n_exp = w_down.shape[0]
    n_chunks = T // R
    n_blocks_max = (2 * T) // R + n_exp
    n_items_max = n_blocks_max + n_exp * (n_chunks - 1)
    xb, info, wcol, cnt = _route(x, router)
    tab = _moe_tables(cnt[:, 0, :n_exp].astype(jnp.int32), n_blocks_max, n_items_max)
    field = lambda k: info[:, k, :].reshape(T).astype(jnp.int32)
    pos1 = tab["base"][field(0)] + field(2)
    pos2 = tab["base"][field(1)] + field(3)
    pos1 = pos1.astype(F32)
    pos2 = pos2.astype(F32)
    posrow = jnp.stack([pos1.reshape(n_chunks, R), pos2.reshape(n_chunks, R)], axis=1)
    posrow = jnp.pad(posrow, ((0, 0), (0, 6), (0, 0)), constant_values=-1.0)
    poscol1 = jnp.broadcast_to(pos1[:, None], (T, LANES))
    poscol2 = jnp.broadcast_to(pos2[:, None], (T, LANES))
    xs, ws = _dispatch(xb, posrow, wcol, tab, n_blocks_max, n_items_max)
    ys = _expert_ffn(xs, ws, w_gu, w_down, tab, n_blocks_max)
    return _combine_ln(ys, poscol1, poscol2, x, g, b, tab, n_items_max)


def kernel(x, ln_g, ln_b, pool_w, pool_scale, nsa_w_in, nsa_pe_k, nsa_w1_k, nsa_w2_k, nsa_pe_v, nsa_w1_v,
           nsa_w2_v, nsa_w_out, ffn_w_gu, ffn_w_down, moe_router, moe_w_gu, moe_w_down):
    B, S, D = x.shape
    T = B * S
    xa = _pool_ln(x, pool_w[0], pool_scale[0], ln_g[0, 0], ln_b[0, 0]).reshape(T, D)
    x1 = _ffn_ln(xa, ffn_w_gu[0].astype(BF16), ffn_w_down[0].astype(BF16), ln_g[0, 1], ln_b[0, 1])
    qt, kcv, ks, vst, kw, vwt, gatet = _in_proj(x1, nsa_w_in[0], B, S)
    kc, vct = _compress(kcv, nsa_w1_k[0], nsa_w2_k[0], nsa_pe_k[0], nsa_w1_v[0], nsa_w2_v[0], nsa_pe_v[0], B, S)
    oct, selt = _cmp_attn(qt, kc, vct, B, S)
    o = _sel_win_attn(qt, selt, ks, vst, kw, vwt, oct, gatet, B, S).reshape(T, D)
    x2 = _out_proj_ln(o, nsa_w_out[0], x1, ln_g[1, 0], ln_b[1, 0])
    y = _moe_ln(x2, moe_router[0], moe_w_gu[0].astype(BF16), moe_w_down[0].astype(BF16), ln_g[1, 1], ln_b[1, 1])
    return y.reshape(B, S, D)
```

```python
import functools

import jax
import jax.numpy as jnp
from jax import lax
from jax.experimental import pallas as pl
from jax.experimental.pallas import tpu as pltpu

D_MODEL = 1024
DEPTH = 2
POOL_WINDOWS = (2, 4, 8, 16)
POOL_GROUP_DIM = D_MODEL // len(POOL_WINDOWS)
POOL_HALO = 16
N_HEADS = 16
N_KV_GROUPS = 4
HEADS_PER_GROUP = N_HEADS // N_KV_GROUPS
HEAD_DIM = D_MODEL // N_HEADS
HALF_DIM = HEAD_DIM // 2
GROUP_Q_DIM = HEADS_PER_GROUP * HEAD_DIM
KV_DIM = N_KV_GROUPS * HEAD_DIM
N_BRANCHES = 3
N_GATES = N_BRANCHES * N_HEADS
CMP_STRIDE = 16
CMP_BLOCK = 2 * CMP_STRIDE
CMP_HIDDEN = 2 * HEAD_DIM
SEL_BLOCK = 64
SEL_SHIFT = SEL_BLOCK.bit_length() - 1
SEL_TOP_N = 16
WINDOW = 512
FORCE_BONUS = 1.0e3
NEG_INF = -1.0e30
ROPE_THETA = 10000.0
ATTN_SCALE = HEAD_DIM ** -0.5
LOG2E = 1.4426950408889634
Q_SCALE = ATTN_SCALE * LOG2E
AUG_DIM = 2 * HEAD_DIM
SEL_SLOTS = AUG_DIM - HEAD_DIM
VT_ROWS = HEAD_DIM + 16
GATE_SLOTS = 16
GATE_ROWS = N_KV_GROUPS * GATE_SLOTS
N_EXPERTS = 8
LN_EPS = 1e-5
ALPHA = (2 * DEPTH) ** 0.25

LANES = 128
VMEM_LIMIT_BYTES = 56 * 1024 * 1024

POOL_ROWS = 512
FFN_ROWS = 512
PROJ_ROWS = 512
PROJ_COLS = 256
CMP_Q_ROWS = 512
ATT_Q_ROWS = 512
ATT_K_ROWS = 512
ATT_HALF = 128
ATT_COLS = 512
ATT_AHEAD = 3
MOE_ROWS = 512
MOE_WIN = 256
MOE_ALIGN = 16

F32 = jnp.float32
BF16 = jnp.bfloat16


def _dot(a, b):
    return jnp.dot(a, b, preferred_element_type=F32)


def _dot_nt(a, b):
    return lax.dot_general(a, b, (((1,), (1,)), ((), ())), preferred_element_type=F32)


def _layer_norm(z, g, b):
    mu = jnp.mean(z, axis=-1, keepdims=True)
    zc = z - mu
    var = jnp.mean(zc * zc, axis=-1, keepdims=True)
    return zc * lax.rsqrt(var + LN_EPS) * g + b


def _silu(x):
    return x / (1.0 + jnp.exp(-x))


def _params(*semantics):
    return pltpu.CompilerParams(dimension_semantics=semantics, vmem_limit_bytes=VMEM_LIMIT_BYTES)


def _pool_ln_kernel(x_ref, halo_ref, w_ref, scale_ref, g_ref, b_ref, o_ref, ext_ref):
    i = pl.program_id(1)
    rows = x_ref.shape[1]
    x = x_ref[0]
    ext_ref[0:POOL_HALO, :] = jnp.where(i > 0, halo_ref[0], 0.0)
    ext_ref[POOL_HALO:, :] = x
    pos = i * rows + lax.broadcasted_iota(jnp.int32, (rows, 1), 0)
    ys = []
    run = ext_ref[...]
    span = 1
    for gi, w in enumerate(POOL_WINDOWS):
        while span < w:
            run = run + pltpu.roll(run, span, 0)
            span *= 2
        xg = x[:, gi * POOL_GROUP_DIM:(gi + 1) * POOL_GROUP_DIM]
        cnt = jnp.minimum(pos + 1, w).astype(F32)
        diff = run[POOL_HALO:, 0:POOL_GROUP_DIM] / cnt - xg
        ys.append(_dot(diff.astype(BF16), w_ref[gi]))
        run = run[:, POOL_GROUP_DIM:]
    h = jnp.concatenate(ys, axis=1) * scale_ref[...]
    o_ref[0] = _layer_norm(ALPHA * x + h, g_ref[...], b_ref[...])


def _pool_ln(x, w, scale, g, b):
    B, S, D = x.shape
    rows = POOL_ROWS
    assert all(a < b_ for a, b_ in zip(POOL_WINDOWS, POOL_WINDOWS[1:]))
    assert all(w_ & (w_ - 1) == 0 for w_ in POOL_WINDOWS) and POOL_WINDOWS[-1] <= POOL_HALO
    halo_blocks = rows // POOL_HALO
    row2 = lambda v: v.reshape(1, D)
    return pl.pallas_call(
        _pool_ln_kernel,
        grid=(B, S // rows),
        in_specs=[
            pl.BlockSpec((1, rows, D), lambda bi, i: (bi, i, 0)),
            pl.BlockSpec((1, POOL_HALO, D), lambda bi, i: (bi, jnp.maximum(i * halo_blocks - 1, 0), 0)),
            pl.BlockSpec(w.shape, lambda bi, i: (0, 0, 0)),
            pl.BlockSpec((1, D), lambda bi, i: (0, 0)),
            pl.BlockSpec((1, D), lambda bi, i: (0, 0)),
            pl.BlockSpec((1, D), lambda bi, i: (0, 0)),
        ],
        out_specs=pl.BlockSpec((1, rows, D), lambda bi, i: (bi, i, 0)),
        out_shape=jax.ShapeDtypeStruct((B, S, D), F32),
        scratch_shapes=[pltpu.VMEM((rows + POOL_HALO, D), F32)],
        compiler_params=_params("parallel", "arbitrary"),
    )(x, x, w.astype(BF16), row2(scale), row2(g), row2(b))


def _ffn_ln_kernel(x_ref, wg_ref, wu_ref, wd_ref, g_ref, b_ref, o_ref, xb_ref, acc_ref):
    f = pl.program_id(1)

    @pl.when(f == 0)
    def _():
        xb_ref[...] = x_ref[...].astype(BF16)
        acc_ref[...] = jnp.zeros_like(acc_ref)

    xb = xb_ref[...]
    a = _silu(_dot(xb, wg_ref[...])) * _dot(xb, wu_ref[...])
    acc_ref[...] += _dot(a.astype(BF16), wd_ref[...])

    @pl.when(f == pl.num_programs(1) - 1)
    def _():
        o_ref[...] = _layer_norm(ALPHA * x_ref[...] + acc_ref[...], g_ref[...], b_ref[...])


def _ff_chunk(d_ff, target):
    best = LANES
    for c in range(LANES, target + 1, LANES):
        if d_ff % c == 0:
            best = c
    return best


def _ffn_ln(x, w_gu, w_down, g, b):
    T, D = x.shape
    d_ff = w_down.shape[0]
    fc = _ff_chunk(d_ff, 1536)
    nf = d_ff // fc
    rows = FFN_ROWS
    return pl.pallas_call(
        _ffn_ln_kernel,
        grid=(T // rows, nf),
        in_specs=[
            pl.BlockSpec((rows, D), lambda i, f: (i, 0)),
            pl.BlockSpec((D, fc), lambda i, f: (0, f)),
            pl.BlockSpec((D, fc), lambda i, f: (0, nf + f)),
            pl.BlockSpec((fc, D), lambda i, f: (f, 0)),
            pl.BlockSpec((1, D), lambda i, f: (0, 0)),
            pl.BlockSpec((1, D), lambda i, f: (0, 0)),
        ],
        out_specs=pl.BlockSpec((rows, D), lambda i, f: (i, 0)),
        out_shape=jax.ShapeDtypeStruct((T, D), F32),
        scratch_shapes=[pltpu.VMEM((rows, D), BF16), pltpu.VMEM((rows, D), F32)],
        compiler_params=_params("parallel", "arbitrary"),
    )(x, w_gu, w_gu, w_down, g.reshape(1, D), b.reshape(1, D))


def _rope_tables(pos, reps):
    freqs = jnp.power(ROPE_THETA, -jnp.arange(HALF_DIM, dtype=F32) / HALF_DIM)
    ang = pos.astype(F32)[:, None] * freqs[None, :]
    cos, sin = jnp.cos(ang), jnp.sin(ang)
    return (jnp.tile(jnp.concatenate([cos, cos], axis=1), (1, reps)),
            jnp.tile(jnp.concatenate([-sin, sin], axis=1), (1, reps)))


def _in_proj_kernel(x_ref, w_ref, wt_ref, cos_ref, sin_ref, cost_ref, sint_ref,
                    qt_ref, kcv_ref, ks_ref, vst_ref, kw_ref, vwt_ref, gatet_ref, *, steps_per_seq):
    rows = x_ref.shape[0]
    xb = x_ref[...].astype(BF16)
    cos = cos_ref[...]
    sin = sin_ref[...]
    lane = lax.broadcasted_iota(jnp.int32, cos.shape, 1)
    first_half = (lane & (HEAD_DIM - 1)) < HALF_DIM
    seq_step = pl.program_id(0) % steps_per_seq
    pos = seq_step * rows + lax.broadcasted_iota(jnp.int32, (rows, HEAD_DIM), 0)
    col = lax.broadcasted_iota(jnp.int32, (rows, HEAD_DIM), 1)
    blk_onehot = jnp.where((pos >> SEL_SHIFT) == col, 1.0, 0.0)
    zeros = jnp.zeros((rows, HEAD_DIM), F32)

    def col_tile(j):
        return _dot(xb, w_ref[:, j * PROJ_COLS:(j + 1) * PROJ_COLS])

    def row_tile(r0, n):
        return _dot_nt(wt_ref[r0:r0 + n, :], xb)

    def rope(y):
        rot = jnp.where(first_half,
                        pltpu.roll(y, PROJ_COLS - HALF_DIM, 1),
                        pltpu.roll(y, HALF_DIM, 1))
        return y * cos + rot * sin

    def rope_t(yt):
        pieces = []
        for h in range(yt.shape[0] // HEAD_DIM):
            pieces.append(yt[h * HEAD_DIM + HALF_DIM:(h + 1) * HEAD_DIM])
            pieces.append(yt[h * HEAD_DIM:h * HEAD_DIM + HALF_DIM])
        return yt * cost_ref[...] + jnp.concatenate(pieces, axis=0) * sint_ref[...]

    def store_keys(ref, y, extra):
        for gi in range(N_KV_GROUPS):
            ref[0, gi, :, 0:HEAD_DIM] = y[:, gi * HEAD_DIM:(gi + 1) * HEAD_DIM].astype(ref.dtype)
            ref[0, gi, :, HEAD_DIM:] = extra.astype(ref.dtype)

    def store_values_t(ref, yt):
        ones = jnp.ones((VT_ROWS - HEAD_DIM, rows), ref.dtype)
        for gi in range(N_KV_GROUPS):
            ref[0, gi, 0, 0:HEAD_DIM, :] = yt[gi * HEAD_DIM:(gi + 1) * HEAD_DIM].astype(ref.dtype)
            ref[0, gi, 0, HEAD_DIM:, :] = ones

    for j in range(D_MODEL // PROJ_COLS):
        qt = rope_t(row_tile(j * PROJ_COLS, PROJ_COLS)) * Q_SCALE
        qt_ref[0, j * PROJ_COLS:(j + 1) * PROJ_COLS, :] = qt.astype(qt_ref.dtype)
    raw_k = col_tile(0)
    raw_v = col_tile(1)
    for gi in range(N_KV_GROUPS):
        kcv_ref[0, gi, :, 0:HEAD_DIM] = raw_k[:, gi * HEAD_DIM:(gi + 1) * HEAD_DIM]
        kcv_ref[0, gi, :, HEAD_DIM:] = raw_v[:, gi * HEAD_DIM:(gi + 1) * HEAD_DIM]
    store_keys(ks_ref, rope(col_tile(2)), blk_onehot)
    store_keys(kw_ref, rope(col_tile(3)), zeros)
    store_values_t(vst_ref, row_tile(D_MODEL, KV_DIM))
    store_values_t(vwt_ref, row_tile(D_MODEL + KV_DIM, KV_DIM))
    logits_t = row_tile(D_MODEL + 2 * KV_DIM, GATE_ROWS)
    gatet_ref[0] = 1.0 / (1.0 + jnp.exp(-logits_t))


def _in_proj(x, w_in, B, S):
    T, D = x.shape
    rows = PROJ_ROWS
    assert rows == ATT_K_ROWS
    steps_per_seq = S // rows
    sec = lambda k: w_in[:, D_MODEL + k * KV_DIM:D_MODEL + (k + 1) * KV_DIM]
    w = jnp.concatenate([sec(0), sec(1), sec(2), sec(4)], axis=1).astype(BF16)
    wg = w_in[:, D_MODEL + 6 * KV_DIM:].reshape(D, N_KV_GROUPS, N_BRANCHES * HEADS_PER_GROUP)
    wg = jnp.pad(wg, ((0, 0), (0, 0), (0, GATE_SLOTS - N_BRANCHES * HEADS_PER_GROUP))).reshape(D, GATE_ROWS)
    wt = jnp.concatenate([w_in[:, :D_MODEL], sec(3), sec(5), wg], axis=1).T.astype(BF16)
    cos, sin = _rope_tables(jnp.arange(S), PROJ_COLS // HEAD_DIM)
    k_shape = jax.ShapeDtypeStruct((B, N_KV_GROUPS, S, AUG_DIM), BF16)
    k_spec = pl.BlockSpec((1, N_KV_GROUPS, rows, AUG_DIM),
                          lambda i: (i // steps_per_seq, 0, i % steps_per_seq, 0))
    vt_shape = jax.ShapeDtypeStruct((B, N_KV_GROUPS, steps_per_seq, VT_ROWS, rows), BF16)
    vt_spec = pl.BlockSpec((1, N_KV_GROUPS, 1, VT_ROWS, rows),
                           lambda i: (i // steps_per_seq, 0, i % steps_per_seq, 0, 0))
    tok_map = lambda i: (i, 0)
    seq_map = lambda i: (i % steps_per_seq, 0)
    feat_map = lambda i: (i // steps_per_seq, 0, i % steps_per_seq)
    return pl.pallas_call(
        functools.partial(_in_proj_kernel, steps_per_seq=steps_per_seq),
        grid=(T // rows,),
        in_specs=[
            pl.BlockSpec((rows, D), tok_map),
            pl.BlockSpec(w.shape, lambda i: (0, 0)),
            pl.BlockSpec(wt.shape, lambda i: (0, 0)),
            pl.BlockSpec((rows, PROJ_COLS), seq_map),
            pl.BlockSpec((rows, PROJ_COLS), seq_map),
            pl.BlockSpec((PROJ_COLS, rows), lambda i: (0, i % steps_per_seq)),
            pl.BlockSpec((PROJ_COLS, rows), lambda i: (0, i % steps_per_seq)),
        ],
        out_specs=[
            pl.BlockSpec((1, D_MODEL, rows), feat_map),
            k_spec,
            k_spec, vt_spec, k_spec, vt_spec,
            pl.BlockSpec((1, GATE_ROWS, rows), feat_map),
        ],
        out_shape=[
            jax.ShapeDtypeStruct((B, D_MODEL, S), BF16),
            jax.ShapeDtypeStruct((B, N_KV_GROUPS, S, AUG_DIM), F32),
            k_shape, vt_shape, k_shape, vt_shape,
            jax.ShapeDtypeStruct((B, GATE_ROWS, S), F32),
        ],
        compiler_params=_params("parallel"),
    )(x, w, wt, cos, sin, cos.T, sin.T)


def _compress_kernel(kv_ref, w1_ref, pea_ref, peb_ref, w2_ref, w2t_ref, cos_ref, sin_ref, kc_ref, vct_ref):
    n = kc_ref.shape[2]
    hid = CMP_HIDDEN
    r = jnp.zeros((n, 4 * hid), F32)
    bias_a = jnp.zeros((8, 4 * hid), F32)
    bias_b = jnp.zeros((8, 4 * hid), F32)
    for p in range(CMP_STRIDE):
        slab = kv_ref[0, 0, pl.ds(p, n, stride=CMP_STRIDE), :].astype(BF16)
        r = r + _dot(slab, w1_ref[p])
        bias_a = bias_a + _dot(pea_ref[p], w1_ref[p])
        bias_b = bias_b + _dot(peb_ref[p], w1_ref[p])
    hidden = []
    for which in range(2):
        c0 = which * 2 * hid
        nxt = pltpu.roll(r[:, c0 + hid:c0 + 2 * hid], n - 1, 0)
        bias = bias_a[0:1, c0:c0 + hid] + bias_b[0:1, c0 + hid:c0 + 2 * hid]
        hidden.append(_silu(r[:, c0:c0 + hid] + nxt + bias).astype(BF16))
    kc = _dot(hidden[0], w2_ref[...])
    rot = jnp.concatenate([kc[:, HALF_DIM:], kc[:, :HALF_DIM]], axis=1)
    kc_ref[0, 0] = (kc * cos_ref[...] + rot * sin_ref[...]).astype(kc_ref.dtype)
    vct_ref[0, 0] = _dot_nt(w2t_ref[...], hidden[1]).astype(vct_ref.dtype)


def _compress(kv_raw, w1_k, w2_k, pe_k, w1_v, w2_v, pe_v, B, S):
    n_chunks = S // CMP_STRIDE
    hid = CMP_HIDDEN
    split = lambda w1: w1.reshape(2, CMP_STRIDE, HEAD_DIM, hid).transpose(1, 2, 0, 3).reshape(CMP_STRIDE, HEAD_DIM, 2 * hid)
    zeros = jnp.zeros((CMP_STRIDE, HEAD_DIM, 2 * hid), F32)
    w1 = jnp.concatenate([jnp.concatenate([split(w1_k), zeros], axis=2),
                          jnp.concatenate([zeros, split(w1_v)], axis=2)], axis=1).astype(BF16)
    pe = jnp.concatenate([pe_k, pe_v], axis=1)
    rows8 = lambda v: jnp.broadcast_to(v[:, None, :], (CMP_STRIDE, 8, AUG_DIM)).astype(BF16)
    pea, peb = rows8(pe[:CMP_STRIDE]), rows8(pe[CMP_STRIDE:])
    cos, sin = _rope_tables(CMP_STRIDE * jnp.arange(n_chunks) + CMP_BLOCK - 1, 1)
    const = lambda a: pl.BlockSpec(a.shape, lambda bi, gi: (0,) * a.ndim)
    w2k = w2_k.astype(BF16)
    w2vt = w2_v.T.astype(BF16)
    return pl.pallas_call(
        _compress_kernel,
        grid=(B, N_KV_GROUPS),
        in_specs=[
            pl.BlockSpec((1, 1, S, AUG_DIM), lambda bi, gi: (bi, gi, 0, 0)),
            const(w1), const(pea), const(peb), const(w2k), const(w2vt), const(cos), const(sin),
        ],
        out_specs=[pl.BlockSpec((1, 1, n_chunks, HEAD_DIM), lambda bi, gi: (bi, gi, 0, 0)),
                   pl.BlockSpec((1, 1, HEAD_DIM, n_chunks), lambda bi, gi: (bi, gi, 0, 0))],
        out_shape=[jax.ShapeDtypeStruct((B, N_KV_GROUPS, n_chunks, HEAD_DIM), BF16),
                   jax.ShapeDtypeStruct((B, N_KV_GROUPS, HEAD_DIM, n_chunks), BF16)],
        compiler_params=_params("parallel", "parallel"),
    )(kv_raw, w1, pea, peb, w2k, w2vt, cos, sin)


def _cmp_attn_kernel(qt_ref, kc_ref, vct_ref, ovl_ref, oct_ref, selt_ref, score_ref, imp_ref):
    i = pl.program_id(2)
    cols = qt_ref.shape[2]
    n_cmp = kc_ref.shape[2]
    n_sel = ovl_ref.shape[0]
    t = i * cols + lax.broadcasted_iota(jnp.int32, (1, cols), 1)
    any_valid = jnp.where(t >= CMP_BLOCK - 1, 1.0, 0.0)

    def attend(n):
        kc = kc_ref[0, 0, 0:n, :]
        vct = vct_ref[0, 0, :, 0:n]
        cend = CMP_STRIDE * lax.broadcasted_iota(jnp.int32, (n, 1), 0) + (CMP_BLOCK - 1)
        cvalid = cend <= t
        pc_sum = jnp.zeros((n, cols), F32)
        for h in range(HEADS_PER_GROUP):
            qh = qt_ref[0, h * HEAD_DIM:(h + 1) * HEAD_DIM, :]
            s = jnp.where(cvalid, _dot(kc, qh), NEG_INF)
            e = jnp.exp2(s - jnp.max(s, axis=0, keepdims=True))
            pc = e * (any_valid / jnp.sum(e, axis=0, keepdims=True))
            oct_ref[0, h * HEAD_DIM:(h + 1) * HEAD_DIM, :] = _dot(vct, pc.astype(BF16))
            pc_sum = pc_sum + pc
        imp_ref[...] = _dot(ovl_ref[:, 0:n], pc_sum.astype(BF16))

    n_half = n_cmp // 2
    early = (i + 1) * cols <= n_half * CMP_STRIDE

    @pl.when(early)
    def _():
        attend(n_half)

    @pl.when(jnp.logical_not(early))
    def _():
        attend(n_cmp)

    imp = imp_ref[...]
    j = lax.broadcasted_iota(jnp.int32, (n_sel, 1), 0)
    blk_t = t >> SEL_SHIFT
    bvalid = j <= blk_t
    forced = (j == 0) | (j == blk_t) | (j == blk_t - 1)
    score = jnp.where(bvalid, imp + jnp.where(forced, FORCE_BONUS, 0.0), -1.0)
    key = pltpu.bitcast(score, jnp.int32)
    key_next = key + 1
    score_ref[...] = key
    def rank_group(grp, rank):
        base = pl.multiple_of(grp * 8, 8)
        others = score_ref[pl.ds(base, 8), :]
        for r in range(8):
            ahead = others[r:r + 1, :] >= jnp.where(j > base + r, key, key_next)
            rank = rank + jnp.where(ahead, 1.0, 0.0)
        return rank

    n_groups = jnp.minimum(((i + 1) * cols) // (8 * SEL_BLOCK), n_sel // 8)
    rank = lax.fori_loop(0, n_groups, rank_group, jnp.zeros((n_sel, cols), F32))
    selected = (rank < float(SEL_TOP_N)) & bvalid
    selt_ref[0, 0] = jnp.where(selected, 0.0, NEG_INF).astype(selt_ref.dtype)


def _cmp_attn(qt, kc, vct, B, S):
    cols = CMP_Q_ROWS
    n_cmp = S // CMP_STRIDE
    n_sel = SEL_SLOTS
    assert S // SEL_BLOCK <= SEL_SLOTS
    cstart = CMP_STRIDE * jnp.arange(n_cmp)
    sstart = SEL_BLOCK * jnp.arange(n_sel)
    overlap = ((cstart[None, :] <= sstart[:, None] + SEL_BLOCK - 1)
               & (cstart[None, :] + CMP_BLOCK - 1 >= sstart[:, None])).astype(BF16)
    q_spec = pl.BlockSpec((1, GROUP_Q_DIM, cols), lambda bi, gi, i: (bi, gi, i))
    return pl.pallas_call(
        _cmp_attn_kernel,
        grid=(B, N_KV_GROUPS, S // cols),
        in_specs=[
            q_spec,
            pl.BlockSpec((1, 1, n_cmp, HEAD_DIM), lambda bi, gi, i: (bi, gi, 0, 0)),
            pl.BlockSpec((1, 1, HEAD_DIM, n_cmp), lambda bi, gi, i: (bi, gi, 0, 0)),
            pl.BlockSpec(overlap.shape, lambda bi, gi, i: (0, 0)),
        ],
        out_specs=[
            q_spec,
            pl.BlockSpec((1, 1, n_sel, cols), lambda bi, gi, i: (bi, gi, 0, i)),
        ],
        out_shape=[
            jax.ShapeDtypeStruct((B, D_MODEL, S), F32),
            jax.ShapeDtypeStruct((B, N_KV_GROUPS, n_sel, S), BF16),
        ],
        scratch_shapes=[pltpu.VMEM((n_sel, cols), jnp.int32), pltpu.VMEM((n_sel, cols), F32)],
        compiler_params=_params("parallel", "parallel", "arbitrary"),
    )(qt, kc, vct, overlap)


def _sel_win_attn_kernel(qt_ref, selt_ref, ks_ref, vst_ref, kw_ref, vwt_ref, diag_ref, band_ref, oct_ref, gatet_ref,
                         o_ref, qaug_ref, m_ref, acc_ref, ow_ref):
    i = pl.program_id(2)
    tq = qt_ref.shape[2]
    hg = HEADS_PER_GROUP
    half = ATT_HALF
    n_half = tq // half
    half_cols = hg * half

    for hq in range(n_half):
        for h in range(hg):
            c0 = (hq * hg + h) * half
            qaug_ref[0:HEAD_DIM, c0:c0 + half] = qt_ref[0, h * HEAD_DIM:(h + 1) * HEAD_DIM, hq * half:(hq + 1) * half]
            qaug_ref[HEAD_DIM:, c0:c0 + half] = selt_ref[0, 0, :, hq * half:(hq + 1) * half]
    n_tiles = (hg * tq) // ATT_COLS

    def keys(ref, start, size):
        return ref[0, 0, pl.ds(pl.multiple_of(start, ATT_HALF), size), :]

    m_ref[...] = jnp.full(m_ref.shape, NEG_INF, F32)
    acc_ref[...] = jnp.zeros(acc_ref.shape, F32)

    def sel_step(c, bias_ref):
        k = keys(ks_ref, c * ATT_K_ROWS, ATT_K_ROWS)
        v = vst_ref[0, 0, c]
        new_m, new_acc = [], []
        tile = lambda ct: slice(ct * ATT_COLS, (ct + 1) * ATT_COLS)
        last_part = lambda ct: ((ct + 1) * ATT_COLS - 1) // half_cols
        n_keys = lambda ct: ATT_K_ROWS if bias_ref is None else min(ATT_K_ROWS, (last_part(ct) + 1) * half)
        score = lambda ct: _dot(k[0:n_keys(ct)], qaug_ref[:, tile(ct)])
        scores = [score(ct) for ct in range(ATT_AHEAD)]
        for ct in range(n_tiles):
            cs = tile(ct)
            if ct + ATT_AHEAD < n_tiles:
                scores.append(score(ct + ATT_AHEAD))
            s = scores[ct]
            if bias_ref is not None:
                s = s + bias_ref[0:n_keys(ct), cs]
            m_prev = m_ref[:, cs]
            m_next = jnp.maximum(m_prev, jnp.max(s, axis=0, keepdims=True))
            p = jnp.exp2(s - m_next)
            alpha = jnp.exp2(m_prev - m_next)
            new_acc.append(alpha * acc_ref[:, cs] + _dot(v[:, 0:n_keys(ct)], p.astype(BF16)))
            new_m.append(m_next)
        for ct in range(n_tiles):
            cs = slice(ct * ATT_COLS, (ct + 1) * ATT_COLS)
            acc_ref[:, cs] = new_acc[ct]
            m_ref[:, cs] = new_m[ct]

    def sel_body(c, carry):
        sel_step(c, None)
        return carry

    lax.fori_loop(0, i, sel_body, 0)
    sel_step(i, diag_ref)

    def probs(s):
        return jnp.exp2(s - jnp.max(s, axis=0, keepdims=True)).astype(BF16)

    @pl.when(i == 0)
    def _():
        k = keys(kw_ref, 0, tq)
        v = vwt_ref[0, 0, 0]
        for ct in range(n_tiles):
            cs = slice(ct * ATT_COLS, (ct + 1) * ATT_COLS)
            ow_ref[:, cs] = _dot(v, probs(_dot(k, qaug_ref[:, cs]) + diag_ref[:, cs]))

    @pl.when(i > 0)
    def _():
        v_prev = vwt_ref[0, 0, i - 1]
        v_here = vwt_ref[0, 0, i]
        for hq in range(n_half):
            k = keys(kw_ref, i * tq + hq * half - WINDOW, WINDOW + half)
            n_prev = tq - hq * half
            for ct in range(half_cols // ATT_COLS):
                cs = slice(hq * half_cols + ct * ATT_COLS, hq * half_cols + (ct + 1) * ATT_COLS)
                p = probs(_dot(k, qaug_ref[:, cs]) + band_ref[:, ct * ATT_COLS:(ct + 1) * ATT_COLS])
                ow_ref[:, cs] = (_dot(v_prev[:, tq - n_prev:], p[0:n_prev])
                                 + _dot(v_here[:, 0:WINDOW + half - n_prev], p[n_prev:]))

    acc_s = acc_ref[...]
    acc_w = ow_ref[...]
    o_s = acc_s[0:HEAD_DIM] / acc_s[HEAD_DIM:HEAD_DIM + 1]
    o_w = acc_w[0:HEAD_DIM] / acc_w[HEAD_DIM:HEAD_DIM + 1]
    for hq in range(n_half):
        tok = slice(hq * half, (hq + 1) * half)
        for h in range(hg):
            c0 = (hq * hg + h) * half
            gc = gatet_ref[0, N_BRANCHES * h + 0:N_BRANCHES * h + 1, tok]
            gs = gatet_ref[0, N_BRANCHES * h + 1:N_BRANCHES * h + 2, tok]
            gw = gatet_ref[0, N_BRANCHES * h + 2:N_BRANCHES * h + 3, tok]
            o = (gc * oct_ref[0, h * HEAD_DIM:(h + 1) * HEAD_DIM, tok]
                 + gs * o_s[:, c0:c0 + half] + gw * o_w[:, c0:c0 + half])
            o_ref[0, tok, h * HEAD_DIM:(h + 1) * HEAD_DIM] = o.T.astype(o_ref.dtype)


def _sel_win_attn(qt, selt, ks, vst, kw, vwt, oct, gatet, B, S):
    tq = ATT_Q_ROWS
    half = ATT_HALF
    hg = HEADS_PER_GROUP
    cols = hg * tq
    n_chunks = S // ATT_K_ROWS
    t_rel = (jnp.arange(tq // half)[:, None, None] * half + jnp.arange(half)[None, None, :])
    t_rel = jnp.broadcast_to(t_rel, (tq // half, hg, half)).reshape(1, cols)
    diag = jnp.where(jnp.arange(ATT_K_ROWS)[:, None] <= t_rel, 0.0, NEG_INF).astype(F32)
    tt = jnp.broadcast_to(jnp.arange(half)[None, :], (hg, half)).reshape(1, hg * half)
    a = jnp.arange(WINDOW + half)[:, None]
    band = jnp.where((a > tt) & (a <= tt + WINDOW), 0.0, NEG_INF).astype(F32)
    k_spec = pl.BlockSpec((1, 1, S, AUG_DIM), lambda bi, gi, i: (bi, gi, 0, 0))
    vt_spec = pl.BlockSpec((1, 1, n_chunks, VT_ROWS, ATT_K_ROWS), lambda bi, gi, i: (bi, gi, 0, 0, 0))
    q_spec = pl.BlockSpec((1, GROUP_Q_DIM, tq), lambda bi, gi, i: (bi, gi, i))
    const = lambda arr: pl.BlockSpec(arr.shape, lambda bi, gi, i: (0, 0))
    return pl.pallas_call(
        _sel_win_attn_kernel,
        grid=(B, N_KV_GROUPS, S // tq),
        in_specs=[
            q_spec,
            pl.BlockSpec((1, 1, SEL_SLOTS, tq), lambda bi, gi, i: (bi, gi, 0, i)),
            k_spec, vt_spec, k_spec, vt_spec,
            const(diag), const(band),
            q_spec,
            pl.BlockSpec((1, GATE_SLOTS, tq), lambda bi, gi, i: (bi, gi, i)),
        ],
        out_specs=pl.BlockSpec((1, tq, GROUP_Q_DIM), lambda bi, gi, i: (bi, i, gi)),
        out_shape=jax.ShapeDtypeStruct((B, S, D_MODEL), BF16),
        scratch_shapes=[
            pltpu.VMEM((AUG_DIM, cols), BF16),
            pltpu.VMEM((1, cols), F32),
            pltpu.VMEM((VT_ROWS, cols), F32),
            pltpu.VMEM((VT_ROWS, cols), F32),
        ],
        compiler_params=_params("parallel", "parallel", "arbitrary"),
    )(qt, selt, ks, vst, kw, vwt, diag, band, oct, gatet)


def _out_proj_ln_kernel(o_ref, w_ref, x_ref, g_ref, b_ref, y_ref):
    h = _dot(o_ref[...], w_ref[...])
    y_ref[...] = _layer_norm(ALPHA * x_ref[...] + h, g_ref[...], b_ref[...])


def _out_proj_ln(o, w_out, x, g, b):
    T, D = x.shape
    rows = PROJ_ROWS
    return pl.pallas_call(
        _out_proj_ln_kernel,
        grid=(T // rows,),
        in_specs=[
            pl.BlockSpec((rows, D), lambda i: (i, 0)),
            pl.BlockSpec((D, D), lambda i: (0, 0)),
            pl.BlockSpec((rows, D), lambda i: (i, 0)),
            pl.BlockSpec((1, D), lambda i: (0, 0)),
            pl.BlockSpec((1, D), lambda i: (0, 0)),
        ],
        out_specs=pl.BlockSpec((rows, D), lambda i: (i, 0)),
        out_shape=jax.ShapeDtypeStruct((T, D), F32),
        compiler_params=_params("parallel"),
    )(o, w_out.astype(BF16), x, g.reshape(1, D), b.reshape(1, D))


def _route_kernel(x_ref, r_ref, tri_ref, xb_ref, info_ref, cnt_ref, carry_ref):
    @pl.when(pl.program_id(0) == 0)
    def _():
        carry_ref[...] = jnp.zeros_like(carry_ref)

    x = x_ref[...]
    xb = x.astype(BF16)
    xb_ref[:, 0:x.shape[1]] = xb
    x_lo = (x - xb.astype(F32)).astype(BF16)
    r_hi = r_ref[0]
    r_lo = r_ref[1]
    logits = _dot(xb, r_hi) + (_dot(x_lo, r_hi) + _dot(xb, r_lo))
    lane = lax.broadcasted_iota(jnp.int32, logits.shape, 1)
    logits = jnp.where(lane < N_EXPERTS, logits, -jnp.inf)
    v1 = jnp.max(logits, axis=-1, keepdims=True)
    i1 = jnp.min(jnp.where(logits == v1, lane, LANES), axis=-1, keepdims=True)
    rest = jnp.where(lane == i1, -jnp.inf, logits)
    v2 = jnp.max(rest, axis=-1, keepdims=True)
    i2 = jnp.min(jnp.where(rest == v2, lane, LANES), axis=-1, keepdims=True)
    e2 = jnp.exp(v2 - v1)
    w1 = 1.0 / (1.0 + e2)
    w2 = e2 / (1.0 + e2)
    m1 = jnp.where(lane == i1, 1.0, 0.0)
    m2 = jnp.where(lane == i2, 1.0, 0.0)
    routed = m1 + m2
    before = _dot(tri_ref[...], routed.astype(BF16)) + carry_ref[0:1, :]
    rank1 = jnp.sum(m1 * before, axis=-1, keepdims=True)
    rank2 = jnp.sum(m2 * before, axis=-1, keepdims=True)
    cnt = jnp.sum(routed, axis=0, keepdims=True)
    carry_ref[...] = carry_ref[...] + cnt
    cnt_ref[0] = jnp.broadcast_to(cnt, cnt_ref.shape[1:])
    info = jnp.where(lane == 0, i1.astype(F32),
                     jnp.where(lane == 1, i2.astype(F32),
                               jnp.where(lane == 2, rank1, jnp.where(lane == 3, rank2, 0.0))))
    info_ref[0] = info.T[0:8, :]
    w1_hi = w1.astype(BF16).astype(F32)
    w2_hi = w2.astype(BF16).astype(F32)
    lo_half = (lane & 1) == 1
    wcol = (jnp.where((lane >> 1) == i1, jnp.where(lo_half, w1 - w1_hi, w1_hi), 0.0)
            + jnp.where((lane >> 1) == i2, jnp.where(lo_half, w2 - w2_hi, w2_hi), 0.0))
    xb_ref[:, x.shape[1]:] = wcol.astype(BF16)


def _route(x, router):
    T, D = x.shape
    rows = MOE_ROWS
    r = jnp.pad(router, ((0, 0), (0, LANES - router.shape[1])))
    r_hi = r.astype(BF16)
    r_lo = (r - r_hi.astype(F32)).astype(BF16)
    r2 = jnp.stack([r_hi, r_lo])
    tri = (jnp.arange(rows)[None, :] < jnp.arange(rows)[:, None]).astype(BF16)
    return pl.pallas_call(
        _route_kernel,
        grid=(T // rows,),
        in_specs=[
            pl.BlockSpec((rows, D), lambda i: (i, 0)),
            pl.BlockSpec(r2.shape, lambda i: (0, 0, 0)),
            pl.BlockSpec(tri.shape, lambda i: (0, 0)),
        ],
        out_specs=[
            pl.BlockSpec((rows, D + LANES), lambda i: (i, 0)),
            pl.BlockSpec((1, 8, rows), lambda i: (i, 0, 0)),
            pl.BlockSpec((1, 8, LANES), lambda i: (i, 0, 0)),
        ],
        out_shape=[
            jax.ShapeDtypeStruct((T, D + LANES), BF16),
            jax.ShapeDtypeStruct((T // rows, 8, rows), F32),
            jax.ShapeDtypeStruct((T // rows, 8, LANES), F32),
        ],
        scratch_shapes=[pltpu.VMEM((8, LANES), F32)],
        compiler_params=_params("arbitrary"),
    )(x, r2, tri)


def _moe_tables(cnt, n_blocks_max, n_items_max):
    R = MOE_ROWS
    C, E = cnt.shape
    i32 = jnp.int32
    count_le = lambda sorted_v, q: jnp.sum(sorted_v[None, :] <= q[:, None], axis=1).astype(i32)

    def windows(ra, rb, live):
        near = jnp.minimum((ra // MOE_ALIGN) * MOE_ALIGN, R - MOE_WIN)
        fits = rb - near < MOE_WIN
        start = jnp.where(fits, near, (ra // MOE_WIN) * MOE_WIN)
        n = jnp.where(fits, 1, rb // MOE_WIN - ra // MOE_WIN + 1)
        return jnp.where(live, start, 0), jnp.where(live, n, 0)

    cum = jnp.concatenate([jnp.zeros((1, E), i32), jnp.cumsum(cnt, axis=0)], axis=0)
    tot = cum[-1]
    nb = (tot + R - 1) // R
    nb_end = jnp.cumsum(nb)
    blk_start = nb_end - nb
    n_blocks = nb_end[-1]
    b = jnp.minimum(jnp.arange(n_blocks_max, dtype=i32), n_blocks - 1)
    bexp = jnp.minimum(count_le(nb_end, b), E - 1)
    bvalid = jnp.arange(n_blocks_max, dtype=i32) < n_blocks
    lb = b - blk_start[bexp]
    rho0 = lb * R
    rho1 = jnp.minimum((lb + 1) * R, tot[bexp]) - 1
    cum_b = cum[1:, :][:, bexp].T
    lo = jnp.minimum(jnp.sum(cum_b <= rho0[:, None], axis=1).astype(i32), C - 1)
    hi = jnp.minimum(jnp.sum(cum_b <= rho1[:, None], axis=1).astype(i32), C - 1)
    nit = jnp.where(bvalid, hi - lo + 1, 0)
    it_end = jnp.cumsum(nit)
    it_start = it_end - nit
    n_items = it_end[-1]
    i = jnp.arange(n_items_max, dtype=i32)
    ic = jnp.minimum(i, n_items - 1)
    d_blk = jnp.minimum(count_le(it_end, ic), n_blocks_max - 1)
    d_chk = lo[d_blk] + ic - it_start[d_blk]
    d_valid = i < n_items
    d_first = d_valid & (ic == it_start[d_blk])
    d_last = d_valid & (ic == it_end[d_blk] - 1)
    d_exp = bexp[d_blk]
    d_ra = jnp.maximum(rho0[d_blk], cum[d_chk, d_exp]) - rho0[d_blk]
    d_rb = jnp.minimum(rho1[d_blk], cum[d_chk + 1, d_exp] - 1) - rho0[d_blk]
    d_start, d_nwin = windows(d_ra, d_rb, d_valid & (d_rb >= d_ra))
    is_fill = (i >= n_items) & (i < n_items + (n_blocks_max - n_blocks))
    d_blk = jnp.where(i < n_items, d_blk, jnp.minimum(n_blocks + i - n_items, n_blocks_max - 1))
    d_first = d_first | is_fill
    d_last = d_last | is_fill
    bl_lo = (blk_start[None, :] + cum[:-1] // R).reshape(-1)
    bl_hi = (blk_start[None, :] + (cum[1:] - 1) // R).reshape(-1)
    npair = jnp.where(cnt.reshape(-1) > 0, bl_hi - bl_lo + 1, 0)
    p_end = jnp.cumsum(npair)
    p_start = p_end - npair
    n_items2 = p_end[-1]
    jc = jnp.minimum(i, n_items2 - 1)
    pair = jnp.minimum(count_le(p_end, jc), C * E - 1)
    c_blk = bl_lo[pair] + jc - p_start[pair]
    c_chk = pair // E
    c_valid = i < n_items2
    prev_chk = jnp.concatenate([jnp.full((1,), -1, i32), c_chk[:-1]])
    next_chk = jnp.concatenate([c_chk[1:], jnp.full((1,), -1, i32)])
    next_valid = jnp.concatenate([c_valid[1:], jnp.zeros((1,), bool)])
    c_first = c_valid & (c_chk != prev_chk)
    c_last = c_valid & ((c_chk != next_chk) | ~next_valid)
    c_exp = pair % E
    c_base = (c_blk - blk_start[c_exp]) * R
    c_ra = jnp.maximum(cum[c_chk, c_exp] - c_base, 0)
    c_rb = jnp.minimum(cum[c_chk + 1, c_exp] - 1 - c_base, R - 1)
    c_start, c_nwin = windows(c_ra, c_rb, c_valid)
    as_i32 = lambda v: v.astype(i32)
    return dict(base=blk_start * R, bexp=bexp, bvalid=as_i32(bvalid),
                d_blk=d_blk, d_chk=d_chk, d_first=as_i32(d_first), d_last=as_i32(d_last), d_start=d_start, d_nwin=d_nwin,
                c_blk=c_blk, c_chk=c_chk, c_first=as_i32(c_first), c_last=as_i32(c_last),
                c_start=c_start, c_nwin=c_nwin)


def _dispatch_kernel(blk_ref, chk_ref, first_ref, last_ref, start_ref, nwin_ref, xb_ref, pos_ref,
                     xs_ref, acc_ref):
    i = pl.program_id(0)
    R = xs_ref.shape[0]

    @pl.when(first_ref[i] == 1)
    def _():
        acc_ref[...] = jnp.zeros_like(acc_ref)

    pos1 = pos_ref[0, 0:1, :]
    pos2 = pos_ref[0, 1:2, :]

    def window(k, carry):
        r0 = pl.multiple_of(start_ref[i] + k * MOE_WIN, MOE_ALIGN)
        rows = (blk_ref[i] * R + r0 + lax.broadcasted_iota(jnp.int32, (MOE_WIN, 1), 0)).astype(F32)
        hit = jnp.where((pos1 - rows) * (pos2 - rows) == 0.0, 1.0, 0.0).astype(BF16)
        acc_ref[pl.ds(r0, MOE_WIN), :] += _dot(hit, xb_ref[...])
        return carry

    lax.fori_loop(0, nwin_ref[i], window, 0)

    @pl.when(last_ref[i] == 1)
    def _():
        xs_ref[...] = acc_ref[...].astype(xs_ref.dtype)


def _dispatch(xb, posrow, tab, n_blocks_max, n_items_max):
    T, D = xb.shape
    R = MOE_ROWS
    chunk_map = lambda i, blk, chk, first, last, start, nwin: (chk[i], 0)
    block_map = lambda i, blk, chk, first, last, start, nwin: (blk[i], 0)
    grid_spec = pltpu.PrefetchScalarGridSpec(
        num_scalar_prefetch=6,
        grid=(n_items_max,),
        in_specs=[
            pl.BlockSpec((R, D), chunk_map),
            pl.BlockSpec((1, 8, R), lambda i, blk, chk, first, last, start, nwin: (chk[i], 0, 0)),
        ],
        out_specs=pl.BlockSpec((R, D), block_map),
        scratch_shapes=[pltpu.VMEM((R, D), F32)],
    )
    return pl.pallas_call(
        _dispatch_kernel,
        grid_spec=grid_spec,
        out_shape=jax.ShapeDtypeStruct((n_blocks_max * R, D), BF16),
        compiler_params=_params("arbitrary"),
    )(tab["d_blk"], tab["d_chk"], tab["d_first"], tab["d_last"], tab["d_start"], tab["d_nwin"], xb, posrow)


def _expert_ffn_kernel(bexp_ref, bvalid_ref, xs_ref, wg_ref, wu_ref, wd_ref, ys_ref, acc_ref):
    b = pl.program_id(0)
    f = pl.program_id(1)

    @pl.when(f == 0)
    def _():
        acc_ref[...] = jnp.zeros_like(acc_ref)

    @pl.when(bvalid_ref[b] == 1)
    def _():
        xb = xs_ref[:, 0:ys_ref.shape[1]]
        a = _silu(_dot(xb, wg_ref[0])) * _dot(xb, wu_ref[0])
        acc_ref[...] += _dot(a.astype(BF16), wd_ref[0])

    @pl.when(f == pl.num_programs(1) - 1)
    def _():
        ws = xs_ref[:, ys_ref.shape[1]:].astype(F32)
        lane = lax.broadcasted_iota(jnp.int32, ws.shape, 1)
        w = jnp.sum(jnp.where((lane >> 1) == bexp_ref[b], ws, 0.0), axis=-1, keepdims=True)
        ys_ref[...] = (acc_ref[...] * w).astype(ys_ref.dtype)


def _expert_ffn(xs, w_gu, w_down, tab, n_blocks_max):
    D = w_down.shape[2]
    R = MOE_ROWS
    d_ff = w_down.shape[1]
    fc = _ff_chunk(d_ff, 1792)
    nf = d_ff // fc
    f_eff = lambda b, f, bvalid: f * bvalid[b] + (nf - 1) * (1 - bvalid[b])
    grid_spec = pltpu.PrefetchScalarGridSpec(
        num_scalar_prefetch=2,
        grid=(n_blocks_max, nf),
        in_specs=[
            pl.BlockSpec((R, D + LANES), lambda b, f, bexp, bvalid: (b, 0)),
            pl.BlockSpec((1, D, fc), lambda b, f, bexp, bvalid: (bexp[b], 0, f_eff(b, f, bvalid))),
            pl.BlockSpec((1, D, fc), lambda b, f, bexp, bvalid: (bexp[b], 0, nf + f_eff(b, f, bvalid))),
            pl.BlockSpec((1, fc, D), lambda b, f, bexp, bvalid: (bexp[b], f_eff(b, f, bvalid), 0)),
        ],
        out_specs=pl.BlockSpec((R, D), lambda b, f, bexp, bvalid: (b, 0)),
        scratch_shapes=[pltpu.VMEM((R, D), F32)],
    )
    return pl.pallas_call(
        _expert_ffn_kernel,
        grid_spec=grid_spec,
        out_shape=jax.ShapeDtypeStruct((xs.shape[0], D), BF16),
        compiler_params=_params("parallel", "arbitrary"),
    )(tab["bexp"], tab["bvalid"], xs, w_gu, w_gu, w_down)


def _combine_ln_kernel(chk_ref, blk_ref, first_ref, last_ref, start_ref, nwin_ref,
                       ys_ref, pc1_ref, pc2_ref, x_ref, g_ref, b_ref, o_ref, acc_ref):
    i = pl.program_id(0)
    R = ys_ref.shape[0]

    @pl.when(first_ref[i] == 1)
    def _():
        acc_ref[...] = jnp.zeros_like(acc_ref)

    def window(k, carry):
        r0 = pl.multiple_of(start_ref[i] + k * MOE_WIN, MOE_ALIGN)
        pc1 = pc1_ref[...]
        pc2 = pc2_ref[...]
        lane = lax.broadcasted_iota(jnp.int32, pc1.shape, 1)
        parts = []
        for t in range(MOE_WIN // LANES):
            col = (blk_ref[i] * R + r0 + t * LANES + lane).astype(F32)
            parts.append(jnp.where((pc1 - col) * (pc2 - col) == 0.0, 1.0, 0.0).astype(BF16))
        acc_ref[...] += _dot(jnp.concatenate(parts, axis=1), ys_ref[pl.ds(r0, MOE_WIN), :])
        return carry

    lax.fori_loop(0, nwin_ref[i], window, 0)

    @pl.when(last_ref[i] == 1)
    def _():
        o_ref[...] = _layer_norm(ALPHA * x_ref[...] + acc_ref[...], g_ref[...], b_ref[...])


def _combine_ln(ys, poscol1, poscol2, x, g, b, tab, n_items_max):
    T, D = x.shape
    R = MOE_ROWS
    chunk_map = lambda i, chk, blk, first, last, start, nwin: (chk[i], 0)
    const_map = lambda i, chk, blk, first, last, start, nwin: (0, 0)
    grid_spec = pltpu.PrefetchScalarGridSpec(
        num_scalar_prefetch=6,
        grid=(n_items_max,),
        in_specs=[
            pl.BlockSpec((R, D), lambda i, chk, blk, first, last, start, nwin: (blk[i], 0)),
            pl.BlockSpec((R, LANES), chunk_map),
            pl.BlockSpec((R, LANES), chunk_map),
            pl.BlockSpec((R, D), chunk_map),
            pl.BlockSpec((1, D), const_map),
            pl.BlockSpec((1, D), const_map),
        ],
        out_specs=pl.BlockSpec((R, D), chunk_map),
        scratch_shapes=[pltpu.VMEM((R, D), F32)],
    )
    return pl.pallas_call(
        _combine_ln_kernel,
        grid_spec=grid_spec,
        out_shape=jax.ShapeDtypeStruct((T, D), F32),
        compiler_params=_params("arbitrary"),
    )(tab["c_chk"], tab["c_blk"], tab["c_first"], tab["c_last"], tab["c_start"], tab["c_nwin"],
      ys, poscol1, poscol2, x, g.reshape(1, D), b.reshape(1, D))


def _moe_ln(x, router, w_gu, w_down, g, b):
    T, D = x.shape
    R = MOE_ROWS
    n_exp = w_down.shape[0]
    n_chunks = T // R
    n_blocks_max = (2 * T) // R + n_exp
    n_items_max = n_blocks_max + n_exp * (n_chunks - 1)
    xb, info, cnt = _route(x, router)
    tab = _moe_tables(cnt[:, 0, :n_exp].astype(jnp.int32), n_blocks_max, n_items_max)
    field = lambda k: info[:, k, :].reshape(T).astype(jnp.int32)
    pos1 = tab["base"][field(0)] + field(2)
    pos2 = tab["base"][field(1)] + field(3)
    pos1 = pos1.astype(F32)
    pos2 = pos2.astype(F32)
    posrow = jnp.stack([pos1.reshape(n_chunks, R), pos2.reshape(n_chunks, R)], axis=1)
    posrow = jnp.pad(posrow, ((0, 0), (0, 6), (0, 0)), constant_values=-1.0)
    poscol1 = jnp.broadcast_to(pos1[:, None], (T, LANES))
    poscol2 = jnp.broadcast_to(pos2[:, None], (T, LANES))
    xs = _dispatch(xb, posrow, tab, n_blocks_max, n_items_max)
    ys = _expert_ffn(xs, w_gu, w_down, tab, n_blocks_max)
    return _combine_ln(ys, poscol1, poscol2, x, g, b, tab, n_items_max)


def kernel(x, ln_g, ln_b, pool_w, pool_scale, nsa_w_in, nsa_pe_k, nsa_w1_k, nsa_w2_k, nsa_pe_v, nsa_w1_v,
           nsa_w2_v, nsa_w_out, ffn_w_gu, ffn_w_down, moe_router, moe_w_gu, moe_w_down):
    B, S, D = x.shape
    T = B * S
    xa = _pool_ln(x, pool_w[0], pool_scale[0], ln_g[0, 0], ln_b[0, 0]).reshape(T, D)
    x1 = _ffn_ln(xa, ffn_w_gu[0].astype(BF16), ffn_w_down[0].astype(BF16), ln_g[0, 1], ln_b[0, 1])
    qt, kcv, ks, vst, kw, vwt, gatet = _in_proj(x1, nsa_w_in[0], B, S)
    kc, vct = _compress(kcv, nsa_w1_k[0], nsa_w2_k[0], nsa_pe_k[0], nsa_w1_v[0], nsa_w2_v[0], nsa_pe_v[0], B, S)
    oct, selt = _cmp_attn(qt, kc, vct, B, S)
    o = _sel_win_attn(qt, selt, ks, vst, kw, vwt, oct, gatet, B, S).reshape(T, D)
    x2 = _out_proj_ln(o, nsa_w_out[0], x1, ln_g[1, 0], ln_b[1, 0])
    y = _moe_ln(x2, moe_router[0], moe_w_gu[0].astype(BF16), moe_w_down[0].astype(BF16), ln_g[1, 1], ln_b[1, 1])
    return y.reshape(B, S, D)
```

```python
import functools

import jax
import jax.numpy as jnp
from jax import lax
from jax.experimental import pallas as pl
from jax.experimental.pallas import tpu as pltpu

D_MODEL = 1024
DEPTH = 2
POOL_WINDOWS = (2, 4, 8, 16)
POOL_GROUP_DIM = D_MODEL // len(POOL_WINDOWS)
POOL_HALO = 16
N_HEADS = 16
N_KV_GROUPS = 4
HEADS_PER_GROUP = N_HEADS // N_KV_GROUPS
HEAD_DIM = D_MODEL // N_HEADS
HALF_DIM = HEAD_DIM // 2
GROUP_Q_DIM = HEADS_PER_GROUP * HEAD_DIM
KV_DIM = N_KV_GROUPS * HEAD_DIM
N_BRANCHES = 3
N_GATES = N_BRANCHES * N_HEADS
CMP_STRIDE = 16
CMP_BLOCK = 2 * CMP_STRIDE
CMP_HIDDEN = 2 * HEAD_DIM
SEL_BLOCK = 64
SEL_SHIFT = SEL_BLOCK.bit_length() - 1
SEL_TOP_N = 16
WINDOW = 512
FORCE_BONUS = 1.0e3
NEG_INF = -1.0e30
ROPE_THETA = 10000.0
ATTN_SCALE = HEAD_DIM ** -0.5
LOG2E = 1.4426950408889634
Q_SCALE = ATTN_SCALE * LOG2E
AUG_DIM = 2 * HEAD_DIM
SEL_SLOTS = AUG_DIM - HEAD_DIM
GATE_SLOTS = 16
GATE_ROWS = N_KV_GROUPS * GATE_SLOTS
N_EXPERTS = 8
LN_EPS = 1e-5
ALPHA = (2 * DEPTH) ** 0.25

LANES = 128
SUBLANES = 8
BF16_SUBLANES = 16
VMEM_LIMIT_BYTES = 56 * 1024 * 1024

POOL_ROWS = 512
FFN_ROWS = 512
PROJ_ROWS = 512
PROJ_COLS = 256
CMP_Q_ROWS = 512
ATT_Q_ROWS = 512
ATT_K_ROWS = 512
ATT_HALF = 128
ATT_COLS = 512
ATT_AHEAD = 3
MOE_ROWS = 512
MOE_WIN = 256
MOE_ALIGN = BF16_SUBLANES
VT_ROWS = HEAD_DIM + BF16_SUBLANES

F32 = jnp.float32
BF16 = jnp.bfloat16


def _dot(a, b):
    return jnp.dot(a, b, preferred_element_type=F32)


def _dot_nt(a, b):
    return lax.dot_general(a, b, (((1,), (1,)), ((), ())), preferred_element_type=F32)


def _layer_norm(z, g, b):
    mu = jnp.mean(z, axis=-1, keepdims=True)
    zc = z - mu
    var = jnp.mean(zc * zc, axis=-1, keepdims=True)
    return zc * lax.rsqrt(var + LN_EPS) * g + b


def _silu(x):
    return x / (1.0 + jnp.exp(-x))


def _params(*semantics):
    return pltpu.CompilerParams(dimension_semantics=semantics, vmem_limit_bytes=VMEM_LIMIT_BYTES)


def _pool_ln_kernel(x_ref, halo_ref, w_ref, scale_ref, g_ref, b_ref, o_ref, ext_ref):
    i = pl.program_id(1)
    rows = x_ref.shape[1]
    x = x_ref[0]
    ext_ref[0:POOL_HALO, :] = jnp.where(i > 0, halo_ref[0], 0.0)
    ext_ref[POOL_HALO:, :] = x
    pos = i * rows + lax.broadcasted_iota(jnp.int32, (rows, 1), 0)
    ys = []
    run = ext_ref[...]
    span = 1
    for gi, w in enumerate(POOL_WINDOWS):
        while span < w:
            run = run + pltpu.roll(run, span, 0)
            span *= 2
        xg = x[:, gi * POOL_GROUP_DIM:(gi + 1) * POOL_GROUP_DIM]
        cnt = jnp.minimum(pos + 1, w).astype(F32)
        diff = run[POOL_HALO:, 0:POOL_GROUP_DIM] / cnt - xg
        ys.append(_dot(diff.astype(BF16), w_ref[gi]))
        run = run[:, POOL_GROUP_DIM:]
    h = jnp.concatenate(ys, axis=1) * scale_ref[...]
    o_ref[0] = _layer_norm(ALPHA * x + h, g_ref[...], b_ref[...])


def _pool_ln(x, w, scale, g, b):
    B, S, D = x.shape
    rows = POOL_ROWS
    assert all(a < b_ for a, b_ in zip(POOL_WINDOWS, POOL_WINDOWS[1:]))
    assert all(w_ & (w_ - 1) == 0 for w_ in POOL_WINDOWS) and POOL_WINDOWS[-1] <= POOL_HALO
    halo_blocks = rows // POOL_HALO
    row2 = lambda v: v.reshape(1, D)
    return pl.pallas_call(
        _pool_ln_kernel,
        grid=(B, S // rows),
        in_specs=[
            pl.BlockSpec((1, rows, D), lambda bi, i: (bi, i, 0)),
            pl.BlockSpec((1, POOL_HALO, D), lambda bi, i: (bi, jnp.maximum(i * halo_blocks - 1, 0), 0)),
            pl.BlockSpec(w.shape, lambda bi, i: (0, 0, 0)),
            pl.BlockSpec((1, D), lambda bi, i: (0, 0)),
            pl.BlockSpec((1, D), lambda bi, i: (0, 0)),
            pl.BlockSpec((1, D), lambda bi, i: (0, 0)),
        ],
        out_specs=pl.BlockSpec((1, rows, D), lambda bi, i: (bi, i, 0)),
        out_shape=jax.ShapeDtypeStruct((B, S, D), F32),
        scratch_shapes=[pltpu.VMEM((rows + POOL_HALO, D), F32)],
        compiler_params=_params("parallel", "arbitrary"),
    )(x, x, w.astype(BF16), row2(scale), row2(g), row2(b))


def _ffn_ln_kernel(x_ref, wg_ref, wu_ref, wd_ref, g_ref, b_ref, o_ref, xb_ref, acc_ref):
    f = pl.program_id(1)

    @pl.when(f == 0)
    def _():
        xb_ref[...] = x_ref[...].astype(BF16)
        acc_ref[...] = jnp.zeros_like(acc_ref)

    xb = xb_ref[...]
    a = _silu(_dot(xb, wg_ref[...])) * _dot(xb, wu_ref[...])
    acc_ref[...] += _dot(a.astype(BF16), wd_ref[...])

    @pl.when(f == pl.num_programs(1) - 1)
    def _():
        o_ref[...] = _layer_norm(ALPHA * x_ref[...] + acc_ref[...], g_ref[...], b_ref[...])


def _ff_chunk(d_ff, target):
    best = LANES
    for c in range(LANES, target + 1, LANES):
        if d_ff % c == 0:
            best = c
    return best


def _ffn_ln(x, w_gu, w_down, g, b):
    T, D = x.shape
    d_ff = w_down.shape[0]
    fc = _ff_chunk(d_ff, 1536)
    nf = d_ff // fc
    rows = FFN_ROWS
    return pl.pallas_call(
        _ffn_ln_kernel,
        grid=(T // rows, nf),
        in_specs=[
            pl.BlockSpec((rows, D), lambda i, f: (i, 0)),
            pl.BlockSpec((D, fc), lambda i, f: (0, f)),
            pl.BlockSpec((D, fc), lambda i, f: (0, nf + f)),
            pl.BlockSpec((fc, D), lambda i, f: (f, 0)),
            pl.BlockSpec((1, D), lambda i, f: (0, 0)),
            pl.BlockSpec((1, D), lambda i, f: (0, 0)),
        ],
        out_specs=pl.BlockSpec((rows, D), lambda i, f: (i, 0)),
        out_shape=jax.ShapeDtypeStruct((T, D), F32),
        scratch_shapes=[pltpu.VMEM((rows, D), BF16), pltpu.VMEM((rows, D), F32)],
        compiler_params=_params("parallel", "arbitrary"),
    )(x, w_gu, w_gu, w_down, g.reshape(1, D), b.reshape(1, D))


def _rope_tables(pos, reps):
    freqs = jnp.power(ROPE_THETA, -jnp.arange(HALF_DIM, dtype=F32) / HALF_DIM)
    ang = pos.astype(F32)[:, None] * freqs[None, :]
    cos, sin = jnp.cos(ang), jnp.sin(ang)
    return (jnp.tile(jnp.concatenate([cos, cos], axis=1), (1, reps)),
            jnp.tile(jnp.concatenate([-sin, sin], axis=1), (1, reps)))


def _in_proj_kernel(x_ref, w_ref, wt_ref, cos_ref, sin_ref, cost_ref, sint_ref,
                    qt_ref, kcv_ref, ks_ref, vst_ref, kw_ref, vwt_ref, gatet_ref, *, steps_per_seq):
    rows = x_ref.shape[0]
    xb = x_ref[...].astype(BF16)
    cos = cos_ref[...]
    sin = sin_ref[...]
    lane = lax.broadcasted_iota(jnp.int32, cos.shape, 1)
    first_half = (lane & (HEAD_DIM - 1)) < HALF_DIM
    seq_step = pl.program_id(0) % steps_per_seq
    pos = seq_step * rows + lax.broadcasted_iota(jnp.int32, (rows, HEAD_DIM), 0)
    col = lax.broadcasted_iota(jnp.int32, (rows, HEAD_DIM), 1)
    blk_onehot = jnp.where((pos >> SEL_SHIFT) == col, 1.0, 0.0)
    zeros = jnp.zeros((rows, HEAD_DIM), F32)

    def col_tile(j):
        return _dot(xb, w_ref[:, j * PROJ_COLS:(j + 1) * PROJ_COLS])

    def row_tile(r0, n):
        return _dot_nt(wt_ref[r0:r0 + n, :], xb)

    def rope(y):
        rot = jnp.where(first_half,
                        pltpu.roll(y, PROJ_COLS - HALF_DIM, 1),
                        pltpu.roll(y, HALF_DIM, 1))
        return y * cos + rot * sin

    def rope_t(yt):
        pieces = []
        for h in range(yt.shape[0] // HEAD_DIM):
            pieces.append(yt[h * HEAD_DIM + HALF_DIM:(h + 1) * HEAD_DIM])
            pieces.append(yt[h * HEAD_DIM:h * HEAD_DIM + HALF_DIM])
        return yt * cost_ref[...] + jnp.concatenate(pieces, axis=0) * sint_ref[...]

    def store_keys(ref, y, extra):
        for gi in range(N_KV_GROUPS):
            ref[0, gi, :, 0:HEAD_DIM] = y[:, gi * HEAD_DIM:(gi + 1) * HEAD_DIM].astype(ref.dtype)
            ref[0, gi, :, HEAD_DIM:] = extra.astype(ref.dtype)

    def store_values_t(ref, yt):
        ones = jnp.ones((VT_ROWS - HEAD_DIM, rows), ref.dtype)
        for gi in range(N_KV_GROUPS):
            ref[0, gi, 0, 0:HEAD_DIM, :] = yt[gi * HEAD_DIM:(gi + 1) * HEAD_DIM].astype(ref.dtype)
            ref[0, gi, 0, HEAD_DIM:, :] = ones

    for j in range(D_MODEL // PROJ_COLS):
        qt = rope_t(row_tile(j * PROJ_COLS, PROJ_COLS)) * Q_SCALE
        qt_ref[0, j * PROJ_COLS:(j + 1) * PROJ_COLS, :] = qt.astype(qt_ref.dtype)
    raw_k = col_tile(0)
    raw_v = col_tile(1)
    for gi in range(N_KV_GROUPS):
        kcv_ref[0, gi, :, 0:HEAD_DIM] = raw_k[:, gi * HEAD_DIM:(gi + 1) * HEAD_DIM]
        kcv_ref[0, gi, :, HEAD_DIM:] = raw_v[:, gi * HEAD_DIM:(gi + 1) * HEAD_DIM]
    store_keys(ks_ref, rope(col_tile(2)), blk_onehot)
    store_keys(kw_ref, rope(col_tile(3)), zeros)
    store_values_t(vst_ref, row_tile(D_MODEL, KV_DIM))
    store_values_t(vwt_ref, row_tile(D_MODEL + KV_DIM, KV_DIM))
    logits_t = row_tile(D_MODEL + 2 * KV_DIM, GATE_ROWS)
    gatet_ref[0] = 1.0 / (1.0 + jnp.exp(-logits_t))


def _in_proj(x, w_in, B, S):
    T, D = x.shape
    rows = PROJ_ROWS
    assert rows == ATT_K_ROWS
    steps_per_seq = S // rows
    sec = lambda k: w_in[:, D_MODEL + k * KV_DIM:D_MODEL + (k + 1) * KV_DIM]
    w = jnp.concatenate([sec(0), sec(1), sec(2), sec(4)], axis=1).astype(BF16)
    wg = w_in[:, D_MODEL + 6 * KV_DIM:].reshape(D, N_KV_GROUPS, N_BRANCHES * HEADS_PER_GROUP)
    wg = jnp.pad(wg, ((0, 0), (0, 0), (0, GATE_SLOTS - N_BRANCHES * HEADS_PER_GROUP))).reshape(D, GATE_ROWS)
    wt = jnp.concatenate([w_in[:, :D_MODEL], sec(3), sec(5), wg], axis=1).T.astype(BF16)
    cos, sin = _rope_tables(jnp.arange(S), PROJ_COLS // HEAD_DIM)
    k_shape = jax.ShapeDtypeStruct((B, N_KV_GROUPS, S, AUG_DIM), BF16)
    k_spec = pl.BlockSpec((1, N_KV_GROUPS, rows, AUG_DIM),
                          lambda i: (i // steps_per_seq, 0, i % steps_per_seq, 0))
    vt_shape = jax.ShapeDtypeStruct((B, N_KV_GROUPS, steps_per_seq, VT_ROWS, rows), BF16)
    vt_spec = pl.BlockSpec((1, N_KV_GROUPS, 1, VT_ROWS, rows),
                           lambda i: (i // steps_per_seq, 0, i % steps_per_seq, 0, 0))
    tok_map = lambda i: (i, 0)
    seq_map = lambda i: (i % steps_per_seq, 0)
    feat_map = lambda i: (i // steps_per_seq, 0, i % steps_per_seq)
    return pl.pallas_call(
        functools.partial(_in_proj_kernel, steps_per_seq=steps_per_seq),
        grid=(T // rows,),
        in_specs=[
            pl.BlockSpec((rows, D), tok_map),
            pl.BlockSpec(w.shape, lambda i: (0, 0)),
            pl.BlockSpec(wt.shape, lambda i: (0, 0)),
            pl.BlockSpec((rows, PROJ_COLS), seq_map),
            pl.BlockSpec((rows, PROJ_COLS), seq_map),
            pl.BlockSpec((PROJ_COLS, rows), lambda i: (0, i % steps_per_seq)),
            pl.BlockSpec((PROJ_COLS, rows), lambda i: (0, i % steps_per_seq)),
        ],
        out_specs=[
            pl.BlockSpec((1, D_MODEL, rows), feat_map),
            k_spec,
            k_spec, vt_spec, k_spec, vt_spec,
            pl.BlockSpec((1, GATE_ROWS, rows), feat_map),
        ],
        out_shape=[
            jax.ShapeDtypeStruct((B, D_MODEL, S), BF16),
            jax.ShapeDtypeStruct((B, N_KV_GROUPS, S, AUG_DIM), F32),
            k_shape, vt_shape, k_shape, vt_shape,
            jax.ShapeDtypeStruct((B, GATE_ROWS, S), F32),
        ],
        compiler_params=_params("parallel"),
    )(x, w, wt, cos, sin, cos.T, sin.T)


def _compress_kernel(kv_ref, w1_ref, pea_ref, peb_ref, w2_ref, w2t_ref, cos_ref, sin_ref, kc_ref, vct_ref):
    n = kc_ref.shape[2]
    hid = CMP_HIDDEN
    r = jnp.zeros((n, 4 * hid), F32)
    bias_a = jnp.zeros((SUBLANES, 4 * hid), F32)
    bias_b = jnp.zeros((SUBLANES, 4 * hid), F32)
    for p in range(CMP_STRIDE):
        slab = kv_ref[0, 0, pl.ds(p, n, stride=CMP_STRIDE), :].astype(BF16)
        r = r + _dot(slab, w1_ref[p])
        bias_a = bias_a + _dot(pea_ref[p], w1_ref[p])
        bias_b = bias_b + _dot(peb_ref[p], w1_ref[p])
    hidden = []
    for which in range(2):
        c0 = which * 2 * hid
        nxt = pltpu.roll(r[:, c0 + hid:c0 + 2 * hid], n - 1, 0)
        bias = bias_a[0:1, c0:c0 + hid] + bias_b[0:1, c0 + hid:c0 + 2 * hid]
        hidden.append(_silu(r[:, c0:c0 + hid] + nxt + bias).astype(BF16))
    kc = _dot(hidden[0], w2_ref[...])
    rot = jnp.concatenate([kc[:, HALF_DIM:], kc[:, :HALF_DIM]], axis=1)
    kc_ref[0, 0] = (kc * cos_ref[...] + rot * sin_ref[...]).astype(kc_ref.dtype)
    vct_ref[0, 0] = _dot_nt(w2t_ref[...], hidden[1]).astype(vct_ref.dtype)


def _compress(kv_raw, w1_k, w2_k, pe_k, w1_v, w2_v, pe_v, B, S):
    n_chunks = S // CMP_STRIDE
    hid = CMP_HIDDEN
    split = lambda w1: w1.reshape(2, CMP_STRIDE, HEAD_DIM, hid).transpose(1, 2, 0, 3).reshape(CMP_STRIDE, HEAD_DIM, 2 * hid)
    zeros = jnp.zeros((CMP_STRIDE, HEAD_DIM, 2 * hid), F32)
    w1 = jnp.concatenate([jnp.concatenate([split(w1_k), zeros], axis=2),
                          jnp.concatenate([zeros, split(w1_v)], axis=2)], axis=1).astype(BF16)
    pe = jnp.concatenate([pe_k, pe_v], axis=1)
    tile_rows = lambda v: jnp.broadcast_to(v[:, None, :], (CMP_STRIDE, SUBLANES, AUG_DIM)).astype(BF16)
    pea, peb = tile_rows(pe[:CMP_STRIDE]), tile_rows(pe[CMP_STRIDE:])
    cos, sin = _rope_tables(CMP_STRIDE * jnp.arange(n_chunks) + CMP_BLOCK - 1, 1)
    const = lambda a: pl.BlockSpec(a.shape, lambda bi, gi: (0,) * a.ndim)
    w2k = w2_k.astype(BF16)
    w2vt = w2_v.T.astype(BF16)
    return pl.pallas_call(
        _compress_kernel,
        grid=(B, N_KV_GROUPS),
        in_specs=[
            pl.BlockSpec((1, 1, S, AUG_DIM), lambda bi, gi: (bi, gi, 0, 0)),
            const(w1), const(pea), const(peb), const(w2k), const(w2vt), const(cos), const(sin),
        ],
        out_specs=[pl.BlockSpec((1, 1, n_chunks, HEAD_DIM), lambda bi, gi: (bi, gi, 0, 0)),
                   pl.BlockSpec((1, 1, HEAD_DIM, n_chunks), lambda bi, gi: (bi, gi, 0, 0))],
        out_shape=[jax.ShapeDtypeStruct((B, N_KV_GROUPS, n_chunks, HEAD_DIM), BF16),
                   jax.ShapeDtypeStruct((B, N_KV_GROUPS, HEAD_DIM, n_chunks), BF16)],
        compiler_params=_params("parallel", "parallel"),
    )(kv_raw, w1, pea, peb, w2k, w2vt, cos, sin)


def _cmp_attn_kernel(qt_ref, kc_ref, vct_ref, ovl_ref, oct_ref, selt_ref, score_ref, imp_ref):
    i = pl.program_id(2)
    cols = qt_ref.shape[2]
    n_cmp = kc_ref.shape[2]
    n_sel = ovl_ref.shape[0]
    t = i * cols + lax.broadcasted_iota(jnp.int32, (1, cols), 1)
    any_valid = jnp.where(t >= CMP_BLOCK - 1, 1.0, 0.0)

    def attend(n):
        kc = kc_ref[0, 0, 0:n, :]
        vct = vct_ref[0, 0, :, 0:n]
        cend = CMP_STRIDE * lax.broadcasted_iota(jnp.int32, (n, 1), 0) + (CMP_BLOCK - 1)
        cvalid = cend <= t
        pc_sum = jnp.zeros((n, cols), F32)
        for h in range(HEADS_PER_GROUP):
            qh = qt_ref[0, h * HEAD_DIM:(h + 1) * HEAD_DIM, :]
            s = jnp.where(cvalid, _dot(kc, qh), NEG_INF)
            e = jnp.exp2(s - jnp.max(s, axis=0, keepdims=True))
            pc = e * (any_valid / jnp.sum(e, axis=0, keepdims=True))
            oct_ref[0, h * HEAD_DIM:(h + 1) * HEAD_DIM, :] = _dot(vct, pc.astype(BF16))
            pc_sum = pc_sum + pc
        imp_ref[...] = _dot(ovl_ref[:, 0:n], pc_sum.astype(BF16))

    n_half = n_cmp // 2
    early = (i + 1) * cols <= n_half * CMP_STRIDE

    @pl.when(early)
    def _():
        attend(n_half)

    @pl.when(jnp.logical_not(early))
    def _():
        attend(n_cmp)

    imp = imp_ref[...]
    j = lax.broadcasted_iota(jnp.int32, (n_sel, 1), 0)
    blk_t = t >> SEL_SHIFT
    bvalid = j <= blk_t
    forced = (j == 0) | (j == blk_t) | (j == blk_t - 1)
    score = jnp.where(bvalid, imp + jnp.where(forced, FORCE_BONUS, 0.0), -1.0)
    key = pltpu.bitcast(score, jnp.int32)
    key_next = key + 1
    score_ref[...] = key
    def rank_group(grp, rank):
        base = pl.multiple_of(grp * SUBLANES, SUBLANES)
        others = score_ref[pl.ds(base, SUBLANES), :]
        for r in range(SUBLANES):
            ahead = others[r:r + 1, :] >= jnp.where(j > base + r, key, key_next)
            rank = rank + jnp.where(ahead, 1.0, 0.0)
        return rank

    n_groups = jnp.minimum(((i + 1) * cols) // (SUBLANES * SEL_BLOCK), n_sel // SUBLANES)
    rank = lax.fori_loop(0, n_groups, rank_group, jnp.zeros((n_sel, cols), F32))
    selected = (rank < float(SEL_TOP_N)) & bvalid
    selt_ref[0, 0] = jnp.where(selected, 0.0, NEG_INF).astype(selt_ref.dtype)


def _cmp_attn(qt, kc, vct, B, S):
    cols = CMP_Q_ROWS
    n_cmp = S // CMP_STRIDE
    n_sel = SEL_SLOTS
    assert S // SEL_BLOCK <= SEL_SLOTS
    cstart = CMP_STRIDE * jnp.arange(n_cmp)
    sstart = SEL_BLOCK * jnp.arange(n_sel)
    overlap = ((cstart[None, :] <= sstart[:, None] + SEL_BLOCK - 1)
               & (cstart[None, :] + CMP_BLOCK - 1 >= sstart[:, None])).astype(BF16)
    q_spec = pl.BlockSpec((1, GROUP_Q_DIM, cols), lambda bi, gi, i: (bi, gi, i))
    return pl.pallas_call(
        _cmp_attn_kernel,
        grid=(B, N_KV_GROUPS, S // cols),
        in_specs=[
            q_spec,
            pl.BlockSpec((1, 1, n_cmp, HEAD_DIM), lambda bi, gi, i: (bi, gi, 0, 0)),
            pl.BlockSpec((1, 1, HEAD_DIM, n_cmp), lambda bi, gi, i: (bi, gi, 0, 0)),
            pl.BlockSpec(overlap.shape, lambda bi, gi, i: (0, 0)),
        ],
        out_specs=[
            q_spec,
            pl.BlockSpec((1, 1, n_sel, cols), lambda bi, gi, i: (bi, gi, 0, i)),
        ],
        out_shape=[
            jax.ShapeDtypeStruct((B, D_MODEL, S), F32),
            jax.ShapeDtypeStruct((B, N_KV_GROUPS, n_sel, S), BF16),
        ],
        scratch_shapes=[pltpu.VMEM((n_sel, cols), jnp.int32), pltpu.VMEM((n_sel, cols), F32)],
        compiler_params=_params("parallel", "parallel", "arbitrary"),
    )(qt, kc, vct, overlap)


def _sel_win_attn_kernel(qt_ref, selt_ref, ks_ref, vst_ref, kw_ref, vwt_ref, diag_ref, band_ref, oct_ref, gatet_ref,
                         o_ref, qaug_ref, m_ref, acc_ref, ow_ref):
    i = pl.program_id(2)
    tq = qt_ref.shape[2]
    hg = HEADS_PER_GROUP
    half = ATT_HALF
    n_half = tq // half
    half_cols = hg * half

    for hq in range(n_half):
        for h in range(hg):
            c0 = (hq * hg + h) * half
            qaug_ref[0:HEAD_DIM, c0:c0 + half] = qt_ref[0, h * HEAD_DIM:(h + 1) * HEAD_DIM, hq * half:(hq + 1) * half]
            qaug_ref[HEAD_DIM:, c0:c0 + half] = selt_ref[0, 0, :, hq * half:(hq + 1) * half]
    n_tiles = (hg * tq) // ATT_COLS

    def keys(ref, start, size):
        return ref[0, 0, pl.ds(pl.multiple_of(start, ATT_HALF), size), :]

    m_ref[...] = jnp.full(m_ref.shape, NEG_INF, F32)
    acc_ref[...] = jnp.zeros(acc_ref.shape, F32)

    def sel_step(c, bias_ref):
        k = keys(ks_ref, c * ATT_K_ROWS, ATT_K_ROWS)
        v = vst_ref[0, 0, c]
        new_m, new_acc = [], []
        tile = lambda ct: slice(ct * ATT_COLS, (ct + 1) * ATT_COLS)
        last_part = lambda ct: ((ct + 1) * ATT_COLS - 1) // half_cols
        n_keys = lambda ct: ATT_K_ROWS if bias_ref is None else min(ATT_K_ROWS, (last_part(ct) + 1) * half)
        score = lambda ct: _dot(k[0:n_keys(ct)], qaug_ref[:, tile(ct)])
        scores = [score(ct) for ct in range(ATT_AHEAD)]
        for ct in range(n_tiles):
            cs = tile(ct)
            if ct + ATT_AHEAD < n_tiles:
                scores.append(score(ct + ATT_AHEAD))
            s = scores[ct]
            if bias_ref is not None:
                s = s + bias_ref[0:n_keys(ct), cs]
            m_prev = m_ref[:, cs]
            m_next = jnp.maximum(m_prev, jnp.max(s, axis=0, keepdims=True))
            p = jnp.exp2(s - m_next)
            alpha = jnp.exp2(m_prev - m_next)
            new_acc.append(alpha * acc_ref[:, cs] + _dot(v[:, 0:n_keys(ct)], p.astype(BF16)))
            new_m.append(m_next)
        for ct in range(n_tiles):
            cs = slice(ct * ATT_COLS, (ct + 1) * ATT_COLS)
            acc_ref[:, cs] = new_acc[ct]
            m_ref[:, cs] = new_m[ct]

    def sel_body(c, carry):
        sel_step(c, None)
        return carry

    lax.fori_loop(0, i, sel_body, 0)
    sel_step(i, diag_ref)

    def probs(s):
        return jnp.exp2(s - jnp.max(s, axis=0, keepdims=True)).astype(BF16)

    @pl.when(i == 0)
    def _():
        k = keys(kw_ref, 0, tq)
        v = vwt_ref[0, 0, 0]
        for ct in range(n_tiles):
            cs = slice(ct * ATT_COLS, (ct + 1) * ATT_COLS)
            ow_ref[:, cs] = _dot(v, probs(_dot(k, qaug_ref[:, cs]) + diag_ref[:, cs]))

    @pl.when(i > 0)
    def _():
        v_prev = vwt_ref[0, 0, i - 1]
        v_here = vwt_ref[0, 0, i]
        for hq in range(n_half):
            k = keys(kw_ref, i * tq + hq * half - WINDOW, WINDOW + half)
            n_prev = tq - hq * half
            for ct in range(half_cols // ATT_COLS):
                cs = slice(hq * half_cols + ct * ATT_COLS, hq * half_cols + (ct + 1) * ATT_COLS)
                p = probs(_dot(k, qaug_ref[:, cs]) + band_ref[:, ct * ATT_COLS:(ct + 1) * ATT_COLS])
                ow_ref[:, cs] = (_dot(v_prev[:, tq - n_prev:], p[0:n_prev])
                                 + _dot(v_here[:, 0:WINDOW + half - n_prev], p[n_prev:]))

    acc_s = acc_ref[...]
    acc_w = ow_ref[...]
    o_s = acc_s[0:HEAD_DIM] / acc_s[HEAD_DIM:HEAD_DIM + 1]
    o_w = acc_w[0:HEAD_DIM] / acc_w[HEAD_DIM:HEAD_DIM + 1]
    for hq in range(n_half):
        tok = slice(hq * half, (hq + 1) * half)
        for h in range(hg):
            c0 = (hq * hg + h) * half
            gc = gatet_ref[0, N_BRANCHES * h + 0:N_BRANCHES * h + 1, tok]
            gs = gatet_ref[0, N_BRANCHES * h + 1:N_BRANCHES * h + 2, tok]
            gw = gatet_ref[0, N_BRANCHES * h + 2:N_BRANCHES * h + 3, tok]
            o = (gc * oct_ref[0, h * HEAD_DIM:(h + 1) * HEAD_DIM, tok]
                 + gs * o_s[:, c0:c0 + half] + gw * o_w[:, c0:c0 + half])
            o_ref[0, tok, h * HEAD_DIM:(h + 1) * HEAD_DIM] = o.T.astype(o_ref.dtype)


def _sel_win_attn(qt, selt, ks, vst, kw, vwt, oct, gatet, B, S):
    tq = ATT_Q_ROWS
    half = ATT_HALF
    hg = HEADS_PER_GROUP
    cols = hg * tq
    n_chunks = S // ATT_K_ROWS
    t_rel = (jnp.arange(tq // half)[:, None, None] * half + jnp.arange(half)[None, None, :])
    t_rel = jnp.broadcast_to(t_rel, (tq // half, hg, half)).reshape(1, cols)
    diag = jnp.where(jnp.arange(ATT_K_ROWS)[:, None] <= t_rel, 0.0, NEG_INF).astype(F32)
    tt = jnp.broadcast_to(jnp.arange(half)[None, :], (hg, half)).reshape(1, hg * half)
    a = jnp.arange(WINDOW + half)[:, None]
    band = jnp.where((a > tt) & (a <= tt + WINDOW), 0.0, NEG_INF).astype(F32)
    k_spec = pl.BlockSpec((1, 1, S, AUG_DIM), lambda bi, gi, i: (bi, gi, 0, 0))
    vt_spec = pl.BlockSpec((1, 1, n_chunks, VT_ROWS, ATT_K_ROWS), lambda bi, gi, i: (bi, gi, 0, 0, 0))
    q_spec = pl.BlockSpec((1, GROUP_Q_DIM, tq), lambda bi, gi, i: (bi, gi, i))
    const = lambda arr: pl.BlockSpec(arr.shape, lambda bi, gi, i: (0, 0))
    return pl.pallas_call(
        _sel_win_attn_kernel,
        grid=(B, N_KV_GROUPS, S // tq),
        in_specs=[
            q_spec,
            pl.BlockSpec((1, 1, SEL_SLOTS, tq), lambda bi, gi, i: (bi, gi, 0, i)),
            k_spec, vt_spec, k_spec, vt_spec,
            const(diag), const(band),
            q_spec,
            pl.BlockSpec((1, GATE_SLOTS, tq), lambda bi, gi, i: (bi, gi, i)),
        ],
        out_specs=pl.BlockSpec((1, tq, GROUP_Q_DIM), lambda bi, gi, i: (bi, i, gi)),
        out_shape=jax.ShapeDtypeStruct((B, S, D_MODEL), BF16),
        scratch_shapes=[
            pltpu.VMEM((AUG_DIM, cols), BF16),
            pltpu.VMEM((1, cols), F32),
            pltpu.VMEM((VT_ROWS, cols), F32),
            pltpu.VMEM((VT_ROWS, cols), F32),
        ],
        compiler_params=_params("parallel", "parallel", "arbitrary"),
    )(qt, selt, ks, vst, kw, vwt, diag, band, oct, gatet)


def _out_proj_ln_kernel(o_ref, w_ref, x_ref, g_ref, b_ref, y_ref):
    h = _dot(o_ref[...], w_ref[...])
    y_ref[...] = _layer_norm(ALPHA * x_ref[...] + h, g_ref[...], b_ref[...])


def _out_proj_ln(o, w_out, x, g, b):
    T, D = x.shape
    rows = PROJ_ROWS
    return pl.pallas_call(
        _out_proj_ln_kernel,
        grid=(T // rows,),
        in_specs=[
            pl.BlockSpec((rows, D), lambda i: (i, 0)),
            pl.BlockSpec((D, D), lambda i: (0, 0)),
            pl.BlockSpec((rows, D), lambda i: (i, 0)),
            pl.BlockSpec((1, D), lambda i: (0, 0)),
            pl.BlockSpec((1, D), lambda i: (0, 0)),
        ],
        out_specs=pl.BlockSpec((rows, D), lambda i: (i, 0)),
        out_shape=jax.ShapeDtypeStruct((T, D), F32),
        compiler_params=_params("parallel"),
    )(o, w_out.astype(BF16), x, g.reshape(1, D), b.reshape(1, D))


def _route_kernel(x_ref, r_ref, tri_ref, xb_ref, info_ref, cnt_ref, carry_ref):
    @pl.when(pl.program_id(0) == 0)
    def _():
        carry_ref[...] = jnp.zeros_like(carry_ref)

    x = x_ref[...]
    xb = x.astype(BF16)
    xb_ref[:, 0:x.shape[1]] = xb
    x_lo = (x - xb.astype(F32)).astype(BF16)
    r_hi = r_ref[0]
    r_lo = r_ref[1]
    logits = _dot(xb, r_hi) + (_dot(x_lo, r_hi) + _dot(xb, r_lo))
    lane = lax.broadcasted_iota(jnp.int32, logits.shape, 1)
    logits = jnp.where(lane < N_EXPERTS, logits, -jnp.inf)
    v1 = jnp.max(logits, axis=-1, keepdims=True)
    i1 = jnp.min(jnp.where(logits == v1, lane, LANES), axis=-1, keepdims=True)
    rest = jnp.where(lane == i1, -jnp.inf, logits)
    v2 = jnp.max(rest, axis=-1, keepdims=True)
    i2 = jnp.min(jnp.where(rest == v2, lane, LANES), axis=-1, keepdims=True)
    e2 = jnp.exp(v2 - v1)
    w1 = 1.0 / (1.0 + e2)
    w2 = e2 / (1.0 + e2)
    m1 = jnp.where(lane == i1, 1.0, 0.0)
    m2 = jnp.where(lane == i2, 1.0, 0.0)
    routed = m1 + m2
    before = _dot(tri_ref[...], routed.astype(BF16)) + carry_ref[0:1, :]
    rank1 = jnp.sum(m1 * before, axis=-1, keepdims=True)
    rank2 = jnp.sum(m2 * before, axis=-1, keepdims=True)
    cnt = jnp.sum(routed, axis=0, keepdims=True)
    carry_ref[...] = carry_ref[...] + cnt
    cnt_ref[0] = jnp.broadcast_to(cnt, cnt_ref.shape[1:])
    info = jnp.where(lane == 0, i1.astype(F32),
                     jnp.where(lane == 1, i2.astype(F32),
                               jnp.where(lane == 2, rank1, jnp.where(lane == 3, rank2, 0.0))))
    info_ref[0] = info.T[0:SUBLANES, :]
    w1_hi = w1.astype(BF16).astype(F32)
    w2_hi = w2.astype(BF16).astype(F32)
    lo_half = (lane & 1) == 1
    wcol = (jnp.where((lane >> 1) == i1, jnp.where(lo_half, w1 - w1_hi, w1_hi), 0.0)
            + jnp.where((lane >> 1) == i2, jnp.where(lo_half, w2 - w2_hi, w2_hi), 0.0))
    xb_ref[:, x.shape[1]:] = wcol.astype(BF16)


def _route(x, router):
    T, D = x.shape
    rows = MOE_ROWS
    r = jnp.pad(router, ((0, 0), (0, LANES - router.shape[1])))
    r_hi = r.astype(BF16)
    r_lo = (r - r_hi.astype(F32)).astype(BF16)
    r2 = jnp.stack([r_hi, r_lo])
    tri = (jnp.arange(rows)[None, :] < jnp.arange(rows)[:, None]).astype(BF16)
    return pl.pallas_call(
        _route_kernel,
        grid=(T // rows,),
        in_specs=[
            pl.BlockSpec((rows, D), lambda i: (i, 0)),
            pl.BlockSpec(r2.shape, lambda i: (0, 0, 0)),
            pl.BlockSpec(tri.shape, lambda i: (0, 0)),
        ],
        out_specs=[
            pl.BlockSpec((rows, D + LANES), lambda i: (i, 0)),
            pl.BlockSpec((1, SUBLANES, rows), lambda i: (i, 0, 0)),
            pl.BlockSpec((1, SUBLANES, LANES), lambda i: (i, 0, 0)),
        ],
        out_shape=[
            jax.ShapeDtypeStruct((T, D + LANES), BF16),
            jax.ShapeDtypeStruct((T // rows, SUBLANES, rows), F32),
            jax.ShapeDtypeStruct((T // rows, SUBLANES, LANES), F32),
        ],
        scratch_shapes=[pltpu.VMEM((SUBLANES, LANES), F32)],
        compiler_params=_params("arbitrary"),
    )(x, r2, tri)


def _moe_tables(cnt, n_blocks_max, n_items_max):
    R = MOE_ROWS
    C, E = cnt.shape
    i32 = jnp.int32
    count_le = lambda sorted_v, q: jnp.sum(sorted_v[None, :] <= q[:, None], axis=1).astype(i32)

    def windows(ra, rb, live):
        near = jnp.minimum((ra // MOE_ALIGN) * MOE_ALIGN, R - MOE_WIN)
        fits = rb - near < MOE_WIN
        start = jnp.where(fits, near, (ra // MOE_WIN) * MOE_WIN)
        n = jnp.where(fits, 1, rb // MOE_WIN - ra // MOE_WIN + 1)
        return jnp.where(live, start, 0), jnp.where(live, n, 0)

    cum = jnp.concatenate([jnp.zeros((1, E), i32), jnp.cumsum(cnt, axis=0)], axis=0)
    tot = cum[-1]
    nb = (tot + R - 1) // R
    nb_end = jnp.cumsum(nb)
    blk_start = nb_end - nb
    n_blocks = nb_end[-1]
    b = jnp.minimum(jnp.arange(n_blocks_max, dtype=i32), n_blocks - 1)
    bexp = jnp.minimum(count_le(nb_end, b), E - 1)
    bvalid = jnp.arange(n_blocks_max, dtype=i32) < n_blocks
    lb = b - blk_start[bexp]
    rho0 = lb * R
    rho1 = jnp.minimum((lb + 1) * R, tot[bexp]) - 1
    cum_b = cum[1:, :][:, bexp].T
    lo = jnp.minimum(jnp.sum(cum_b <= rho0[:, None], axis=1).astype(i32), C - 1)
    hi = jnp.minimum(jnp.sum(cum_b <= rho1[:, None], axis=1).astype(i32), C - 1)
    nit = jnp.where(bvalid, hi - lo + 1, 0)
    it_end = jnp.cumsum(nit)
    it_start = it_end - nit
    n_items = it_end[-1]
    i = jnp.arange(n_items_max, dtype=i32)
    ic = jnp.minimum(i, n_items - 1)
    d_blk = jnp.minimum(count_le(it_end, ic), n_blocks_max - 1)
    d_chk = lo[d_blk] + ic - it_start[d_blk]
    d_valid = i < n_items
    d_first = d_valid & (ic == it_start[d_blk])
    d_last = d_valid & (ic == it_end[d_blk] - 1)
    d_exp = bexp[d_blk]
    d_ra = jnp.maximum(rho0[d_blk], cum[d_chk, d_exp]) - rho0[d_blk]
    d_rb = jnp.minimum(rho1[d_blk], cum[d_chk + 1, d_exp] - 1) - rho0[d_blk]
    d_start, d_nwin = windows(d_ra, d_rb, d_valid & (d_rb >= d_ra))
    is_fill = (i >= n_items) & (i < n_items + (n_blocks_max - n_blocks))
    d_blk = jnp.where(i < n_items, d_blk, jnp.minimum(n_blocks + i - n_items, n_blocks_max - 1))
    d_first = d_first | is_fill
    d_last = d_last | is_fill
    bl_lo = (blk_start[None, :] + cum[:-1] // R).reshape(-1)
    bl_hi = (blk_start[None, :] + (cum[1:] - 1) // R).reshape(-1)
    npair = jnp.where(cnt.reshape(-1) > 0, bl_hi - bl_lo + 1, 0)
    p_end = jnp.cumsum(npair)
    p_start = p_end - npair
    n_items2 = p_end[-1]
    jc = jnp.minimum(i, n_items2 - 1)
    pair = jnp.minimum(count_le(p_end, jc), C * E - 1)
    c_blk = bl_lo[pair] + jc - p_start[pair]
    c_chk = pair // E
    c_valid = i < n_items2
    prev_chk = jnp.concatenate([jnp.full((1,), -1, i32), c_chk[:-1]])
    next_chk = jnp.concatenate([c_chk[1:], jnp.full((1,), -1, i32)])
    next_valid = jnp.concatenate([c_valid[1:], jnp.zeros((1,), bool)])
    c_first = c_valid & (c_chk != prev_chk)
    c_last = c_valid & ((c_chk != next_chk) | ~next_valid)
    c_exp = pair % E
    c_base = (c_blk - blk_start[c_exp]) * R
    c_ra = jnp.maximum(cum[c_chk, c_exp] - c_base, 0)
    c_rb = jnp.minimum(cum[c_chk + 1, c_exp] - 1 - c_base, R - 1)
    c_start, c_nwin = windows(c_ra, c_rb, c_valid)
    as_i32 = lambda v: v.astype(i32)
    return dict(base=blk_start * R, bexp=bexp, bvalid=as_i32(bvalid),
                d_blk=d_blk, d_chk=d_chk, d_first=as_i32(d_first), d_last=as_i32(d_last), d_start=d_start, d_nwin=d_nwin,
                c_blk=c_blk, c_chk=c_chk, c_first=as_i32(c_first), c_last=as_i32(c_last),
                c_start=c_start, c_nwin=c_nwin)


def _dispatch_kernel(blk_ref, chk_ref, first_ref, last_ref, start_ref, nwin_ref, xb_ref, pos_ref,
                     xs_ref, acc_ref):
    i = pl.program_id(0)
    R = xs_ref.shape[0]

    @pl.when(first_ref[i] == 1)
    def _():
        acc_ref[...] = jnp.zeros_like(acc_ref)

    pos1 = pos_ref[0, 0:1, :]
    pos2 = pos_ref[0, 1:2, :]

    def window(k, carry):
        r0 = pl.multiple_of(start_ref[i] + k * MOE_WIN, MOE_ALIGN)
        rows = (blk_ref[i] * R + r0 + lax.broadcasted_iota(jnp.int32, (MOE_WIN, 1), 0)).astype(F32)
        hit = jnp.where((pos1 - rows) * (pos2 - rows) == 0.0, 1.0, 0.0).astype(BF16)
        acc_ref[pl.ds(r0, MOE_WIN), :] += _dot(hit, xb_ref[...])
        return carry

    lax.fori_loop(0, nwin_ref[i], window, 0)

    @pl.when(last_ref[i] == 1)
    def _():
        xs_ref[...] = acc_ref[...].astype(xs_ref.dtype)


def _dispatch(xb, posrow, tab, n_blocks_max, n_items_max):
    T, D = xb.shape
    R = MOE_ROWS
    chunk_map = lambda i, blk, chk, first, last, start, nwin: (chk[i], 0)
    block_map = lambda i, blk, chk, first, last, start, nwin: (blk[i], 0)
    grid_spec = pltpu.PrefetchScalarGridSpec(
        num_scalar_prefetch=6,
        grid=(n_items_max,),
        in_specs=[
            pl.BlockSpec((R, D), chunk_map),
            pl.BlockSpec((1, SUBLANES, R), lambda i, blk, chk, first, last, start, nwin: (chk[i], 0, 0)),
        ],
        out_specs=pl.BlockSpec((R, D), block_map),
        scratch_shapes=[pltpu.VMEM((R, D), F32)],
    )
    return pl.pallas_call(
        _dispatch_kernel,
        grid_spec=grid_spec,
        out_shape=jax.ShapeDtypeStruct((n_blocks_max * R, D), BF16),
        compiler_params=_params("arbitrary"),
    )(tab["d_blk"], tab["d_chk"], tab["d_first"], tab["d_last"], tab["d_start"], tab["d_nwin"], xb, posrow)


def _expert_ffn_kernel(bexp_ref, bvalid_ref, xs_ref, wg_ref, wu_ref, wd_ref, ys_ref, acc_ref):
    b = pl.program_id(0)
    f = pl.program_id(1)

    @pl.when(f == 0)
    def _():
        acc_ref[...] = jnp.zeros_like(acc_ref)

    @pl.when(bvalid_ref[b] == 1)
    def _():
        xb = xs_ref[:, 0:ys_ref.shape[1]]
        a = _silu(_dot(xb, wg_ref[0])) * _dot(xb, wu_ref[0])
        acc_ref[...] += _dot(a.astype(BF16), wd_ref[0])

    @pl.when(f == pl.num_programs(1) - 1)
    def _():
        ws = xs_ref[:, ys_ref.shape[1]:].astype(F32)
        lane = lax.broadcasted_iota(jnp.int32, ws.shape, 1)
        w = jnp.sum(jnp.where((lane >> 1) == bexp_ref[b], ws, 0.0), axis=-1, keepdims=True)
        ys_ref[...] = (acc_ref[...] * w).astype(ys_ref.dtype)


def _expert_ffn(xs, w_gu, w_down, tab, n_blocks_max):
    D = w_down.shape[2]
    R = MOE_ROWS
    d_ff = w_down.shape[1]
    fc = _ff_chunk(d_ff, 1792)
    nf = d_ff // fc
    f_eff = lambda b, f, bvalid: f * bvalid[b] + (nf - 1) * (1 - bvalid[b])
    grid_spec = pltpu.PrefetchScalarGridSpec(
        num_scalar_prefetch=2,
        grid=(n_blocks_max, nf),
        in_specs=[
            pl.BlockSpec((R, D + LANES), lambda b, f, bexp, bvalid: (b, 0)),
            pl.BlockSpec((1, D, fc), lambda b, f, bexp, bvalid: (bexp[b], 0, f_eff(b, f, bvalid))),
            pl.BlockSpec((1, D, fc), lambda b, f, bexp, bvalid: (bexp[b], 0, nf + f_eff(b, f, bvalid))),
            pl.BlockSpec((1, fc, D), lambda b, f, bexp, bvalid: (bexp[b], f_eff(b, f, bvalid), 0)),
        ],
        out_specs=pl.BlockSpec((R, D), lambda b, f, bexp, bvalid: (b, 0)),
        scratch_shapes=[pltpu.VMEM((R, D), F32)],
    )
    return pl.pallas_call(
        _expert_ffn_kernel,
        grid_spec=grid_spec,
        out_shape=jax.ShapeDtypeStruct((xs.shape[0], D), BF16),
        compiler_params=_params("parallel", "arbitrary"),
    )(tab["bexp"], tab["bvalid"], xs, w_gu, w_gu, w_down)


def _combine_ln_kernel(chk_ref, blk_ref, first_ref, last_ref, start_ref, nwin_ref,
                       ys_ref, pc1_ref, pc2_ref, x_ref, g_ref, b_ref, o_ref, acc_ref):
    i = pl.program_id(0)
    R = ys_ref.shape[0]

    @pl.when(first_ref[i] == 1)
    def _():
        acc_ref[...] = jnp.zeros_like(acc_ref)

    def window(k, carry):
        r0 = pl.multiple_of(start_ref[i] + k * MOE_WIN, MOE_ALIGN)
        pc1 = pc1_ref[...]
        pc2 = pc2_ref[...]
        lane = lax.broadcasted_iota(jnp.int32, pc1.shape, 1)
        parts = []
        for t in range(MOE_WIN // LANES):
            col = (blk_ref[i] * R + r0 + t * LANES + lane).astype(F32)
            parts.append(jnp.where((pc1 - col) * (pc2 - col) == 0.0, 1.0, 0.0).astype(BF16))
        acc_ref[...] += _dot(jnp.concatenate(parts, axis=1), ys_ref[pl.ds(r0, MOE_WIN), :])
        return carry

    lax.fori_loop(0, nwin_ref[i], window, 0)

    @pl.when(last_ref[i] == 1)
    def _():
        o_ref[...] = _layer_norm(ALPHA * x_ref[...] + acc_ref[...], g_ref[...], b_ref[...])


def _combine_ln(ys, poscol1, poscol2, x, g, b, tab, n_items_max):
    T, D = x.shape
    R = MOE_ROWS
    chunk_map = lambda i, chk, blk, first, last, start, nwin: (chk[i], 0)
    const_map = lambda i, chk, blk, first, last, start, nwin: (0, 0)
    grid_spec = pltpu.PrefetchScalarGridSpec(
        num_scalar_prefetch=6,
        grid=(n_items_max,),
        in_specs=[
            pl.BlockSpec((R, D), lambda i, chk, blk, first, last, start, nwin: (blk[i], 0)),
            pl.BlockSpec((R, LANES), chunk_map),
            pl.BlockSpec((R, LANES), chunk_map),
            pl.BlockSpec((R, D), chunk_map),
            pl.BlockSpec((1, D), const_map),
            pl.BlockSpec((1, D), const_map),
        ],
        out_specs=pl.BlockSpec((R, D), chunk_map),
        scratch_shapes=[pltpu.VMEM((R, D), F32)],
    )
    return pl.pallas_call(
        _combine_ln_kernel,
        grid_spec=grid_spec,
        out_shape=jax.ShapeDtypeStruct((T, D), F32),
        compiler_params=_params("arbitrary"),
    )(tab["c_chk"], tab["c_blk"], tab["c_first"], tab["c_last"], tab["c_start"], tab["c_nwin"],
      ys, poscol1, poscol2, x, g.reshape(1, D), b.reshape(1, D))


def _moe_ln(x, router, w_gu, w_down, g, b):
    T, D = x.shape
    R = MOE_ROWS
    n_exp = w_down.shape[0]
    n_chunks = T // R
    n_blocks_max = (2 * T) // R + n_exp
    n_items_max = n_blocks_max + n_exp * (n_chunks - 1)
    xb, info, cnt = _route(x, router)
    tab = _moe_tables(cnt[:, 0, :n_exp].astype(jnp.int32), n_blocks_max, n_items_max)
    field = lambda k: info[:, k, :].reshape(T).astype(jnp.int32)
    pos1 = tab["base"][field(0)] + field(2)
    pos2 = tab["base"][field(1)] + field(3)
    pos1 = pos1.astype(F32)
    pos2 = pos2.astype(F32)
    posrow = jnp.stack([pos1.reshape(n_chunks, R), pos2.reshape(n_chunks, R)], axis=1)
    posrow = jnp.pad(posrow, ((0, 0), (0, SUBLANES - 2), (0, 0)), constant_values=-1.0)
    poscol1 = jnp.broadcast_to(pos1[:, None], (T, LANES))
    poscol2 = jnp.broadcast_to(pos2[:, None], (T, LANES))
    xs = _dispatch(xb, posrow, tab, n_blocks_max, n_items_max)
    ys = _expert_ffn(xs, w_gu, w_down, tab, n_blocks_max)
    return _combine_ln(ys, poscol1, poscol2, x, g, b, tab, n_items_max)


def kernel(x, ln_g, ln_b, pool_w, pool_scale, nsa_w_in, nsa_pe_k, nsa_w1_k, nsa_w2_k, nsa_pe_v, nsa_w1_v,
           nsa_w2_v, nsa_w_out, ffn_w_gu, ffn_w_down, moe_router, moe_w_gu, moe_w_down):
    B, S, D = x.shape
    T = B * S
    xa = _pool_ln(x, pool_w[0], pool_scale[0], ln_g[0, 0], ln_b[0, 0]).reshape(T, D)
    x1 = _ffn_ln(xa, ffn_w_gu[0].astype(BF16), ffn_w_down[0].astype(BF16), ln_g[0, 1], ln_b[0, 1])
    qt, kcv, ks, vst, kw, vwt, gatet = _in_proj(x1, nsa_w_in[0], B, S)
    kc, vct = _compress(kcv, nsa_w1_k[0], nsa_w2_k[0], nsa_pe_k[0], nsa_w1_v[0], nsa_w2_v[0], nsa_pe_v[0], B, S)
    oct, selt = _cmp_attn(qt, kc, vct, B, S)
    o = _sel_win_attn(qt, selt, ks, vst, kw, vwt, oct, gatet, B, S).reshape(T, D)
    x2 = _out_proj_ln(o, nsa_w_out[0], x1, ln_g[1, 0], ln_b[1, 0])
    y = _moe_ln(x2, moe_router[0], moe_w_gu[0].astype(BF16), moe_w_down[0].astype(BF16), ln_g[1, 1], ln_b[1, 1])
    return y.reshape(B, S, D)
```

```python
import functools

import jax
import jax.numpy as jnp
from jax import lax
from jax.experimental import pallas as pl
from jax.experimental.pallas import tpu as pltpu

D_MODEL = 1024
DEPTH = 2
POOL_WINDOWS = (2, 4, 8, 16)
POOL_GROUP_DIM = D_MODEL // len(POOL_WINDOWS)
POOL_HALO = 16
N_HEADS = 16
N_KV_GROUPS = 4
HEADS_PER_GROUP = N_HEADS // N_KV_GROUPS
HEAD_DIM = D_MODEL // N_HEADS
HALF_DIM = HEAD_DIM // 2
GROUP_Q_DIM = HEADS_PER_GROUP * HEAD_DIM
KV_DIM = N_KV_GROUPS * HEAD_DIM
N_BRANCHES = 3
N_GATES = N_BRANCHES * N_HEADS
CMP_STRIDE = 16
CMP_BLOCK = 2 * CMP_STRIDE
CMP_HIDDEN = 2 * HEAD_DIM
SEL_BLOCK = 64
SEL_SHIFT = SEL_BLOCK.bit_length() - 1
SEL_TOP_N = 16
WINDOW = 512
FORCE_BONUS = 1.0e3
NEG_INF = -1.0e30
ROPE_THETA = 10000.0
ATTN_SCALE = HEAD_DIM ** -0.5
LOG2E = 1.4426950408889634
Q_SCALE = ATTN_SCALE * LOG2E
AUG_DIM = 2 * HEAD_DIM
SEL_SLOTS = AUG_DIM - HEAD_DIM
GATE_SLOTS = 16
GATE_ROWS = N_KV_GROUPS * GATE_SLOTS
N_EXPERTS = 8
LN_EPS = 1e-5
ALPHA = (2 * DEPTH) ** 0.25

LANES = 128
SUBLANES = 8
BF16_SUBLANES = 16
VMEM_LIMIT_BYTES = 56 * 1024 * 1024

POOL_ROWS = 512
FFN_ROWS = 512
PROJ_ROWS = 512
PROJ_COLS = 256
CMP_Q_ROWS = 512
ATT_Q_ROWS = 512
ATT_K_ROWS = 512
ATT_HALF = 128
ATT_COLS = 512
ATT_AHEAD = 3
MOE_ROWS = 512
MOE_WIN = 256
MOE_ALIGN = BF16_SUBLANES
MOE_RING = 3
VT_ROWS = HEAD_DIM + BF16_SUBLANES

F32 = jnp.float32
BF16 = jnp.bfloat16


def _dot(a, b):
    return jnp.dot(a, b, preferred_element_type=F32)


def _dot_nt(a, b):
    return lax.dot_general(a, b, (((1,), (1,)), ((), ())), preferred_element_type=F32)


def _layer_norm(z, g, b):
    mu = jnp.mean(z, axis=-1, keepdims=True)
    zc = z - mu
    var = jnp.mean(zc * zc, axis=-1, keepdims=True)
    return zc * lax.rsqrt(var + LN_EPS) * g + b


def _silu(x):
    return x / (1.0 + jnp.exp(-x))


def _params(*semantics):
    return pltpu.CompilerParams(dimension_semantics=semantics, vmem_limit_bytes=VMEM_LIMIT_BYTES)


def _pool_ln_kernel(x_ref, halo_ref, w_ref, scale_ref, g_ref, b_ref, o_ref, ext_ref):
    i = pl.program_id(1)
    rows = x_ref.shape[1]
    x = x_ref[0]
    ext_ref[0:POOL_HALO, :] = jnp.where(i > 0, halo_ref[0], 0.0)
    ext_ref[POOL_HALO:, :] = x
    pos = i * rows + lax.broadcasted_iota(jnp.int32, (rows, 1), 0)
    ys = []
    run = ext_ref[...]
    span = 1
    for gi, w in enumerate(POOL_WINDOWS):
        while span < w:
            run = run + pltpu.roll(run, span, 0)
            span *= 2
        xg = x[:, gi * POOL_GROUP_DIM:(gi + 1) * POOL_GROUP_DIM]
        cnt = jnp.minimum(pos + 1, w).astype(F32)
        diff = run[POOL_HALO:, 0:POOL_GROUP_DIM] / cnt - xg
        ys.append(_dot(diff.astype(BF16), w_ref[gi]))
        run = run[:, POOL_GROUP_DIM:]
    h = jnp.concatenate(ys, axis=1) * scale_ref[...]
    o_ref[0] = _layer_norm(ALPHA * x + h, g_ref[...], b_ref[...])


def _pool_ln(x, w, scale, g, b):
    B, S, D = x.shape
    rows = POOL_ROWS
    assert all(a < b_ for a, b_ in zip(POOL_WINDOWS, POOL_WINDOWS[1:]))
    assert all(w_ & (w_ - 1) == 0 for w_ in POOL_WINDOWS) and POOL_WINDOWS[-1] <= POOL_HALO
    halo_blocks = rows // POOL_HALO
    row2 = lambda v: v.reshape(1, D)
    return pl.pallas_call(
        _pool_ln_kernel,
        grid=(B, S // rows),
        in_specs=[
            pl.BlockSpec((1, rows, D), lambda bi, i: (bi, i, 0)),
            pl.BlockSpec((1, POOL_HALO, D), lambda bi, i: (bi, jnp.maximum(i * halo_blocks - 1, 0), 0)),
            pl.BlockSpec(w.shape, lambda bi, i: (0, 0, 0)),
            pl.BlockSpec((1, D), lambda bi, i: (0, 0)),
            pl.BlockSpec((1, D), lambda bi, i: (0, 0)),
            pl.BlockSpec((1, D), lambda bi, i: (0, 0)),
        ],
        out_specs=pl.BlockSpec((1, rows, D), lambda bi, i: (bi, i, 0)),
        out_shape=jax.ShapeDtypeStruct((B, S, D), F32),
        scratch_shapes=[pltpu.VMEM((rows + POOL_HALO, D), F32)],
        compiler_params=_params("parallel", "arbitrary"),
    )(x, x, w.astype(BF16), row2(scale), row2(g), row2(b))


def _ffn_ln_kernel(x_ref, wg_ref, wu_ref, wd_ref, g_ref, b_ref, o_ref, xb_ref, acc_ref):
    f = pl.program_id(1)

    @pl.when(f == 0)
    def _():
        xb_ref[...] = x_ref[...].astype(BF16)
        acc_ref[...] = jnp.zeros_like(acc_ref)

    xb = xb_ref[...]
    a = _silu(_dot(xb, wg_ref[...])) * _dot(xb, wu_ref[...])
    acc_ref[...] += _dot(a.astype(BF16), wd_ref[...])

    @pl.when(f == pl.num_programs(1) - 1)
    def _():
        o_ref[...] = _layer_norm(ALPHA * x_ref[...] + acc_ref[...], g_ref[...], b_ref[...])


def _ff_chunk(d_ff, target):
    best = LANES
    for c in range(LANES, target + 1, LANES):
        if d_ff % c == 0:
            best = c
    return best


def _ffn_ln(x, w_gu, w_down, g, b):
    T, D = x.shape
    d_ff = w_down.shape[0]
    fc = _ff_chunk(d_ff, 1536)
    nf = d_ff // fc
    rows = FFN_ROWS
    return pl.pallas_call(
        _ffn_ln_kernel,
        grid=(T // rows, nf),
        in_specs=[
            pl.BlockSpec((rows, D), lambda i, f: (i, 0)),
            pl.BlockSpec((D, fc), lambda i, f: (0, f)),
            pl.BlockSpec((D, fc), lambda i, f: (0, nf + f)),
            pl.BlockSpec((fc, D), lambda i, f: (f, 0)),
            pl.BlockSpec((1, D), lambda i, f: (0, 0)),
            pl.BlockSpec((1, D), lambda i, f: (0, 0)),
        ],
        out_specs=pl.BlockSpec((rows, D), lambda i, f: (i, 0)),
        out_shape=jax.ShapeDtypeStruct((T, D), F32),
        scratch_shapes=[pltpu.VMEM((rows, D), BF16), pltpu.VMEM((rows, D), F32)],
        compiler_params=_params("parallel", "arbitrary"),
    )(x, w_gu, w_gu, w_down, g.reshape(1, D), b.reshape(1, D))


def _rope_tables(pos, reps):
    freqs = jnp.power(ROPE_THETA, -jnp.arange(HALF_DIM, dtype=F32) / HALF_DIM)
    ang = pos.astype(F32)[:, None] * freqs[None, :]
    cos, sin = jnp.cos(ang), jnp.sin(ang)
    return (jnp.tile(jnp.concatenate([cos, cos], axis=1), (1, reps)),
            jnp.tile(jnp.concatenate([-sin, sin], axis=1), (1, reps)))


def _in_proj_kernel(x_ref, w_ref, wt_ref, cos_ref, sin_ref, cost_ref, sint_ref,
                    qt_ref, kcv_ref, ks_ref, vst_ref, kw_ref, vwt_ref, gatet_ref, *, steps_per_seq):
    rows = x_ref.shape[0]
    xb = x_ref[...].astype(BF16)
    cos = cos_ref[...]
    sin = sin_ref[...]
    lane = lax.broadcasted_iota(jnp.int32, cos.shape, 1)
    first_half = (lane & (HEAD_DIM - 1)) < HALF_DIM
    seq_step = pl.program_id(0) % steps_per_seq
    pos = seq_step * rows + lax.broadcasted_iota(jnp.int32, (rows, HEAD_DIM), 0)
    col = lax.broadcasted_iota(jnp.int32, (rows, HEAD_DIM), 1)
    blk_onehot = jnp.where((pos >> SEL_SHIFT) == col, 1.0, 0.0)
    zeros = jnp.zeros((rows, HEAD_DIM), F32)

    def col_tile(j):
        return _dot(xb, w_ref[:, j * PROJ_COLS:(j + 1) * PROJ_COLS])

    def row_tile(r0, n):
        return _dot_nt(wt_ref[r0:r0 + n, :], xb)

    def rope(y):
        rot = jnp.where(first_half,
                        pltpu.roll(y, PROJ_COLS - HALF_DIM, 1),
                        pltpu.roll(y, HALF_DIM, 1))
        return y * cos + rot * sin

    def rope_t(yt):
        pieces = []
        for h in range(yt.shape[0] // HEAD_DIM):
            pieces.append(yt[h * HEAD_DIM + HALF_DIM:(h + 1) * HEAD_DIM])
            pieces.append(yt[h * HEAD_DIM:h * HEAD_DIM + HALF_DIM])
        return yt * cost_ref[...] + jnp.concatenate(pieces, axis=0) * sint_ref[...]

    def store_keys(ref, y, extra):
        for gi in range(N_KV_GROUPS):
            ref[0, gi, :, 0:HEAD_DIM] = y[:, gi * HEAD_DIM:(gi + 1) * HEAD_DIM].astype(ref.dtype)
            ref[0, gi, :, HEAD_DIM:] = extra.astype(ref.dtype)

    def store_values_t(ref, yt):
        ones = jnp.ones((VT_ROWS - HEAD_DIM, rows), ref.dtype)
        for gi in range(N_KV_GROUPS):
            ref[0, gi, 0, 0:HEAD_DIM, :] = yt[gi * HEAD_DIM:(gi + 1) * HEAD_DIM].astype(ref.dtype)
            ref[0, gi, 0, HEAD_DIM:, :] = ones

    for j in range(D_MODEL // PROJ_COLS):
        qt = rope_t(row_tile(j * PROJ_COLS, PROJ_COLS)) * Q_SCALE
        qt_ref[0, j * PROJ_COLS:(j + 1) * PROJ_COLS, :] = qt.astype(qt_ref.dtype)
    raw_k = col_tile(0)
    raw_v = col_tile(1)
    for gi in range(N_KV_GROUPS):
        kcv_ref[0, gi, :, 0:HEAD_DIM] = raw_k[:, gi * HEAD_DIM:(gi + 1) * HEAD_DIM]
        kcv_ref[0, gi, :, HEAD_DIM:] = raw_v[:, gi * HEAD_DIM:(gi + 1) * HEAD_DIM]
    store_keys(ks_ref, rope(col_tile(2)), blk_onehot)
    store_keys(kw_ref, rope(col_tile(3)), zeros)
    store_values_t(vst_ref, row_tile(D_MODEL, KV_DIM))
    store_values_t(vwt_ref, row_tile(D_MODEL + KV_DIM, KV_DIM))
    logits_t = row_tile(D_MODEL + 2 * KV_DIM, GATE_ROWS)
    gatet_ref[0] = 1.0 / (1.0 + jnp.exp(-logits_t))


def _in_proj(x, w_in, B, S):
    T, D = x.shape
    rows = PROJ_ROWS
    assert rows == ATT_K_ROWS
    steps_per_seq = S // rows
    sec = lambda k: w_in[:, D_MODEL + k * KV_DIM:D_MODEL + (k + 1) * KV_DIM]
    w = jnp.concatenate([sec(0), sec(1), sec(2), sec(4)], axis=1).astype(BF16)
    wg = w_in[:, D_MODEL + 6 * KV_DIM:].reshape(D, N_KV_GROUPS, N_BRANCHES * HEADS_PER_GROUP)
    wg = jnp.pad(wg, ((0, 0), (0, 0), (0, GATE_SLOTS - N_BRANCHES * HEADS_PER_GROUP))).reshape(D, GATE_ROWS)
    wt = jnp.concatenate([w_in[:, :D_MODEL], sec(3), sec(5), wg], axis=1).T.astype(BF16)
    cos, sin = _rope_tables(jnp.arange(S), PROJ_COLS // HEAD_DIM)
    k_shape = jax.ShapeDtypeStruct((B, N_KV_GROUPS, S, AUG_DIM), BF16)
    k_spec = pl.BlockSpec((1, N_KV_GROUPS, rows, AUG_DIM),
                          lambda i: (i // steps_per_seq, 0, i % steps_per_seq, 0))
    vt_shape = jax.ShapeDtypeStruct((B, N_KV_GROUPS, steps_per_seq, VT_ROWS, rows), BF16)
    vt_spec = pl.BlockSpec((1, N_KV_GROUPS, 1, VT_ROWS, rows),
                           lambda i: (i // steps_per_seq, 0, i % steps_per_seq, 0, 0))
    tok_map = lambda i: (i, 0)
    seq_map = lambda i: (i % steps_per_seq, 0)
    feat_map = lambda i: (i // steps_per_seq, 0, i % steps_per_seq)
    return pl.pallas_call(
        functools.partial(_in_proj_kernel, steps_per_seq=steps_per_seq),
        grid=(T // rows,),
        in_specs=[
            pl.BlockSpec((rows, D), tok_map),
            pl.BlockSpec(w.shape, lambda i: (0, 0)),
            pl.BlockSpec(wt.shape, lambda i: (0, 0)),
            pl.BlockSpec((rows, PROJ_COLS), seq_map),
            pl.BlockSpec((rows, PROJ_COLS), seq_map),
            pl.BlockSpec((PROJ_COLS, rows), lambda i: (0, i % steps_per_seq)),
            pl.BlockSpec((PROJ_COLS, rows), lambda i: (0, i % steps_per_seq)),
        ],
        out_specs=[
            pl.BlockSpec((1, D_MODEL, rows), feat_map),
            k_spec,
            k_spec, vt_spec, k_spec, vt_spec,
            pl.BlockSpec((1, GATE_ROWS, rows), feat_map),
        ],
        out_shape=[
            jax.ShapeDtypeStruct((B, D_MODEL, S), BF16),
            jax.ShapeDtypeStruct((B, N_KV_GROUPS, S, AUG_DIM), F32),
            k_shape, vt_shape, k_shape, vt_shape,
            jax.ShapeDtypeStruct((B, GATE_ROWS, S), F32),
        ],
        compiler_params=_params("parallel"),
    )(x, w, wt, cos, sin, cos.T, sin.T)


def _compress_kernel(kv_ref, w1_ref, pea_ref, peb_ref, w2_ref, w2t_ref, cos_ref, sin_ref, kc_ref, vct_ref):
    n = kc_ref.shape[2]
    hid = CMP_HIDDEN
    r = jnp.zeros((n, 4 * hid), F32)
    bias_a = jnp.zeros((SUBLANES, 4 * hid), F32)
    bias_b = jnp.zeros((SUBLANES, 4 * hid), F32)
    for p in range(CMP_STRIDE):
        slab = kv_ref[0, 0, pl.ds(p, n, stride=CMP_STRIDE), :].astype(BF16)
        r = r + _dot(slab, w1_ref[p])
        bias_a = bias_a + _dot(pea_ref[p], w1_ref[p])
        bias_b = bias_b + _dot(peb_ref[p], w1_ref[p])
    hidden = []
    for which in range(2):
        c0 = which * 2 * hid
        nxt = pltpu.roll(r[:, c0 + hid:c0 + 2 * hid], n - 1, 0)
        bias = bias_a[0:1, c0:c0 + hid] + bias_b[0:1, c0 + hid:c0 + 2 * hid]
        hidden.append(_silu(r[:, c0:c0 + hid] + nxt + bias).astype(BF16))
    kc = _dot(hidden[0], w2_ref[...])
    rot = jnp.concatenate([kc[:, HALF_DIM:], kc[:, :HALF_DIM]], axis=1)
    kc_ref[0, 0] = (kc * cos_ref[...] + rot * sin_ref[...]).astype(kc_ref.dtype)
    vct_ref[0, 0] = _dot_nt(w2t_ref[...], hidden[1]).astype(vct_ref.dtype)


def _compress(kv_raw, w1_k, w2_k, pe_k, w1_v, w2_v, pe_v, B, S):
    n_chunks = S // CMP_STRIDE
    hid = CMP_HIDDEN
    split = lambda w1: w1.reshape(2, CMP_STRIDE, HEAD_DIM, hid).transpose(1, 2, 0, 3).reshape(CMP_STRIDE, HEAD_DIM, 2 * hid)
    zeros = jnp.zeros((CMP_STRIDE, HEAD_DIM, 2 * hid), F32)
    w1 = jnp.concatenate([jnp.concatenate([split(w1_k), zeros], axis=2),
                          jnp.concatenate([zeros, split(w1_v)], axis=2)], axis=1).astype(BF16)
    pe = jnp.concatenate([pe_k, pe_v], axis=1)
    tile_rows = lambda v: jnp.broadcast_to(v[:, None, :], (CMP_STRIDE, SUBLANES, AUG_DIM)).astype(BF16)
    pea, peb = tile_rows(pe[:CMP_STRIDE]), tile_rows(pe[CMP_STRIDE:])
    cos, sin = _rope_tables(CMP_STRIDE * jnp.arange(n_chunks) + CMP_BLOCK - 1, 1)
    const = lambda a: pl.BlockSpec(a.shape, lambda bi, gi: (0,) * a.ndim)
    w2k = w2_k.astype(BF16)
    w2vt = w2_v.T.astype(BF16)
    return pl.pallas_call(
        _compress_kernel,
        grid=(B, N_KV_GROUPS),
        in_specs=[
            pl.BlockSpec((1, 1, S, AUG_DIM), lambda bi, gi: (bi, gi, 0, 0)),
            const(w1), const(pea), const(peb), const(w2k), const(w2vt), const(cos), const(sin),
        ],
        out_specs=[pl.BlockSpec((1, 1, n_chunks, HEAD_DIM), lambda bi, gi: (bi, gi, 0, 0)),
                   pl.BlockSpec((1, 1, HEAD_DIM, n_chunks), lambda bi, gi: (bi, gi, 0, 0))],
        out_shape=[jax.ShapeDtypeStruct((B, N_KV_GROUPS, n_chunks, HEAD_DIM), BF16),
                   jax.ShapeDtypeStruct((B, N_KV_GROUPS, HEAD_DIM, n_chunks), BF16)],
        compiler_params=_params("parallel", "parallel"),
    )(kv_raw, w1, pea, peb, w2k, w2vt, cos, sin)


def _cmp_attn_kernel(qt_ref, kc_ref, vct_ref, ovl_ref, oct_ref, selt_ref, score_ref, imp_ref):
    i = pl.program_id(2)
    cols = qt_ref.shape[2]
    n_cmp = kc_ref.shape[2]
    n_sel = ovl_ref.shape[0]
    t = i * cols + lax.broadcasted_iota(jnp.int32, (1, cols), 1)
    any_valid = jnp.where(t >= CMP_BLOCK - 1, 1.0, 0.0)

    def attend(n):
        kc = kc_ref[0, 0, 0:n, :]
        vct = vct_ref[0, 0, :, 0:n]
        cend = CMP_STRIDE * lax.broadcasted_iota(jnp.int32, (n, 1), 0) + (CMP_BLOCK - 1)
        cvalid = cend <= t
        pc_sum = jnp.zeros((n, cols), F32)
        for h in range(HEADS_PER_GROUP):
            qh = qt_ref[0, h * HEAD_DIM:(h + 1) * HEAD_DIM, :]
            s = jnp.where(cvalid, _dot(kc, qh), NEG_INF)
            e = jnp.exp2(s - jnp.max(s, axis=0, keepdims=True))
            pc = e * (any_valid / jnp.sum(e, axis=0, keepdims=True))
            oct_ref[0, h * HEAD_DIM:(h + 1) * HEAD_DIM, :] = _dot(vct, pc.astype(BF16))
            pc_sum = pc_sum + pc
        imp_ref[...] = _dot(ovl_ref[:, 0:n], pc_sum.astype(BF16))

    n_half = n_cmp // 2
    early = (i + 1) * cols <= n_half * CMP_STRIDE

    @pl.when(early)
    def _():
        attend(n_half)

    @pl.when(jnp.logical_not(early))
    def _():
        attend(n_cmp)

    imp = imp_ref[...]
    j = lax.broadcasted_iota(jnp.int32, (n_sel, 1), 0)
    blk_t = t >> SEL_SHIFT
    bvalid = j <= blk_t
    forced = (j == 0) | (j == blk_t) | (j == blk_t - 1)
    score = jnp.where(bvalid, imp + jnp.where(forced, FORCE_BONUS, 0.0), -1.0)
    key = pltpu.bitcast(score, jnp.int32)
    key_next = key + 1
    score_ref[...] = key
    def rank_group(grp, rank):
        base = pl.multiple_of(grp * SUBLANES, SUBLANES)
        others = score_ref[pl.ds(base, SUBLANES), :]
        for r in range(SUBLANES):
            ahead = others[r:r + 1, :] >= jnp.where(j > base + r, key, key_next)
            rank = rank + jnp.where(ahead, 1.0, 0.0)
        return rank

    n_groups = jnp.minimum(((i + 1) * cols) // (SUBLANES * SEL_BLOCK), n_sel // SUBLANES)
    rank = lax.fori_loop(0, n_groups, rank_group, jnp.zeros((n_sel, cols), F32))
    selected = (rank < float(SEL_TOP_N)) & bvalid
    selt_ref[0, 0] = jnp.where(selected, 0.0, NEG_INF).astype(selt_ref.dtype)


def _cmp_attn(qt, kc, vct, B, S):
    cols = CMP_Q_ROWS
    n_cmp = S // CMP_STRIDE
    n_sel = SEL_SLOTS
    assert S // SEL_BLOCK <= SEL_SLOTS
    cstart = CMP_STRIDE * jnp.arange(n_cmp)
    sstart = SEL_BLOCK * jnp.arange(n_sel)
    overlap = ((cstart[None, :] <= sstart[:, None] + SEL_BLOCK - 1)
               & (cstart[None, :] + CMP_BLOCK - 1 >= sstart[:, None])).astype(BF16)
    q_spec = pl.BlockSpec((1, GROUP_Q_DIM, cols), lambda bi, gi, i: (bi, gi, i))
    return pl.pallas_call(
        _cmp_attn_kernel,
        grid=(B, N_KV_GROUPS, S // cols),
        in_specs=[
            q_spec,
            pl.BlockSpec((1, 1, n_cmp, HEAD_DIM), lambda bi, gi, i: (bi, gi, 0, 0)),
            pl.BlockSpec((1, 1, HEAD_DIM, n_cmp), lambda bi, gi, i: (bi, gi, 0, 0)),
            pl.BlockSpec(overlap.shape, lambda bi, gi, i: (0, 0)),
        ],
        out_specs=[
            q_spec,
            pl.BlockSpec((1, 1, n_sel, cols), lambda bi, gi, i: (bi, gi, 0, i)),
        ],
        out_shape=[
            jax.ShapeDtypeStruct((B, D_MODEL, S), F32),
            jax.ShapeDtypeStruct((B, N_KV_GROUPS, n_sel, S), BF16),
        ],
        scratch_shapes=[pltpu.VMEM((n_sel, cols), jnp.int32), pltpu.VMEM((n_sel, cols), F32)],
        compiler_params=_params("parallel", "parallel", "arbitrary"),
    )(qt, kc, vct, overlap)


def _sel_win_attn_kernel(qt_ref, selt_ref, ks_ref, vst_ref, kw_ref, vwt_ref, diag_ref, band_ref, oct_ref, gatet_ref,
                         o_ref, qaug_ref, m_ref, acc_ref, ow_ref):
    i = pl.program_id(2)
    tq = qt_ref.shape[2]
    hg = HEADS_PER_GROUP
    half = ATT_HALF
    n_half = tq // half
    half_cols = hg * half

    for hq in range(n_half):
        for h in range(hg):
            c0 = (hq * hg + h) * half
            qaug_ref[0:HEAD_DIM, c0:c0 + half] = qt_ref[0, h * HEAD_DIM:(h + 1) * HEAD_DIM, hq * half:(hq + 1) * half]
            qaug_ref[HEAD_DIM:, c0:c0 + half] = selt_ref[0, 0, :, hq * half:(hq + 1) * half]
    n_tiles = (hg * tq) // ATT_COLS

    def keys(ref, start, size):
        return ref[0, 0, pl.ds(pl.multiple_of(start, ATT_HALF), size), :]

    m_ref[...] = jnp.full(m_ref.shape, NEG_INF, F32)
    acc_ref[...] = jnp.zeros(acc_ref.shape, F32)

    def sel_step(c, bias_ref):
        k = keys(ks_ref, c * ATT_K_ROWS, ATT_K_ROWS)
        v = vst_ref[0, 0, c]
        new_m, new_acc = [], []
        tile = lambda ct: slice(ct * ATT_COLS, (ct + 1) * ATT_COLS)
        last_part = lambda ct: ((ct + 1) * ATT_COLS - 1) // half_cols
        n_keys = lambda ct: ATT_K_ROWS if bias_ref is None else min(ATT_K_ROWS, (last_part(ct) + 1) * half)
        score = lambda ct: _dot(k[0:n_keys(ct)], qaug_ref[:, tile(ct)])
        scores = [score(ct) for ct in range(ATT_AHEAD)]
        for ct in range(n_tiles):
            cs = tile(ct)
            if ct + ATT_AHEAD < n_tiles:
                scores.append(score(ct + ATT_AHEAD))
            s = scores[ct]
            if bias_ref is not None:
                s = s + bias_ref[0:n_keys(ct), cs]
            m_prev = m_ref[:, cs]
            m_next = jnp.maximum(m_prev, jnp.max(s, axis=0, keepdims=True))
            p = jnp.exp2(s - m_next)
            alpha = jnp.exp2(m_prev - m_next)
            new_acc.append(alpha * acc_ref[:, cs] + _dot(v[:, 0:n_keys(ct)], p.astype(BF16)))
            new_m.append(m_next)
        for ct in range(n_tiles):
            cs = slice(ct * ATT_COLS, (ct + 1) * ATT_COLS)
            acc_ref[:, cs] = new_acc[ct]
            m_ref[:, cs] = new_m[ct]

    def sel_body(c, carry):
        sel_step(c, None)
        return carry

    lax.fori_loop(0, i, sel_body, 0)
    sel_step(i, diag_ref)

    def probs(s):
        return jnp.exp2(s - jnp.max(s, axis=0, keepdims=True)).astype(BF16)

    @pl.when(i == 0)
    def _():
        k = keys(kw_ref, 0, tq)
        v = vwt_ref[0, 0, 0]
        for ct in range(n_tiles):
            cs = slice(ct * ATT_COLS, (ct + 1) * ATT_COLS)
            ow_ref[:, cs] = _dot(v, probs(_dot(k, qaug_ref[:, cs]) + diag_ref[:, cs]))

    @pl.when(i > 0)
    def _():
        v_prev = vwt_ref[0, 0, i - 1]
        v_here = vwt_ref[0, 0, i]
        for hq in range(n_half):
            k = keys(kw_ref, i * tq + hq * half - WINDOW, WINDOW + half)
            n_prev = tq - hq * half
            for ct in range(half_cols // ATT_COLS):
                cs = slice(hq * half_cols + ct * ATT_COLS, hq * half_cols + (ct + 1) * ATT_COLS)
                p = probs(_dot(k, qaug_ref[:, cs]) + band_ref[:, ct * ATT_COLS:(ct + 1) * ATT_COLS])
                ow_ref[:, cs] = (_dot(v_prev[:, tq - n_prev:], p[0:n_prev])
                                 + _dot(v_here[:, 0:WINDOW + half - n_prev], p[n_prev:]))

    acc_s = acc_ref[...]
    acc_w = ow_ref[...]
    o_s = acc_s[0:HEAD_DIM] / acc_s[HEAD_DIM:HEAD_DIM + 1]
    o_w = acc_w[0:HEAD_DIM] / acc_w[HEAD_DIM:HEAD_DIM + 1]
    for hq in range(n_half):
        tok = slice(hq * half, (hq + 1) * half)
        for h in range(hg):
            c0 = (hq * hg + h) * half
            gc = gatet_ref[0, N_BRANCHES * h + 0:N_BRANCHES * h + 1, tok]
            gs = gatet_ref[0, N_BRANCHES * h + 1:N_BRANCHES * h + 2, tok]
            gw = gatet_ref[0, N_BRANCHES * h + 2:N_BRANCHES * h + 3, tok]
            o = (gc * oct_ref[0, h * HEAD_DIM:(h + 1) * HEAD_DIM, tok]
                 + gs * o_s[:, c0:c0 + half] + gw * o_w[:, c0:c0 + half])
            o_ref[0, tok, h * HEAD_DIM:(h + 1) * HEAD_DIM] = o.T.astype(o_ref.dtype)


def _sel_win_attn(qt, selt, ks, vst, kw, vwt, oct, gatet, B, S):
    tq = ATT_Q_ROWS
    half = ATT_HALF
    hg = HEADS_PER_GROUP
    cols = hg * tq
    n_chunks = S // ATT_K_ROWS
    t_rel = (jnp.arange(tq // half)[:, None, None] * half + jnp.arange(half)[None, None, :])
    t_rel = jnp.broadcast_to(t_rel, (tq // half, hg, half)).reshape(1, cols)
    diag = jnp.where(jnp.arange(ATT_K_ROWS)[:, None] <= t_rel, 0.0, NEG_INF).astype(F32)
    tt = jnp.broadcast_to(jnp.arange(half)[None, :], (hg, half)).reshape(1, hg * half)
    a = jnp.arange(WINDOW + half)[:, None]
    band = jnp.where((a > tt) & (a <= tt + WINDOW), 0.0, NEG_INF).astype(F32)
    k_spec = pl.BlockSpec((1, 1, S, AUG_DIM), lambda bi, gi, i: (bi, gi, 0, 0))
    vt_spec = pl.BlockSpec((1, 1, n_chunks, VT_ROWS, ATT_K_ROWS), lambda bi, gi, i: (bi, gi, 0, 0, 0))
    q_spec = pl.BlockSpec((1, GROUP_Q_DIM, tq), lambda bi, gi, i: (bi, gi, i))
    const = lambda arr: pl.BlockSpec(arr.shape, lambda bi, gi, i: (0, 0))
    return pl.pallas_call(
        _sel_win_attn_kernel,
        grid=(B, N_KV_GROUPS, S // tq),
        in_specs=[
            q_spec,
            pl.BlockSpec((1, 1, SEL_SLOTS, tq), lambda bi, gi, i: (bi, gi, 0, i)),
            k_spec, vt_spec, k_spec, vt_spec,
            const(diag), const(band),
            q_spec,
            pl.BlockSpec((1, GATE_SLOTS, tq), lambda bi, gi, i: (bi, gi, i)),
        ],
        out_specs=pl.BlockSpec((1, tq, GROUP_Q_DIM), lambda bi, gi, i: (bi, i, gi)),
        out_shape=jax.ShapeDtypeStruct((B, S, D_MODEL), BF16),
        scratch_shapes=[
            pltpu.VMEM((AUG_DIM, cols), BF16),
            pltpu.VMEM((1, cols), F32),
            pltpu.VMEM((VT_ROWS, cols), F32),
            pltpu.VMEM((VT_ROWS, cols), F32),
        ],
        compiler_params=_params("parallel", "parallel", "arbitrary"),
    )(qt, selt, ks, vst, kw, vwt, diag, band, oct, gatet)


def _out_proj_ln_kernel(o_ref, w_ref, x_ref, g_ref, b_ref, y_ref):
    h = _dot(o_ref[...], w_ref[...])
    y_ref[...] = _layer_norm(ALPHA * x_ref[...] + h, g_ref[...], b_ref[...])


def _out_proj_ln(o, w_out, x, g, b):
    T, D = x.shape
    rows = PROJ_ROWS
    return pl.pallas_call(
        _out_proj_ln_kernel,
        grid=(T // rows,),
        in_specs=[
            pl.BlockSpec((rows, D), lambda i: (i, 0)),
            pl.BlockSpec((D, D), lambda i: (0, 0)),
            pl.BlockSpec((rows, D), lambda i: (i, 0)),
            pl.BlockSpec((1, D), lambda i: (0, 0)),
            pl.BlockSpec((1, D), lambda i: (0, 0)),
        ],
        out_specs=pl.BlockSpec((rows, D), lambda i: (i, 0)),
        out_shape=jax.ShapeDtypeStruct((T, D), F32),
        compiler_params=_params("parallel"),
    )(o, w_out.astype(BF16), x, g.reshape(1, D), b.reshape(1, D))


def _route_kernel(x_ref, r_ref, tri_ref, xb_ref, info_ref, cnt_ref, carry_ref):
    @pl.when(pl.program_id(0) == 0)
    def _():
        carry_ref[...] = jnp.zeros_like(carry_ref)

    x = x_ref[...]
    xb = x.astype(BF16)
    xb_ref[:, 0:x.shape[1]] = xb
    x_lo = (x - xb.astype(F32)).astype(BF16)
    r_hi = r_ref[0]
    r_lo = r_ref[1]
    logits = _dot(xb, r_hi) + (_dot(x_lo, r_hi) + _dot(xb, r_lo))
    lane = lax.broadcasted_iota(jnp.int32, logits.shape, 1)
    logits = jnp.where(lane < N_EXPERTS, logits, -jnp.inf)
    v1 = jnp.max(logits, axis=-1, keepdims=True)
    i1 = jnp.min(jnp.where(logits == v1, lane, LANES), axis=-1, keepdims=True)
    rest = jnp.where(lane == i1, -jnp.inf, logits)
    v2 = jnp.max(rest, axis=-1, keepdims=True)
    i2 = jnp.min(jnp.where(rest == v2, lane, LANES), axis=-1, keepdims=True)
    e2 = jnp.exp(v2 - v1)
    w1 = 1.0 / (1.0 + e2)
    w2 = e2 / (1.0 + e2)
    m1 = jnp.where(lane == i1, 1.0, 0.0)
    m2 = jnp.where(lane == i2, 1.0, 0.0)
    routed = m1 + m2
    before = _dot(tri_ref[...], routed.astype(BF16)) + carry_ref[0:1, :]
    rank1 = jnp.sum(m1 * before, axis=-1, keepdims=True)
    rank2 = jnp.sum(m2 * before, axis=-1, keepdims=True)
    cnt = jnp.sum(routed, axis=0, keepdims=True)
    carry_ref[...] = carry_ref[...] + cnt
    cnt_ref[0] = jnp.broadcast_to(cnt, cnt_ref.shape[1:])
    info = jnp.where(lane == 0, i1.astype(F32),
                     jnp.where(lane == 1, i2.astype(F32),
                               jnp.where(lane == 2, rank1, jnp.where(lane == 3, rank2, 0.0))))
    info_ref[0] = info.T[0:SUBLANES, :]
    w1_hi = w1.astype(BF16).astype(F32)
    w2_hi = w2.astype(BF16).astype(F32)
    lo_half = (lane & 1) == 1
    wcol = (jnp.where((lane >> 1) == i1, jnp.where(lo_half, w1 - w1_hi, w1_hi), 0.0)
            + jnp.where((lane >> 1) == i2, jnp.where(lo_half, w2 - w2_hi, w2_hi), 0.0))
    xb_ref[:, x.shape[1]:] = wcol.astype(BF16)


def _route(x, router):
    T, D = x.shape
    rows = MOE_ROWS
    r = jnp.pad(router, ((0, 0), (0, LANES - router.shape[1])))
    r_hi = r.astype(BF16)
    r_lo = (r - r_hi.astype(F32)).astype(BF16)
    r2 = jnp.stack([r_hi, r_lo])
    tri = (jnp.arange(rows)[None, :] < jnp.arange(rows)[:, None]).astype(BF16)
    return pl.pallas_call(
        _route_kernel,
        grid=(T // rows,),
        in_specs=[
            pl.BlockSpec((rows, D), lambda i: (i, 0)),
            pl.BlockSpec(r2.shape, lambda i: (0, 0, 0)),
            pl.BlockSpec(tri.shape, lambda i: (0, 0)),
        ],
        out_specs=[
            pl.BlockSpec((rows, D + LANES), lambda i: (i, 0)),
            pl.BlockSpec((1, SUBLANES, rows), lambda i: (i, 0, 0)),
            pl.BlockSpec((1, SUBLANES, LANES), lambda i: (i, 0, 0)),
        ],
        out_shape=[
            jax.ShapeDtypeStruct((T, D + LANES), BF16),
            jax.ShapeDtypeStruct((T // rows, SUBLANES, rows), F32),
            jax.ShapeDtypeStruct((T // rows, SUBLANES, LANES), F32),
        ],
        scratch_shapes=[pltpu.VMEM((SUBLANES, LANES), F32)],
        compiler_params=_params("arbitrary"),
    )(x, r2, tri)


def _moe_tables(cnt, n_blocks_max, n_items_max):
    R = MOE_ROWS
    C, E = cnt.shape
    i32 = jnp.int32
    count_le = lambda sorted_v, q: jnp.sum(sorted_v[None, :] <= q[:, None], axis=1).astype(i32)

    def windows(ra, rb, live):
        near = jnp.minimum((ra // MOE_ALIGN) * MOE_ALIGN, R - MOE_WIN)
        fits = rb - near < MOE_WIN
        start = jnp.where(fits, near, (ra // MOE_WIN) * MOE_WIN)
        n = jnp.where(fits, 1, rb // MOE_WIN - ra // MOE_WIN + 1)
        return jnp.where(live, start, 0), jnp.where(live, n, 0)

    cum = jnp.concatenate([jnp.zeros((1, E), i32), jnp.cumsum(cnt, axis=0)], axis=0)
    tot = cum[-1]
    nb = (tot + R - 1) // R
    nb_end = jnp.cumsum(nb)
    blk_start = nb_end - nb
    n_blocks = nb_end[-1]
    b = jnp.minimum(jnp.arange(n_blocks_max, dtype=i32), n_blocks - 1)
    bexp = jnp.minimum(count_le(nb_end, b), E - 1)
    bvalid = jnp.arange(n_blocks_max, dtype=i32) < n_blocks
    lb = b - blk_start[bexp]
    rho0 = lb * R
    rho1 = jnp.minimum((lb + 1) * R, tot[bexp]) - 1
    cum_b = cum[1:, :][:, bexp].T
    lo = jnp.minimum(jnp.sum(cum_b <= rho0[:, None], axis=1).astype(i32), C - 1)
    hi = jnp.minimum(jnp.sum(cum_b <= rho1[:, None], axis=1).astype(i32), C - 1)
    nit = jnp.where(bvalid, hi - lo + 1, 0)
    it_end = jnp.cumsum(nit)
    it_start = it_end - nit
    n_items = it_end[-1]
    i = jnp.arange(n_items_max, dtype=i32)
    ic = jnp.minimum(i, n_items - 1)
    d_blk = jnp.minimum(count_le(it_end, ic), n_blocks_max - 1)
    d_chk = lo[d_blk] + ic - it_start[d_blk]
    d_valid = i < n_items
    d_first = d_valid & (ic == it_start[d_blk])
    d_last = d_valid & (ic == it_end[d_blk] - 1)
    d_exp = bexp[d_blk]
    d_ra = jnp.maximum(rho0[d_blk], cum[d_chk, d_exp]) - rho0[d_blk]
    d_rb = jnp.minimum(rho1[d_blk], cum[d_chk + 1, d_exp] - 1) - rho0[d_blk]
    d_start, d_nwin = windows(d_ra, d_rb, d_valid & (d_rb >= d_ra))
    is_fill = (i >= n_items) & (i < n_items + (n_blocks_max - n_blocks))
    d_blk = jnp.where(i < n_items, d_blk, jnp.minimum(n_blocks + i - n_items, n_blocks_max - 1))
    d_first = d_first | is_fill
    d_last = d_last | is_fill
    bl_lo = (blk_start[None, :] + cum[:-1] // R).reshape(-1)
    bl_hi = (blk_start[None, :] + (cum[1:] - 1) // R).reshape(-1)
    npair = jnp.where(cnt.reshape(-1) > 0, bl_hi - bl_lo + 1, 0)
    p_end = jnp.cumsum(npair)
    p_start = p_end - npair
    n_items2 = p_end[-1]
    jc = jnp.minimum(i, n_items2 - 1)
    pair = jnp.minimum(count_le(p_end, jc), C * E - 1)
    c_blk = bl_lo[pair] + jc - p_start[pair]
    c_chk = pair // E
    c_valid = i < n_items2
    prev_chk = jnp.concatenate([jnp.full((1,), -1, i32), c_chk[:-1]])
    next_chk = jnp.concatenate([c_chk[1:], jnp.full((1,), -1, i32)])
    next_valid = jnp.concatenate([c_valid[1:], jnp.zeros((1,), bool)])
    c_first = c_valid & (c_chk != prev_chk)
    c_last = c_valid & ((c_chk != next_chk) | ~next_valid)
    c_exp = pair % E
    c_base = (c_blk - blk_start[c_exp]) * R
    c_ra = jnp.maximum(cum[c_chk, c_exp] - c_base, 0)
    c_rb = jnp.minimum(cum[c_chk + 1, c_exp] - 1 - c_base, R - 1)
    c_start, c_nwin = windows(c_ra, c_rb, c_valid)
    as_i32 = lambda v: v.astype(i32)
    return dict(base=blk_start * R, bexp=bexp, bvalid=as_i32(bvalid),
                d_blk=d_blk, d_chk=d_chk, d_first=as_i32(d_first), d_last=as_i32(d_last), d_start=d_start, d_nwin=d_nwin,
                c_blk=c_blk, c_chk=c_chk, c_first=as_i32(c_first), c_last=as_i32(c_last),
                c_start=c_start, c_nwin=c_nwin)


def _dispatch_kernel(blk_ref, chk_ref, first_ref, last_ref, start_ref, nwin_ref, xb_hbm, pos_ref,
                     xs_ref, acc_ref, ring_ref, sem_ref):
    i = pl.program_id(0)
    n_steps = pl.num_programs(0)
    R = xs_ref.shape[0]

    def chunk_copy(step, slot):
        src = xb_hbm.at[pl.ds(pl.multiple_of(chk_ref[step] * R, R), R), :]
        return pltpu.make_async_copy(src, ring_ref.at[slot], sem_ref.at[slot])

    @pl.when(i == 0)
    def _():
        for k in range(MOE_RING - 1):
            @pl.when(k < n_steps)
            def _(k=k):
                chunk_copy(k, k).start()

    ahead = i + MOE_RING - 1

    @pl.when(ahead < n_steps)
    def _():
        chunk_copy(ahead, ahead % MOE_RING).start()

    slot = i % MOE_RING
    chunk_copy(i, slot).wait()
    xb_ref = ring_ref.at[slot]

    @pl.when(first_ref[i] == 1)
    def _():
        acc_ref[...] = jnp.zeros_like(acc_ref)

    pos1 = pos_ref[0, 0:1, :]
    pos2 = pos_ref[0, 1:2, :]

    def window(k, carry):
        r0 = pl.multiple_of(start_ref[i] + k * MOE_WIN, MOE_ALIGN)
        rows = (blk_ref[i] * R + r0 + lax.broadcasted_iota(jnp.int32, (MOE_WIN, 1), 0)).astype(F32)
        hit = jnp.where((pos1 - rows) * (pos2 - rows) == 0.0, 1.0, 0.0).astype(BF16)
        acc_ref[pl.ds(r0, MOE_WIN), :] += _dot(hit, xb_ref[...])
        return carry

    lax.fori_loop(0, nwin_ref[i], window, 0)

    @pl.when(last_ref[i] == 1)
    def _():
        xs_ref[...] = acc_ref[...].astype(xs_ref.dtype)


def _dispatch(xb, posrow, tab, n_blocks_max, n_items_max):
    T, D = xb.shape
    R = MOE_ROWS
    chunk_map = lambda i, blk, chk, first, last, start, nwin: (chk[i], 0)
    block_map = lambda i, blk, chk, first, last, start, nwin: (blk[i], 0)
    grid_spec = pltpu.PrefetchScalarGridSpec(
        num_scalar_prefetch=6,
        grid=(n_items_max,),
        in_specs=[
            pl.BlockSpec(memory_space=pl.ANY),
            pl.BlockSpec((1, SUBLANES, R), lambda i, blk, chk, first, last, start, nwin: (chk[i], 0, 0)),
        ],
        out_specs=pl.BlockSpec((R, D), block_map),
        scratch_shapes=[pltpu.VMEM((R, D), F32), pltpu.VMEM((MOE_RING, R, D), BF16),
                        pltpu.SemaphoreType.DMA((MOE_RING,))],
    )
    return pl.pallas_call(
        _dispatch_kernel,
        grid_spec=grid_spec,
        out_shape=jax.ShapeDtypeStruct((n_blocks_max * R, D), BF16),
        compiler_params=_params("arbitrary"),
    )(tab["d_blk"], tab["d_chk"], tab["d_first"], tab["d_last"], tab["d_start"], tab["d_nwin"], xb, posrow)


def _expert_ffn_kernel(bexp_ref, bvalid_ref, xs_ref, wg_ref, wu_ref, wd_ref, ys_ref, acc_ref):
    b = pl.program_id(0)
    f = pl.program_id(1)

    @pl.when(f == 0)
    def _():
        acc_ref[...] = jnp.zeros_like(acc_ref)

    @pl.when(bvalid_ref[b] == 1)
    def _():
        xb = xs_ref[:, 0:ys_ref.shape[1]]
        a = _silu(_dot(xb, wg_ref[0])) * _dot(xb, wu_ref[0])
        acc_ref[...] += _dot(a.astype(BF16), wd_ref[0])

    @pl.when(f == pl.num_programs(1) - 1)
    def _():
        ws = xs_ref[:, ys_ref.shape[1]:].astype(F32)
        lane = lax.broadcasted_iota(jnp.int32, ws.shape, 1)
        w = jnp.sum(jnp.where((lane >> 1) == bexp_ref[b], ws, 0.0), axis=-1, keepdims=True)
        ys_ref[...] = (acc_ref[...] * w).astype(ys_ref.dtype)


def _expert_ffn(xs, w_gu, w_down, tab, n_blocks_max):
    D = w_down.shape[2]
    R = MOE_ROWS
    d_ff = w_down.shape[1]
    fc = _ff_chunk(d_ff, 1792)
    nf = d_ff // fc
    f_eff = lambda b, f, bvalid: f * bvalid[b] + (nf - 1) * (1 - bvalid[b])
    grid_spec = pltpu.PrefetchScalarGridSpec(
        num_scalar_prefetch=2,
        grid=(n_blocks_max, nf),
        in_specs=[
            pl.BlockSpec((R, D + LANES), lambda b, f, bexp, bvalid: (b, 0)),
            pl.BlockSpec((1, D, fc), lambda b, f, bexp, bvalid: (bexp[b], 0, f_eff(b, f, bvalid))),
            pl.BlockSpec((1, D, fc), lambda b, f, bexp, bvalid: (bexp[b], 0, nf + f_eff(b, f, bvalid))),
            pl.BlockSpec((1, fc, D), lambda b, f, bexp, bvalid: (bexp[b], f_eff(b, f, bvalid), 0)),
        ],
        out_specs=pl.BlockSpec((R, D), lambda b, f, bexp, bvalid: (b, 0)),
        scratch_shapes=[pltpu.VMEM((R, D), F32)],
    )
    return pl.pallas_call(
        _expert_ffn_kernel,
        grid_spec=grid_spec,
        out_shape=jax.ShapeDtypeStruct((xs.shape[0], D), BF16),
        compiler_params=_params("parallel", "arbitrary"),
    )(tab["bexp"], tab["bvalid"], xs, w_gu, w_gu, w_down)


def _combine_ln_kernel(chk_ref, blk_ref, first_ref, last_ref, start_ref, nwin_ref,
                       ys_ref, pc1_ref, pc2_ref, x_ref, g_ref, b_ref, o_ref, acc_ref):
    i = pl.program_id(0)
    R = ys_ref.shape[0]

    @pl.when(first_ref[i] == 1)
    def _():
        acc_ref[...] = jnp.zeros_like(acc_ref)

    def window(k, carry):
        r0 = pl.multiple_of(start_ref[i] + k * MOE_WIN, MOE_ALIGN)
        pc1 = pc1_ref[...]
        pc2 = pc2_ref[...]
        lane = lax.broadcasted_iota(jnp.int32, pc1.shape, 1)
        parts = []
        for t in range(MOE_WIN // LANES):
            col = (blk_ref[i] * R + r0 + t * LANES + lane).astype(F32)
            parts.append(jnp.where((pc1 - col) * (pc2 - col) == 0.0, 1.0, 0.0).astype(BF16))
        acc_ref[...] += _dot(jnp.concatenate(parts, axis=1), ys_ref[pl.ds(r0, MOE_WIN), :])
        return carry

    lax.fori_loop(0, nwin_ref[i], window, 0)

    @pl.when(last_ref[i] == 1)
    def _():
        o_ref[...] = _layer_norm(ALPHA * x_ref[...] + acc_ref[...], g_ref[...], b_ref[...])


def _combine_ln(ys, poscol1, poscol2, x, g, b, tab, n_items_max):
    T, D = x.shape
    R = MOE_ROWS
    chunk_map = lambda i, chk, blk, first, last, start, nwin: (chk[i], 0)
    const_map = lambda i, chk, blk, first, last, start, nwin: (0, 0)
    grid_spec = pltpu.PrefetchScalarGridSpec(
        num_scalar_prefetch=6,
        grid=(n_items_max,),
        in_specs=[
            pl.BlockSpec((R, D), lambda i, chk, blk, first, last, start, nwin: (blk[i], 0)),
            pl.BlockSpec((R, LANES), chunk_map),
            pl.BlockSpec((R, LANES), chunk_map),
            pl.BlockSpec((R, D), chunk_map),
            pl.BlockSpec((1, D), const_map),
            pl.BlockSpec((1, D), const_map),
        ],
        out_specs=pl.BlockSpec((R, D), chunk_map),
        scratch_shapes=[pltpu.VMEM((R, D), F32)],
    )
    return pl.pallas_call(
        _combine_ln_kernel,
        grid_spec=grid_spec,
        out_shape=jax.ShapeDtypeStruct((T, D), F32),
        compiler_params=_params("arbitrary"),
    )(tab["c_chk"], tab["c_blk"], tab["c_first"], tab["c_last"], tab["c_start"], tab["c_nwin"],
      ys, poscol1, poscol2, x, g.reshape(1, D), b.reshape(1, D))


def _moe_ln(x, router, w_gu, w_down, g, b):
    T, D = x.shape
    R = MOE_ROWS
    n_exp = w_down.shape[0]
    n_chunks = T // R
    n_blocks_max = (2 * T) // R + n_exp
    n_items_max = n_blocks_max + n_exp * (n_chunks - 1)
    xb, info, cnt = _route(x, router)
    tab = _moe_tables(cnt[:, 0, :n_exp].astype(jnp.int32), n_blocks_max, n_items_max)
    field = lambda k: info[:, k, :].reshape(T).astype(jnp.int32)
    pos1 = tab["base"][field(0)] + field(2)
    pos2 = tab["base"][field(1)] + field(3)
    pos1 = pos1.astype(F32)
    pos2 = pos2.astype(F32)
    posrow = jnp.stack([pos1.reshape(n_chunks, R), pos2.reshape(n_chunks, R)], axis=1)
    posrow = jnp.pad(posrow, ((0, 0), (0, SUBLANES - 2), (0, 0)), constant_values=-1.0)
    poscol1 = jnp.broadcast_to(pos1[:, None], (T, LANES))
    poscol2 = jnp.broadcast_to(pos2[:, None], (T, LANES))
    xs = _dispatch(xb, posrow, tab, n_blocks_max, n_items_max)
    ys = _expert_ffn(xs, w_gu, w_down, tab, n_blocks_max)
    return _combine_ln(ys, poscol1, poscol2, x, g, b, tab, n_items_max)


def kernel(x, ln_g, ln_b, pool_w, pool_scale, nsa_w_in, nsa_pe_k, nsa_w1_k, nsa_w2_k, nsa_pe_v, nsa_w1_v,
           nsa_w2_v, nsa_w_out, ffn_w_gu, ffn_w_down, moe_router, moe_w_gu, moe_w_down):
    B, S, D = x.shape
    T = B * S
    xa = _pool_ln(x, pool_w[0], pool_scale[0], ln_g[0, 0], ln_b[0, 0]).reshape(T, D)
    x1 = _ffn_ln(xa, ffn_w_gu[0].astype(BF16), ffn_w_down[0].astype(BF16), ln_g[0, 1], ln_b[0, 1])
    qt, kcv, ks, vst, kw, vwt, gatet = _in_proj(x1, nsa_w_in[0], B, S)
    kc, vct = _compress(kcv, nsa_w1_k[0], nsa_w2_k[0], nsa_pe_k[0], nsa_w1_v[0], nsa_w2_v[0], nsa_pe_v[0], B, S)
    oct, selt = _cmp_attn(qt, kc, vct, B, S)
    o = _sel_win_attn(qt, selt, ks, vst, kw, vwt, oct, gatet, B, S).reshape(T, D)
    x2 = _out_proj_ln(o, nsa_w_out[0], x1, ln_g[1, 0], ln_b[1, 0])
    y = _moe_ln(x2, moe_router[0], moe_w_gu[0].astype(BF16), moe_w_down[0].astype(BF16), ln_g[1, 1], ln_b[1, 1])
    return y.reshape(B, S, D)
```

```python
import functools

import jax
import jax.numpy as jnp
from jax import lax
from jax.experimental import pallas as pl
from jax.experimental.pallas import tpu as pltpu

D_MODEL = 1024
DEPTH = 2
POOL_WINDOWS = (2, 4, 8, 16)
POOL_GROUP_DIM = D_MODEL // len(POOL_WINDOWS)
POOL_HALO = 16
N_HEADS = 16
N_KV_GROUPS = 4
HEADS_PER_GROUP = N_HEADS // N_KV_GROUPS
HEAD_DIM = D_MODEL // N_HEADS
HALF_DIM = HEAD_DIM // 2
GROUP_Q_DIM = HEADS_PER_GROUP * HEAD_DIM
KV_DIM = N_KV_GROUPS * HEAD_DIM
N_BRANCHES = 3
N_GATES = N_BRANCHES * N_HEADS
CMP_STRIDE = 16
CMP_BLOCK = 2 * CMP_STRIDE
CMP_HIDDEN = 2 * HEAD_DIM
SEL_BLOCK = 64
SEL_SHIFT = SEL_BLOCK.bit_length() - 1
SEL_TOP_N = 16
WINDOW = 512
FORCE_BONUS = 1.0e3
NEG_INF = -1.0e30
ROPE_THETA = 10000.0
ATTN_SCALE = HEAD_DIM ** -0.5
LOG2E = 1.4426950408889634
Q_SCALE = ATTN_SCALE * LOG2E
AUG_DIM = 2 * HEAD_DIM
SEL_SLOTS = AUG_DIM - HEAD_DIM
GATE_SLOTS = 16
GATE_ROWS = N_KV_GROUPS * GATE_SLOTS
N_EXPERTS = 8
LN_EPS = 1e-5
ALPHA = (2 * DEPTH) ** 0.25

LANES = 128
SUBLANES = 8
BF16_SUBLANES = 16
VMEM_LIMIT_BYTES = 56 * 1024 * 1024

POOL_ROWS = 512
FFN_ROWS = 512
PROJ_ROWS = 512
PROJ_COLS = 256
CMP_Q_ROWS = 512
ATT_Q_ROWS = 512
ATT_K_ROWS = 512
ATT_HALF = 128
ATT_COLS = 512
ATT_AHEAD = 3
MOE_ROWS = 512
MOE_WIN = 256
MOE_ALIGN = BF16_SUBLANES
MOE_RING = 3
VT_ROWS = HEAD_DIM + BF16_SUBLANES

F32 = jnp.float32
BF16 = jnp.bfloat16


def _dot(a, b):
    return jnp.dot(a, b, preferred_element_type=F32)


def _dot_nt(a, b):
    return lax.dot_general(a, b, (((1,), (1,)), ((), ())), preferred_element_type=F32)


def _layer_norm(z, g, b):
    mu = jnp.mean(z, axis=-1, keepdims=True)
    zc = z - mu
    var = jnp.mean(zc * zc, axis=-1, keepdims=True)
    return zc * lax.rsqrt(var + LN_EPS) * g + b


def _silu(x):
    return x / (1.0 + jnp.exp(-x))


def _params(*semantics):
    return pltpu.CompilerParams(dimension_semantics=semantics, vmem_limit_bytes=VMEM_LIMIT_BYTES)


def _pool_ln_kernel(x_ref, halo_ref, w_ref, scale_ref, g_ref, b_ref, o_ref, ext_ref):
    i = pl.program_id(1)
    rows = x_ref.shape[1]
    x = x_ref[0]
    ext_ref[0:POOL_HALO, :] = jnp.where(i > 0, halo_ref[0], 0.0)
    ext_ref[POOL_HALO:, :] = x
    pos = i * rows + lax.broadcasted_iota(jnp.int32, (rows, 1), 0)
    ys = []
    run = ext_ref[...]
    span = 1
    for gi, w in enumerate(POOL_WINDOWS):
        while span < w:
            run = run + pltpu.roll(run, span, 0)
            span *= 2
        xg = x[:, gi * POOL_GROUP_DIM:(gi + 1) * POOL_GROUP_DIM]
        cnt = jnp.minimum(pos + 1, w).astype(F32)
        diff = run[POOL_HALO:, 0:POOL_GROUP_DIM] / cnt - xg
        ys.append(_dot(diff.astype(BF16), w_ref[gi]))
        run = run[:, POOL_GROUP_DIM:]
    h = jnp.concatenate(ys, axis=1) * scale_ref[...]
    o_ref[0] = _layer_norm(ALPHA * x + h, g_ref[...], b_ref[...])


def _pool_ln(x, w, scale, g, b):
    B, S, D = x.shape
    rows = POOL_ROWS
    assert all(a < b_ for a, b_ in zip(POOL_WINDOWS, POOL_WINDOWS[1:]))
    assert all(w_ & (w_ - 1) == 0 for w_ in POOL_WINDOWS) and POOL_WINDOWS[-1] <= POOL_HALO
    halo_blocks = rows // POOL_HALO
    row2 = lambda v: v.reshape(1, D)
    return pl.pallas_call(
        _pool_ln_kernel,
        grid=(B, S // rows),
        in_specs=[
            pl.BlockSpec((1, rows, D), lambda bi, i: (bi, i, 0)),
            pl.BlockSpec((1, POOL_HALO, D), lambda bi, i: (bi, jnp.maximum(i * halo_blocks - 1, 0), 0)),
            pl.BlockSpec(w.shape, lambda bi, i: (0, 0, 0)),
            pl.BlockSpec((1, D), lambda bi, i: (0, 0)),
            pl.BlockSpec((1, D), lambda bi, i: (0, 0)),
            pl.BlockSpec((1, D), lambda bi, i: (0, 0)),
        ],
        out_specs=pl.BlockSpec((1, rows, D), lambda bi, i: (bi, i, 0)),
        out_shape=jax.ShapeDtypeStruct((B, S, D), F32),
        scratch_shapes=[pltpu.VMEM((rows + POOL_HALO, D), F32)],
        compiler_params=_params("parallel", "arbitrary"),
    )(x, x, w.astype(BF16), row2(scale), row2(g), row2(b))


def _ffn_ln_kernel(x_ref, wg_ref, wu_ref, wd_ref, g_ref, b_ref, o_ref, xb_ref, acc_ref):
    f = pl.program_id(1)

    @pl.when(f == 0)
    def _():
        xb_ref[...] = x_ref[...].astype(BF16)
        acc_ref[...] = jnp.zeros_like(acc_ref)

    xb = xb_ref[...]
    a = _silu(_dot(xb, wg_ref[...])) * _dot(xb, wu_ref[...])
    acc_ref[...] += _dot(a.astype(BF16), wd_ref[...])

    @pl.when(f == pl.num_programs(1) - 1)
    def _():
        o_ref[...] = _layer_norm(ALPHA * x_ref[...] + acc_ref[...], g_ref[...], b_ref[...])


def _ff_chunk(d_ff, target):
    best = LANES
    for c in range(LANES, target + 1, LANES):
        if d_ff % c == 0:
            best = c
    return best


def _ffn_ln(x, w_gu, w_down, g, b):
    T, D = x.shape
    d_ff = w_down.shape[0]
    fc = _ff_chunk(d_ff, 1536)
    nf = d_ff // fc
    rows = FFN_ROWS
    return pl.pallas_call(
        _ffn_ln_kernel,
        grid=(T // rows, nf),
        in_specs=[
            pl.BlockSpec((rows, D), lambda i, f: (i, 0)),
            pl.BlockSpec((D, fc), lambda i, f: (0, f)),
            pl.BlockSpec((D, fc), lambda i, f: (0, nf + f)),
            pl.BlockSpec((fc, D), lambda i, f: (f, 0)),
            pl.BlockSpec((1, D), lambda i, f: (0, 0)),
            pl.BlockSpec((1, D), lambda i, f: (0, 0)),
        ],
        out_specs=pl.BlockSpec((rows, D), lambda i, f: (i, 0)),
        out_shape=jax.ShapeDtypeStruct((T, D), F32),
        scratch_shapes=[pltpu.VMEM((rows, D), BF16), pltpu.VMEM((rows, D), F32)],
        compiler_params=_params("parallel", "arbitrary"),
    )(x, w_gu, w_gu, w_down, g.reshape(1, D), b.reshape(1, D))


def _rope_tables(pos, reps):
    freqs = jnp.power(ROPE_THETA, -jnp.arange(HALF_DIM, dtype=F32) / HALF_DIM)
    ang = pos.astype(F32)[:, None] * freqs[None, :]
    cos, sin = jnp.cos(ang), jnp.sin(ang)
    return (jnp.tile(jnp.concatenate([cos, cos], axis=1), (1, reps)),
            jnp.tile(jnp.concatenate([-sin, sin], axis=1), (1, reps)))


def _in_proj_kernel(x_ref, w_ref, wt_ref, cos_ref, sin_ref, cost_ref, sint_ref,
                    qt_ref, kcv_ref, ks_ref, vst_ref, kw_ref, vwt_ref, gatet_ref, *, steps_per_seq):
    rows = x_ref.shape[0]
    xb = x_ref[...].astype(BF16)
    cos = cos_ref[...]
    sin = sin_ref[...]
    lane = lax.broadcasted_iota(jnp.int32, cos.shape, 1)
    first_half = (lane & (HEAD_DIM - 1)) < HALF_DIM
    seq_step = pl.program_id(0) % steps_per_seq
    pos = seq_step * rows + lax.broadcasted_iota(jnp.int32, (rows, HEAD_DIM), 0)
    col = lax.broadcasted_iota(jnp.int32, (rows, HEAD_DIM), 1)
    blk_onehot = jnp.where((pos >> SEL_SHIFT) == col, 1.0, 0.0)
    zeros = jnp.zeros((rows, HEAD_DIM), F32)

    def col_tile(j):
        return _dot(xb, w_ref[:, j * PROJ_COLS:(j + 1) * PROJ_COLS])

    def row_tile(r0, n):
        return _dot_nt(wt_ref[r0:r0 + n, :], xb)

    def rope(y):
        rot = jnp.where(first_half,
                        pltpu.roll(y, PROJ_COLS - HALF_DIM, 1),
                        pltpu.roll(y, HALF_DIM, 1))
        return y * cos + rot * sin

    def rope_t(yt):
        pieces = []
        for h in range(yt.shape[0] // HEAD_DIM):
            pieces.append(yt[h * HEAD_DIM + HALF_DIM:(h + 1) * HEAD_DIM])
            pieces.append(yt[h * HEAD_DIM:h * HEAD_DIM + HALF_DIM])
        return yt * cost_ref[...] + jnp.concatenate(pieces, axis=0) * sint_ref[...]

    def store_keys(ref, y, extra):
        for gi in range(N_KV_GROUPS):
            ref[0, gi, :, 0:HEAD_DIM] = y[:, gi * HEAD_DIM:(gi + 1) * HEAD_DIM].astype(ref.dtype)
            ref[0, gi, :, HEAD_DIM:] = extra.astype(ref.dtype)

    def store_values_t(ref, yt):
        ones = jnp.ones((VT_ROWS - HEAD_DIM, rows), ref.dtype)
        for gi in range(N_KV_GROUPS):
            ref[0, gi, 0, 0:HEAD_DIM, :] = yt[gi * HEAD_DIM:(gi + 1) * HEAD_DIM].astype(ref.dtype)
            ref[0, gi, 0, HEAD_DIM:, :] = ones

    for j in range(D_MODEL // PROJ_COLS):
        qt = rope_t(row_tile(j * PROJ_COLS, PROJ_COLS)) * Q_SCALE
        qt_ref[0, j * PROJ_COLS:(j + 1) * PROJ_COLS, :] = qt.astype(qt_ref.dtype)
    raw_k = col_tile(0)
    raw_v = col_tile(1)
    for gi in range(N_KV_GROUPS):
        kcv_ref[0, gi, :, 0:HEAD_DIM] = raw_k[:, gi * HEAD_DIM:(gi + 1) * HEAD_DIM]
        kcv_ref[0, gi, :, HEAD_DIM:] = raw_v[:, gi * HEAD_DIM:(gi + 1) * HEAD_DIM]
    store_keys(ks_ref, rope(col_tile(2)), blk_onehot)
    store_keys(kw_ref, rope(col_tile(3)), zeros)
    store_values_t(vst_ref, row_tile(D_MODEL, KV_DIM))
    store_values_t(vwt_ref, row_tile(D_MODEL + KV_DIM, KV_DIM))
    logits_t = row_tile(D_MODEL + 2 * KV_DIM, GATE_ROWS)
    gatet_ref[0] = 1.0 / (1.0 + jnp.exp(-logits_t))


def _in_proj(x, w_in, B, S):
    T, D = x.shape
    rows = PROJ_ROWS
    assert rows == ATT_K_ROWS
    steps_per_seq = S // rows
    sec = lambda k: w_in[:, D_MODEL + k * KV_DIM:D_MODEL + (k + 1) * KV_DIM]
    w = jnp.concatenate([sec(0), sec(1), sec(2), sec(4)], axis=1).astype(BF16)
    wg = w_in[:, D_MODEL + 6 * KV_DIM:].reshape(D, N_KV_GROUPS, N_BRANCHES * HEADS_PER_GROUP)
    wg = jnp.pad(wg, ((0, 0), (0, 0), (0, GATE_SLOTS - N_BRANCHES * HEADS_PER_GROUP))).reshape(D, GATE_ROWS)
    wt = jnp.concatenate([w_in[:, :D_MODEL], sec(3), sec(5), wg], axis=1).T.astype(BF16)
    cos, sin = _rope_tables(jnp.arange(S), PROJ_COLS // HEAD_DIM)
    k_shape = jax.ShapeDtypeStruct((B, N_KV_GROUPS, S, AUG_DIM), BF16)
    k_spec = pl.BlockSpec((1, N_KV_GROUPS, rows, AUG_DIM),
                          lambda i: (i // steps_per_seq, 0, i % steps_per_seq, 0))
    vt_shape = jax.ShapeDtypeStruct((B, N_KV_GROUPS, steps_per_seq, VT_ROWS, rows), BF16)
    vt_spec = pl.BlockSpec((1, N_KV_GROUPS, 1, VT_ROWS, rows),
                           lambda i: (i // steps_per_seq, 0, i % steps_per_seq, 0, 0))
    tok_map = lambda i: (i, 0)
    seq_map = lambda i: (i % steps_per_seq, 0)
    feat_map = lambda i: (i // steps_per_seq, 0, i % steps_per_seq)
    return pl.pallas_call(
        functools.partial(_in_proj_kernel, steps_per_seq=steps_per_seq),
        grid=(T // rows,),
        in_specs=[
            pl.BlockSpec((rows, D), tok_map),
            pl.BlockSpec(w.shape, lambda i: (0, 0)),
            pl.BlockSpec(wt.shape, lambda i: (0, 0)),
            pl.BlockSpec((rows, PROJ_COLS), seq_map),
            pl.BlockSpec((rows, PROJ_COLS), seq_map),
            pl.BlockSpec((PROJ_COLS, rows), lambda i: (0, i % steps_per_seq)),
            pl.BlockSpec((PROJ_COLS, rows), lambda i: (0, i % steps_per_seq)),
        ],
        out_specs=[
            pl.BlockSpec((1, D_MODEL, rows), feat_map),
            k_spec,
            k_spec, vt_spec, k_spec, vt_spec,
            pl.BlockSpec((1, GATE_ROWS, rows), feat_map),
        ],
        out_shape=[
            jax.ShapeDtypeStruct((B, D_MODEL, S), BF16),
            jax.ShapeDtypeStruct((B, N_KV_GROUPS, S, AUG_DIM), F32),
            k_shape, vt_shape, k_shape, vt_shape,
            jax.ShapeDtypeStruct((B, GATE_ROWS, S), F32),
        ],
        compiler_params=_params("parallel"),
    )(x, w, wt, cos, sin, cos.T, sin.T)


def _compress_kernel(kv_ref, w1_ref, pea_ref, peb_ref, w2_ref, w2t_ref, cos_ref, sin_ref, kc_ref, vct_ref):
    n = kc_ref.shape[2]
    hid = CMP_HIDDEN
    r = jnp.zeros((n, 4 * hid), F32)
    bias_a = jnp.zeros((SUBLANES, 4 * hid), F32)
    bias_b = jnp.zeros((SUBLANES, 4 * hid), F32)
    for p in range(CMP_STRIDE):
        slab = kv_ref[0, 0, pl.ds(p, n, stride=CMP_STRIDE), :].astype(BF16)
        r = r + _dot(slab, w1_ref[p])
        bias_a = bias_a + _dot(pea_ref[p], w1_ref[p])
        bias_b = bias_b + _dot(peb_ref[p], w1_ref[p])
    hidden = []
    for which in range(2):
        c0 = which * 2 * hid
        nxt = pltpu.roll(r[:, c0 + hid:c0 + 2 * hid], n - 1, 0)
        bias = bias_a[0:1, c0:c0 + hid] + bias_b[0:1, c0 + hid:c0 + 2 * hid]
        hidden.append(_silu(r[:, c0:c0 + hid] + nxt + bias).astype(BF16))
    kc = _dot(hidden[0], w2_ref[...])
    rot = jnp.concatenate([kc[:, HALF_DIM:], kc[:, :HALF_DIM]], axis=1)
    kc_ref[0, 0] = (kc * cos_ref[...] + rot * sin_ref[...]).astype(kc_ref.dtype)
    vct_ref[0, 0] = _dot_nt(w2t_ref[...], hidden[1]).astype(vct_ref.dtype)


def _compress(kv_raw, w1_k, w2_k, pe_k, w1_v, w2_v, pe_v, B, S):
    n_chunks = S // CMP_STRIDE
    hid = CMP_HIDDEN
    split = lambda w1: w1.reshape(2, CMP_STRIDE, HEAD_DIM, hid).transpose(1, 2, 0, 3).reshape(CMP_STRIDE, HEAD_DIM, 2 * hid)
    zeros = jnp.zeros((CMP_STRIDE, HEAD_DIM, 2 * hid), F32)
    w1 = jnp.concatenate([jnp.concatenate([split(w1_k), zeros], axis=2),
                          jnp.concatenate([zeros, split(w1_v)], axis=2)], axis=1).astype(BF16)
    pe = jnp.concatenate([pe_k, pe_v], axis=1)
    tile_rows = lambda v: jnp.broadcast_to(v[:, None, :], (CMP_STRIDE, SUBLANES, AUG_DIM)).astype(BF16)
    pea, peb = tile_rows(pe[:CMP_STRIDE]), tile_rows(pe[CMP_STRIDE:])
    cos, sin = _rope_tables(CMP_STRIDE * jnp.arange(n_chunks) + CMP_BLOCK - 1, 1)
    const = lambda a: pl.BlockSpec(a.shape, lambda bi, gi: (0,) * a.ndim)
    w2k = w2_k.astype(BF16)
    w2vt = w2_v.T.astype(BF16)
    return pl.pallas_call(
        _compress_kernel,
        grid=(B, N_KV_GROUPS),
        in_specs=[
            pl.BlockSpec((1, 1, S, AUG_DIM), lambda bi, gi: (bi, gi, 0, 0)),
            const(w1), const(pea), const(peb), const(w2k), const(w2vt), const(cos), const(sin),
        ],
        out_specs=[pl.BlockSpec((1, 1, n_chunks, HEAD_DIM), lambda bi, gi: (bi, gi, 0, 0)),
                   pl.BlockSpec((1, 1, HEAD_DIM, n_chunks), lambda bi, gi: (bi, gi, 0, 0))],
        out_shape=[jax.ShapeDtypeStruct((B, N_KV_GROUPS, n_chunks, HEAD_DIM), BF16),
                   jax.ShapeDtypeStruct((B, N_KV_GROUPS, HEAD_DIM, n_chunks), BF16)],
        compiler_params=_params("parallel", "parallel"),
    )(kv_raw, w1, pea, peb, w2k, w2vt, cos, sin)


def _cmp_attn_kernel(qt_ref, kc_ref, vct_ref, ovl_ref, oct_ref, selt_ref, score_ref, imp_ref):
    i = pl.program_id(2)
    cols = qt_ref.shape[2]
    n_cmp = kc_ref.shape[2]
    n_sel = ovl_ref.shape[0]
    t = i * cols + lax.broadcasted_iota(jnp.int32, (1, cols), 1)
    any_valid = jnp.where(t >= CMP_BLOCK - 1, 1.0, 0.0)

    def attend(n):
        kc = kc_ref[0, 0, 0:n, :]
        vct = vct_ref[0, 0, :, 0:n]
        cend = CMP_STRIDE * lax.broadcasted_iota(jnp.int32, (n, 1), 0) + (CMP_BLOCK - 1)
        cvalid = cend <= t
        pc_sum = jnp.zeros((n, cols), F32)
        for h in range(HEADS_PER_GROUP):
            qh = qt_ref[0, h * HEAD_DIM:(h + 1) * HEAD_DIM, :]
            s = jnp.where(cvalid, _dot(kc, qh), NEG_INF)
            e = jnp.exp2(s - jnp.max(s, axis=0, keepdims=True))
            pc = e * (any_valid / jnp.sum(e, axis=0, keepdims=True))
            oct_ref[0, h * HEAD_DIM:(h + 1) * HEAD_DIM, :] = _dot(vct, pc.astype(BF16))
            pc_sum = pc_sum + pc
        imp_ref[...] = _dot(ovl_ref[:, 0:n], pc_sum.astype(BF16))

    n_half = n_cmp // 2
    early = (i + 1) * cols <= n_half * CMP_STRIDE

    @pl.when(early)
    def _():
        attend(n_half)

    @pl.when(jnp.logical_not(early))
    def _():
        attend(n_cmp)

    imp = imp_ref[...]
    j = lax.broadcasted_iota(jnp.int32, (n_sel, 1), 0)
    blk_t = t >> SEL_SHIFT
    bvalid = j <= blk_t
    forced = (j == 0) | (j == blk_t) | (j == blk_t - 1)
    score = jnp.where(bvalid, imp + jnp.where(forced, FORCE_BONUS, 0.0), -1.0)
    key = pltpu.bitcast(score, jnp.int32)
    key_next = key + 1
    score_ref[...] = key
    def rank_group(grp, rank):
        base = pl.multiple_of(grp * SUBLANES, SUBLANES)
        others = score_ref[pl.ds(base, SUBLANES), :]
        for r in range(SUBLANES):
            ahead = others[r:r + 1, :] >= jnp.where(j > base + r, key, key_next)
            rank = rank + jnp.where(ahead, 1.0, 0.0)
        return rank

    n_groups = jnp.minimum(((i + 1) * cols) // (SUBLANES * SEL_BLOCK), n_sel // SUBLANES)
    rank = lax.fori_loop(0, n_groups, rank_group, jnp.zeros((n_sel, cols), F32))
    selected = (rank < float(SEL_TOP_N)) & bvalid
    selt_ref[0, 0] = jnp.where(selected, 0.0, NEG_INF).astype(selt_ref.dtype)


def _cmp_attn(qt, kc, vct, B, S):
    cols = CMP_Q_ROWS
    n_cmp = S // CMP_STRIDE
    n_sel = SEL_SLOTS
    assert S // SEL_BLOCK <= SEL_SLOTS
    cstart = CMP_STRIDE * jnp.arange(n_cmp)
    sstart = SEL_BLOCK * jnp.arange(n_sel)
    overlap = ((cstart[None, :] <= sstart[:, None] + SEL_BLOCK - 1)
               & (cstart[None, :] + CMP_BLOCK - 1 >= sstart[:, None])).astype(BF16)
    q_spec = pl.BlockSpec((1, GROUP_Q_DIM, cols), lambda bi, gi, i: (bi, gi, i))
    return pl.pallas_call(
        _cmp_attn_kernel,
        grid=(B, N_KV_GROUPS, S // cols),
        in_specs=[
            q_spec,
            pl.BlockSpec((1, 1, n_cmp, HEAD_DIM), lambda bi, gi, i: (bi, gi, 0, 0)),
            pl.BlockSpec((1, 1, HEAD_DIM, n_cmp), lambda bi, gi, i: (bi, gi, 0, 0)),
            pl.BlockSpec(overlap.shape, lambda bi, gi, i: (0, 0)),
        ],
        out_specs=[
            q_spec,
            pl.BlockSpec((1, 1, n_sel, cols), lambda bi, gi, i: (bi, gi, 0, i)),
        ],
        out_shape=[
            jax.ShapeDtypeStruct((B, D_MODEL, S), F32),
            jax.ShapeDtypeStruct((B, N_KV_GROUPS, n_sel, S), BF16),
        ],
        scratch_shapes=[pltpu.VMEM((n_sel, cols), jnp.int32), pltpu.VMEM((n_sel, cols), F32)],
        compiler_params=_params("parallel", "parallel", "arbitrary"),
    )(qt, kc, vct, overlap)


def _sel_win_attn_kernel(qt_ref, selt_ref, ks_ref, vst_ref, kw_ref, vwt_ref, diag_ref, band_ref, oct_ref, gatet_ref,
                         o_ref, qaug_ref, m_ref, acc_ref, ow_ref):
    i = pl.program_id(2)
    tq = qt_ref.shape[2]
    hg = HEADS_PER_GROUP
    half = ATT_HALF
    n_half = tq // half
    half_cols = hg * half

    for hq in range(n_half):
        for h in range(hg):
            c0 = (hq * hg + h) * half
            qaug_ref[0:HEAD_DIM, c0:c0 + half] = qt_ref[0, h * HEAD_DIM:(h + 1) * HEAD_DIM, hq * half:(hq + 1) * half]
            qaug_ref[HEAD_DIM:, c0:c0 + half] = selt_ref[0, 0, :, hq * half:(hq + 1) * half]
    n_tiles = (hg * tq) // ATT_COLS

    def keys(ref, start, size):
        return ref[0, 0, pl.ds(pl.multiple_of(start, ATT_HALF), size), :]

    m_ref[...] = jnp.full(m_ref.shape, NEG_INF, F32)
    acc_ref[...] = jnp.zeros(acc_ref.shape, F32)

    def sel_step(c, bias_ref):
        k = keys(ks_ref, c * ATT_K_ROWS, ATT_K_ROWS)
        v = vst_ref[0, 0, c]
        new_m, new_acc = [], []
        tile = lambda ct: slice(ct * ATT_COLS, (ct + 1) * ATT_COLS)
        last_part = lambda ct: ((ct + 1) * ATT_COLS - 1) // half_cols
        n_keys = lambda ct: ATT_K_ROWS if bias_ref is None else min(ATT_K_ROWS, (last_part(ct) + 1) * half)
        score = lambda ct: _dot(k[0:n_keys(ct)], qaug_ref[:, tile(ct)])
        scores = [score(ct) for ct in range(ATT_AHEAD)]
        for ct in range(n_tiles):
            cs = tile(ct)
            if ct + ATT_AHEAD < n_tiles:
                scores.append(score(ct + ATT_AHEAD))
            s = scores[ct]
            if bias_ref is not None:
                s = s + bias_ref[0:n_keys(ct), cs]
            m_prev = m_ref[:, cs]
            m_next = jnp.maximum(m_prev, jnp.max(s, axis=0, keepdims=True))
            p = jnp.exp2(s - m_next)
            alpha = jnp.exp2(m_prev - m_next)
            new_acc.append(alpha * acc_ref[:, cs] + _dot(v[:, 0:n_keys(ct)], p.astype(BF16)))
            new_m.append(m_next)
        for ct in range(n_tiles):
            cs = slice(ct * ATT_COLS, (ct + 1) * ATT_COLS)
            acc_ref[:, cs] = new_acc[ct]
            m_ref[:, cs] = new_m[ct]

    def sel_body(c, carry):
        sel_step(c, None)
        return carry

    lax.fori_loop(0, i, sel_body, 0)
    sel_step(i, diag_ref)

    def probs(s):
        return jnp.exp2(s - jnp.max(s, axis=0, keepdims=True)).astype(BF16)

    @pl.when(i == 0)
    def _():
        k = keys(kw_ref, 0, tq)
        v = vwt_ref[0, 0, 0]
        for ct in range(n_tiles):
            cs = slice(ct * ATT_COLS, (ct + 1) * ATT_COLS)
            ow_ref[:, cs] = _dot(v, probs(_dot(k, qaug_ref[:, cs]) + diag_ref[:, cs]))

    @pl.when(i > 0)
    def _():
        v_prev = vwt_ref[0, 0, i - 1]
        v_here = vwt_ref[0, 0, i]
        for hq in range(n_half):
            k = keys(kw_ref, i * tq + hq * half - WINDOW, WINDOW + half)
            n_prev = tq - hq * half
            for ct in range(half_cols // ATT_COLS):
                cs = slice(hq * half_cols + ct * ATT_COLS, hq * half_cols + (ct + 1) * ATT_COLS)
                p = probs(_dot(k, qaug_ref[:, cs]) + band_ref[:, ct * ATT_COLS:(ct + 1) * ATT_COLS])
                ow_ref[:, cs] = (_dot(v_prev[:, tq - n_prev:], p[0:n_prev])
                                 + _dot(v_here[:, 0:WINDOW + half - n_prev], p[n_prev:]))

    acc_s = acc_ref[...]
    acc_w = ow_ref[...]
    o_s = acc_s[0:HEAD_DIM] / acc_s[HEAD_DIM:HEAD_DIM + 1]
    o_w = acc_w[0:HEAD_DIM] / acc_w[HEAD_DIM:HEAD_DIM + 1]
    for hq in range(n_half):
        tok = slice(hq * half, (hq + 1) * half)
        for h in range(hg):
            c0 = (hq * hg + h) * half
            gc = gatet_ref[0, N_BRANCHES * h + 0:N_BRANCHES * h + 1, tok]
            gs = gatet_ref[0, N_BRANCHES * h + 1:N_BRANCHES * h + 2, tok]
            gw = gatet_ref[0, N_BRANCHES * h + 2:N_BRANCHES * h + 3, tok]
            o = (gc * oct_ref[0, h * HEAD_DIM:(h + 1) * HEAD_DIM, tok]
                 + gs * o_s[:, c0:c0 + half] + gw * o_w[:, c0:c0 + half])
            o_ref[0, tok, h * HEAD_DIM:(h + 1) * HEAD_DIM] = o.T.astype(o_ref.dtype)


def _sel_win_attn(qt, selt, ks, vst, kw, vwt, oct, gatet, B, S):
    tq = ATT_Q_ROWS
    half = ATT_HALF
    hg = HEADS_PER_GROUP
    cols = hg * tq
    n_chunks = S // ATT_K_ROWS
    t_rel = (jnp.arange(tq // half)[:, None, None] * half + jnp.arange(half)[None, None, :])
    t_rel = jnp.broadcast_to(t_rel, (tq // half, hg, half)).reshape(1, cols)
    diag = jnp.where(jnp.arange(ATT_K_ROWS)[:, None] <= t_rel, 0.0, NEG_INF).astype(F32)
    tt = jnp.broadcast_to(jnp.arange(half)[None, :], (hg, half)).reshape(1, hg * half)
    a = jnp.arange(WINDOW + half)[:, None]
    band = jnp.where((a > tt) & (a <= tt + WINDOW), 0.0, NEG_INF).astype(F32)
    k_spec = pl.BlockSpec((1, 1, S, AUG_DIM), lambda bi, gi, i: (bi, gi, 0, 0))
    vt_spec = pl.BlockSpec((1, 1, n_chunks, VT_ROWS, ATT_K_ROWS), lambda bi, gi, i: (bi, gi, 0, 0, 0))
    q_spec = pl.BlockSpec((1, GROUP_Q_DIM, tq), lambda bi, gi, i: (bi, gi, i))
    const = lambda arr: pl.BlockSpec(arr.shape, lambda bi, gi, i: (0, 0))
    return pl.pallas_call(
        _sel_win_attn_kernel,
        grid=(B, N_KV_GROUPS, S // tq),
        in_specs=[
            q_spec,
            pl.BlockSpec((1, 1, SEL_SLOTS, tq), lambda bi, gi, i: (bi, gi, 0, i)),
            k_spec, vt_spec, k_spec, vt_spec,
            const(diag), const(band),
            q_spec,
            pl.BlockSpec((1, GATE_SLOTS, tq), lambda bi, gi, i: (bi, gi, i)),
        ],
        out_specs=pl.BlockSpec((1, tq, GROUP_Q_DIM), lambda bi, gi, i: (bi, i, gi)),
        out_shape=jax.ShapeDtypeStruct((B, S, D_MODEL), BF16),
        scratch_shapes=[
            pltpu.VMEM((AUG_DIM, cols), BF16),
            pltpu.VMEM((1, cols), F32),
            pltpu.VMEM((VT_ROWS, cols), F32),
            pltpu.VMEM((VT_ROWS, cols), F32),
        ],
        compiler_params=_params("parallel", "parallel", "arbitrary"),
    )(qt, selt, ks, vst, kw, vwt, diag, band, oct, gatet)


def _out_proj_ln_kernel(o_ref, w_ref, x_ref, g_ref, b_ref, y_ref):
    h = _dot(o_ref[...], w_ref[...])
    y_ref[...] = _layer_norm(ALPHA * x_ref[...] + h, g_ref[...], b_ref[...])


def _out_proj_ln(o, w_out, x, g, b):
    T, D = x.shape
    rows = PROJ_ROWS
    return pl.pallas_call(
        _out_proj_ln_kernel,
        grid=(T // rows,),
        in_specs=[
            pl.BlockSpec((rows, D), lambda i: (i, 0)),
            pl.BlockSpec((D, D), lambda i: (0, 0)),
            pl.BlockSpec((rows, D), lambda i: (i, 0)),
            pl.BlockSpec((1, D), lambda i: (0, 0)),
            pl.BlockSpec((1, D), lambda i: (0, 0)),
        ],
        out_specs=pl.BlockSpec((rows, D), lambda i: (i, 0)),
        out_shape=jax.ShapeDtypeStruct((T, D), F32),
        compiler_params=_params("parallel"),
    )(o, w_out.astype(BF16), x, g.reshape(1, D), b.reshape(1, D))


def _route_kernel(x_ref, r_ref, tri_ref, xb_ref, info_ref, cnt_ref, carry_ref):
    @pl.when(pl.program_id(0) == 0)
    def _():
        carry_ref[...] = jnp.zeros_like(carry_ref)

    x = x_ref[...]
    xb = x.astype(BF16)
    xb_ref[:, 0:x.shape[1]] = xb
    x_lo = (x - xb.astype(F32)).astype(BF16)
    r_hi = r_ref[0]
    r_lo = r_ref[1]
    logits = _dot(xb, r_hi) + (_dot(x_lo, r_hi) + _dot(xb, r_lo))
    lane = lax.broadcasted_iota(jnp.int32, logits.shape, 1)
    logits = jnp.where(lane < N_EXPERTS, logits, -jnp.inf)
    v1 = jnp.max(logits, axis=-1, keepdims=True)
    i1 = jnp.min(jnp.where(logits == v1, lane, LANES), axis=-1, keepdims=True)
    rest = jnp.where(lane == i1, -jnp.inf, logits)
    v2 = jnp.max(rest, axis=-1, keepdims=True)
    i2 = jnp.min(jnp.where(rest == v2, lane, LANES), axis=-1, keepdims=True)
    e2 = jnp.exp(v2 - v1)
    w1 = 1.0 / (1.0 + e2)
    w2 = e2 / (1.0 + e2)
    m1 = jnp.where(lane == i1, 1.0, 0.0)
    m2 = jnp.where(lane == i2, 1.0, 0.0)
    routed = m1 + m2
    before = _dot(tri_ref[...], routed.astype(BF16)) + carry_ref[0:1, :]
    rank1 = jnp.sum(m1 * before, axis=-1, keepdims=True)
    rank2 = jnp.sum(m2 * before, axis=-1, keepdims=True)
    cnt = jnp.sum(routed, axis=0, keepdims=True)
    carry_ref[...] = carry_ref[...] + cnt
    cnt_ref[0] = jnp.broadcast_to(cnt, cnt_ref.shape[1:])
    info = jnp.where(lane == 0, i1.astype(F32),
                     jnp.where(lane == 1, i2.astype(F32),
                               jnp.where(lane == 2, rank1, jnp.where(lane == 3, rank2, 0.0))))
    info_ref[0] = info.T[0:SUBLANES, :]
    w1_hi = w1.astype(BF16).astype(F32)
    w2_hi = w2.astype(BF16).astype(F32)
    lo_half = (lane & 1) == 1
    wcol = (jnp.where((lane >> 1) == i1, jnp.where(lo_half, w1 - w1_hi, w1_hi), 0.0)
            + jnp.where((lane >> 1) == i2, jnp.where(lo_half, w2 - w2_hi, w2_hi), 0.0))
    xb_ref[:, x.shape[1]:] = wcol.astype(BF16)


def _route(x, router):
    T, D = x.shape
    rows = MOE_ROWS
    r = jnp.pad(router, ((0, 0), (0, LANES - router.shape[1])))
    r_hi = r.astype(BF16)
    r_lo = (r - r_hi.astype(F32)).astype(BF16)
    r2 = jnp.stack([r_hi, r_lo])
    tri = (jnp.arange(rows)[None, :] < jnp.arange(rows)[:, None]).astype(BF16)
    return pl.pallas_call(
        _route_kernel,
        grid=(T // rows,),
        in_specs=[
            pl.BlockSpec((rows, D), lambda i: (i, 0)),
            pl.BlockSpec(r2.shape, lambda i: (0, 0, 0)),
            pl.BlockSpec(tri.shape, lambda i: (0, 0)),
        ],
        out_specs=[
            pl.BlockSpec((rows, D + LANES), lambda i: (i, 0)),
            pl.BlockSpec((1, SUBLANES, rows), lambda i: (i, 0, 0)),
            pl.BlockSpec((1, SUBLANES, LANES), lambda i: (i, 0, 0)),
        ],
        out_shape=[
            jax.ShapeDtypeStruct((T, D + LANES), BF16),
            jax.ShapeDtypeStruct((T // rows, SUBLANES, rows), F32),
            jax.ShapeDtypeStruct((T // rows, SUBLANES, LANES), F32),
        ],
        scratch_shapes=[pltpu.VMEM((SUBLANES, LANES), F32)],
        compiler_params=_params("arbitrary"),
    )(x, r2, tri)


def _moe_tables(cnt, n_blocks_max, n_items_max):
    R = MOE_ROWS
    C, E = cnt.shape
    i32 = jnp.int32
    count_le = lambda sorted_v, q: jnp.sum(sorted_v[None, :] <= q[:, None], axis=1).astype(i32)

    def windows(ra, rb, live):
        near = jnp.minimum((ra // MOE_ALIGN) * MOE_ALIGN, R - MOE_WIN)
        fits = rb - near < MOE_WIN
        start = jnp.where(fits, near, (ra // MOE_WIN) * MOE_WIN)
        n = jnp.where(fits, 1, rb // MOE_WIN - ra // MOE_WIN + 1)
        return jnp.where(live, start, 0), jnp.where(live, n, 0)

    cum = jnp.concatenate([jnp.zeros((1, E), i32), jnp.cumsum(cnt, axis=0)], axis=0)
    tot = cum[-1]
    nb = (tot + R - 1) // R
    nb_end = jnp.cumsum(nb)
    blk_start = nb_end - nb
    n_blocks = nb_end[-1]
    b = jnp.minimum(jnp.arange(n_blocks_max, dtype=i32), n_blocks - 1)
    bexp = jnp.minimum(count_le(nb_end, b), E - 1)
    bvalid = jnp.arange(n_blocks_max, dtype=i32) < n_blocks
    lb = b - blk_start[bexp]
    rho0 = lb * R
    rho1 = jnp.minimum((lb + 1) * R, tot[bexp]) - 1
    cum_b = cum[1:, :][:, bexp].T
    lo = jnp.minimum(jnp.sum(cum_b <= rho0[:, None], axis=1).astype(i32), C - 1)
    hi = jnp.minimum(jnp.sum(cum_b <= rho1[:, None], axis=1).astype(i32), C - 1)
    nit = jnp.where(bvalid, hi - lo + 1, 0)
    it_end = jnp.cumsum(nit)
    it_start = it_end - nit
    n_items = it_end[-1]
    i = jnp.arange(n_items_max, dtype=i32)
    ic = jnp.minimum(i, n_items - 1)
    d_blk = jnp.minimum(count_le(it_end, ic), n_blocks_max - 1)
    d_chk = lo[d_blk] + ic - it_start[d_blk]
    d_valid = i < n_items
    d_first = d_valid & (ic == it_start[d_blk])
    d_last = d_valid & (ic == it_end[d_blk] - 1)
    d_exp = bexp[d_blk]
    d_ra = jnp.maximum(rho0[d_blk], cum[d_chk, d_exp]) - rho0[d_blk]
    d_rb = jnp.minimum(rho1[d_blk], cum[d_chk + 1, d_exp] - 1) - rho0[d_blk]
    d_start, d_nwin = windows(d_ra, d_rb, d_valid & (d_rb >= d_ra))
    is_fill = (i >= n_items) & (i < n_items + (n_blocks_max - n_blocks))
    d_blk = jnp.where(i < n_items, d_blk, jnp.minimum(n_blocks + i - n_items, n_blocks_max - 1))
    d_first = d_first | is_fill
    d_last = d_last | is_fill
    bl_lo = (blk_start[None, :] + cum[:-1] // R).reshape(-1)
    bl_hi = (blk_start[None, :] + (cum[1:] - 1) // R).reshape(-1)
    npair = jnp.where(cnt.reshape(-1) > 0, bl_hi - bl_lo + 1, 0)
    p_end = jnp.cumsum(npair)
    p_start = p_end - npair
    n_items2 = p_end[-1]
    jc = jnp.minimum(i, n_items2 - 1)
    pair = jnp.minimum(count_le(p_end, jc), C * E - 1)
    c_blk = bl_lo[pair] + jc - p_start[pair]
    c_chk = pair // E
    c_valid = i < n_items2
    prev_chk = jnp.concatenate([jnp.full((1,), -1, i32), c_chk[:-1]])
    next_chk = jnp.concatenate([c_chk[1:], jnp.full((1,), -1, i32)])
    next_valid = jnp.concatenate([c_valid[1:], jnp.zeros((1,), bool)])
    c_first = c_valid & (c_chk != prev_chk)
    c_last = c_valid & ((c_chk != next_chk) | ~next_valid)
    c_exp = pair % E
    c_base = (c_blk - blk_start[c_exp]) * R
    c_ra = jnp.maximum(cum[c_chk, c_exp] - c_base, 0)
    c_rb = jnp.minimum(cum[c_chk + 1, c_exp] - 1 - c_base, R - 1)
    c_start, c_nwin = windows(c_ra, c_rb, c_valid)
    as_i32 = lambda v: v.astype(i32)
    return dict(base=blk_start * R, bexp=bexp, bvalid=as_i32(bvalid),
                d_blk=d_blk, d_chk=d_chk, d_first=as_i32(d_first), d_last=as_i32(d_last), d_start=d_start, d_nwin=d_nwin,
                c_blk=c_blk, c_chk=c_chk, c_first=as_i32(c_first), c_last=as_i32(c_last),
                c_start=c_start, c_nwin=c_nwin)


def _dispatch_kernel(blk_ref, chk_ref, first_ref, last_ref, start_ref, nwin_ref, xb_hbm, pos_ref,
                     xs_ref, acc_ref, ring_ref, sem_ref):
    i = pl.program_id(0)
    n_steps = pl.num_programs(0)
    R = xs_ref.shape[0]

    def chunk_copy(step, slot):
        src = xb_hbm.at[pl.ds(pl.multiple_of(chk_ref[step] * R, R), R), :]
        return pltpu.make_async_copy(src, ring_ref.at[slot], sem_ref.at[slot])

    @pl.when(i == 0)
    def _():
        for k in range(MOE_RING - 1):
            @pl.when(k < n_steps)
            def _(k=k):
                chunk_copy(k, k).start()

    ahead = i + MOE_RING - 1

    @pl.when(ahead < n_steps)
    def _():
        chunk_copy(ahead, ahead % MOE_RING).start()

    slot = i % MOE_RING
    chunk_copy(i, slot).wait()
    xb_ref = ring_ref.at[slot]

    @pl.when(first_ref[i] == 1)
    def _():
        acc_ref[...] = jnp.zeros_like(acc_ref)

    pos1 = pos_ref[0, 0:1, :]
    pos2 = pos_ref[0, 1:2, :]

    def window(k, carry):
        r0 = pl.multiple_of(start_ref[i] + k * MOE_WIN, MOE_ALIGN)
        rows = (blk_ref[i] * R + r0 + lax.broadcasted_iota(jnp.int32, (MOE_WIN, 1), 0)).astype(F32)
        hit = jnp.where((pos1 - rows) * (pos2 - rows) == 0.0, 1.0, 0.0).astype(BF16)
        acc_ref[pl.ds(r0, MOE_WIN), :] += _dot(hit, xb_ref[...])
        return carry

    lax.fori_loop(0, nwin_ref[i], window, 0)

    @pl.when(last_ref[i] == 1)
    def _():
        xs_ref[...] = acc_ref[...].astype(xs_ref.dtype)


def _dispatch(xb, posrow, tab, n_blocks_max, n_items_max):
    T, D = xb.shape
    R = MOE_ROWS
    chunk_map = lambda i, blk, chk, first, last, start, nwin: (chk[i], 0)
    block_map = lambda i, blk, chk, first, last, start, nwin: (blk[i], 0)
    grid_spec = pltpu.PrefetchScalarGridSpec(
        num_scalar_prefetch=6,
        grid=(n_items_max,),
        in_specs=[
            pl.BlockSpec(memory_space=pl.ANY),
            pl.BlockSpec((1, SUBLANES, R), lambda i, blk, chk, first, last, start, nwin: (chk[i], 0, 0)),
        ],
        out_specs=pl.BlockSpec((R, D), block_map),
        scratch_shapes=[pltpu.VMEM((R, D), F32), pltpu.VMEM((MOE_RING, R, D), BF16),
                        pltpu.SemaphoreType.DMA((MOE_RING,))],
    )
    return pl.pallas_call(
        _dispatch_kernel,
        grid_spec=grid_spec,
        out_shape=jax.ShapeDtypeStruct((n_blocks_max * R, D), BF16),
        compiler_params=_params("arbitrary"),
    )(tab["d_blk"], tab["d_chk"], tab["d_first"], tab["d_last"], tab["d_start"], tab["d_nwin"], xb, posrow)


def _expert_ffn_kernel(bexp_ref, bvalid_ref, xs_ref, wg_ref, wu_ref, wd_ref, ys_ref, acc_ref):
    b = pl.program_id(0)
    f = pl.program_id(1)

    @pl.when(f == 0)
    def _():
        acc_ref[...] = jnp.zeros_like(acc_ref)

    @pl.when(bvalid_ref[b] == 1)
    def _():
        xb = xs_ref[:, 0:ys_ref.shape[1]]
        a = _silu(_dot(xb, wg_ref[0])) * _dot(xb, wu_ref[0])
        acc_ref[...] += _dot(a.astype(BF16), wd_ref[0])

    @pl.when(f == pl.num_programs(1) - 1)
    def _():
        ws = xs_ref[:, ys_ref.shape[1]:].astype(F32)
        lane = lax.broadcasted_iota(jnp.int32, ws.shape, 1)
        w = jnp.sum(jnp.where((lane >> 1) == bexp_ref[b], ws, 0.0), axis=-1, keepdims=True)
        ys_ref[...] = (acc_ref[...] * w).astype(ys_ref.dtype)


def _expert_ffn(xs, w_gu, w_down, tab, n_blocks_max):
    D = w_down.shape[2]
    R = MOE_ROWS
    d_ff = w_down.shape[1]
    fc = _ff_chunk(d_ff, 1792)
    nf = d_ff // fc
    f_eff = lambda b, f, bvalid: f * bvalid[b] + (nf - 1) * (1 - bvalid[b])
    grid_spec = pltpu.PrefetchScalarGridSpec(
        num_scalar_prefetch=2,
        grid=(n_blocks_max, nf),
        in_specs=[
            pl.BlockSpec((R, D + LANES), lambda b, f, bexp, bvalid: (b, 0)),
            pl.BlockSpec((1, D, fc), lambda b, f, bexp, bvalid: (bexp[b], 0, f_eff(b, f, bvalid))),
            pl.BlockSpec((1, D, fc), lambda b, f, bexp, bvalid: (bexp[b], 0, nf + f_eff(b, f, bvalid))),
            pl.BlockSpec((1, fc, D), lambda b, f, bexp, bvalid: (bexp[b], f_eff(b, f, bvalid), 0)),
        ],
        out_specs=pl.BlockSpec((R, D), lambda b, f, bexp, bvalid: (b, 0)),
        scratch_shapes=[pltpu.VMEM((R, D), F32)],
    )
    return pl.pallas_call(
        _expert_ffn_kernel,
        grid_spec=grid_spec,
        out_shape=jax.ShapeDtypeStruct((xs.shape[0], D), BF16),
        compiler_params=_params("parallel", "arbitrary"),
    )(tab["bexp"], tab["bvalid"], xs, w_gu, w_gu, w_down)


def _combine_ln_kernel(chk_ref, blk_ref, first_ref, last_ref, start_ref, nwin_ref,
                       ys_hbm, pc1_ref, pc2_ref, x_ref, g_ref, b_ref, o_ref, acc_ref, ring_ref, sem_ref):
    i = pl.program_id(0)
    n_steps = pl.num_programs(0)
    R = ring_ref.shape[1]

    def block_copy(step, slot):
        src = ys_hbm.at[pl.ds(pl.multiple_of(blk_ref[step] * R, R), R), :]
        return pltpu.make_async_copy(src, ring_ref.at[slot], sem_ref.at[slot])

    @pl.when(i == 0)
    def _():
        for k in range(MOE_RING - 1):
            @pl.when(k < n_steps)
            def _(k=k):
                block_copy(k, k).start()

    ahead = i + MOE_RING - 1

    @pl.when(ahead < n_steps)
    def _():
        block_copy(ahead, ahead % MOE_RING).start()

    slot = i % MOE_RING
    block_copy(i, slot).wait()
    ys_ref = ring_ref.at[slot]

    @pl.when(first_ref[i] == 1)
    def _():
        acc_ref[...] = jnp.zeros_like(acc_ref)

    def window(k, carry):
        r0 = pl.multiple_of(start_ref[i] + k * MOE_WIN, MOE_ALIGN)
        pc1 = pc1_ref[...]
        pc2 = pc2_ref[...]
        lane = lax.broadcasted_iota(jnp.int32, pc1.shape, 1)
        parts = []
        for t in range(MOE_WIN // LANES):
            col = (blk_ref[i] * R + r0 + t * LANES + lane).astype(F32)
            parts.append(jnp.where((pc1 - col) * (pc2 - col) == 0.0, 1.0, 0.0).astype(BF16))
        acc_ref[...] += _dot(jnp.concatenate(parts, axis=1), ys_ref[pl.ds(r0, MOE_WIN), :])
        return carry

    lax.fori_loop(0, nwin_ref[i], window, 0)

    @pl.when(last_ref[i] == 1)
    def _():
        o_ref[...] = _layer_norm(ALPHA * x_ref[...] + acc_ref[...], g_ref[...], b_ref[...])


def _combine_ln(ys, poscol1, poscol2, x, g, b, tab, n_items_max):
    T, D = x.shape
    R = MOE_ROWS
    chunk_map = lambda i, chk, blk, first, last, start, nwin: (chk[i], 0)
    const_map = lambda i, chk, blk, first, last, start, nwin: (0, 0)
    grid_spec = pltpu.PrefetchScalarGridSpec(
        num_scalar_prefetch=6,
        grid=(n_items_max,),
        in_specs=[
            pl.BlockSpec(memory_space=pl.ANY),
            pl.BlockSpec((R, LANES), chunk_map),
            pl.BlockSpec((R, LANES), chunk_map),
            pl.BlockSpec((R, D), chunk_map),
            pl.BlockSpec((1, D), const_map),
            pl.BlockSpec((1, D), const_map),
        ],
        out_specs=pl.BlockSpec((R, D), chunk_map),
        scratch_shapes=[pltpu.VMEM((R, D), F32), pltpu.VMEM((MOE_RING, R, D), BF16),
                        pltpu.SemaphoreType.DMA((MOE_RING,))],
    )
    return pl.pallas_call(
        _combine_ln_kernel,
        grid_spec=grid_spec,
        out_shape=jax.ShapeDtypeStruct((T, D), F32),
        compiler_params=_params("arbitrary"),
    )(tab["c_chk"], tab["c_blk"], tab["c_first"], tab["c_last"], tab["c_start"], tab["c_nwin"],
      ys, poscol1, poscol2, x, g.reshape(1, D), b.reshape(1, D))


def _moe_ln(x, router, w_gu, w_down, g, b):
    T, D = x.shape
    R = MOE_ROWS
    n_exp = w_down.shape[0]
    n_chunks = T // R
    n_blocks_max = (2 * T) // R + n_exp
    n_items_max = n_blocks_max + n_exp * (n_chunks - 1)
    xb, info, cnt = _route(x, router)
    tab = _moe_tables(cnt[:, 0, :n_exp].astype(jnp.int32), n_blocks_max, n_items_max)
    field = lambda k: info[:, k, :].reshape(T).astype(jnp.int32)
    pos1 = tab["base"][field(0)] + field(2)
    pos2 = tab["base"][field(1)] + field(3)
    pos1 = pos1.astype(F32)
    pos2 = pos2.astype(F32)
    posrow = jnp.stack([pos1.reshape(n_chunks, R), pos2.reshape(n_chunks, R)], axis=1)
    posrow = jnp.pad(posrow, ((0, 0), (0, SUBLANES - 2), (0, 0)), constant_values=-1.0)
    poscol1 = jnp.broadcast_to(pos1[:, None], (T, LANES))
    poscol2 = jnp.broadcast_to(pos2[:, None], (T, LANES))
    xs = _dispatch(xb, posrow, tab, n_blocks_max, n_items_max)
    ys = _expert_ffn(xs, w_gu, w_down, tab, n_blocks_max)
    return _combine_ln(ys, poscol1, poscol2, x, g, b, tab, n_items_max)


def kernel(x, ln_g, ln_b, pool_w, pool_scale, nsa_w_in, nsa_pe_k, nsa_w1_k, nsa_w2_k, nsa_pe_v, nsa_w1_v,
           nsa_w2_v, nsa_w_out, ffn_w_gu, ffn_w_down, moe_router, moe_w_gu, moe_w_down):
    B, S, D = x.shape
    T = B * S
    xa = _pool_ln(x, pool_w[0], pool_scale[0], ln_g[0, 0], ln_b[0, 0]).reshape(T, D)
    x1 = _ffn_ln(xa, ffn_w_gu[0].astype(BF16), ffn_w_down[0].astype(BF16), ln_g[0, 1], ln_b[0, 1])
    qt, kcv, ks, vst, kw, vwt, gatet = _in_proj(x1, nsa_w_in[0], B, S)
    kc, vct = _compress(kcv, nsa_w1_k[0], nsa_w2_k[0], nsa_pe_k[0], nsa_w1_v[0], nsa_w2_v[0], nsa_pe_v[0], B, S)
    oct, selt = _cmp_attn(qt, kc, vct, B, S)
    o = _sel_win_attn(qt, selt, ks, vst, kw, vwt, oct, gatet, B, S).reshape(T, D)
    x2 = _out_proj_ln(o, nsa_w_out[0], x1, ln_g[1, 0], ln_b[1, 0])
    y = _moe_ln(x2, moe_router[0], moe_w_gu[0].astype(BF16), moe_w_down[0].astype(BF16), ln_g[1, 1], ln_b[1, 1])
    return y.reshape(B, S, D)
```
